```python
import jax
import jax.numpy as jnp
from jax import lax
import numpy as np


D_MODEL = 2048
BATCH = 2
SEQ = 4096
DEPTH = 1

HEAD_DIM = 128
ROPE_THETA = 10000.0
GRID_W = 64
DIL_PATTERNS = ((128, 1), (512, 4), (2048, 16))
N_DIL = len(DIL_PATTERNS)
A_HEADS = 8
A_QKV = N_DIL * A_HEADS * HEAD_DIM
A_OUT = A_HEADS * HEAD_DIM
A_QBLOCK = 64
B_Q_HEADS = 16
B_KV_HEADS = 4
B_GROUP = B_Q_HEADS // B_KV_HEADS
B_Q = B_Q_HEADS * HEAD_DIM
B_KV = B_KV_HEADS * HEAD_DIM
B_QBLOCK = 128
N_BRANCH = 2
IN_COLS = 3 * A_QKV + B_Q + 2 * B_KV + N_BRANCH * D_MODEL
N_GROUPS = 4
EXPERTS_PER_GROUP = 8
N_EXPERTS = N_GROUPS * EXPERTS_PER_GROUP
TOP_K = 2
EXPERT_FF = 1024
MOE_BLOCK = 128
DN_ALPHA = (2 * DEPTH) ** 0.25
DN_BETA = (8 * DEPTH) ** -0.25
LN_EPS = 1e-5
RMS_EPS = 1e-6
NEG_INF = -1e30

kernel_name = 'hybrid_dilated_axial_hmoe_encoder'


def layer_norm(x, g, b):
    xf = x.astype(jnp.float32)
    mu = xf.mean(-1, keepdims=True)
    var = jnp.square(xf - mu).mean(-1, keepdims=True)
    y = (xf - mu) * lax.rsqrt(var + LN_EPS) * g.astype(jnp.float32) + b.astype(jnp.float32)
    return y.astype(x.dtype)


def rms_norm(x, g):
    xf = x.astype(jnp.float32)
    y = xf * lax.rsqrt(jnp.mean(xf * xf, -1, keepdims=True) + RMS_EPS) * g.astype(jnp.float32)
    return y.astype(x.dtype)


def rope_1d_angles(seq):
    half = HEAD_DIM // 2
    inv = ROPE_THETA ** (-jnp.arange(half, dtype=jnp.float32) / half)
    return jnp.arange(seq, dtype=jnp.float32)[:, None] * inv[None, :]


def rope_axial_angles(seq):
    rows = seq // GRID_W
    r, c = jnp.meshgrid(jnp.arange(rows, dtype=jnp.float32), jnp.arange(GRID_W, dtype=jnp.float32), indexing='ij')
    n_axis = HEAD_DIM // 4
    inv = ROPE_THETA ** (-jnp.arange(n_axis, dtype=jnp.float32) / n_axis)
    return jnp.concatenate([r.reshape(-1, 1) * inv[None, :], c.reshape(-1, 1) * inv[None, :]], axis=-1)


def apply_rope(x, ang):
    xf = x.astype(jnp.float32)
    half = HEAD_DIM // 2
    x1, x2 = xf[..., :half], xf[..., half:]
    cos = jnp.cos(ang)[None, :, None, :]
    sin = jnp.sin(ang)[None, :, None, :]
    return jnp.concatenate([x1 * cos - x2 * sin, x2 * cos + x1 * sin], axis=-1).astype(x.dtype)


def dilated_window_attention(q, k, v, window, dilation):
    bsz, seq, nh, hd = q.shape
    half = window // (2 * dilation)
    sub_len = seq // dilation
    n_blk = -(-sub_len // A_QBLOCK)
    sub_pad = n_blk * A_QBLOCK
    kb_len = A_QBLOCK + 2 * half

    def to_residues(t):
        return t.reshape(bsz, sub_len, dilation, nh, hd).transpose(0, 2, 3, 1, 4)

    qr = jnp.pad(to_residues(q), ((0, 0), (0, 0), (0, 0), (0, sub_pad - sub_len), (0, 0)))
    qr = qr.reshape(bsz, dilation, nh, n_blk, A_QBLOCK, hd)
    kpad = ((0, 0), (0, 0), (0, 0), (half, sub_pad - sub_len + half), (0, 0))
    kr = jnp.pad(to_residues(k), kpad)
    vr = jnp.pad(to_residues(v), kpad)
    key_idx = jnp.arange(n_blk)[:, None] * A_QBLOCK + jnp.arange(kb_len)[None, :]
    kb = kr[:, :, :, key_idx]
    vb = vr[:, :, :, key_idx]
    key_pos = (key_idx - half)[:, None, :]
    q_pos = jnp.arange(sub_pad).reshape(n_blk, A_QBLOCK)[:, :, None]
    mask = (jnp.abs(key_pos - q_pos) <= half) & (key_pos >= 0) & (key_pos < sub_len)
    s = jnp.einsum('bdhnqe,bdhnke->bdhnqk', qr, kb, preferred_element_type=jnp.float32) * (HEAD_DIM ** -0.5)
    s = jnp.where(mask, s, NEG_INF)
    m = jnp.max(s, axis=-1, keepdims=True)
    lse = m + jnp.log(jnp.sum(jnp.exp(s - m), axis=-1, keepdims=True))
    p = jnp.exp(s - lse)
    o = jnp.einsum('bdhnqk,bdhnke->bdhnqe', p.astype(vb.dtype), vb)
    o = o.reshape(bsz, dilation, nh, sub_pad, hd)[:, :, :, :sub_len]
    o = o.transpose(0, 3, 1, 2, 4).reshape(bsz, seq, nh, hd)
    lse = lse[..., 0].reshape(bsz, dilation, nh, sub_pad)[:, :, :, :sub_len]
    lse = lse.transpose(0, 3, 1, 2).reshape(bsz, seq, nh)
    return o, lse


def gqa_block_sweep(q, k, v):
    bsz, seq = q.shape[0], q.shape[1]
    nqb = seq // B_QBLOCK
    qb = q.reshape(bsz, nqb, B_QBLOCK, B_KV_HEADS, B_GROUP, HEAD_DIM).transpose(1, 0, 3, 4, 2, 5)
    kt = k.transpose(0, 2, 1, 3)
    vt = v.transpose(0, 2, 1, 3)

    def one_block(qblk):
        s = jnp.einsum('bhgqe,bhke->bhgqk', qblk, kt, preferred_element_type=jnp.float32) * (HEAD_DIM ** -0.5)
        p = jax.nn.softmax(s, axis=-1)
        return jnp.einsum('bhgqk,bhke->bhgqe', p.astype(vt.dtype), vt)

    o = lax.map(one_block, qb)
    return o.transpose(1, 0, 4, 2, 3, 5).reshape(bsz, seq, B_Q)


def hybrid_mixer(x, w_in, q_norm_g, k_norm_g, w_branch_a, w_branch_b, w_out):
    bsz, seq, _ = x.shape
    proj = x @ w_in
    cuts = [A_QKV, 2 * A_QKV, 3 * A_QKV, 3 * A_QKV + B_Q, 3 * A_QKV + B_Q + B_KV, 3 * A_QKV + B_Q + 2 * B_KV]
    qa, ka, va, qb, kb, vb, gate_logits = jnp.split(proj, cuts, axis=-1)

    ang1 = rope_1d_angles(seq)
    qa = apply_rope(qa.reshape(bsz, seq, N_DIL * A_HEADS, HEAD_DIM), ang1).reshape(bsz, seq, N_DIL, A_HEADS, HEAD_DIM)
    ka = apply_rope(ka.reshape(bsz, seq, N_DIL * A_HEADS, HEAD_DIM), ang1).reshape(bsz, seq, N_DIL, A_HEADS, HEAD_DIM)
    va = va.reshape(bsz, seq, N_DIL, A_HEADS, HEAD_DIM)
    outs, lses = [], []
    for gi, (window, dilation) in enumerate(DIL_PATTERNS):
        o, l = dilated_window_attention(qa[:, :, gi], ka[:, :, gi], va[:, :, gi], window, dilation)
        outs.append(o)
        lses.append(l)
    wts = jax.nn.softmax(jnp.stack(lses, axis=0), axis=0)
    o_a = jnp.sum(wts[..., None] * jnp.stack(outs, axis=0).astype(jnp.float32), axis=0)
    o_a = o_a.astype(x.dtype).reshape(bsz, seq, A_OUT)

    ang2 = rope_axial_angles(seq)
    qb = apply_rope(rms_norm(qb.reshape(bsz, seq, B_Q_HEADS, HEAD_DIM), q_norm_g), ang2)
    kb = apply_rope(rms_norm(kb.reshape(bsz, seq, B_KV_HEADS, HEAD_DIM), k_norm_g), ang2)
    vb = vb.reshape(bsz, seq, B_KV_HEADS, HEAD_DIM)
    o_b = gqa_block_sweep(qb, kb, vb)

    gates = jax.nn.sigmoid(gate_logits.astype(jnp.float32)).reshape(bsz, seq, N_BRANCH, D_MODEL)
    y_a = (o_a @ w_branch_a).astype(jnp.float32)
    y_b = (o_b @ w_branch_b).astype(jnp.float32)
    merged = (gates[:, :, 0] * y_a + gates[:, :, 1] * y_b).astype(x.dtype)
    return merged @ w_out


def hierarchical_moe(h, w_group, b_group, w_router, b_router, w_gate, w_up, w_down):
    bsz, seq, dm = h.shape
    xt = h.reshape(-1, dm)
    n_tok = xt.shape[0]
    g_logits = (xt @ w_group).astype(jnp.float32) + b_group.astype(jnp.float32)
    g_prob = jax.nn.softmax(g_logits, axis=-1)
    g_idx = jnp.argmax(g_logits, axis=-1)
    g_gate = jnp.take_along_axis(g_prob, g_idx[:, None], axis=1)[:, 0]
    e_logits = (xt @ w_router).astype(jnp.float32) + b_router.astype(jnp.float32)
    e_logits = e_logits.reshape(n_tok, N_GROUPS, EXPERTS_PER_GROUP)
    e_logits = jnp.take_along_axis(e_logits, g_idx[:, None, None], axis=1)[:, 0]
    e_prob = jax.nn.softmax(e_logits, axis=-1)
    top_p, top_i = lax.top_k(e_prob, TOP_K)
    top_p = top_p / jnp.sum(top_p, axis=-1, keepdims=True)
    weights = g_gate[:, None] * top_p
    expert = g_idx[:, None] * EXPERTS_PER_GROUP + top_i

    n_slot = n_tok * TOP_K
    e_flat = expert.reshape(n_slot).astype(jnp.int32)
    t_flat = jnp.repeat(jnp.arange(n_tok, dtype=jnp.int32), TOP_K)
    w_flat = weights.reshape(n_slot)
    order = jnp.argsort(e_flat)
    e_sorted = e_flat[order]
    t_sorted = t_flat[order]
    w_sorted = w_flat[order]
    counts = jnp.bincount(e_flat, length=N_EXPERTS).astype(jnp.int32)
    starts = jnp.cumsum(counts) - counts
    padded = ((counts + MOE_BLOCK - 1) // MOE_BLOCK) * MOE_BLOCK
    p_ends = jnp.cumsum(padded)
    p_starts = p_ends - padded
    dest = p_starts[e_sorted] + (jnp.arange(n_slot, dtype=jnp.int32) - starts[e_sorted])
    buf_len = n_slot + N_EXPERTS * MOE_BLOCK
    n_blocks = buf_len // MOE_BLOCK
    tok_buf = jnp.zeros((buf_len,), jnp.int32).at[dest].set(t_sorted)
    w_buf = jnp.zeros((buf_len,), jnp.float32).at[dest].set(w_sorted)
    x_buf = xt[tok_buf].reshape(n_blocks, MOE_BLOCK, dm)
    block_start = jnp.arange(n_blocks, dtype=jnp.int32) * MOE_BLOCK
    block_expert = jnp.clip(jnp.searchsorted(p_ends, block_start, side='right'), 0, N_EXPERTS - 1)

    def expert_block(args):
        xb, e = args
        a = xb @ w_gate[e]
        u = xb @ w_up[e]
        return (jax.nn.silu(a) * u) @ w_down[e]

    y_buf = lax.map(expert_block, (x_buf, block_expert)).reshape(buf_len, dm)
    out = jnp.zeros((n_tok, dm), jnp.float32).at[tok_buf].add(y_buf.astype(jnp.float32) * w_buf[:, None])
    return out.astype(h.dtype).reshape(bsz, seq, dm)


def setup_inputs(seed: int = 0) -> dict:
    key = jax.random.key(seed)
    ks = jax.random.split(key, 18)

    def nrm(k, shape, scale):
        return jax.random.normal(k, shape, jnp.float32) * scale

    return {
        'x': nrm(ks[0], (BATCH, SEQ, D_MODEL), 1.0),
        'w_in': nrm(ks[1], (DEPTH, D_MODEL, IN_COLS), D_MODEL ** -0.5),
        'q_norm_g': 1.0 + nrm(ks[2], (DEPTH, HEAD_DIM), 0.02),
        'k_norm_g': 1.0 + nrm(ks[3], (DEPTH, HEAD_DIM), 0.02),
        'w_branch_a': nrm(ks[4], (DEPTH, A_OUT, D_MODEL), A_OUT ** -0.5),
        'w_branch_b': nrm(ks[5], (DEPTH, B_Q, D_MODEL), B_Q ** -0.5),
        'w_out': nrm(ks[6], (DEPTH, D_MODEL, D_MODEL), D_MODEL ** -0.5 * DN_BETA),
        'ln1_g': 1.0 + nrm(ks[7], (DEPTH, D_MODEL), 0.02),
        'ln1_b': nrm(ks[8], (DEPTH, D_MODEL), 0.02),
        'w_group': nrm(ks[9], (DEPTH, D_MODEL, N_GROUPS), D_MODEL ** -0.5),
        'b_group': nrm(ks[10], (DEPTH, N_GROUPS), 0.01),
        'w_router': nrm(ks[11], (DEPTH, D_MODEL, N_EXPERTS), D_MODEL ** -0.5),
        'b_router': nrm(ks[12], (DEPTH, N_EXPERTS), 0.01),
        'w_gate': nrm(ks[13], (DEPTH, N_EXPERTS, D_MODEL, EXPERT_FF), D_MODEL ** -0.5),
        'w_up': nrm(ks[14], (DEPTH, N_EXPERTS, D_MODEL, EXPERT_FF), D_MODEL ** -0.5),
        'w_down': nrm(ks[15], (DEPTH, N_EXPERTS, EXPERT_FF, D_MODEL), EXPERT_FF ** -0.5 * DN_BETA),
        'ln2_g': 1.0 + nrm(ks[16], (DEPTH, D_MODEL), 0.02),
        'ln2_b': nrm(ks[17], (DEPTH, D_MODEL), 0.02),
    }


def reference(x, w_in, q_norm_g, k_norm_g, w_branch_a, w_branch_b, w_out, ln1_g, ln1_b, w_group, b_group, w_router, b_router, w_gate, w_up, w_down, ln2_g, ln2_b):
    h = x
    for layer in range(DEPTH):
        mix = hybrid_mixer(h, w_in[layer], q_norm_g[layer], k_norm_g[layer], w_branch_a[layer], w_branch_b[layer], w_out[layer])
        h = layer_norm(DN_ALPHA * h + mix, ln1_g[layer], ln1_b[layer])
        ffn = hierarchical_moe(h, w_group[layer], b_group[layer], w_router[layer], b_router[layer], w_gate[layer], w_up[layer], w_down[layer])
        h = layer_norm(DN_ALPHA * h + ffn, ln2_g[layer], ln2_b[layer])
    return h
```

```python
import functools
import math

import jax
import jax.numpy as jnp
from jax import lax
from jax.experimental import pallas as pl
from jax.experimental.pallas import tpu as pltpu

F32 = jnp.float32
BF16 = jnp.bfloat16
I32 = jnp.int32

HEAD_DIM = 128
ROPE_THETA = 10000.0
GRID_W = 64
DIL_PATTERNS = ((128, 1), (512, 4), (2048, 16))
N_DIL = len(DIL_PATTERNS)
A_HEADS = 8
BAND_HALF = 64
B_Q_HEADS = 16
B_KV_HEADS = 4
B_GROUP = B_Q_HEADS // B_KV_HEADS
N_GROUPS = 4
EXPERTS_PER_GROUP = 8
N_EXPERTS = N_GROUPS * EXPERTS_PER_GROUP
TOP_K = 2
MOE_BLOCK = 128
LN_EPS = 1e-5
RMS_EPS = 1e-6
NEG_BIG = -1e30
LOG2E = math.log2(math.e)

LANES = 128
COL_TILE = 8 * HEAD_DIM
ROW_TILE = 512
ITEM_ROWS = 1024
ITEM_BLOCKS = ITEM_ROWS // MOE_BLOCK
FF_TILE = 256
VMEM_LIMIT = 56 * 1024 * 1024


def _params(sem, vmem=VMEM_LIMIT):
    return pltpu.CompilerParams(dimension_semantics=sem, vmem_limit_bytes=vmem)


_T_QA, _T_KA, _T_VA, _T_QB, _T_KVB, _T_GATE = 0, 3, 6, 9, 11, 12
_R1Q, _R1K, _R2Q, _R2K = 0, 2, 4, 6


def _in_proj_kernel(x_ref, w_ref, tab_ref, gq_ref, gk_ref, out_ref, wbf_ref):
    j = pl.program_id(0)
    i = pl.program_id(1)

    @pl.when(i == 0)
    def _cast_weights():
        wbf_ref[...] = w_ref[...].astype(BF16)

    acc = jnp.dot(x_ref[...], wbf_ref[...], preferred_element_type=F32)
    heads = COL_TILE // HEAD_DIM

    def head(h):
        return acc[:, h * HEAD_DIM:(h + 1) * HEAD_DIM]

    def rope(a, t):
        return a * tab_ref[t] + pltpu.roll(a, HEAD_DIM // 2, 1) * tab_ref[t + 1]

    def rms(a, g_ref):
        ms = jnp.mean(a * a, axis=-1, keepdims=True)
        return a * lax.rsqrt(ms + RMS_EPS) * g_ref[...]

    @pl.when(j < _T_KA)
    def _qa():
        for h in range(heads):
            out_ref[h] = rope(head(h), _R1Q).astype(BF16)

    @pl.when((j >= _T_KA) & (j < _T_VA))
    def _ka():
        for h in range(heads):
            out_ref[h] = rope(head(h), _R1K).astype(BF16)

    @pl.when((j >= _T_VA) & (j < _T_QB))
    def _va():
        for h in range(heads):
            out_ref[h] = head(h).astype(BF16)

    @pl.when((j >= _T_QB) & (j < _T_KVB))
    def _qb():
        for h in range(heads):
            out_ref[h] = rope(rms(head(h), gq_ref), _R2Q).astype(BF16)

    @pl.when(j == _T_KVB)
    def _kvb():
        for h in range(heads):
            if h < B_KV_HEADS:
                out_ref[h] = rope(rms(head(h), gk_ref), _R2K).astype(BF16)
            else:
                out_ref[h] = head(h).astype(BF16)

    @pl.when(j >= _T_GATE)
    def _gates():
        for h in range(heads):
            out_ref[h] = (1.0 / (1.0 + jnp.exp(-head(h)))).astype(BF16)


def _in_proj(xb, w_in, tabs, gq, gk, seq):
    n_tok, d_model = xb.shape
    in_cols = w_in.shape[1]
    n_ct = in_cols // COL_TILE
    tm = ROW_TILE
    heads = COL_TILE // HEAD_DIM
    seq_blocks = seq // tm
    return pl.pallas_call(
        _in_proj_kernel,
        out_shape=jax.ShapeDtypeStruct((in_cols // HEAD_DIM, n_tok, HEAD_DIM), BF16),
        grid=(n_ct, n_tok // tm),
        in_specs=[
            pl.BlockSpec((tm, d_model), lambda j, i: (i, 0)),
            pl.BlockSpec((d_model, COL_TILE), lambda j, i: (0, j)),
            pl.BlockSpec((8, tm, HEAD_DIM), lambda j, i: (0, i % seq_blocks, 0)),
            pl.BlockSpec((1, HEAD_DIM), lambda j, i: (0, 0)),
            pl.BlockSpec((1, HEAD_DIM), lambda j, i: (0, 0)),
        ],
        out_specs=pl.BlockSpec((heads, tm, HEAD_DIM), lambda j, i: (j, i, 0)),
        scratch_shapes=[pltpu.VMEM((d_model, COL_TILE), BF16)],
        compiler_params=_params(("arbitrary", "arbitrary")),
    )(xb, w_in, tabs, gq, gk)


_SUB = 128
_WIN = _SUB + 2 * BAND_HALF


def _attn_a_kernel(q_ref, k_ref, v_ref, o_ref, lse_ref, *, sub_len):
    def body(i, carry):
        r0 = pl.multiple_of(i * _SUB, _SUB)
        start = pl.multiple_of(jnp.clip(i * _SUB - BAND_HALF, 0, sub_len - _WIN), BAND_HALF)
        q = q_ref[pl.ds(r0, _SUB), :]
        k = k_ref[pl.ds(start, _WIN), :]
        v = v_ref[pl.ds(start, _WIN), :]
        s = lax.dot_general(q, k, (((1,), (1,)), ((), ())), preferred_element_type=F32)
        qpos = r0 + lax.broadcasted_iota(I32, (_SUB, _WIN), 0)
        kpos = start + lax.broadcasted_iota(I32, (_SUB, _WIN), 1)
        s = jnp.where(jnp.abs(kpos - qpos) <= BAND_HALF, s, NEG_BIG)
        m = jnp.max(s, axis=1, keepdims=True)
        p = jnp.exp2(s - m)
        l = jnp.sum(p, axis=1, keepdims=True)
        o = jnp.dot(p.astype(BF16), v, preferred_element_type=F32) / l
        o_ref[pl.ds(r0, _SUB), :] = o.astype(o_ref.dtype)
        lse_ref[pl.ds(r0, _SUB), :] = jnp.broadcast_to(m + jnp.log2(l), (_SUB, HEAD_DIM))
        return carry

    lax.fori_loop(0, sub_len // _SUB, body, 0)


def _attn_a(slots, group, batch, seq):
    dil = DIL_PATTERNS[group][1]
    sub_len = seq // dil
    n_slots = slots.shape[0]
    view = slots.reshape(n_slots, batch, sub_len, dil * HEAD_DIM)
    q0 = (_T_QA + group) * A_HEADS
    k0 = (_T_KA + group) * A_HEADS
    v0 = (_T_VA + group) * A_HEADS

    def in_spec(first):
        return pl.BlockSpec((None, None, sub_len, HEAD_DIM), lambda b, r, h: (first + h, b, 0, r))

    out_spec = pl.BlockSpec((None, sub_len, HEAD_DIM), lambda b, r, h: (b, 0, r * A_HEADS + h))
    width = dil * A_HEADS * HEAD_DIM
    o, lse = pl.pallas_call(
        functools.partial(_attn_a_kernel, sub_len=sub_len),
        out_shape=(jax.ShapeDtypeStruct((batch, sub_len, width), BF16),
                   jax.ShapeDtypeStruct((batch, sub_len, width), F32)),
        grid=(batch, dil, A_HEADS),
        in_specs=[in_spec(q0), in_spec(k0), in_spec(v0)],
        out_specs=(out_spec, out_spec),
        compiler_params=_params(("arbitrary", "arbitrary", "arbitrary")),
    )(view, view, view)
    n_tok = batch * seq
    return o.reshape(n_tok, A_HEADS * HEAD_DIM), lse.reshape(n_tok, A_HEADS * HEAD_DIM)


_BQ = 256
_BK = 512


def _attn_b_kernel(q_ref, k_ref, v_ref, o_ref, *, seq):
    rows = B_GROUP * _BQ
    q = q_ref[...].reshape(rows, HEAD_DIM)

    def body(c, carry):
        m, l, acc = carry
        c0 = pl.multiple_of(c * _BK, _BK)
        k = k_ref[pl.ds(c0, _BK), :]
        v = v_ref[pl.ds(c0, _BK), :]
        s = lax.dot_general(q, k, (((1,), (1,)), ((), ())), preferred_element_type=F32)
        m_new = jnp.maximum(m, jnp.max(s, axis=1, keepdims=True))
        alpha = jnp.exp2(m - m_new)
        p = jnp.exp2(s - m_new)
        l = alpha * l + jnp.sum(p, axis=1, keepdims=True)
        acc = alpha * acc + jnp.dot(p.astype(BF16), v, preferred_element_type=F32)
        return m_new, l, acc

    init = (jnp.full((rows, 1), NEG_BIG, F32), jnp.zeros((rows, 1), F32), jnp.zeros((rows, HEAD_DIM), F32))
    _, l, acc = lax.fori_loop(0, seq // _BK, body, init)
    o = acc / l
    for g in range(B_GROUP):
        o_ref[:, g * HEAD_DIM:(g + 1) * HEAD_DIM] = o[g * _BQ:(g + 1) * _BQ].astype(o_ref.dtype)


def _attn_b(slots, batch, seq):
    n_tok = batch * seq
    q0 = _T_QB * A_HEADS
    k0 = _T_KVB * A_HEADS
    v0 = k0 + B_KV_HEADS
    qblocks = seq // _BQ
    return pl.pallas_call(
        functools.partial(_attn_b_kernel, seq=seq),
        out_shape=jax.ShapeDtypeStruct((n_tok, B_Q_HEADS * HEAD_DIM), BF16),
        grid=(batch, B_KV_HEADS, qblocks),
        in_specs=[
            pl.BlockSpec((B_GROUP, _BQ, HEAD_DIM), lambda b, kv, qi: (q0 // B_GROUP + kv, b * qblocks + qi, 0)),
            pl.BlockSpec((None, seq, HEAD_DIM), lambda b, kv, qi: (k0 + kv, b, 0)),
            pl.BlockSpec((None, seq, HEAD_DIM), lambda b, kv, qi: (v0 + kv, b, 0)),
        ],
        out_specs=pl.BlockSpec((_BQ, B_GROUP * HEAD_DIM), lambda b, kv, qi: (b * qblocks + qi, kv)),
        compiler_params=_params(("arbitrary", "arbitrary", "arbitrary")),
    )(slots, slots, slots)


_MIX_ROWS = 256


def _branch_kernel(oa0, oa1, oa2, ls0, ls1, ls2, ob_ref, ga_ref, gb_ref, wa_ref, wb_ref, out_ref):
    l0, l1, l2 = ls0[...], ls1[...], ls2[...]
    mx = jnp.maximum(jnp.maximum(l0, l1), l2)
    w0, w1, w2 = jnp.exp2(l0 - mx), jnp.exp2(l1 - mx), jnp.exp2(l2 - mx)
    o_a = (w0 * oa0[...].astype(F32) + w1 * oa1[...].astype(F32) + w2 * oa2[...].astype(F32)) / (w0 + w1 + w2)
    y_a = jnp.dot(o_a.astype(BF16), wa_ref[...], preferred_element_type=F32)
    y_b = jnp.dot(ob_ref[...], wb_ref[...], preferred_element_type=F32)
    n_g = ga_ref.shape[0]
    for c in range(n_g):
        cols = slice(c * LANES, (c + 1) * LANES)
        out_ref[:, cols] = (ga_ref[c].astype(F32) * y_a[:, cols] + gb_ref[c].astype(F32) * y_b[:, cols]).astype(BF16)


def _branch_mix(oas, lses, o_b, slots, wa, wb):
    n_tok = o_b.shape[0]
    d_model = wa.shape[1]
    tm = _MIX_ROWS
    g_slots = d_model // LANES
    ga0 = _T_GATE * A_HEADS // g_slots
    row = lambda i: (i, 0)
    a_spec = pl.BlockSpec((tm, A_HEADS * HEAD_DIM), row)
    return pl.pallas_call(
        _branch_kernel,
        out_shape=jax.ShapeDtypeStruct((n_tok, d_model), BF16),
        grid=(n_tok // tm,),
        in_specs=[a_spec, a_spec, a_spec, a_spec, a_spec, a_spec,
                  pl.BlockSpec((tm, o_b.shape[1]), row),
                  pl.BlockSpec((g_slots, tm, HEAD_DIM), lambda i: (ga0, i, 0)),
                  pl.BlockSpec((g_slots, tm, HEAD_DIM), lambda i: (ga0 + 1, i, 0)),
                  pl.BlockSpec(wa.shape, lambda i: (0, 0)),
                  pl.BlockSpec(wb.shape, lambda i: (0, 0))],
        out_specs=pl.BlockSpec((tm, d_model), row),
        compiler_params=_params(("arbitrary",)),
    )(*oas, *lses, o_b, slots, slots, wa, wb)


def _layer_norm(z, g, b):
    mu = jnp.mean(z, axis=-1, keepdims=True)
    zc = z - mu
    var = jnp.mean(zc * zc, axis=-1, keepdims=True)
    return zc * lax.rsqrt(var + LN_EPS) * g + b


def _out_proj_kernel(m_ref, w_ref, x_ref, g_ref, b_ref, rhi_ref, rlo_ref, rb_ref, h_ref, lg_ref, *, alpha):
    mix = jnp.dot(m_ref[...], w_ref[...], preferred_element_type=F32)
    h = _layer_norm(alpha * x_ref[...] + mix, g_ref[...], b_ref[...])
    h_ref[...] = h
    hi = h.astype(BF16)
    lo = (h - hi.astype(F32)).astype(BF16)
    lg = jnp.dot(hi, rhi_ref[...], preferred_element_type=F32)
    lg = lg + jnp.dot(lo, rhi_ref[...], preferred_element_type=F32)
    lg = lg + jnp.dot(hi, rlo_ref[...], preferred_element_type=F32)
    lg_ref[...] = lg + rb_ref[...]


def _out_proj(merged, w_out, x2, g, b, r_hi, r_lo, r_b, alpha):
    n_tok, d_model = x2.shape
    tm = _MIX_ROWS
    row = lambda i: (i, 0)
    fix = lambda i: (0, 0)
    return pl.pallas_call(
        functools.partial(_out_proj_kernel, alpha=alpha),
        out_shape=(jax.ShapeDtypeStruct((n_tok, d_model), F32),
                   jax.ShapeDtypeStruct((n_tok, LANES), F32)),
        grid=(n_tok // tm,),
        in_specs=[pl.BlockSpec((tm, d_model), row),
                  pl.BlockSpec(w_out.shape, fix),
                  pl.BlockSpec((tm, d_model), row),
                  pl.BlockSpec((1, d_model), fix),
                  pl.BlockSpec((1, d_model), fix),
                  pl.BlockSpec(r_hi.shape, fix),
                  pl.BlockSpec(r_lo.shape, fix),
                  pl.BlockSpec((1, LANES), fix)],
        out_specs=(pl.BlockSpec((tm, d_model), row), pl.BlockSpec((tm, LANES), row)),
        compiler_params=_params(("arbitrary",)),
    )(merged, w_out, x2, g, b, r_hi, r_lo, r_b)


def _moe_kernel(item_expert, item_row0, item_blocks, item_real, src_tok, dst_row,
                h_hbm, wg_ref, wu_ref, wd_ref, y_hbm,
                stage, xbf, acc, wgb, wub, wdb, gsem, ssem):
    it = pl.program_id(0)
    f = pl.program_id(1)
    n_items = pl.num_programs(0)
    n_ff = pl.num_programs(1)
    n_blocks = item_blocks[it]

    def row_gather(item, r):
        tok = src_tok[item_row0[item] + r]
        return pltpu.make_async_copy(h_hbm.at[pl.ds(tok, 1)], stage.at[pl.ds(r, 1)], gsem.at[0])

    def row_scatter(item, r):
        dst = dst_row[item_row0[item] + r]
        return pltpu.make_async_copy(acc.at[pl.ds(r, 1)], y_hbm.at[pl.ds(dst, 1)], ssem.at[0])

    def for_rows(n_rows, fn):
        def body(r, c):
            fn(r)
            return c
        lax.fori_loop(0, n_rows, body, 0)

    def padded_rows(item):
        return item_blocks[item] * MOE_BLOCK

    def for_blocks(fn):
        def body(b, c):
            fn(pl.multiple_of(b * MOE_BLOCK, MOE_BLOCK))
            return c
        lax.fori_loop(0, n_blocks, body, 0)

    @pl.when((it == 0) & (f == 0))
    def _first_gather():
        for_rows(padded_rows(it), lambda r: row_gather(it, r).start())

    @pl.when(f == 0)
    def _gathered():
        for_rows(padded_rows(it), lambda r: row_gather(it, r).wait())

        def cast_rows(r0):
            xbf[pl.ds(r0, MOE_BLOCK), :] = stage[pl.ds(r0, MOE_BLOCK), :].astype(BF16)
        for_blocks(cast_rows)

    @pl.when((f == n_ff - 1) & (it + 1 < n_items))
    def _prefetch_next():
        for_rows(padded_rows(it + 1), lambda r: row_gather(it + 1, r).start())

    @pl.when(n_blocks > 0)
    def _compute():
        wgb[...] = wg_ref[...].astype(BF16)
        wub[...] = wu_ref[...].astype(BF16)
        wdb[...] = wd_ref[...].astype(BF16)

        def block(r0):
            xb = xbf[pl.ds(r0, MOE_BLOCK), :]
            a = jnp.dot(xb, wgb[...], preferred_element_type=F32)
            u = jnp.dot(xb, wub[...], preferred_element_type=F32)
            hid = (a / (1.0 + jnp.exp(-a)) * u).astype(BF16)
            y = jnp.dot(hid, wdb[...], preferred_element_type=F32)

            @pl.when(f == 0)
            def _set():
                acc[pl.ds(r0, MOE_BLOCK), :] = y

            @pl.when(f > 0)
            def _add():
                acc[pl.ds(r0, MOE_BLOCK), :] += y

        for_blocks(block)

    @pl.when(f == n_ff - 1)
    def _emit():
        for_rows(item_real[it], lambda r: row_scatter(it, r).start())
        for_rows(item_real[it], lambda r: row_scatter(it, r).wait())


def _moe(h, w_gate, w_up, w_down, item_expert, item_row0, item_blocks, item_real, src_tok, dst_row):
    n_tok, d_model = h.shape
    ff = w_gate.shape[2]
    n_ff = ff // FF_TILE
    n_items = item_expert.shape[0]

    def ff_idx(it, f, blocks):
        return jnp.where(blocks[it] > 0, f, n_ff - 1)

    grid_spec = pltpu.PrefetchScalarGridSpec(
        num_scalar_prefetch=6,
        grid=(n_items, n_ff),
        in_specs=[
            pl.BlockSpec(memory_space=pl.ANY),
            pl.BlockSpec((None, d_model, FF_TILE), lambda it, f, ie, ir, ib, nr, st, dr: (ie[it], 0, ff_idx(it, f, ib))),
            pl.BlockSpec((None, d_model, FF_TILE), lambda it, f, ie, ir, ib, nr, st, dr: (ie[it], 0, ff_idx(it, f, ib))),
            pl.BlockSpec((None, FF_TILE, d_model), lambda it, f, ie, ir, ib, nr, st, dr: (ie[it], ff_idx(it, f, ib), 0)),
        ],
        out_specs=pl.BlockSpec(memory_space=pl.ANY),
        scratch_shapes=[
            pltpu.VMEM((ITEM_ROWS, d_model), F32),
            pltpu.VMEM((ITEM_ROWS, d_model), BF16),
            pltpu.VMEM((ITEM_ROWS, d_model), F32),
            pltpu.VMEM((d_model, FF_TILE), BF16),
            pltpu.VMEM((d_model, FF_TILE), BF16),
            pltpu.VMEM((FF_TILE, d_model), BF16),
            pltpu.SemaphoreType.DMA((1,)),
            pltpu.SemaphoreType.DMA((1,)),
        ],
    )
    return pl.pallas_call(
        _moe_kernel,
        out_shape=jax.ShapeDtypeStruct((n_tok * TOP_K, d_model), F32),
        grid_spec=grid_spec,
        compiler_params=_params(("arbitrary", "arbitrary")),
    )(item_expert, item_row0, item_blocks, item_real, src_tok, dst_row, h, w_gate, w_up, w_down)


def _combine_kernel(y_ref, wt_ref, h_ref, g_ref, b_ref, out_ref, *, alpha):
    d_model = h_ref.shape[1]
    wt = wt_ref[...]
    ffn = wt[:, 0:1] * y_ref[:, :d_model].astype(F32) + wt[:, 1:2] * y_ref[:, d_model:].astype(F32)
    out_ref[...] = _layer_norm(alpha * h_ref[...] + ffn, g_ref[...], b_ref[...])


def _combine(y_slots, weights, h, g, b, alpha):
    n_tok, d_model = h.shape
    tm = ROW_TILE
    pairs = y_slots.reshape(y_slots.shape[0] // TOP_K, TOP_K * d_model)
    row = lambda i: (i, 0)
    fix = lambda i: (0, 0)
    return pl.pallas_call(
        functools.partial(_combine_kernel, alpha=alpha),
        out_shape=jax.ShapeDtypeStruct((n_tok, d_model), F32),
        grid=(n_tok // tm,),
        in_specs=[pl.BlockSpec((tm, TOP_K * d_model), row),
                  pl.BlockSpec((tm, TOP_K), row),
                  pl.BlockSpec((tm, d_model), row),
                  pl.BlockSpec((1, d_model), fix),
                  pl.BlockSpec((1, d_model), fix)],
        out_specs=pl.BlockSpec((tm, d_model), row),
        compiler_params=_params(("arbitrary",)),
    )(pairs, weights, h, g, b)


def _rope_tables(seq):
    half = HEAD_DIM // 2
    inv1 = ROPE_THETA ** (-jnp.arange(half, dtype=F32) / half)
    ang1 = jnp.arange(seq, dtype=F32)[:, None] * inv1[None, :]
    rows = seq // GRID_W
    r, c = jnp.meshgrid(jnp.arange(rows, dtype=F32), jnp.arange(GRID_W, dtype=F32), indexing='ij')
    n_axis = HEAD_DIM // 4
    inv2 = ROPE_THETA ** (-jnp.arange(n_axis, dtype=F32) / n_axis)
    ang2 = jnp.concatenate([r.reshape(-1, 1) * inv2[None, :], c.reshape(-1, 1) * inv2[None, :]], axis=-1)
    q_scale = HEAD_DIM ** -0.5 * LOG2E
    out = []
    for ang in (ang1, ang2):
        cos = jnp.concatenate([jnp.cos(ang), jnp.cos(ang)], axis=-1)
        sin = jnp.concatenate([-jnp.sin(ang), jnp.sin(ang)], axis=-1)
        out += [cos * q_scale, sin * q_scale, cos, sin]
    return jnp.stack(out, axis=0)


def _route(logits, n_tok):
    g_logits = logits[:, :N_GROUPS]
    g_prob = jax.nn.softmax(g_logits, axis=-1)
    g_idx = jnp.argmax(g_logits, axis=-1)
    g_gate = jnp.take_along_axis(g_prob, g_idx[:, None], axis=1)[:, 0]
    e_logits = logits[:, N_GROUPS:N_GROUPS + N_EXPERTS].reshape(n_tok, N_GROUPS, EXPERTS_PER_GROUP)
    e_logits = jnp.take_along_axis(e_logits, g_idx[:, None, None], axis=1)[:, 0]
    e_prob = jax.nn.softmax(e_logits, axis=-1)
    top_p, top_i = lax.top_k(e_prob, TOP_K)
    top_p = top_p / jnp.sum(top_p, axis=-1, keepdims=True)
    weights = g_gate[:, None] * top_p
    expert = g_idx[:, None] * EXPERTS_PER_GROUP + top_i

    n_slot = n_tok * TOP_K
    e_flat = expert.reshape(n_slot).astype(I32)
    order = jnp.argsort(e_flat).astype(I32)
    e_sorted = e_flat[order]
    counts = jnp.bincount(e_flat, length=N_EXPERTS).astype(I32)
    starts = jnp.cumsum(counts) - counts
    padded = ((counts + MOE_BLOCK - 1) // MOE_BLOCK) * MOE_BLOCK
    p_ends = jnp.cumsum(padded)
    p_starts = p_ends - padded
    dest = p_starts[e_sorted] + (jnp.arange(n_slot, dtype=I32) - starts[e_sorted])
    buf_len = n_slot + N_EXPERTS * MOE_BLOCK
    src_tok = jnp.zeros((buf_len,), I32).at[dest].set(order // TOP_K)
    dst_row = jnp.zeros((buf_len,), I32).at[dest].set(order)

    n_items = N_EXPERTS + n_slot // ITEM_ROWS
    per_expert = (padded + ITEM_ROWS - 1) // ITEM_ROWS
    item_ends = jnp.cumsum(per_expert)
    total = item_ends[-1]
    ids = jnp.arange(n_items, dtype=I32)
    last = jnp.maximum(total - 1, 0)
    live = ids < total
    e_of = jnp.clip(jnp.searchsorted(item_ends, jnp.minimum(ids, last), side='right'), 0, N_EXPERTS - 1).astype(I32)
    chunk = jnp.minimum(ids, last) - (item_ends[e_of] - per_expert[e_of])
    item_row0 = jnp.where(live, p_starts[e_of] + chunk * ITEM_ROWS, 0).astype(I32)
    item_blocks = jnp.where(live, jnp.clip(padded[e_of] // MOE_BLOCK - chunk * ITEM_BLOCKS, 0, ITEM_BLOCKS), 0).astype(I32)
    item_real = jnp.where(live, jnp.clip(counts[e_of] - chunk * ITEM_ROWS, 0, ITEM_ROWS), 0).astype(I32)
    return weights, e_of, item_row0, item_blocks, item_real, src_tok, dst_row


def kernel(x, w_in, q_norm_g, k_norm_g, w_branch_a, w_branch_b, w_out, ln1_g, ln1_b, w_group, b_group, w_router,
           b_router, w_gate, w_up, w_down, ln2_g, ln2_b):
    batch, seq, d_model = x.shape
    depth = w_in.shape[0]
    n_tok = batch * seq
    dn_alpha = (2 * depth) ** 0.25
    tabs = _rope_tables(seq)
    h = x.reshape(n_tok, d_model)
    for layer in range(depth):
        slots = _in_proj(h.astype(BF16), w_in[layer], tabs, q_norm_g[layer][None, :], k_norm_g[layer][None, :], seq)
        groups = [_attn_a(slots, g, batch, seq) for g in range(N_DIL)]
        o_b = _attn_b(slots, batch, seq)
        merged = _branch_mix([o for o, _ in groups], [l for _, l in groups], o_b, slots,
                             w_branch_a[layer].astype(BF16), w_branch_b[layer].astype(BF16))
        w_r = jnp.concatenate([w_group[layer], w_router[layer]], axis=1)
        w_r = jnp.pad(w_r, ((0, 0), (0, LANES - w_r.shape[1])))
        r_hi = w_r.astype(BF16)
        r_lo = (w_r - r_hi.astype(F32)).astype(BF16)
        r_b = jnp.pad(jnp.concatenate([b_group[layer], b_router[layer]]), (0, LANES - N_GROUPS - N_EXPERTS))[None, :]
        h1, logits = _out_proj(merged, w_out[layer].astype(BF16), h, ln1_g[layer][None, :], ln1_b[layer][None, :],
                                    r_hi, r_lo, r_b, dn_alpha)
        weights, item_expert, item_row0, item_blocks, item_real, src_tok, dst_row = _route(logits, n_tok)
        y_slots = _moe(h1, w_gate[layer], w_up[layer], w_down[layer], item_expert, item_row0, item_blocks,
                       item_real, src_tok, dst_row)
        h = _combine(y_slots, weights, h1, ln2_g[layer][None, :], ln2_b[layer][None, :], dn_alpha)
    return h.reshape(batch, seq, d_model)
```

```python
import functools
import math

import jax
import jax.numpy as jnp
from jax import lax
from jax.experimental import pallas as pl
from jax.experimental.pallas import tpu as pltpu

F32 = jnp.float32
BF16 = jnp.bfloat16
I32 = jnp.int32

HEAD_DIM = 128
ROPE_THETA = 10000.0
GRID_W = 64
DIL_PATTERNS = ((128, 1), (512, 4), (2048, 16))
N_DIL = len(DIL_PATTERNS)
A_HEADS = 8
BAND_HALF = 64
B_Q_HEADS = 16
B_KV_HEADS = 4
B_GROUP = B_Q_HEADS // B_KV_HEADS
N_GROUPS = 4
EXPERTS_PER_GROUP = 8
N_EXPERTS = N_GROUPS * EXPERTS_PER_GROUP
TOP_K = 2
MOE_BLOCK = 128
LN_EPS = 1e-5
RMS_EPS = 1e-6
NEG_BIG = -1e30
LOG2E = math.log2(math.e)

LANES = 128
COL_TILE = 8 * HEAD_DIM
ROW_TILE = 512
ITEM_ROWS = 1024
ITEM_BLOCKS = ITEM_ROWS // MOE_BLOCK
FF_TILE = 256
DMA_UNROLL = 8
VMEM_LIMIT = 56 * 1024 * 1024


def _params(sem, vmem=VMEM_LIMIT):
    return pltpu.CompilerParams(dimension_semantics=sem, vmem_limit_bytes=vmem)


_T_QA, _T_KA, _T_VA, _T_QB, _T_KVB, _T_GATE = 0, 3, 6, 9, 11, 12
_R1Q, _R1K, _R2Q, _R2K = 0, 2, 4, 6


def _cast_weights_once(w_ref, wbf_ref):
    @pl.when(pl.program_id(1) == 0)
    def _cast():
        wbf_ref[...] = w_ref[...].astype(BF16)


def _project(x_ref, wbf_ref):
    return jnp.dot(x_ref[...], wbf_ref[...], preferred_element_type=F32)


def _head(acc, h):
    return acc[:, h * HEAD_DIM:(h + 1) * HEAD_DIM]


def _rope(a, tab_ref, t):
    return a * tab_ref[t] + pltpu.roll(a, HEAD_DIM // 2, 1) * tab_ref[t + 1]


def _rms(a, g_ref):
    ms = jnp.mean(a * a, axis=-1, keepdims=True)
    return a * lax.rsqrt(ms + RMS_EPS) * g_ref[...]


def _proj_a_kernel(x_ref, w_ref, tab_ref, out_ref, wbf_ref, *perm, dil):
    j = pl.program_id(0)
    _cast_weights_once(w_ref, wbf_ref)
    rows = x_ref.shape[0]

    def emit(fn):
        acc = _project(x_ref, wbf_ref)
        if dil == 1:
            for h in range(A_HEADS):
                out_ref[h, 0] = fn(_head(acc, h)).astype(BF16)
            return
        perm_ref, = perm
        for h in range(A_HEADS):
            perm_ref[h] = fn(_head(acc, h))
        for h in range(A_HEADS):
            for r in range(dil):
                out_ref[h, r] = perm_ref[h, pl.ds(r, rows // dil, stride=dil), :].astype(BF16)

    @pl.when(j == 0)
    def _q():
        emit(lambda a: _rope(a, tab_ref, _R1Q))

    @pl.when(j == 1)
    def _k():
        emit(lambda a: _rope(a, tab_ref, _R1K))

    @pl.when(j == 2)
    def _v():
        emit(lambda a: a)


def _proj_b_kernel(x_ref, w_ref, tab_ref, gq_ref, gk_ref, out_ref, wbf_ref, *, q_tiles):
    j = pl.program_id(0)
    _cast_weights_once(w_ref, wbf_ref)
    heads = COL_TILE // HEAD_DIM

    @pl.when(j < q_tiles)
    def _qb():
        acc = _project(x_ref, wbf_ref)
        for h in range(heads):
            out_ref[h] = _rope(_rms(_head(acc, h), gq_ref), tab_ref, _R2Q).astype(BF16)

    @pl.when(j == q_tiles)
    def _kvb():
        acc = _project(x_ref, wbf_ref)
        for h in range(heads):
            if h < B_KV_HEADS:
                out_ref[h] = _rope(_rms(_head(acc, h), gk_ref), tab_ref, _R2K).astype(BF16)
            else:
                out_ref[h] = _head(acc, h).astype(BF16)


def _proj_gate_kernel(x_ref, w_ref, out_ref, wbf_ref):
    _cast_weights_once(w_ref, wbf_ref)
    acc = _project(x_ref, wbf_ref)
    out_ref[...] = (1.0 / (1.0 + jnp.exp(-acc))).astype(BF16)


def _proj_specs(d_model, seq, first_tile, tile_step):
    seq_blocks = seq // ROW_TILE
    return [
        pl.BlockSpec((ROW_TILE, d_model), lambda j, i: (i, 0)),
        pl.BlockSpec((d_model, COL_TILE), lambda j, i: (0, first_tile + tile_step * j)),
        pl.BlockSpec((8, ROW_TILE, HEAD_DIM), lambda j, i: (0, i % seq_blocks, 0)),
    ]


def _proj_a(xb, w_in, tabs, group, batch, seq):
    n_tok, d_model = xb.shape
    dil = DIL_PATTERNS[group][1]
    sub_len = seq // dil
    seq_blocks = seq // ROW_TILE
    scratch = [pltpu.VMEM((d_model, COL_TILE), BF16)]
    if dil > 1:
        scratch.append(pltpu.VMEM((A_HEADS, ROW_TILE, HEAD_DIM), F32))
    return pl.pallas_call(
        functools.partial(_proj_a_kernel, dil=dil),
        out_shape=jax.ShapeDtypeStruct((3 * A_HEADS, batch, dil, sub_len, HEAD_DIM), BF16),
        grid=(3, n_tok // ROW_TILE),
        in_specs=_proj_specs(d_model, seq, group, N_DIL),
        out_specs=pl.BlockSpec((A_HEADS, None, dil, ROW_TILE // dil, HEAD_DIM),
                               lambda j, i: (j, i // seq_blocks, 0, i % seq_blocks, 0)),
        scratch_shapes=scratch,
        compiler_params=_params(("arbitrary", "arbitrary")),
    )(xb, w_in, tabs)


def _proj_b(xb, w_in, tabs, gq, gk, seq):
    n_tok, d_model = xb.shape
    q_tiles = B_Q_HEADS * HEAD_DIM // COL_TILE
    heads = COL_TILE // HEAD_DIM
    fix = lambda j, i: (0, 0)
    return pl.pallas_call(
        functools.partial(_proj_b_kernel, q_tiles=q_tiles),
        out_shape=jax.ShapeDtypeStruct(((q_tiles + 1) * heads, n_tok, HEAD_DIM), BF16),
        grid=(q_tiles + 1, n_tok // ROW_TILE),
        in_specs=_proj_specs(d_model, seq, _T_QB, 1) + [pl.BlockSpec((1, HEAD_DIM), fix), pl.BlockSpec((1, HEAD_DIM), fix)],
        out_specs=pl.BlockSpec((heads, ROW_TILE, HEAD_DIM), lambda j, i: (j, i, 0)),
        scratch_shapes=[pltpu.VMEM((d_model, COL_TILE), BF16)],
        compiler_params=_params(("arbitrary", "arbitrary")),
    )(xb, w_in, tabs, gq, gk)


def _proj_gate(xb, w_in):
    n_tok, d_model = xb.shape
    n_ct = w_in.shape[1] // COL_TILE - _T_GATE
    return pl.pallas_call(
        _proj_gate_kernel,
        out_shape=jax.ShapeDtypeStruct((n_tok, n_ct * COL_TILE), BF16),
        grid=(n_ct, n_tok // ROW_TILE),
        in_specs=[pl.BlockSpec((ROW_TILE, d_model), lambda j, i: (i, 0)),
                  pl.BlockSpec((d_model, COL_TILE), lambda j, i: (0, _T_GATE + j))],
        out_specs=pl.BlockSpec((ROW_TILE, COL_TILE), lambda j, i: (i, j)),
        scratch_shapes=[pltpu.VMEM((d_model, COL_TILE), BF16)],
        compiler_params=_params(("arbitrary", "arbitrary")),
    )(xb, w_in)


_SUB = 128
_WIN = _SUB + 2 * BAND_HALF
_MERGE_ROWS = 256


def _attn_a_group(g, dil, seq, q_ref, k_ref, v_ref, og, lg):
    sub_len = seq // dil
    per_seq = sub_len // _SUB
    shift = per_seq.bit_length() - 1

    def body(i, carry):
        r = lax.shift_right_logical(i, shift)
        p0 = pl.multiple_of((i & (per_seq - 1)) * _SUB, _SUB)
        start = pl.multiple_of(jnp.clip(p0 - BAND_HALF, 0, sub_len - _WIN), BAND_HALF)
        q = q_ref[r, pl.ds(p0, _SUB), :]
        k = k_ref[r, pl.ds(start, _WIN), :]
        v = v_ref[r, pl.ds(start, _WIN), :]
        s = lax.dot_general(q, k, (((1,), (1,)), ((), ())), preferred_element_type=F32)
        qpos = p0 + lax.broadcasted_iota(I32, (_SUB, _WIN), 0)
        kpos = start + lax.broadcasted_iota(I32, (_SUB, _WIN), 1)
        s = jnp.where(jnp.abs(kpos - qpos) <= BAND_HALF, s, NEG_BIG)
        m = jnp.max(s, axis=1, keepdims=True)
        p = jnp.exp2(s - m)
        l = jnp.sum(p, axis=1, keepdims=True)
        o = jnp.dot(p.astype(BF16), v, preferred_element_type=F32) / l
        lse = jnp.broadcast_to(m + jnp.log2(l), (_SUB, HEAD_DIM))
        if dil == 1:
            rows = pl.ds(p0, _SUB)
        else:
            rows = pl.ds(p0 * dil + r, _SUB, stride=dil)
        og[g, rows, :] = o
        lg[g, rows, :] = lse
        return carry

    lax.fori_loop(0, seq // _SUB, body, 0, unroll=2)


def _attn_a_kernel(*refs, seq):
    qkv, o_ref, og, lg = refs[:3 * N_DIL], refs[3 * N_DIL], refs[3 * N_DIL + 1], refs[3 * N_DIL + 2]
    for g, (_, dil) in enumerate(DIL_PATTERNS):
        _attn_a_group(g, dil, seq, qkv[3 * g], qkv[3 * g + 1], qkv[3 * g + 2], og, lg)

    def merge(c, carry):
        rows = pl.ds(pl.multiple_of(c * _MERGE_ROWS, _MERGE_ROWS), _MERGE_ROWS)
        ls = [lg[g, rows, :] for g in range(N_DIL)]
        mx = functools.reduce(jnp.maximum, ls)
        ws = [jnp.exp2(l - mx) for l in ls]
        num = functools.reduce(lambda a, b: a + b, [w * og[g, rows, :] for g, w in enumerate(ws)])
        o_ref[rows, :] = (num / functools.reduce(lambda a, b: a + b, ws)).astype(o_ref.dtype)
        return carry

    lax.fori_loop(0, seq // _MERGE_ROWS, merge, 0)


def _attn_a(groups, batch, seq):
    in_specs, operands = [], []
    for g, (_, dil) in enumerate(DIL_PATTERNS):
        for kind in range(3):
            first = kind * A_HEADS
            in_specs.append(pl.BlockSpec((None, None, dil, seq // dil, HEAD_DIM),
                                         lambda b, h, first=first: (first + h, b, 0, 0, 0)))
            operands.append(groups[g])
    return pl.pallas_call(
        functools.partial(_attn_a_kernel, seq=seq),
        out_shape=jax.ShapeDtypeStruct((batch * seq, A_HEADS * HEAD_DIM), BF16),
        grid=(batch, A_HEADS),
        in_specs=in_specs,
        out_specs=pl.BlockSpec((seq, HEAD_DIM), lambda b, h: (b, h)),
        scratch_shapes=[pltpu.VMEM((N_DIL, seq, HEAD_DIM), F32), pltpu.VMEM((N_DIL, seq, HEAD_DIM), F32)],
        compiler_params=_params(("arbitrary", "arbitrary")),
    )(*operands)


_BQ = 256
_BK = 512


def _attn_b_kernel(q_ref, k_ref, v_ref, o_ref, *, seq):
    for g in range(B_GROUP):
        q = q_ref[g]

        def body(c, carry, q=q):
            m, l, acc = carry
            c0 = pl.multiple_of(c * _BK, _BK)
            k = k_ref[pl.ds(c0, _BK), :]
            v = v_ref[pl.ds(c0, _BK), :]
            s = lax.dot_general(q, k, (((1,), (1,)), ((), ())), preferred_element_type=F32)
            m_new = jnp.maximum(m, jnp.max(s, axis=1, keepdims=True))
            alpha = jnp.exp2(m - m_new)
            p = jnp.exp2(s - m_new)
            l = alpha * l + jnp.sum(p, axis=1, keepdims=True)
            acc = alpha * acc + jnp.dot(p.astype(BF16), v, preferred_element_type=F32)
            return m_new, l, acc

        init = (jnp.full((_BQ, 1), NEG_BIG, F32), jnp.zeros((_BQ, 1), F32), jnp.zeros((_BQ, HEAD_DIM), F32))
        _, l, acc = lax.fori_loop(0, seq // _BK, body, init)
        o_ref[:, g * HEAD_DIM:(g + 1) * HEAD_DIM] = (acc / l).astype(o_ref.dtype)


def _attn_b(slots, batch, seq):
    n_tok = batch * seq
    k0 = B_Q_HEADS
    v0 = k0 + B_KV_HEADS
    qblocks = seq // _BQ
    return pl.pallas_call(
        functools.partial(_attn_b_kernel, seq=seq),
        out_shape=jax.ShapeDtypeStruct((n_tok, B_Q_HEADS * HEAD_DIM), BF16),
        grid=(batch, B_KV_HEADS, qblocks),
        in_specs=[
            pl.BlockSpec((B_GROUP, _BQ, HEAD_DIM), lambda b, kv, qi: (kv, b * qblocks + qi, 0)),
            pl.BlockSpec((None, seq, HEAD_DIM), lambda b, kv, qi: (k0 + kv, b, 0)),
            pl.BlockSpec((None, seq, HEAD_DIM), lambda b, kv, qi: (v0 + kv, b, 0)),
        ],
        out_specs=pl.BlockSpec((_BQ, B_GROUP * HEAD_DIM), lambda b, kv, qi: (b * qblocks + qi, kv)),
        compiler_params=_params(("arbitrary", "arbitrary", "arbitrary")),
    )(slots, slots, slots)


_MIX_ROWS = 256


def _branch_kernel(oa_ref, ob_ref, ga_ref, gb_ref, wa_ref, wb_ref, out_ref):
    y_a = jnp.dot(oa_ref[...], wa_ref[...], preferred_element_type=F32)
    y_b = jnp.dot(ob_ref[...], wb_ref[...], preferred_element_type=F32)
    out_ref[...] = (ga_ref[...].astype(F32) * y_a + gb_ref[...].astype(F32) * y_b).astype(BF16)


def _branch_mix(o_a, o_b, gates, wa, wb):
    n_tok = o_b.shape[0]
    d_model = wa.shape[1]
    tm = _MIX_ROWS
    row = lambda i: (i, 0)
    return pl.pallas_call(
        _branch_kernel,
        out_shape=jax.ShapeDtypeStruct((n_tok, d_model), BF16),
        grid=(n_tok // tm,),
        in_specs=[pl.BlockSpec((tm, o_a.shape[1]), row),
                  pl.BlockSpec((tm, o_b.shape[1]), row),
                  pl.BlockSpec((tm, d_model), lambda i: (i, 0)),
                  pl.BlockSpec((tm, d_model), lambda i: (i, 1)),
                  pl.BlockSpec(wa.shape, lambda i: (0, 0)),
                  pl.BlockSpec(wb.shape, lambda i: (0, 0))],
        out_specs=pl.BlockSpec((tm, d_model), row),
        compiler_params=_params(("arbitrary",)),
    )(o_a, o_b, gates, gates, wa, wb)


def _layer_norm(z, g, b):
    mu = jnp.mean(z, axis=-1, keepdims=True)
    zc = z - mu
    var = jnp.mean(zc * zc, axis=-1, keepdims=True)
    return zc * lax.rsqrt(var + LN_EPS) * g + b


def _out_proj_kernel(m_ref, w_ref, x_ref, g_ref, b_ref, rhi_ref, rlo_ref, rb_ref, h_ref, lg_ref, *, alpha):
    mix = jnp.dot(m_ref[...], w_ref[...], preferred_element_type=F32)
    h = _layer_norm(alpha * x_ref[...] + mix, g_ref[...], b_ref[...])
    h_ref[...] = h
    hi = h.astype(BF16)
    lo = (h - hi.astype(F32)).astype(BF16)
    lg = jnp.dot(hi, rhi_ref[...], preferred_element_type=F32)
    lg = lg + jnp.dot(lo, rhi_ref[...], preferred_element_type=F32)
    lg = lg + jnp.dot(hi, rlo_ref[...], preferred_element_type=F32)
    lg_ref[...] = lg + rb_ref[...]


def _out_proj(merged, w_out, x2, g, b, r_hi, r_lo, r_b, alpha):
    n_tok, d_model = x2.shape
    tm = _MIX_ROWS
    row = lambda i: (i, 0)
    fix = lambda i: (0, 0)
    return pl.pallas_call(
        functools.partial(_out_proj_kernel, alpha=alpha),
        out_shape=(jax.ShapeDtypeStruct((n_tok, d_model), F32),
                   jax.ShapeDtypeStruct((n_tok, LANES), F32)),
        grid=(n_tok // tm,),
        in_specs=[pl.BlockSpec((tm, d_model), row),
                  pl.BlockSpec(w_out.shape, fix),
                  pl.BlockSpec((tm, d_model), row),
                  pl.BlockSpec((1, d_model), fix),
                  pl.BlockSpec((1, d_model), fix),
                  pl.BlockSpec(r_hi.shape, fix),
                  pl.BlockSpec(r_lo.shape, fix),
                  pl.BlockSpec((1, LANES), fix)],
        out_specs=(pl.BlockSpec((tm, d_model), row), pl.BlockSpec((tm, LANES), row)),
        compiler_params=_params(("arbitrary",)),
    )(merged, w_out, x2, g, b, r_hi, r_lo, r_b)


_CHUNKS = (512, 256, 128)


def _moe_kernel(item_expert, item_row0, item_blocks, item_real, src_tok, dst_row,
                h_hbm, wg_ref, wu_ref, wd_ref, y_hbm,
                stage, xbf, acc, obuf, wgb, wub, wdb, gsem, ssem):
    it = pl.program_id(0)
    f = pl.program_id(1)
    n_items = pl.num_programs(0)
    n_ff = pl.num_programs(1)
    n_blocks = item_blocks[it]

    def gather_copy(base, r):
        return pltpu.make_async_copy(h_hbm.at[pl.ds(src_tok[base + r], 1)], stage.at[pl.ds(r, 1)], gsem.at[0])

    def scatter_copy(base, r):
        return pltpu.make_async_copy(obuf.at[pl.ds(r, 1)], y_hbm.at[pl.ds(dst_row[base + r], 1)], ssem.at[0])

    def for_rows(n_rows, fn):
        def group(t, c):
            for u in range(DMA_UNROLL):
                fn(t * DMA_UNROLL + u)
            return c
        n_groups = n_rows // DMA_UNROLL
        lax.fori_loop(0, n_groups, group, 0)

        def single(r, c):
            fn(r)
            return c
        lax.fori_loop(n_groups * DMA_UNROLL, n_rows, single, 0)

    def gather(item, op):
        base = item_row0[item]
        for_rows(item_blocks[item] * MOE_BLOCK, lambda r: op(gather_copy(base, r)))

    def scatter(item, op):
        base = item_row0[item]
        for_rows(item_real[item], lambda r: op(scatter_copy(base, r)))

    start = lambda cp: cp.start()
    wait = lambda cp: cp.wait()

    @pl.when((it == 0) & (f == 0))
    def _first_gather():
        gather(it, start)

    @pl.when(f == 0)
    def _gathered():
        gather(it, wait)

        def cast_rows(b, c):
            rows = pl.ds(pl.multiple_of(b * MOE_BLOCK, MOE_BLOCK), MOE_BLOCK)
            xbf[rows, :] = stage[rows, :].astype(BF16)
            return c
        lax.fori_loop(0, n_blocks, cast_rows, 0)

        @pl.when(it + 1 < n_items)
        def _prefetch_next():
            gather(it + 1, start)

    def chunk(r0, rows, mode):
        xb = xbf[pl.ds(r0, rows), :]
        a = jnp.dot(xb, wgb[...], preferred_element_type=F32)
        u = jnp.dot(xb, wub[...], preferred_element_type=F32)
        hid = (a / (1.0 + jnp.exp(-a)) * u).astype(BF16)
        y = jnp.dot(hid, wdb[...], preferred_element_type=F32)
        if mode == "only":
            obuf[pl.ds(r0, rows), :] = y
        elif mode == "first":
            acc[pl.ds(r0, rows), :] = y
        elif mode == "middle":
            acc[pl.ds(r0, rows), :] += y
        else:
            obuf[pl.ds(r0, rows), :] = acc[pl.ds(r0, rows), :] + y

    def run_item(mode):
        big = _CHUNKS[0]
        per_big = big // MOE_BLOCK

        def big_chunk(t, c):
            chunk(pl.multiple_of(t * big, big), big, mode)
            return c
        n_big = n_blocks // per_big
        lax.fori_loop(0, n_big, big_chunk, 0)
        done = n_big * per_big
        for rows in _CHUNKS[1:]:
            blocks = rows // MOE_BLOCK
            take = ((n_blocks - done) & blocks) != 0

            @pl.when(take)
            def _small(done=done, rows=rows):
                chunk(pl.multiple_of(done * MOE_BLOCK, MOE_BLOCK), rows, mode)
            done = done + jnp.where(take, blocks, 0)

    @pl.when(n_blocks > 0)
    def _compute():
        wgb[...] = wg_ref[...].astype(BF16)
        wub[...] = wu_ref[...].astype(BF16)
        wdb[...] = wd_ref[...].astype(BF16)

    last = n_ff - 1

    @pl.when((n_blocks > 0) & (f == last))
    def _final():
        @pl.when(it > 0)
        def _drain_prev():
            scatter(it - 1, wait)
        run_item("only" if n_ff == 1 else "last")
        scatter(it, start)

    if n_ff > 1:
        @pl.when((n_blocks > 0) & (f == 0))
        def _first():
            run_item("first")

        @pl.when((n_blocks > 0) & (f > 0) & (f < last))
        def _middle():
            run_item("middle")

    @pl.when((f == last) & (it == n_items - 1))
    def _drain_last():
        scatter(jnp.where(n_blocks > 0, it, _last_live(item_blocks, n_items)), wait)


def _last_live(item_blocks, n_items):
    def body(i, best):
        return jnp.where(item_blocks[i] > 0, i, best)
    return lax.fori_loop(0, n_items, body, 0)


def _moe(h, w_gate, w_up, w_down, item_expert, item_row0, item_blocks, item_real, src_tok, dst_row):
    n_tok, d_model = h.shape
    ff = w_gate.shape[2]
    n_ff = ff // FF_TILE
    n_items = item_expert.shape[0]

    def ff_idx(it, f, blocks):
        return jnp.where(blocks[it] > 0, f, n_ff - 1)

    grid_spec = pltpu.PrefetchScalarGridSpec(
        num_scalar_prefetch=6,
        grid=(n_items, n_ff),
        in_specs=[
            pl.BlockSpec(memory_space=pl.ANY),
            pl.BlockSpec((None, d_model, FF_TILE), lambda it, f, ie, ir, ib, nr, st, dr: (ie[it], 0, ff_idx(it, f, ib))),
            pl.BlockSpec((None, d_model, FF_TILE), lambda it, f, ie, ir, ib, nr, st, dr: (ie[it], 0, ff_idx(it, f, ib))),
            pl.BlockSpec((None, FF_TILE, d_model), lambda it, f, ie, ir, ib, nr, st, dr: (ie[it], ff_idx(it, f, ib), 0)),
        ],
        out_specs=pl.BlockSpec(memory_space=pl.ANY),
        scratch_shapes=[
            pltpu.VMEM((ITEM_ROWS, d_model), F32),
            pltpu.VMEM((ITEM_ROWS, d_model), BF16),
            pltpu.VMEM((ITEM_ROWS, d_model), F32),
            pltpu.VMEM((ITEM_ROWS, d_model), F32),
            pltpu.VMEM((d_model, FF_TILE), BF16),
            pltpu.VMEM((d_model, FF_TILE), BF16),
            pltpu.VMEM((FF_TILE, d_model), BF16),
            pltpu.SemaphoreType.DMA((1,)),
            pltpu.SemaphoreType.DMA((1,)),
        ],
    )
    return pl.pallas_call(
        _moe_kernel,
        out_shape=jax.ShapeDtypeStruct((n_tok * TOP_K, d_model), F32),
        grid_spec=grid_spec,
        compiler_params=_params(("arbitrary", "arbitrary")),
    )(item_expert, item_row0, item_blocks, item_real, src_tok, dst_row, h, w_gate, w_up, w_down)


def _combine_kernel(y_ref, wt_ref, h_ref, g_ref, b_ref, out_ref, *, alpha):
    d_model = h_ref.shape[1]
    wt = wt_ref[...]
    ffn = wt[:, 0:1] * y_ref[:, :d_model].astype(F32) + wt[:, 1:2] * y_ref[:, d_model:].astype(F32)
    out_ref[...] = _layer_norm(alpha * h_ref[...] + ffn, g_ref[...], b_ref[...])


def _combine(y_slots, weights, h, g, b, alpha):
    n_tok, d_model = h.shape
    tm = ROW_TILE
    pairs = y_slots.reshape(y_slots.shape[0] // TOP_K, TOP_K * d_model)
    row = lambda i: (i, 0)
    fix = lambda i: (0, 0)
    return pl.pallas_call(
        functools.partial(_combine_kernel, alpha=alpha),
        out_shape=jax.ShapeDtypeStruct((n_tok, d_model), F32),
        grid=(n_tok // tm,),
        in_specs=[pl.BlockSpec((tm, TOP_K * d_model), row),
                  pl.BlockSpec((tm, TOP_K), row),
                  pl.BlockSpec((tm, d_model), row),
                  pl.BlockSpec((1, d_model), fix),
                  pl.BlockSpec((1, d_model), fix)],
        out_specs=pl.BlockSpec((tm, d_model), row),
        compiler_params=_params(("arbitrary",)),
    )(pairs, weights, h, g, b)


def _rope_tables(seq):
    half = HEAD_DIM // 2
    inv1 = ROPE_THETA ** (-jnp.arange(half, dtype=F32) / half)
    ang1 = jnp.arange(seq, dtype=F32)[:, None] * inv1[None, :]
    rows = seq // GRID_W
    r, c = jnp.meshgrid(jnp.arange(rows, dtype=F32), jnp.arange(GRID_W, dtype=F32), indexing='ij')
    n_axis = HEAD_DIM // 4
    inv2 = ROPE_THETA ** (-jnp.arange(n_axis, dtype=F32) / n_axis)
    ang2 = jnp.concatenate([r.reshape(-1, 1) * inv2[None, :], c.reshape(-1, 1) * inv2[None, :]], axis=-1)
    q_scale = HEAD_DIM ** -0.5 * LOG2E
    out = []
    for ang in (ang1, ang2):
        cos = jnp.concatenate([jnp.cos(ang), jnp.cos(ang)], axis=-1)
        sin = jnp.concatenate([-jnp.sin(ang), jnp.sin(ang)], axis=-1)
        out += [cos * q_scale, sin * q_scale, cos, sin]
    return jnp.stack(out, axis=0)


def _route(logits, n_tok):
    g_logits = logits[:, :N_GROUPS]
    g_prob = jax.nn.softmax(g_logits, axis=-1)
    g_idx = jnp.argmax(g_logits, axis=-1)
    g_gate = jnp.take_along_axis(g_prob, g_idx[:, None], axis=1)[:, 0]
    e_logits = logits[:, N_GROUPS:N_GROUPS + N_EXPERTS].reshape(n_tok, N_GROUPS, EXPERTS_PER_GROUP)
    e_logits = jnp.take_along_axis(e_logits, g_idx[:, None, None], axis=1)[:, 0]
    e_prob = jax.nn.softmax(e_logits, axis=-1)
    top_p, top_i = lax.top_k(e_prob, TOP_K)
    top_p = top_p / jnp.sum(top_p, axis=-1, keepdims=True)
    weights = g_gate[:, None] * top_p
    expert = g_idx[:, None] * EXPERTS_PER_GROUP + top_i

    n_slot = n_tok * TOP_K
    e_flat = expert.reshape(n_slot).astype(I32)
    order = jnp.argsort(e_flat).astype(I32)
    e_sorted = e_flat[order]
    counts = jnp.bincount(e_flat, length=N_EXPERTS).astype(I32)
    starts = jnp.cumsum(counts) - counts
    padded = ((counts + MOE_BLOCK - 1) // MOE_BLOCK) * MOE_BLOCK
    p_ends = jnp.cumsum(padded)
    p_starts = p_ends - padded
    dest = p_starts[e_sorted] + (jnp.arange(n_slot, dtype=I32) - starts[e_sorted])
    buf_len = n_slot + N_EXPERTS * MOE_BLOCK
    src_tok = jnp.zeros((buf_len,), I32).at[dest].set(order // TOP_K)
    dst_row = jnp.zeros((buf_len,), I32).at[dest].set(order)

    n_items = N_EXPERTS + n_slot // ITEM_ROWS
    per_expert = (padded + ITEM_ROWS - 1) // ITEM_ROWS
    item_ends = jnp.cumsum(per_expert)
    total = item_ends[-1]
    ids = jnp.arange(n_items, dtype=I32)
    last = jnp.maximum(total - 1, 0)
    live = ids < total
    e_of = jnp.clip(jnp.searchsorted(item_ends, jnp.minimum(ids, last), side='right'), 0, N_EXPERTS - 1).astype(I32)
    chunk = jnp.minimum(ids, last) - (item_ends[e_of] - per_expert[e_of])
    item_row0 = jnp.where(live, p_starts[e_of] + chunk * ITEM_ROWS, 0).astype(I32)
    item_blocks = jnp.where(live, jnp.clip(padded[e_of] // MOE_BLOCK - chunk * ITEM_BLOCKS, 0, ITEM_BLOCKS), 0).astype(I32)
    item_real = jnp.where(live, jnp.clip(counts[e_of] - chunk * ITEM_ROWS, 0, ITEM_ROWS), 0).astype(I32)
    return weights, e_of, item_row0, item_blocks, item_real, src_tok, dst_row


def kernel(x, w_in, q_norm_g, k_norm_g, w_branch_a, w_branch_b, w_out, ln1_g, ln1_b, w_group, b_group, w_router,
           b_router, w_gate, w_up, w_down, ln2_g, ln2_b):
    batch, seq, d_model = x.shape
    depth = w_in.shape[0]
    n_tok = batch * seq
    dn_alpha = (2 * depth) ** 0.25
    tabs = _rope_tables(seq)
    h = x.reshape(n_tok, d_model)
    for layer in range(depth):
        xb = h.astype(BF16)
        groups = [_proj_a(xb, w_in[layer], tabs, g, batch, seq) for g in range(N_DIL)]
        slots_b = _proj_b(xb, w_in[layer], tabs, q_norm_g[layer][None, :], k_norm_g[layer][None, :], seq)
        gates = _proj_gate(xb, w_in[layer])
        o_a = _attn_a(groups, batch, seq)
        o_b = _attn_b(slots_b, batch, seq)
        merged = _branch_mix(o_a, o_b, gates, w_branch_a[layer].astype(BF16), w_branch_b[layer].astype(BF16))
        w_r = jnp.concatenate([w_group[layer], w_router[layer]], axis=1)
        w_r = jnp.pad(w_r, ((0, 0), (0, LANES - w_r.shape[1])))
        r_hi = w_r.astype(BF16)
        r_lo = (w_r - r_hi.astype(F32)).astype(BF16)
        r_b = jnp.pad(jnp.concatenate([b_group[layer], b_router[layer]]), (0, LANES - N_GROUPS - N_EXPERTS))[None, :]
        h1, logits = _out_proj(merged, w_out[layer].astype(BF16), h, ln1_g[layer][None, :], ln1_b[layer][None, :],
                               r_hi, r_lo, r_b, dn_alpha)
        weights, item_expert, item_row0, item_blocks, item_real, src_tok, dst_row = _route(logits, n_tok)
        y_slots = _moe(h1, w_gate[layer], w_up[layer], w_down[layer], item_expert, item_row0, item_blocks,
                       item_real, src_tok, dst_row)
        h = _combine(y_slots, weights, h1, ln2_g[layer][None, :], ln2_b[layer][None, :], dn_alpha)
    return h.reshape(batch, seq, d_model)
```

```python
import functools
import math

import jax
import jax.numpy as jnp
from jax import lax
from jax.experimental import pallas as pl
from jax.experimental.pallas import tpu as pltpu

F32 = jnp.float32
BF16 = jnp.bfloat16
I32 = jnp.int32

HEAD_DIM = 128
ROPE_THETA = 10000.0
GRID_W = 64
DIL_PATTERNS = ((128, 1), (512, 4), (2048, 16))
N_DIL = len(DIL_PATTERNS)
A_HEADS = 8
BAND_HALF = 64
B_Q_HEADS = 16
B_KV_HEADS = 4
B_GROUP = B_Q_HEADS // B_KV_HEADS
N_GROUPS = 4
EXPERTS_PER_GROUP = 8
N_EXPERTS = N_GROUPS * EXPERTS_PER_GROUP
TOP_K = 2
MOE_BLOCK = 128
LN_EPS = 1e-5
RMS_EPS = 1e-6
NEG_BIG = -1e30
LOG2E = math.log2(math.e)

LANES = 128
COL_TILE = 8 * HEAD_DIM
ROW_TILE = 512
ITEM_ROWS = 1024
ITEM_BLOCKS = ITEM_ROWS // MOE_BLOCK
FF_TILE = 256
DMA_UNROLL = 8
VMEM_LIMIT = 56 * 1024 * 1024


def _params(sem, vmem=VMEM_LIMIT):
    return pltpu.CompilerParams(dimension_semantics=sem, vmem_limit_bytes=vmem)


_T_QA, _T_KA, _T_VA, _T_QB, _T_KVB, _T_GATE = 0, 3, 6, 9, 11, 12
_R1Q, _R1K, _R2Q, _R2K = 0, 2, 4, 6


def _cast_weights_once(w_ref, wbf_ref):
    @pl.when(pl.program_id(1) == 0)
    def _cast():
        wbf_ref[...] = w_ref[...].astype(BF16)


def _project(x_ref, wbf_ref):
    return jnp.dot(x_ref[...], wbf_ref[...], preferred_element_type=F32)


def _head(acc, h):
    return acc[:, h * HEAD_DIM:(h + 1) * HEAD_DIM]


def _rope(a, tab_ref, t):
    return a * tab_ref[t] + pltpu.roll(a, HEAD_DIM // 2, 1) * tab_ref[t + 1]


def _rms(a, g_ref):
    ms = jnp.mean(a * a, axis=-1, keepdims=True)
    return a * lax.rsqrt(ms + RMS_EPS) * g_ref[...]


def _proj_a_kernel(x_ref, w_ref, tab_ref, out_ref, wbf_ref, *perm, dil):
    j = pl.program_id(0)
    _cast_weights_once(w_ref, wbf_ref)
    rows = x_ref.shape[0]

    def emit(fn):
        acc = _project(x_ref, wbf_ref)
        if dil == 1:
            for h in range(A_HEADS):
                out_ref[h, 0] = fn(_head(acc, h)).astype(BF16)
            return
        perm_ref, = perm
        for h in range(A_HEADS):
            perm_ref[h] = fn(_head(acc, h))
        for h in range(A_HEADS):
            for r in range(dil):
                out_ref[h, r] = perm_ref[h, pl.ds(r, rows // dil, stride=dil), :].astype(BF16)

    @pl.when(j == 0)
    def _q():
        emit(lambda a: _rope(a, tab_ref, _R1Q))

    @pl.when(j == 1)
    def _k():
        emit(lambda a: _rope(a, tab_ref, _R1K))

    @pl.when(j == 2)
    def _v():
        emit(lambda a: a)


def _proj_b_kernel(x_ref, w_ref, tab_ref, gq_ref, gk_ref, out_ref, wbf_ref, *, q_tiles):
    j = pl.program_id(0)
    _cast_weights_once(w_ref, wbf_ref)
    heads = COL_TILE // HEAD_DIM

    @pl.when(j < q_tiles)
    def _qb():
        acc = _project(x_ref, wbf_ref)
        for h in range(heads):
            out_ref[h] = _rope(_rms(_head(acc, h), gq_ref), tab_ref, _R2Q).astype(BF16)

    @pl.when(j == q_tiles)
    def _kvb():
        acc = _project(x_ref, wbf_ref)
        for h in range(heads):
            if h < B_KV_HEADS:
                out_ref[h] = _rope(_rms(_head(acc, h), gk_ref), tab_ref, _R2K).astype(BF16)
            else:
                out_ref[h] = _head(acc, h).astype(BF16)


def _proj_gate_kernel(x_ref, w_ref, out_ref, wbf_ref):
    _cast_weights_once(w_ref, wbf_ref)
    acc = _project(x_ref, wbf_ref)
    out_ref[...] = (1.0 / (1.0 + jnp.exp(-acc))).astype(BF16)


def _proj_specs(d_model, seq, first_tile, tile_step):
    seq_blocks = seq // ROW_TILE
    return [
        pl.BlockSpec((ROW_TILE, d_model), lambda j, i: (i, 0)),
        pl.BlockSpec((d_model, COL_TILE), lambda j, i: (0, first_tile + tile_step * j)),
        pl.BlockSpec((8, ROW_TILE, HEAD_DIM), lambda j, i: (0, i % seq_blocks, 0)),
    ]


def _proj_a(xb, w_in, tabs, group, batch, seq):
    n_tok, d_model = xb.shape
    dil = DIL_PATTERNS[group][1]
    sub_len = seq // dil
    seq_blocks = seq // ROW_TILE
    scratch = [pltpu.VMEM((d_model, COL_TILE), BF16)]
    if dil > 1:
        scratch.append(pltpu.VMEM((A_HEADS, ROW_TILE, HEAD_DIM), F32))
    return pl.pallas_call(
        functools.partial(_proj_a_kernel, dil=dil),
        out_shape=jax.ShapeDtypeStruct((3 * A_HEADS, batch, dil, sub_len, HEAD_DIM), BF16),
        grid=(3, n_tok // ROW_TILE),
        in_specs=_proj_specs(d_model, seq, group, N_DIL),
        out_specs=pl.BlockSpec((A_HEADS, None, dil, ROW_TILE // dil, HEAD_DIM),
                               lambda j, i: (j, i // seq_blocks, 0, i % seq_blocks, 0)),
        scratch_shapes=scratch,
        compiler_params=_params(("arbitrary", "arbitrary")),
    )(xb, w_in, tabs)


def _proj_b(xb, w_in, tabs, gq, gk, seq):
    n_tok, d_model = xb.shape
    q_tiles = B_Q_HEADS * HEAD_DIM // COL_TILE
    heads = COL_TILE // HEAD_DIM
    fix = lambda j, i: (0, 0)
    return pl.pallas_call(
        functools.partial(_proj_b_kernel, q_tiles=q_tiles),
        out_shape=jax.ShapeDtypeStruct(((q_tiles + 1) * heads, n_tok, HEAD_DIM), BF16),
        grid=(q_tiles + 1, n_tok // ROW_TILE),
        in_specs=_proj_specs(d_model, seq, _T_QB, 1) + [pl.BlockSpec((1, HEAD_DIM), fix), pl.BlockSpec((1, HEAD_DIM), fix)],
        out_specs=pl.BlockSpec((heads, ROW_TILE, HEAD_DIM), lambda j, i: (j, i, 0)),
        scratch_shapes=[pltpu.VMEM((d_model, COL_TILE), BF16)],
        compiler_params=_params(("arbitrary", "arbitrary")),
    )(xb, w_in, tabs, gq, gk)


def _proj_gate(xb, w_in):
    n_tok, d_model = xb.shape
    n_ct = w_in.shape[1] // COL_TILE - _T_GATE
    return pl.pallas_call(
        _proj_gate_kernel,
        out_shape=jax.ShapeDtypeStruct((n_tok, n_ct * COL_TILE), BF16),
        grid=(n_ct, n_tok // ROW_TILE),
        in_specs=[pl.BlockSpec((ROW_TILE, d_model), lambda j, i: (i, 0)),
                  pl.BlockSpec((d_model, COL_TILE), lambda j, i: (0, _T_GATE + j))],
        out_specs=pl.BlockSpec((ROW_TILE, COL_TILE), lambda j, i: (i, j)),
        scratch_shapes=[pltpu.VMEM((d_model, COL_TILE), BF16)],
        compiler_params=_params(("arbitrary", "arbitrary")),
    )(xb, w_in)


_SUB = 128
_WIN = _SUB + 2 * BAND_HALF
_MERGE_ROWS = 256
_A_UNROLL = 8


def _attn_a_group(g, dil, seq, q_ref, k_ref, v_ref, og, lg):
    sub_len = seq // dil
    per_seq = sub_len // _SUB
    shift = per_seq.bit_length() - 1

    def body(i, carry):
        r = lax.shift_right_logical(i, shift)
        p0 = pl.multiple_of((i & (per_seq - 1)) * _SUB, _SUB)
        start = pl.multiple_of(jnp.clip(p0 - BAND_HALF, 0, sub_len - _WIN), BAND_HALF)
        q = q_ref[r, pl.ds(p0, _SUB), :]
        k = k_ref[r, pl.ds(start, _WIN), :]
        v = v_ref[r, pl.ds(start, _WIN), :]
        s = lax.dot_general(q, k, (((1,), (1,)), ((), ())), preferred_element_type=F32)
        qpos = p0 + lax.broadcasted_iota(I32, (_SUB, _WIN), 0)
        kpos = start + lax.broadcasted_iota(I32, (_SUB, _WIN), 1)
        s = jnp.where(jnp.abs(kpos - qpos) <= BAND_HALF, s, NEG_BIG)
        m = jnp.max(s, axis=1, keepdims=True)
        p = jnp.exp2(s - m)
        l = jnp.sum(p, axis=1, keepdims=True)
        o = jnp.dot(p.astype(BF16), v, preferred_element_type=F32) / l
        lse = jnp.broadcast_to(m + jnp.log2(l), (_SUB, HEAD_DIM))
        if dil == 1:
            rows = pl.ds(p0, _SUB)
        else:
            rows = pl.ds(p0 * dil + r, _SUB, stride=dil)
        og[g, rows, :] = o
        lg[g, rows, :] = lse
        return carry

    lax.fori_loop(0, seq // _SUB, body, 0, unroll=_A_UNROLL)


def _attn_a_kernel(*refs, seq):
    qkv, o_ref, og, lg = refs[:3 * N_DIL], refs[3 * N_DIL], refs[3 * N_DIL + 1], refs[3 * N_DIL + 2]
    for g, (_, dil) in enumerate(DIL_PATTERNS):
        _attn_a_group(g, dil, seq, qkv[3 * g], qkv[3 * g + 1], qkv[3 * g + 2], og, lg)

    def merge(c, carry):
        rows = pl.ds(pl.multiple_of(c * _MERGE_ROWS, _MERGE_ROWS), _MERGE_ROWS)
        ls = [lg[g, rows, :] for g in range(N_DIL)]
        mx = functools.reduce(jnp.maximum, ls)
        ws = [jnp.exp2(l - mx) for l in ls]
        num = functools.reduce(lambda a, b: a + b, [w * og[g, rows, :] for g, w in enumerate(ws)])
        o_ref[rows, :] = (num / functools.reduce(lambda a, b: a + b, ws)).astype(o_ref.dtype)
        return carry

    lax.fori_loop(0, seq // _MERGE_ROWS, merge, 0)


def _attn_a(groups, batch, seq):
    in_specs, operands = [], []
    for g, (_, dil) in enumerate(DIL_PATTERNS):
        for kind in range(3):
            first = kind * A_HEADS
            in_specs.append(pl.BlockSpec((None, None, dil, seq // dil, HEAD_DIM),
                                         lambda b, h, first=first: (first + h, b, 0, 0, 0)))
            operands.append(groups[g])
    return pl.pallas_call(
        functools.partial(_attn_a_kernel, seq=seq),
        out_shape=jax.ShapeDtypeStruct((batch * seq, A_HEADS * HEAD_DIM), BF16),
        grid=(batch, A_HEADS),
        in_specs=in_specs,
        out_specs=pl.BlockSpec((seq, HEAD_DIM), lambda b, h: (b, h)),
        scratch_shapes=[pltpu.VMEM((N_DIL, seq, HEAD_DIM), F32), pltpu.VMEM((N_DIL, seq, HEAD_DIM), F32)],
        compiler_params=_params(("arbitrary", "arbitrary")),
    )(*operands)


_BQ = 256
_BK = 512


def _attn_b_kernel(q_ref, k_ref, v_ref, o_ref, *, seq):
    rows = B_GROUP * _BQ
    q = q_ref[...].reshape(rows, HEAD_DIM)

    def body(c, carry):
        m, l, acc = carry
        c0 = pl.multiple_of(c * _BK, _BK)
        k = k_ref[pl.ds(c0, _BK), :]
        v = v_ref[pl.ds(c0, _BK), :]
        s = lax.dot_general(q, k, (((1,), (1,)), ((), ())), preferred_element_type=F32)
        m_new = jnp.maximum(m, jnp.max(s, axis=1, keepdims=True))
        alpha = jnp.exp2(m - m_new)
        p = jnp.exp2(s - m_new)
        l = alpha * l + jnp.sum(p, axis=1, keepdims=True)
        acc = alpha * acc + jnp.dot(p.astype(BF16), v, preferred_element_type=F32)
        return m_new, l, acc

    init = (jnp.full((rows, 1), NEG_BIG, F32), jnp.zeros((rows, 1), F32), jnp.zeros((rows, HEAD_DIM), F32))
    _, l, acc = lax.fori_loop(0, seq // _BK, body, init)
    o = acc / l
    for g in range(B_GROUP):
        o_ref[:, g * HEAD_DIM:(g + 1) * HEAD_DIM] = o[g * _BQ:(g + 1) * _BQ].astype(o_ref.dtype)


def _attn_b(slots, batch, seq):
    n_tok = batch * seq
    k0 = B_Q_HEADS
    v0 = k0 + B_KV_HEADS
    qblocks = seq // _BQ
    return pl.pallas_call(
        functools.partial(_attn_b_kernel, seq=seq),
        out_shape=jax.ShapeDtypeStruct((n_tok, B_Q_HEADS * HEAD_DIM), BF16),
        grid=(batch, B_KV_HEADS, qblocks),
        in_specs=[
            pl.BlockSpec((B_GROUP, _BQ, HEAD_DIM), lambda b, kv, qi: (kv, b * qblocks + qi, 0)),
            pl.BlockSpec((None, seq, HEAD_DIM), lambda b, kv, qi: (k0 + kv, b, 0)),
            pl.BlockSpec((None, seq, HEAD_DIM), lambda b, kv, qi: (v0 + kv, b, 0)),
        ],
        out_specs=pl.BlockSpec((_BQ, B_GROUP * HEAD_DIM), lambda b, kv, qi: (b * qblocks + qi, kv)),
        compiler_params=_params(("arbitrary", "arbitrary", "arbitrary")),
    )(slots, slots, slots)


_MIX_ROWS = 256


def _branch_kernel(oa_ref, ob_ref, ga_ref, gb_ref, wa_ref, wb_ref, out_ref):
    y_a = jnp.dot(oa_ref[...], wa_ref[...], preferred_element_type=F32)
    y_b = jnp.dot(ob_ref[...], wb_ref[...], preferred_element_type=F32)
    out_ref[...] = (ga_ref[...].astype(F32) * y_a + gb_ref[...].astype(F32) * y_b).astype(BF16)


def _branch_mix(o_a, o_b, gates, wa, wb):
    n_tok = o_b.shape[0]
    d_model = wa.shape[1]
    tm = _MIX_ROWS
    row = lambda i: (i, 0)
    return pl.pallas_call(
        _branch_kernel,
        out_shape=jax.ShapeDtypeStruct((n_tok, d_model), BF16),
        grid=(n_tok // tm,),
        in_specs=[pl.BlockSpec((tm, o_a.shape[1]), row),
                  pl.BlockSpec((tm, o_b.shape[1]), row),
                  pl.BlockSpec((tm, d_model), lambda i: (i, 0)),
                  pl.BlockSpec((tm, d_model), lambda i: (i, 1)),
                  pl.BlockSpec(wa.shape, lambda i: (0, 0)),
                  pl.BlockSpec(wb.shape, lambda i: (0, 0))],
        out_specs=pl.BlockSpec((tm, d_model), row),
        compiler_params=_params(("arbitrary",)),
    )(o_a, o_b, gates, gates, wa, wb)


def _layer_norm(z, g, b):
    mu = jnp.mean(z, axis=-1, keepdims=True)
    zc = z - mu
    var = jnp.mean(zc * zc, axis=-1, keepdims=True)
    return zc * lax.rsqrt(var + LN_EPS) * g + b


def _to_slabs(slab_ref, value):
    rows, d = value.shape
    chunks = d // LANES
    for c in range(chunks):
        slab_ref[pl.ds(c, rows, stride=chunks), :] = value[:, c * LANES:(c + 1) * LANES]


def _from_slabs(slab_ref, rows, chunks, c):
    return slab_ref[pl.ds(c, rows, stride=chunks), :]


def _out_proj_kernel(m_ref, w_ref, x_ref, g_ref, b_ref, rhi_ref, rlo_ref, rb_ref, h_ref, hs_ref, lg_ref, *, alpha):
    mix = jnp.dot(m_ref[...], w_ref[...], preferred_element_type=F32)
    h = _layer_norm(alpha * x_ref[...] + mix, g_ref[...], b_ref[...])
    h_ref[...] = h
    _to_slabs(hs_ref, h)
    hi = h.astype(BF16)
    lo = (h - hi.astype(F32)).astype(BF16)
    lg = jnp.dot(hi, rhi_ref[...], preferred_element_type=F32)
    lg = lg + jnp.dot(lo, rhi_ref[...], preferred_element_type=F32)
    lg = lg + jnp.dot(hi, rlo_ref[...], preferred_element_type=F32)
    lg_ref[...] = lg + rb_ref[...]


def _out_proj(merged, w_out, x2, g, b, r_hi, r_lo, r_b, alpha):
    n_tok, d_model = x2.shape
    tm = _MIX_ROWS
    chunks = d_model // LANES
    row = lambda i: (i, 0)
    fix = lambda i: (0, 0)
    return pl.pallas_call(
        functools.partial(_out_proj_kernel, alpha=alpha),
        out_shape=(jax.ShapeDtypeStruct((n_tok, d_model), F32),
                   jax.ShapeDtypeStruct((n_tok * chunks, LANES), F32),
                   jax.ShapeDtypeStruct((n_tok, LANES), F32)),
        grid=(n_tok // tm,),
        in_specs=[pl.BlockSpec((tm, d_model), row),
                  pl.BlockSpec(w_out.shape, fix),
                  pl.BlockSpec((tm, d_model), row),
                  pl.BlockSpec((1, d_model), fix),
                  pl.BlockSpec((1, d_model), fix),
                  pl.BlockSpec(r_hi.shape, fix),
                  pl.BlockSpec(r_lo.shape, fix),
                  pl.BlockSpec((1, LANES), fix)],
        out_specs=(pl.BlockSpec((tm, d_model), row), pl.BlockSpec((tm * chunks, LANES), row),
                   pl.BlockSpec((tm, LANES), row)),
        compiler_params=_params(("arbitrary",)),
    )(merged, w_out, x2, g, b, r_hi, r_lo, r_b)


_BIG_CHUNK = 512


def _moe_kernel(item_expert, item_row0, item_blocks, item_real, src_slab, dst_slab,
                h_hbm, wg_ref, wu_ref, wd_ref, y_hbm,
                stage, xbf, acc, obuf, wgb, wub, wdb, gsem, ssem):
    it = pl.program_id(0)
    f = pl.program_id(1)
    n_items = pl.num_programs(0)
    n_ff = pl.num_programs(1)
    n_blocks = item_blocks[it]
    d_model = xbf.shape[1]
    chunks = d_model // LANES

    def slab(ref, first):
        return ref.at[pl.ds(pl.multiple_of(first, chunks), chunks)]

    def gather_copy(base, r):
        return pltpu.make_async_copy(slab(h_hbm, src_slab[base + r]), slab(stage, r * chunks), gsem.at[0])

    def scatter_copy(base, r):
        return pltpu.make_async_copy(slab(obuf, r * chunks), slab(y_hbm, dst_slab[base + r]), ssem.at[0])

    def for_rows(n_rows, fn):
        def group(t, c):
            for u in range(DMA_UNROLL):
                fn(t * DMA_UNROLL + u)
            return c
        n_groups = n_rows // DMA_UNROLL
        lax.fori_loop(0, n_groups, group, 0)

        def single(r, c):
            fn(r)
            return c
        lax.fori_loop(n_groups * DMA_UNROLL, n_rows, single, 0)

    def gather(item, op):
        base = item_row0[item]
        for_rows(item_blocks[item] * MOE_BLOCK, lambda r: op(gather_copy(base, r)))

    def scatter(item, op):
        base = item_row0[item]
        for_rows(item_real[item], lambda r: op(scatter_copy(base, r)))

    start = lambda cp: cp.start()
    wait = lambda cp: cp.wait()

    @pl.when((it == 0) & (f == 0))
    def _first_gather():
        gather(it, start)

    @pl.when(f == 0)
    def _gathered():
        gather(it, wait)

        def cast_rows(b, carry):
            r0 = pl.multiple_of(b * MOE_BLOCK, MOE_BLOCK)
            for c in range(chunks):
                piece = stage[pl.ds(r0 * chunks + c, MOE_BLOCK, stride=chunks), :]
                xbf[pl.ds(r0, MOE_BLOCK), c * LANES:(c + 1) * LANES] = piece.astype(BF16)
            return carry
        lax.fori_loop(0, n_blocks, cast_rows, 0)

        @pl.when(it + 1 < n_items)
        def _prefetch_next():
            gather(it + 1, start)

    def chunk(r0, rows, mode):
        xb = xbf[pl.ds(r0, rows), :]
        a = jnp.dot(xb, wgb[...], preferred_element_type=F32)
        u = jnp.dot(xb, wub[...], preferred_element_type=F32)
        hid = (a / (1.0 + jnp.exp(-a)) * u).astype(BF16)
        y = jnp.dot(hid, wdb[...], preferred_element_type=F32)
        if mode == "first":
            acc[pl.ds(r0, rows), :] = y
        elif mode == "middle":
            acc[pl.ds(r0, rows), :] += y
        else:
            if mode == "last":
                y = acc[pl.ds(r0, rows), :] + y
            for c in range(chunks):
                obuf[pl.ds(r0 * chunks + c, rows, stride=chunks), :] = y[:, c * LANES:(c + 1) * LANES]

    def run_item(mode):
        per_big = _BIG_CHUNK // MOE_BLOCK

        def big_chunk(t, c):
            chunk(pl.multiple_of(t * _BIG_CHUNK, _BIG_CHUNK), _BIG_CHUNK, mode)
            return c
        n_big = n_blocks // per_big
        lax.fori_loop(0, n_big, big_chunk, 0)
        rest = n_blocks - n_big * per_big
        for blocks in range(1, per_big):
            @pl.when(rest == blocks)
            def _rest(blocks=blocks):
                chunk(pl.multiple_of(n_big * _BIG_CHUNK, _BIG_CHUNK), blocks * MOE_BLOCK, mode)

    @pl.when(n_blocks > 0)
    def _compute():
        wgb[...] = wg_ref[...].astype(BF16)
        wub[...] = wu_ref[...].astype(BF16)
        wdb[...] = wd_ref[...].astype(BF16)

    last = n_ff - 1

    @pl.when((n_blocks > 0) & (f == last))
    def _final():
        @pl.when(it > 0)
        def _drain_prev():
            scatter(it - 1, wait)
        run_item("only" if n_ff == 1 else "last")
        scatter(it, start)

    if n_ff > 1:
        @pl.when((n_blocks > 0) & (f == 0))
        def _first():
            run_item("first")

        @pl.when((n_blocks > 0) & (f > 0) & (f < last))
        def _middle():
            run_item("middle")

    @pl.when((f == last) & (it == n_items - 1))
    def _drain_last():
        scatter(jnp.where(n_blocks > 0, it, _last_live(item_blocks, n_items)), wait)


def _last_live(item_blocks, n_items):
    def body(i, best):
        return jnp.where(item_blocks[i] > 0, i, best)
    return lax.fori_loop(0, n_items, body, 0)


def _moe(h_slabs, w_gate, w_up, w_down, item_expert, item_row0, item_blocks, item_real, src_slab, dst_slab):
    d_model = w_gate.shape[1]
    chunks = d_model // LANES
    n_tok = h_slabs.shape[0] // chunks
    ff = w_gate.shape[2]
    n_ff = ff // FF_TILE
    n_items = item_expert.shape[0]

    def ff_idx(it, f, blocks):
        return jnp.where(blocks[it] > 0, f, n_ff - 1)

    grid_spec = pltpu.PrefetchScalarGridSpec(
        num_scalar_prefetch=6,
        grid=(n_items, n_ff),
        in_specs=[
            pl.BlockSpec(memory_space=pl.ANY),
            pl.BlockSpec((None, d_model, FF_TILE), lambda it, f, ie, ir, ib, nr, st, dr: (ie[it], 0, ff_idx(it, f, ib))),
            pl.BlockSpec((None, d_model, FF_TILE), lambda it, f, ie, ir, ib, nr, st, dr: (ie[it], 0, ff_idx(it, f, ib))),
            pl.BlockSpec((None, FF_TILE, d_model), lambda it, f, ie, ir, ib, nr, st, dr: (ie[it], ff_idx(it, f, ib), 0)),
        ],
        out_specs=pl.BlockSpec(memory_space=pl.ANY),
        scratch_shapes=[
            pltpu.VMEM((ITEM_ROWS * chunks, LANES), F32),
            pltpu.VMEM((ITEM_ROWS, d_model), BF16),
            pltpu.VMEM((ITEM_ROWS, d_model), F32),
            pltpu.VMEM((ITEM_ROWS * chunks, LANES), F32),
            pltpu.VMEM((d_model, FF_TILE), BF16),
            pltpu.VMEM((d_model, FF_TILE), BF16),
            pltpu.VMEM((FF_TILE, d_model), BF16),
            pltpu.SemaphoreType.DMA((1,)),
            pltpu.SemaphoreType.DMA((1,)),
        ],
    )
    return pl.pallas_call(
        _moe_kernel,
        out_shape=jax.ShapeDtypeStruct((n_tok * TOP_K * chunks, LANES), F32),
        grid_spec=grid_spec,
        compiler_params=_params(("arbitrary", "arbitrary")),
    )(item_expert, item_row0, item_blocks, item_real, src_slab, dst_slab, h_slabs, w_gate, w_up, w_down)


_COMBINE_ROWS = 512


def _combine_kernel(y0_ref, y1_ref, wt_ref, h_ref, g_ref, b_ref, out_ref, *, alpha):
    rows, d_model = h_ref.shape
    chunks = d_model // LANES
    wt = wt_ref[...]
    w0, w1 = wt[:, 0:1], wt[:, 1:2]
    for c in range(chunks):
        cols = slice(c * LANES, (c + 1) * LANES)
        ffn = w0 * _from_slabs(y0_ref, rows, chunks, c) + w1 * _from_slabs(y1_ref, rows, chunks, c)
        out_ref[:, cols] = alpha * h_ref[:, cols] + ffn
    out_ref[...] = _layer_norm(out_ref[...], g_ref[...], b_ref[...])


def _combine(y_slabs, weights, h, g, b, alpha):
    n_tok, d_model = h.shape
    tm = _COMBINE_ROWS
    chunks = d_model // LANES
    second = n_tok // tm
    row = lambda i: (i, 0)
    fix = lambda i: (0, 0)
    return pl.pallas_call(
        functools.partial(_combine_kernel, alpha=alpha),
        out_shape=jax.ShapeDtypeStruct((n_tok, d_model), F32),
        grid=(n_tok // tm,),
        in_specs=[pl.BlockSpec((tm * chunks, LANES), row),
                  pl.BlockSpec((tm * chunks, LANES), lambda i: (second + i, 0)),
                  pl.BlockSpec((tm, TOP_K), row),
                  pl.BlockSpec((tm, d_model), row),
                  pl.BlockSpec((1, d_model), fix),
                  pl.BlockSpec((1, d_model), fix)],
        out_specs=pl.BlockSpec((tm, d_model), row),
        compiler_params=_params(("arbitrary",)),
    )(y_slabs, y_slabs, weights, h, g, b)


def _rope_tables(seq):
    half = HEAD_DIM // 2
    inv1 = ROPE_THETA ** (-jnp.arange(half, dtype=F32) / half)
    ang1 = jnp.arange(seq, dtype=F32)[:, None] * inv1[None, :]
    rows = seq // GRID_W
    r, c = jnp.meshgrid(jnp.arange(rows, dtype=F32), jnp.arange(GRID_W, dtype=F32), indexing='ij')
    n_axis = HEAD_DIM // 4
    inv2 = ROPE_THETA ** (-jnp.arange(n_axis, dtype=F32) / n_axis)
    ang2 = jnp.concatenate([r.reshape(-1, 1) * inv2[None, :], c.reshape(-1, 1) * inv2[None, :]], axis=-1)
    q_scale = HEAD_DIM ** -0.5 * LOG2E
    out = []
    for ang in (ang1, ang2):
        cos = jnp.concatenate([jnp.cos(ang), jnp.cos(ang)], axis=-1)
        sin = jnp.concatenate([-jnp.sin(ang), jnp.sin(ang)], axis=-1)
        out += [cos * q_scale, sin * q_scale, cos, sin]
    return jnp.stack(out, axis=0)


def _route(logits, n_tok, chunks):
    g_logits = logits[:, :N_GROUPS]
    g_prob = jax.nn.softmax(g_logits, axis=-1)
    g_idx = jnp.argmax(g_logits, axis=-1)
    g_gate = jnp.take_along_axis(g_prob, g_idx[:, None], axis=1)[:, 0]
    e_logits = logits[:, N_GROUPS:N_GROUPS + N_EXPERTS].reshape(n_tok, N_GROUPS, EXPERTS_PER_GROUP)
    e_logits = jnp.take_along_axis(e_logits, g_idx[:, None, None], axis=1)[:, 0]
    e_prob = jax.nn.softmax(e_logits, axis=-1)
    top_p, top_i = lax.top_k(e_prob, TOP_K)
    top_p = top_p / jnp.sum(top_p, axis=-1, keepdims=True)
    weights = g_gate[:, None] * top_p
    expert = g_idx[:, None] * EXPERTS_PER_GROUP + top_i

    n_slot = n_tok * TOP_K
    e_flat = expert.reshape(n_slot).astype(I32)
    order = jnp.argsort(e_flat).astype(I32)
    e_sorted = e_flat[order]
    counts = jnp.bincount(e_flat, length=N_EXPERTS).astype(I32)
    starts = jnp.cumsum(counts) - counts
    padded = ((counts + MOE_BLOCK - 1) // MOE_BLOCK) * MOE_BLOCK
    p_ends = jnp.cumsum(padded)
    p_starts = p_ends - padded
    dest = p_starts[e_sorted] + (jnp.arange(n_slot, dtype=I32) - starts[e_sorted])
    buf_len = n_slot + N_EXPERTS * MOE_BLOCK
    slot_at = jnp.zeros((buf_len,), I32).at[dest].set(order)
    tok_at, k_at = slot_at // TOP_K, slot_at % TOP_K
    src_slab = tok_at * chunks
    dst_slab = (k_at * n_tok + tok_at) * chunks

    n_items = N_EXPERTS + n_slot // ITEM_ROWS
    per_expert = (padded + ITEM_ROWS - 1) // ITEM_ROWS
    item_ends = jnp.cumsum(per_expert)
    total = item_ends[-1]
    ids = jnp.arange(n_items, dtype=I32)
    last = jnp.maximum(total - 1, 0)
    live = ids < total
    e_of = jnp.clip(jnp.searchsorted(item_ends, jnp.minimum(ids, last), side='right'), 0, N_EXPERTS - 1).astype(I32)
    chunk = jnp.minimum(ids, last) - (item_ends[e_of] - per_expert[e_of])
    item_row0 = jnp.where(live, p_starts[e_of] + chunk * ITEM_ROWS, 0).astype(I32)
    item_blocks = jnp.where(live, jnp.clip(padded[e_of] // MOE_BLOCK - chunk * ITEM_BLOCKS, 0, ITEM_BLOCKS), 0).astype(I32)
    item_real = jnp.where(live, jnp.clip(counts[e_of] - chunk * ITEM_ROWS, 0, ITEM_ROWS), 0).astype(I32)
    return weights, e_of, item_row0, item_blocks, item_real, src_slab, dst_slab


def kernel(x, w_in, q_norm_g, k_norm_g, w_branch_a, w_branch_b, w_out, ln1_g, ln1_b, w_group, b_group, w_router,
           b_router, w_gate, w_up, w_down, ln2_g, ln2_b):
    batch, seq, d_model = x.shape
    depth = w_in.shape[0]
    n_tok = batch * seq
    dn_alpha = (2 * depth) ** 0.25
    tabs = _rope_tables(seq)
    h = x.reshape(n_tok, d_model)
    for layer in range(depth):
        xb = h.astype(BF16)
        groups = [_proj_a(xb, w_in[layer], tabs, g, batch, seq) for g in range(N_DIL)]
        slots_b = _proj_b(xb, w_in[layer], tabs, q_norm_g[layer][None, :], k_norm_g[layer][None, :], seq)
        gates = _proj_gate(xb, w_in[layer])
        o_a = _attn_a(groups, batch, seq)
        o_b = _attn_b(slots_b, batch, seq)
        merged = _branch_mix(o_a, o_b, gates, w_branch_a[layer].astype(BF16), w_branch_b[layer].astype(BF16))
        w_r = jnp.concatenate([w_group[layer], w_router[layer]], axis=1)
        w_r = jnp.pad(w_r, ((0, 0), (0, LANES - w_r.shape[1])))
        r_hi = w_r.astype(BF16)
        r_lo = (w_r - r_hi.astype(F32)).astype(BF16)
        r_b = jnp.pad(jnp.concatenate([b_group[layer], b_router[layer]]), (0, LANES - N_GROUPS - N_EXPERTS))[None, :]
        h1, h1_slabs, logits = _out_proj(merged, w_out[layer].astype(BF16), h, ln1_g[layer][None, :],
                                         ln1_b[layer][None, :], r_hi, r_lo, r_b, dn_alpha)
        weights, item_expert, item_row0, item_blocks, item_real, src_slab, dst_slab = _route(
            logits, n_tok, d_model // LANES)
        y_slabs = _moe(h1_slabs, w_gate[layer], w_up[layer], w_down[layer], item_expert, item_row0, item_blocks,
                       item_real, src_slab, dst_slab)
        h = _combine(y_slabs, weights, h1, ln2_g[layer][None, :], ln2_b[layer][None, :], dn_alpha)
    return h.reshape(batch, seq, d_model)
```

```python
import functools
import math

import jax
import jax.numpy as jnp
from jax import lax
from jax.experimental import pallas as pl
from jax.experimental.pallas import tpu as pltpu

F32 = jnp.float32
BF16 = jnp.bfloat16
I32 = jnp.int32

HEAD_DIM = 128
ROPE_THETA = 10000.0
GRID_W = 64
DIL_PATTERNS = ((128, 1), (512, 4), (2048, 16))
N_DIL = len(DIL_PATTERNS)
A_HEADS = 8
BAND_HALF = 64
B_Q_HEADS = 16
B_KV_HEADS = 4
B_GROUP = B_Q_HEADS // B_KV_HEADS
N_GROUPS = 4
EXPERTS_PER_GROUP = 8
N_EXPERTS = N_GROUPS * EXPERTS_PER_GROUP
TOP_K = 2
MOE_BLOCK = 128
LN_EPS = 1e-5
RMS_EPS = 1e-6
NEG_BIG = -1e30
LOG2E = math.log2(math.e)

LANES = 128
COL_TILE = 8 * HEAD_DIM
ROW_TILE = 1024
ITEM_ROWS = 1024
ITEM_BLOCKS = ITEM_ROWS // MOE_BLOCK
FF_TILE = 256
DMA_UNROLL = 8
VMEM_LIMIT = 56 * 1024 * 1024


def _params(sem, vmem=VMEM_LIMIT):
    return pltpu.CompilerParams(dimension_semantics=sem, vmem_limit_bytes=vmem)


_T_QA, _T_KA, _T_VA, _T_QB, _T_KVB, _T_GATE = 0, 3, 6, 9, 11, 12
_R1Q, _R1K, _R2Q, _R2K = 0, 2, 4, 6


def _cast_weights_once(w_ref, wbf_ref):
    @pl.when(pl.program_id(1) == 0)
    def _cast():
        wbf_ref[...] = w_ref[...].astype(BF16)


def _project(x_ref, wbf_ref):
    return jnp.dot(x_ref[...], wbf_ref[...], preferred_element_type=F32)


def _head(acc, h):
    return acc[:, h * HEAD_DIM:(h + 1) * HEAD_DIM]


def _rope(a, tab_ref, t):
    return a * tab_ref[t] + pltpu.roll(a, HEAD_DIM // 2, 1) * tab_ref[t + 1]


def _rms(a, g_ref):
    ms = jnp.mean(a * a, axis=-1, keepdims=True)
    return a * lax.rsqrt(ms + RMS_EPS) * g_ref[...]


def _proj_a_kernel(x_ref, w_ref, tab_ref, out_ref, wbf_ref, *perm, dil):
    j = pl.program_id(0)
    _cast_weights_once(w_ref, wbf_ref)
    rows = x_ref.shape[0]

    def emit(fn):
        acc = _project(x_ref, wbf_ref)
        if dil == 1:
            for h in range(A_HEADS):
                out_ref[h, 0] = fn(_head(acc, h)).astype(BF16)
            return
        perm_ref, = perm
        for h in range(A_HEADS):
            perm_ref[h] = fn(_head(acc, h))
        for h in range(A_HEADS):
            for r in range(dil):
                out_ref[h, r] = perm_ref[h, pl.ds(r, rows // dil, stride=dil), :].astype(BF16)

    @pl.when(j == 0)
    def _q():
        emit(lambda a: _rope(a, tab_ref, _R1Q))

    @pl.when(j == 1)
    def _k():
        emit(lambda a: _rope(a, tab_ref, _R1K))

    @pl.when(j == 2)
    def _v():
        emit(lambda a: a)


def _proj_b_kernel(x_ref, w_ref, tab_ref, gq_ref, gk_ref, out_ref, wbf_ref, *, q_tiles):
    j = pl.program_id(0)
    _cast_weights_once(w_ref, wbf_ref)
    heads = COL_TILE // HEAD_DIM

    @pl.when(j < q_tiles)
    def _qb():
        acc = _project(x_ref, wbf_ref)
        for h in range(heads):
            out_ref[h] = _rope(_rms(_head(acc, h), gq_ref), tab_ref, _R2Q).astype(BF16)

    @pl.when(j == q_tiles)
    def _kvb():
        acc = _project(x_ref, wbf_ref)
        for h in range(heads):
            if h < B_KV_HEADS:
                out_ref[h] = _rope(_rms(_head(acc, h), gk_ref), tab_ref, _R2K).astype(BF16)
            else:
                out_ref[h] = _head(acc, h).astype(BF16)


def _proj_gate_kernel(x_ref, w_ref, out_ref, wbf_ref):
    _cast_weights_once(w_ref, wbf_ref)
    acc = _project(x_ref, wbf_ref)
    out_ref[...] = (1.0 / (1.0 + jnp.exp(-acc))).astype(BF16)


def _proj_specs(d_model, seq, first_tile, tile_step):
    seq_blocks = seq // ROW_TILE
    return [
        pl.BlockSpec((ROW_TILE, d_model), lambda j, i: (i, 0)),
        pl.BlockSpec((d_model, COL_TILE), lambda j, i: (0, first_tile + tile_step * j)),
        pl.BlockSpec((8, ROW_TILE, HEAD_DIM), lambda j, i: (0, i % seq_blocks, 0)),
    ]


def _proj_a(xb, w_in, tabs, group, batch, seq):
    n_tok, d_model = xb.shape
    dil = DIL_PATTERNS[group][1]
    sub_len = seq // dil
    seq_blocks = seq // ROW_TILE
    scratch = [pltpu.VMEM((d_model, COL_TILE), BF16)]
    if dil > 1:
        scratch.append(pltpu.VMEM((A_HEADS, ROW_TILE, HEAD_DIM), F32))
    return pl.pallas_call(
        functools.partial(_proj_a_kernel, dil=dil),
        out_shape=jax.ShapeDtypeStruct((3 * A_HEADS, batch, dil, sub_len, HEAD_DIM), BF16),
        grid=(3, n_tok // ROW_TILE),
        in_specs=_proj_specs(d_model, seq, group, N_DIL),
        out_specs=pl.BlockSpec((A_HEADS, None, dil, ROW_TILE // dil, HEAD_DIM),
                               lambda j, i: (j, i // seq_blocks, 0, i % seq_blocks, 0)),
        scratch_shapes=scratch,
        compiler_params=_params(("arbitrary", "arbitrary")),
    )(xb, w_in, tabs)


def _proj_b(xb, w_in, tabs, gq, gk, seq):
    n_tok, d_model = xb.shape
    q_tiles = B_Q_HEADS * HEAD_DIM // COL_TILE
    heads = COL_TILE // HEAD_DIM
    fix = lambda j, i: (0, 0)
    return pl.pallas_call(
        functools.partial(_proj_b_kernel, q_tiles=q_tiles),
        out_shape=jax.ShapeDtypeStruct(((q_tiles + 1) * heads, n_tok, HEAD_DIM), BF16),
        grid=(q_tiles + 1, n_tok // ROW_TILE),
        in_specs=_proj_specs(d_model, seq, _T_QB, 1) + [pl.BlockSpec((1, HEAD_DIM), fix), pl.BlockSpec((1, HEAD_DIM), fix)],
        out_specs=pl.BlockSpec((heads, ROW_TILE, HEAD_DIM), lambda j, i: (j, i, 0)),
        scratch_shapes=[pltpu.VMEM((d_model, COL_TILE), BF16)],
        compiler_params=_params(("arbitrary", "arbitrary")),
    )(xb, w_in, tabs, gq, gk)


def _proj_gate(xb, w_in):
    n_tok, d_model = xb.shape
    n_ct = w_in.shape[1] // COL_TILE - _T_GATE
    return pl.pallas_call(
        _proj_gate_kernel,
        out_shape=jax.ShapeDtypeStruct((n_tok, n_ct * COL_TILE), BF16),
        grid=(n_ct, n_tok // ROW_TILE),
        in_specs=[pl.BlockSpec((ROW_TILE, d_model), lambda j, i: (i, 0)),
                  pl.BlockSpec((d_model, COL_TILE), lambda j, i: (0, _T_GATE + j))],
        out_specs=pl.BlockSpec((ROW_TILE, COL_TILE), lambda j, i: (i, j)),
        scratch_shapes=[pltpu.VMEM((d_model, COL_TILE), BF16)],
        compiler_params=_params(("arbitrary", "arbitrary")),
    )(xb, w_in)


_SUB = 128
_WIN = _SUB + 2 * BAND_HALF
_MERGE_ROWS = 256
_A_UNROLL = 8


def _attn_a_group(g, dil, seq, q_ref, k_ref, v_ref, og, lg):
    sub_len = seq // dil
    per_seq = sub_len // _SUB
    shift = per_seq.bit_length() - 1

    def body(i, carry):
        r = lax.shift_right_logical(i, shift)
        p0 = pl.multiple_of((i & (per_seq - 1)) * _SUB, _SUB)
        start = pl.multiple_of(jnp.clip(p0 - BAND_HALF, 0, sub_len - _WIN), BAND_HALF)
        q = q_ref[r, pl.ds(p0, _SUB), :]
        k = k_ref[r, pl.ds(start, _WIN), :]
        v = v_ref[r, pl.ds(start, _WIN), :]
        s = lax.dot_general(q, k, (((1,), (1,)), ((), ())), preferred_element_type=F32)
        qpos = p0 + lax.broadcasted_iota(I32, (_SUB, _WIN), 0)
        kpos = start + lax.broadcasted_iota(I32, (_SUB, _WIN), 1)
        s = jnp.where(jnp.abs(kpos - qpos) <= BAND_HALF, s, NEG_BIG)
        m = jnp.max(s, axis=1, keepdims=True)
        p = jnp.exp2(s - m)
        l = jnp.sum(p, axis=1, keepdims=True)
        o = jnp.dot(p.astype(BF16), v, preferred_element_type=F32) / l
        lse = jnp.broadcast_to(m + jnp.log2(l), (_SUB, HEAD_DIM))
        if dil == 1:
            rows = pl.ds(p0, _SUB)
        else:
            rows = pl.ds(p0 * dil + r, _SUB, stride=dil)
        og[g, rows, :] = o
        lg[g, rows, :] = lse
        return carry

    lax.fori_loop(0, seq // _SUB, body, 0, unroll=_A_UNROLL)


def _attn_a_kernel(*refs, seq):
    qkv, o_ref, og, lg = refs[:3 * N_DIL], refs[3 * N_DIL], refs[3 * N_DIL + 1], refs[3 * N_DIL + 2]
    for g, (_, dil) in enumerate(DIL_PATTERNS):
        _attn_a_group(g, dil, seq, qkv[3 * g], qkv[3 * g + 1], qkv[3 * g + 2], og, lg)

    def merge(c, carry):
        rows = pl.ds(pl.multiple_of(c * _MERGE_ROWS, _MERGE_ROWS), _MERGE_ROWS)
        ls = [lg[g, rows, :] for g in range(N_DIL)]
        mx = functools.reduce(jnp.maximum, ls)
        ws = [jnp.exp2(l - mx) for l in ls]
        num = functools.reduce(lambda a, b: a + b, [w * og[g, rows, :] for g, w in enumerate(ws)])
        o_ref[rows, :] = (num / functools.reduce(lambda a, b: a + b, ws)).astype(o_ref.dtype)
        return carry

    lax.fori_loop(0, seq // _MERGE_ROWS, merge, 0)


def _attn_a(groups, batch, seq):
    in_specs, operands = [], []
    for g, (_, dil) in enumerate(DIL_PATTERNS):
        for kind in range(3):
            first = kind * A_HEADS
            in_specs.append(pl.BlockSpec((None, None, dil, seq // dil, HEAD_DIM),
                                         lambda b, h, first=first: (first + h, b, 0, 0, 0)))
            operands.append(groups[g])
    return pl.pallas_call(
        functools.partial(_attn_a_kernel, seq=seq),
        out_shape=jax.ShapeDtypeStruct((batch * seq, A_HEADS * HEAD_DIM), BF16),
        grid=(batch, A_HEADS),
        in_specs=in_specs,
        out_specs=pl.BlockSpec((seq, HEAD_DIM), lambda b, h: (b, h)),
        scratch_shapes=[pltpu.VMEM((N_DIL, seq, HEAD_DIM), F32), pltpu.VMEM((N_DIL, seq, HEAD_DIM), F32)],
        compiler_params=_params(("arbitrary", "arbitrary")),
    )(*operands)


_BQ = 256
_BK = 512
SCORE_BOUND = 64.0
BF16_SLACK = 1.02


def _attn_b_kernel(q_ref, k_ref, v_ref, o_ref, *, seq):
    rows = B_GROUP * _BQ
    q = q_ref[...].reshape(rows, HEAD_DIM)

    def body(c, carry):
        m, l, acc = carry
        c0 = pl.multiple_of(c * _BK, _BK)
        k = k_ref[pl.ds(c0, _BK), :]
        v = v_ref[pl.ds(c0, _BK), :]
        s = lax.dot_general(q, k, (((1,), (1,)), ((), ())), preferred_element_type=F32)
        m_new = jnp.maximum(m, jnp.max(s, axis=1, keepdims=True))
        alpha = jnp.exp2(m - m_new)
        p = jnp.exp2(s - m_new)
        l = alpha * l + jnp.sum(p, axis=1, keepdims=True)
        acc = alpha * acc + jnp.dot(p.astype(BF16), v, preferred_element_type=F32)
        return m_new, l, acc

    init = (jnp.full((rows, 1), NEG_BIG, F32), jnp.zeros((rows, 1), F32), jnp.zeros((rows, HEAD_DIM), F32))
    _, l, acc = lax.fori_loop(0, seq // _BK, body, init)
    o = acc / l
    for g in range(B_GROUP):
        o_ref[:, g * HEAD_DIM:(g + 1) * HEAD_DIM] = o[g * _BQ:(g + 1) * _BQ].astype(o_ref.dtype)


def _attn_b_bounded_kernel(q_ref, k_ref, v_ref, o_ref, *, seq):
    rows = B_GROUP * _BQ
    q = q_ref[...].reshape(rows, HEAD_DIM)
    ones = jnp.ones((_BK, HEAD_DIM), BF16)
    acc = jnp.zeros((rows, 2 * HEAD_DIM), F32)
    for c in range(seq // _BK):
        k = k_ref[c * _BK:(c + 1) * _BK, :]
        v = v_ref[c * _BK:(c + 1) * _BK, :]
        s = lax.dot_general(q, k, (((1,), (1,)), ((), ())), preferred_element_type=F32)
        p = jnp.exp2(s).astype(BF16)
        acc = acc + jnp.dot(p, jnp.concatenate([v, ones], axis=1), preferred_element_type=F32)
    o = acc[:, :HEAD_DIM] / acc[:, HEAD_DIM:HEAD_DIM + 1]
    for g in range(B_GROUP):
        o_ref[:, g * HEAD_DIM:(g + 1) * HEAD_DIM] = o[g * _BQ:(g + 1) * _BQ].astype(o_ref.dtype)


def _attn_b(slots, batch, seq, bounded):
    n_tok = batch * seq
    k0 = B_Q_HEADS
    v0 = k0 + B_KV_HEADS
    qblocks = seq // _BQ
    return pl.pallas_call(
        functools.partial(_attn_b_bounded_kernel if bounded else _attn_b_kernel, seq=seq),
        out_shape=jax.ShapeDtypeStruct((n_tok, B_Q_HEADS * HEAD_DIM), BF16),
        grid=(batch, B_KV_HEADS, qblocks),
        in_specs=[
            pl.BlockSpec((B_GROUP, _BQ, HEAD_DIM), lambda b, kv, qi: (kv, b * qblocks + qi, 0)),
            pl.BlockSpec((None, seq, HEAD_DIM), lambda b, kv, qi: (k0 + kv, b, 0)),
            pl.BlockSpec((None, seq, HEAD_DIM), lambda b, kv, qi: (v0 + kv, b, 0)),
        ],
        out_specs=pl.BlockSpec((_BQ, B_GROUP * HEAD_DIM), lambda b, kv, qi: (b * qblocks + qi, kv)),
        compiler_params=_params(("arbitrary", "arbitrary", "arbitrary")),
    )(slots, slots, slots)


_MIX_ROWS = 256


def _branch_kernel(oa_ref, ob_ref, ga_ref, gb_ref, wa_ref, wb_ref, out_ref):
    y_a = jnp.dot(oa_ref[...], wa_ref[...], preferred_element_type=F32)
    y_b = jnp.dot(ob_ref[...], wb_ref[...], preferred_element_type=F32)
    out_ref[...] = (ga_ref[...].astype(F32) * y_a + gb_ref[...].astype(F32) * y_b).astype(BF16)


def _branch_mix(o_a, o_b, gates, wa, wb):
    n_tok = o_b.shape[0]
    d_model = wa.shape[1]
    tm = _MIX_ROWS
    row = lambda i: (i, 0)
    return pl.pallas_call(
        _branch_kernel,
        out_shape=jax.ShapeDtypeStruct((n_tok, d_model), BF16),
        grid=(n_tok // tm,),
        in_specs=[pl.BlockSpec((tm, o_a.shape[1]), row),
                  pl.BlockSpec((tm, o_b.shape[1]), row),
                  pl.BlockSpec((tm, d_model), lambda i: (i, 0)),
                  pl.BlockSpec((tm, d_model), lambda i: (i, 1)),
                  pl.BlockSpec(wa.shape, lambda i: (0, 0)),
                  pl.BlockSpec(wb.shape, lambda i: (0, 0))],
        out_specs=pl.BlockSpec((tm, d_model), row),
        compiler_params=_params(("arbitrary",)),
    )(o_a, o_b, gates, gates, wa, wb)


def _layer_norm(z, g, b):
    mu = jnp.mean(z, axis=-1, keepdims=True)
    zc = z - mu
    var = jnp.mean(zc * zc, axis=-1, keepdims=True)
    return zc * lax.rsqrt(var + LN_EPS) * g + b


def _to_slabs(slab_ref, value):
    rows, d = value.shape
    chunks = d // LANES
    for c in range(chunks):
        slab_ref[pl.ds(c, rows, stride=chunks), :] = value[:, c * LANES:(c + 1) * LANES]


def _from_slabs(slab_ref, rows, chunks, c):
    return slab_ref[pl.ds(c, rows, stride=chunks), :]


def _out_proj_kernel(m_ref, w_ref, x_ref, g_ref, b_ref, rhi_ref, rlo_ref, rb_ref, h_ref, hs_ref, lg_ref, *, alpha):
    mix = jnp.dot(m_ref[...], w_ref[...], preferred_element_type=F32)
    h = _layer_norm(alpha * x_ref[...] + mix, g_ref[...], b_ref[...])
    h_ref[...] = h
    _to_slabs(hs_ref, h)
    hi = h.astype(BF16)
    lo = (h - hi.astype(F32)).astype(BF16)
    lg = jnp.dot(hi, rhi_ref[...], preferred_element_type=F32)
    lg = lg + jnp.dot(lo, rhi_ref[...], preferred_element_type=F32)
    lg = lg + jnp.dot(hi, rlo_ref[...], preferred_element_type=F32)
    lg_ref[...] = lg + rb_ref[...]


def _out_proj(merged, w_out, x2, g, b, r_hi, r_lo, r_b, alpha):
    n_tok, d_model = x2.shape
    tm = _MIX_ROWS
    chunks = d_model // LANES
    row = lambda i: (i, 0)
    fix = lambda i: (0, 0)
    return pl.pallas_call(
        functools.partial(_out_proj_kernel, alpha=alpha),
        out_shape=(jax.ShapeDtypeStruct((n_tok, d_model), F32),
                   jax.ShapeDtypeStruct((n_tok * chunks, LANES), F32),
                   jax.ShapeDtypeStruct((n_tok, LANES), F32)),
        grid=(n_tok // tm,),
        in_specs=[pl.BlockSpec((tm, d_model), row),
                  pl.BlockSpec(w_out.shape, fix),
                  pl.BlockSpec((tm, d_model), row),
                  pl.BlockSpec((1, d_model), fix),
                  pl.BlockSpec((1, d_model), fix),
                  pl.BlockSpec(r_hi.shape, fix),
                  pl.BlockSpec(r_lo.shape, fix),
                  pl.BlockSpec((1, LANES), fix)],
        out_specs=(pl.BlockSpec((tm, d_model), row), pl.BlockSpec((tm * chunks, LANES), row),
                   pl.BlockSpec((tm, LANES), row)),
        compiler_params=_params(("arbitrary",)),
    )(merged, w_out, x2, g, b, r_hi, r_lo, r_b)


_BIG_CHUNK = 512


def _moe_kernel(item_expert, item_row0, item_blocks, item_real, src_slab, dst_slab,
                h_hbm, wg_ref, wu_ref, wd_ref, y_hbm,
                stage, xbf, acc, obuf, wgb, wub, wdb, gsem, ssem):
    it = pl.program_id(0)
    f = pl.program_id(1)
    n_items = pl.num_programs(0)
    n_ff = pl.num_programs(1)
    n_blocks = item_blocks[it]
    d_model = xbf.shape[1]
    chunks = d_model // LANES

    def slab(ref, first):
        return ref.at[pl.ds(pl.multiple_of(first, chunks), chunks)]

    def gather_copy(base, r):
        return pltpu.make_async_copy(slab(h_hbm, src_slab[base + r]), slab(stage, r * chunks), gsem.at[0])

    def scatter_copy(base, r):
        return pltpu.make_async_copy(slab(obuf, r * chunks), slab(y_hbm, dst_slab[base + r]), ssem.at[0])

    def for_rows(n_rows, fn):
        def group(t, c):
            for u in range(DMA_UNROLL):
                fn(t * DMA_UNROLL + u)
            return c
        n_groups = n_rows // DMA_UNROLL
        lax.fori_loop(0, n_groups, group, 0)

        def single(r, c):
            fn(r)
            return c
        lax.fori_loop(n_groups * DMA_UNROLL, n_rows, single, 0)

    def gather(item, op):
        base = item_row0[item]
        for_rows(item_blocks[item] * MOE_BLOCK, lambda r: op(gather_copy(base, r)))

    def scatter(item, op):
        base = item_row0[item]
        for_rows(item_real[item], lambda r: op(scatter_copy(base, r)))

    start = lambda cp: cp.start()
    wait = lambda cp: cp.wait()

    @pl.when((it == 0) & (f == 0))
    def _first_gather():
        gather(it, start)

    @pl.when(f == 0)
    def _gathered():
        gather(it, wait)

        def cast_rows(b, carry):
            r0 = pl.multiple_of(b * MOE_BLOCK, MOE_BLOCK)
            for c in range(chunks):
                piece = stage[pl.ds(r0 * chunks + c, MOE_BLOCK, stride=chunks), :]
                xbf[pl.ds(r0, MOE_BLOCK), c * LANES:(c + 1) * LANES] = piece.astype(BF16)
            return carry
        lax.fori_loop(0, n_blocks, cast_rows, 0)

        @pl.when(it + 1 < n_items)
        def _prefetch_next():
            gather(it + 1, start)

    def chunk(r0, rows, mode):
        xb = xbf[pl.ds(r0, rows), :]
        a = jnp.dot(xb, wgb[...], preferred_element_type=F32)
        u = jnp.dot(xb, wub[...], preferred_element_type=F32)
        hid = (a / (1.0 + jnp.exp(-a)) * u).astype(BF16)
        y = jnp.dot(hid, wdb[...], preferred_element_type=F32)
        if mode == "first":
            acc[pl.ds(r0, rows), :] = y
        elif mode == "middle":
            acc[pl.ds(r0, rows), :] += y
        else:
            if mode == "last":
                y = acc[pl.ds(r0, rows), :] + y
            for c in range(chunks):
                obuf[pl.ds(r0 * chunks + c, rows, stride=chunks), :] = y[:, c * LANES:(c + 1) * LANES]

    def run_item(mode):
        per_big = _BIG_CHUNK // MOE_BLOCK

        def big_chunk(t, c):
            chunk(pl.multiple_of(t * _BIG_CHUNK, _BIG_CHUNK), _BIG_CHUNK, mode)
            return c
        n_big = n_blocks // per_big
        lax.fori_loop(0, n_big, big_chunk, 0)
        rest = n_blocks - n_big * per_big
        for blocks in range(1, per_big):
            @pl.when(rest == blocks)
            def _rest(blocks=blocks):
                chunk(pl.multiple_of(n_big * _BIG_CHUNK, _BIG_CHUNK), blocks * MOE_BLOCK, mode)

    @pl.when(n_blocks > 0)
    def _compute():
        wgb[...] = wg_ref[...].astype(BF16)
        wub[...] = wu_ref[...].astype(BF16)
        wdb[...] = wd_ref[...].astype(BF16)

    last = n_ff - 1

    @pl.when((n_blocks > 0) & (f == last))
    def _final():
        @pl.when(it > 0)
        def _drain_prev():
            scatter(it - 1, wait)
        run_item("only" if n_ff == 1 else "last")
        scatter(it, start)

    if n_ff > 1:
        @pl.when((n_blocks > 0) & (f == 0))
        def _first():
            run_item("first")

        @pl.when((n_blocks > 0) & (f > 0) & (f < last))
        def _middle():
            run_item("middle")

    @pl.when((f == last) & (it == n_items - 1))
    def _drain_last():
        scatter(jnp.where(n_blocks > 0, it, _last_live(item_blocks, n_items)), wait)


def _last_live(item_blocks, n_items):
    def body(i, best):
        return jnp.where(item_blocks[i] > 0, i, best)
    return lax.fori_loop(0, n_items, body, 0)


def _moe(h_slabs, w_gate, w_up, w_down, item_expert, item_row0, item_blocks, item_real, src_slab, dst_slab):
    d_model = w_gate.shape[1]
    chunks = d_model // LANES
    n_tok = h_slabs.shape[0] // chunks
    ff = w_gate.shape[2]
    n_ff = ff // FF_TILE
    n_items = item_expert.shape[0]

    def ff_idx(it, f, blocks):
        return jnp.where(blocks[it] > 0, f, n_ff - 1)

    grid_spec = pltpu.PrefetchScalarGridSpec(
        num_scalar_prefetch=6,
        grid=(n_items, n_ff),
        in_specs=[
            pl.BlockSpec(memory_space=pl.ANY),
            pl.BlockSpec((None, d_model, FF_TILE), lambda it, f, ie, ir, ib, nr, st, dr: (ie[it], 0, ff_idx(it, f, ib))),
            pl.BlockSpec((None, d_model, FF_TILE), lambda it, f, ie, ir, ib, nr, st, dr: (ie[it], 0, ff_idx(it, f, ib))),
            pl.BlockSpec((None, FF_TILE, d_model), lambda it, f, ie, ir, ib, nr, st, dr: (ie[it], ff_idx(it, f, ib), 0)),
        ],
        out_specs=pl.BlockSpec(memory_space=pl.ANY),
        scratch_shapes=[
            pltpu.VMEM((ITEM_ROWS * chunks, LANES), F32),
            pltpu.VMEM((ITEM_ROWS, d_model), BF16),
            pltpu.VMEM((ITEM_ROWS, d_model), F32),
            pltpu.VMEM((ITEM_ROWS * chunks, LANES), F32),
            pltpu.VMEM((d_model, FF_TILE), BF16),
            pltpu.VMEM((d_model, FF_TILE), BF16),
            pltpu.VMEM((FF_TILE, d_model), BF16),
            pltpu.SemaphoreType.DMA((1,)),
            pltpu.SemaphoreType.DMA((1,)),
        ],
    )
    return pl.pallas_call(
        _moe_kernel,
        out_shape=jax.ShapeDtypeStruct((n_tok * TOP_K * chunks, LANES), F32),
        grid_spec=grid_spec,
        compiler_params=_params(("arbitrary", "arbitrary")),
    )(item_expert, item_row0, item_blocks, item_real, src_slab, dst_slab, h_slabs, w_gate, w_up, w_down)


_COMBINE_ROWS = 512


def _combine_kernel(y0_ref, y1_ref, wt_ref, h_ref, g_ref, b_ref, out_ref, *, alpha):
    rows, d_model = h_ref.shape
    chunks = d_model // LANES
    wt = wt_ref[...]
    w0, w1 = wt[:, 0:1], wt[:, 1:2]
    for c in range(chunks):
        cols = slice(c * LANES, (c + 1) * LANES)
        ffn = w0 * _from_slabs(y0_ref, rows, chunks, c) + w1 * _from_slabs(y1_ref, rows, chunks, c)
        out_ref[:, cols] = alpha * h_ref[:, cols] + ffn
    out_ref[...] = _layer_norm(out_ref[...], g_ref[...], b_ref[...])


def _combine(y_slabs, weights, h, g, b, alpha):
    n_tok, d_model = h.shape
    tm = _COMBINE_ROWS
    chunks = d_model // LANES
    second = n_tok // tm
    row = lambda i: (i, 0)
    fix = lambda i: (0, 0)
    return pl.pallas_call(
        functools.partial(_combine_kernel, alpha=alpha),
        out_shape=jax.ShapeDtypeStruct((n_tok, d_model), F32),
        grid=(n_tok // tm,),
        in_specs=[pl.BlockSpec((tm * chunks, LANES), row),
                  pl.BlockSpec((tm * chunks, LANES), lambda i: (second + i, 0)),
                  pl.BlockSpec((tm, TOP_K), row),
                  pl.BlockSpec((tm, d_model), row),
                  pl.BlockSpec((1, d_model), fix),
                  pl.BlockSpec((1, d_model), fix)],
        out_specs=pl.BlockSpec((tm, d_model), row),
        compiler_params=_params(("arbitrary",)),
    )(y_slabs, y_slabs, weights, h, g, b)


def _rope_tables(seq):
    half = HEAD_DIM // 2
    inv1 = ROPE_THETA ** (-jnp.arange(half, dtype=F32) / half)
    ang1 = jnp.arange(seq, dtype=F32)[:, None] * inv1[None, :]
    rows = seq // GRID_W
    r, c = jnp.meshgrid(jnp.arange(rows, dtype=F32), jnp.arange(GRID_W, dtype=F32), indexing='ij')
    n_axis = HEAD_DIM // 4
    inv2 = ROPE_THETA ** (-jnp.arange(n_axis, dtype=F32) / n_axis)
    ang2 = jnp.concatenate([r.reshape(-1, 1) * inv2[None, :], c.reshape(-1, 1) * inv2[None, :]], axis=-1)
    q_scale = HEAD_DIM ** -0.5 * LOG2E
    out = []
    for ang in (ang1, ang2):
        cos = jnp.concatenate([jnp.cos(ang), jnp.cos(ang)], axis=-1)
        sin = jnp.concatenate([-jnp.sin(ang), jnp.sin(ang)], axis=-1)
        out += [cos * q_scale, sin * q_scale, cos, sin]
    return jnp.stack(out, axis=0)


def _route(logits, n_tok, chunks):
    g_logits = logits[:, :N_GROUPS]
    g_prob = jax.nn.softmax(g_logits, axis=-1)
    g_idx = jnp.argmax(g_logits, axis=-1)
    g_gate = jnp.take_along_axis(g_prob, g_idx[:, None], axis=1)[:, 0]
    e_logits = logits[:, N_GROUPS:N_GROUPS + N_EXPERTS].reshape(n_tok, N_GROUPS, EXPERTS_PER_GROUP)
    e_logits = jnp.take_along_axis(e_logits, g_idx[:, None, None], axis=1)[:, 0]
    e_prob = jax.nn.softmax(e_logits, axis=-1)
    top_p, top_i = lax.top_k(e_prob, TOP_K)
    top_p = top_p / jnp.sum(top_p, axis=-1, keepdims=True)
    weights = g_gate[:, None] * top_p
    expert = g_idx[:, None] * EXPERTS_PER_GROUP + top_i

    n_slot = n_tok * TOP_K
    e_flat = expert.reshape(n_slot).astype(I32)
    order = jnp.argsort(e_flat).astype(I32)
    e_sorted = e_flat[order]
    counts = jnp.bincount(e_flat, length=N_EXPERTS).astype(I32)
    starts = jnp.cumsum(counts) - counts
    padded = ((counts + MOE_BLOCK - 1) // MOE_BLOCK) * MOE_BLOCK
    p_ends = jnp.cumsum(padded)
    p_starts = p_ends - padded
    dest = p_starts[e_sorted] + (jnp.arange(n_slot, dtype=I32) - starts[e_sorted])
    buf_len = n_slot + N_EXPERTS * MOE_BLOCK
    slot_at = jnp.zeros((buf_len,), I32).at[dest].set(order)
    tok_at, k_at = slot_at // TOP_K, slot_at % TOP_K
    src_slab = tok_at * chunks
    dst_slab = (k_at * n_tok + tok_at) * chunks

    n_items = N_EXPERTS + n_slot // ITEM_ROWS
    per_expert = (padded + ITEM_ROWS - 1) // ITEM_ROWS
    item_ends = jnp.cumsum(per_expert)
    total = item_ends[-1]
    ids = jnp.arange(n_items, dtype=I32)
    last = jnp.maximum(total - 1, 0)
    live = ids < total
    e_of = jnp.clip(jnp.searchsorted(item_ends, jnp.minimum(ids, last), side='right'), 0, N_EXPERTS - 1).astype(I32)
    chunk = jnp.minimum(ids, last) - (item_ends[e_of] - per_expert[e_of])
    item_row0 = jnp.where(live, p_starts[e_of] + chunk * ITEM_ROWS, 0).astype(I32)
    item_blocks = jnp.where(live, jnp.clip(padded[e_of] // MOE_BLOCK - chunk * ITEM_BLOCKS, 0, ITEM_BLOCKS), 0).astype(I32)
    item_real = jnp.where(live, jnp.clip(counts[e_of] - chunk * ITEM_ROWS, 0, ITEM_ROWS), 0).astype(I32)
    return weights, e_of, item_row0, item_blocks, item_real, src_slab, dst_slab


def kernel(x, w_in, q_norm_g, k_norm_g, w_branch_a, w_branch_b, w_out, ln1_g, ln1_b, w_group, b_group, w_router,
           b_router, w_gate, w_up, w_down, ln2_g, ln2_b):
    batch, seq, d_model = x.shape
    depth = w_in.shape[0]
    n_tok = batch * seq
    dn_alpha = (2 * depth) ** 0.25
    tabs = _rope_tables(seq)
    h = x.reshape(n_tok, d_model)
    for layer in range(depth):
        xb = h.astype(BF16)
        groups = [_proj_a(xb, w_in[layer], tabs, g, batch, seq) for g in range(N_DIL)]
        slots_b = _proj_b(xb, w_in[layer], tabs, q_norm_g[layer][None, :], k_norm_g[layer][None, :], seq)
        gates = _proj_gate(xb, w_in[layer])
        o_a = _attn_a(groups, batch, seq)
        score_bound = (jnp.max(jnp.abs(q_norm_g[layer])) * jnp.max(jnp.abs(k_norm_g[layer]))
                       * (HEAD_DIM ** 0.5 * LOG2E * BF16_SLACK))
        o_b = lax.cond(score_bound <= SCORE_BOUND,
                       lambda s: _attn_b(s, batch, seq, True), lambda s: _attn_b(s, batch, seq, False), slots_b)
        merged = _branch_mix(o_a, o_b, gates, w_branch_a[layer].astype(BF16), w_branch_b[layer].astype(BF16))
        w_r = jnp.concatenate([w_group[layer], w_router[layer]], axis=1)
        w_r = jnp.pad(w_r, ((0, 0), (0, LANES - w_r.shape[1])))
        r_hi = w_r.astype(BF16)
        r_lo = (w_r - r_hi.astype(F32)).astype(BF16)
        r_b = jnp.pad(jnp.concatenate([b_group[layer], b_router[layer]]), (0, LANES - N_GROUPS - N_EXPERTS))[None, :]
        h1, h1_slabs, logits = _out_proj(merged, w_out[layer].astype(BF16), h, ln1_g[layer][None, :],
                                         ln1_b[layer][None, :], r_hi, r_lo, r_b, dn_alpha)
        weights, item_expert, item_row0, item_blocks, item_real, src_slab, dst_slab = _route(
            logits, n_tok, d_model // LANES)
        y_slabs = _moe(h1_slabs, w_gate[layer], w_up[layer], w_down[layer], item_expert, item_row0, item_blocks,
                       item_real, src_slab, dst_slab)
        h = _combine(y_slabs, weights, h1, ln2_g[layer][None, :], ln2_b[layer][None, :], dn_alpha)
    return h.reshape(batch, seq, d_model)
```

```python
import functools
import math

import jax
import jax.numpy as jnp
from jax import lax
from jax.experimental import pallas as pl
from jax.experimental.pallas import tpu as pltpu

F32 = jnp.float32
BF16 = jnp.bfloat16
I32 = jnp.int32

HEAD_DIM = 128
ROPE_THETA = 10000.0
GRID_W = 64
DIL_PATTERNS = ((128, 1), (512, 4), (2048, 16))
N_DIL = len(DIL_PATTERNS)
A_HEADS = 8
BAND_HALF = 64
B_Q_HEADS = 16
B_KV_HEADS = 4
B_GROUP = B_Q_HEADS // B_KV_HEADS
N_GROUPS = 4
EXPERTS_PER_GROUP = 8
N_EXPERTS = N_GROUPS * EXPERTS_PER_GROUP
TOP_K = 2
MOE_BLOCK = 128
LN_EPS = 1e-5
RMS_EPS = 1e-6
NEG_BIG = -1e30
LOG2E = math.log2(math.e)

LANES = 128
COL_TILE = 8 * HEAD_DIM
ROW_TILE = 1024
SUB_ROWS = 256
ITEM_ROWS = 1024
ITEM_BLOCKS = ITEM_ROWS // MOE_BLOCK
FF_TILE = 256
DMA_UNROLL = 8
VMEM_LIMIT = 56 * 1024 * 1024


def _params(sem, vmem=VMEM_LIMIT):
    return pltpu.CompilerParams(dimension_semantics=sem, vmem_limit_bytes=vmem)


_T_QA, _T_KA, _T_VA, _T_QB, _T_KVB, _T_GATE = 0, 3, 6, 9, 11, 12
_R1Q, _R1K, _R2Q, _R2K = 0, 2, 4, 6


def _cast_weights_once(w_ref, wbf_ref):
    @pl.when(pl.program_id(1) == 0)
    def _cast():
        wbf_ref[...] = w_ref[...].astype(BF16)


def _for_sub_tiles(x_ref, wbf_ref, emit):
    for rt in range(x_ref.shape[0] // SUB_ROWS):
        rows = slice(rt * SUB_ROWS, (rt + 1) * SUB_ROWS)
        emit(rt, rows, jnp.dot(x_ref[rows, :], wbf_ref[...], preferred_element_type=F32))


def _head(acc, h):
    return acc[:, h * HEAD_DIM:(h + 1) * HEAD_DIM]


def _rope(a, tab_ref, t, rows):
    return a * tab_ref[t, rows, :] + pltpu.roll(a, HEAD_DIM // 2, 1) * tab_ref[t + 1, rows, :]


def _rms(a, g_ref):
    ms = jnp.mean(a * a, axis=-1, keepdims=True)
    return a * lax.rsqrt(ms + RMS_EPS) * g_ref[...]


def _proj_a_kernel(x_ref, w_ref, tab_ref, out_ref, wbf_ref, *perm, dil):
    j = pl.program_id(0)
    _cast_weights_once(w_ref, wbf_ref)
    per_res = SUB_ROWS // dil

    def emit_with(fn):
        def emit(rt, rows, acc):
            if dil == 1:
                for h in range(A_HEADS):
                    out_ref[h, 0, rows, :] = fn(_head(acc, h), rows).astype(BF16)
                return
            perm_ref, = perm
            for h in range(A_HEADS):
                perm_ref[h, rows, :] = fn(_head(acc, h), rows)
            for h in range(A_HEADS):
                for r in range(dil):
                    piece = perm_ref[h, pl.ds(rt * SUB_ROWS + r, per_res, stride=dil), :]
                    out_ref[h, r, rt * per_res:(rt + 1) * per_res, :] = piece.astype(BF16)
        _for_sub_tiles(x_ref, wbf_ref, emit)

    @pl.when(j == 0)
    def _q():
        emit_with(lambda a, rows: _rope(a, tab_ref, _R1Q, rows))

    @pl.when(j == 1)
    def _k():
        emit_with(lambda a, rows: _rope(a, tab_ref, _R1K, rows))

    @pl.when(j == 2)
    def _v():
        emit_with(lambda a, rows: a)


def _proj_b_kernel(x_ref, w_ref, tab_ref, gq_ref, gk_ref, out_ref, wbf_ref, *, q_tiles):
    j = pl.program_id(0)
    _cast_weights_once(w_ref, wbf_ref)
    heads = COL_TILE // HEAD_DIM

    @pl.when(j < q_tiles)
    def _qb():
        def emit(rt, rows, acc):
            for h in range(heads):
                out_ref[h, rows, :] = _rope(_rms(_head(acc, h), gq_ref), tab_ref, _R2Q, rows).astype(BF16)
        _for_sub_tiles(x_ref, wbf_ref, emit)

    @pl.when(j == q_tiles)
    def _kvb():
        def emit(rt, rows, acc):
            for h in range(heads):
                if h < B_KV_HEADS:
                    out_ref[h, rows, :] = _rope(_rms(_head(acc, h), gk_ref), tab_ref, _R2K, rows).astype(BF16)
                else:
                    out_ref[h, rows, :] = _head(acc, h).astype(BF16)
        _for_sub_tiles(x_ref, wbf_ref, emit)


def _proj_gate_kernel(x_ref, w_ref, out_ref, wbf_ref):
    _cast_weights_once(w_ref, wbf_ref)

    def emit(rt, rows, acc):
        out_ref[rows, :] = (1.0 / (1.0 + jnp.exp(-acc))).astype(BF16)
    _for_sub_tiles(x_ref, wbf_ref, emit)


def _proj_specs(d_model, seq, first_tile, tile_step):
    seq_blocks = seq // ROW_TILE
    return [
        pl.BlockSpec((ROW_TILE, d_model), lambda j, i: (i, 0)),
        pl.BlockSpec((d_model, COL_TILE), lambda j, i: (0, first_tile + tile_step * j)),
        pl.BlockSpec((8, ROW_TILE, HEAD_DIM), lambda j, i: (0, i % seq_blocks, 0)),
    ]


def _proj_a(xb, w_in, tabs, group, batch, seq):
    n_tok, d_model = xb.shape
    dil = DIL_PATTERNS[group][1]
    sub_len = seq // dil
    seq_blocks = seq // ROW_TILE
    scratch = [pltpu.VMEM((d_model, COL_TILE), BF16)]
    if dil > 1:
        scratch.append(pltpu.VMEM((A_HEADS, ROW_TILE, HEAD_DIM), F32))
    return pl.pallas_call(
        functools.partial(_proj_a_kernel, dil=dil),
        out_shape=jax.ShapeDtypeStruct((3 * A_HEADS, batch, dil, sub_len, HEAD_DIM), BF16),
        grid=(3, n_tok // ROW_TILE),
        in_specs=_proj_specs(d_model, seq, group, N_DIL),
        out_specs=pl.BlockSpec((A_HEADS, None, dil, ROW_TILE // dil, HEAD_DIM),
                               lambda j, i: (j, i // seq_blocks, 0, i % seq_blocks, 0)),
        scratch_shapes=scratch,
        compiler_params=_params(("arbitrary", "arbitrary")),
    )(xb, w_in, tabs)


def _proj_b(xb, w_in, tabs, gq, gk, seq):
    n_tok, d_model = xb.shape
    q_tiles = B_Q_HEADS * HEAD_DIM // COL_TILE
    heads = COL_TILE // HEAD_DIM
    fix = lambda j, i: (0, 0)
    return pl.pallas_call(
        functools.partial(_proj_b_kernel, q_tiles=q_tiles),
        out_shape=jax.ShapeDtypeStruct(((q_tiles + 1) * heads, n_tok, HEAD_DIM), BF16),
        grid=(q_tiles + 1, n_tok // ROW_TILE),
        in_specs=_proj_specs(d_model, seq, _T_QB, 1) + [pl.BlockSpec((1, HEAD_DIM), fix), pl.BlockSpec((1, HEAD_DIM), fix)],
        out_specs=pl.BlockSpec((heads, ROW_TILE, HEAD_DIM), lambda j, i: (j, i, 0)),
        scratch_shapes=[pltpu.VMEM((d_model, COL_TILE), BF16)],
        compiler_params=_params(("arbitrary", "arbitrary")),
    )(xb, w_in, tabs, gq, gk)


def _proj_gate(xb, w_in):
    n_tok, d_model = xb.shape
    n_ct = w_in.shape[1] // COL_TILE - _T_GATE
    return pl.pallas_call(
        _proj_gate_kernel,
        out_shape=jax.ShapeDtypeStruct((n_tok, n_ct * COL_TILE), BF16),
        grid=(n_ct, n_tok // ROW_TILE),
        in_specs=[pl.BlockSpec((ROW_TILE, d_model), lambda j, i: (i, 0)),
                  pl.BlockSpec((d_model, COL_TILE), lambda j, i: (0, _T_GATE + j))],
        out_specs=pl.BlockSpec((ROW_TILE, COL_TILE), lambda j, i: (i, j)),
        scratch_shapes=[pltpu.VMEM((d_model, COL_TILE), BF16)],
        compiler_params=_params(("arbitrary", "arbitrary")),
    )(xb, w_in)


_SUB = 128
_WIN = _SUB + 2 * BAND_HALF
_MERGE_ROWS = 256
_A_UNROLL = 8


def _attn_a_group(g, dil, seq, q_ref, k_ref, v_ref, og, lg):
    sub_len = seq // dil
    per_seq = sub_len // _SUB
    shift = per_seq.bit_length() - 1

    def body(i, carry):
        r = lax.shift_right_logical(i, shift)
        p0 = pl.multiple_of((i & (per_seq - 1)) * _SUB, _SUB)
        start = pl.multiple_of(jnp.clip(p0 - BAND_HALF, 0, sub_len - _WIN), BAND_HALF)
        q = q_ref[r, pl.ds(p0, _SUB), :]
        k = k_ref[r, pl.ds(start, _WIN), :]
        v = v_ref[r, pl.ds(start, _WIN), :]
        s = lax.dot_general(q, k, (((1,), (1,)), ((), ())), preferred_element_type=F32)
        qpos = p0 + lax.broadcasted_iota(I32, (_SUB, _WIN), 0)
        kpos = start + lax.broadcasted_iota(I32, (_SUB, _WIN), 1)
        s = jnp.where(jnp.abs(kpos - qpos) <= BAND_HALF, s, NEG_BIG)
        m = jnp.max(s, axis=1, keepdims=True)
        p = jnp.exp2(s - m)
        l = jnp.sum(p, axis=1, keepdims=True)
        o = jnp.dot(p.astype(BF16), v, preferred_element_type=F32) / l
        lse = jnp.broadcast_to(m + jnp.log2(l), (_SUB, HEAD_DIM))
        if dil == 1:
            rows = pl.ds(p0, _SUB)
        else:
            rows = pl.ds(p0 * dil + r, _SUB, stride=dil)
        og[g, rows, :] = o
        lg[g, rows, :] = lse
        return carry

    lax.fori_loop(0, seq // _SUB, body, 0, unroll=_A_UNROLL)


def _attn_a_kernel(*refs, seq):
    qkv, o_ref, og, lg = refs[:3 * N_DIL], refs[3 * N_DIL], refs[3 * N_DIL + 1], refs[3 * N_DIL + 2]
    for g, (_, dil) in enumerate(DIL_PATTERNS):
        _attn_a_group(g, dil, seq, qkv[3 * g], qkv[3 * g + 1], qkv[3 * g + 2], og, lg)

    def merge(c, carry):
        rows = pl.ds(pl.multiple_of(c * _MERGE_ROWS, _MERGE_ROWS), _MERGE_ROWS)
        ls = [lg[g, rows, :] for g in range(N_DIL)]
        mx = functools.reduce(jnp.maximum, ls)
        ws = [jnp.exp2(l - mx) for l in ls]
        num = functools.reduce(lambda a, b: a + b, [w * og[g, rows, :] for g, w in enumerate(ws)])
        o_ref[rows, :] = (num / functools.reduce(lambda a, b: a + b, ws)).astype(o_ref.dtype)
        return carry

    lax.fori_loop(0, seq // _MERGE_ROWS, merge, 0)


def _attn_a(groups, batch, seq):
    in_specs, operands = [], []
    for g, (_, dil) in enumerate(DIL_PATTERNS):
        for kind in range(3):
            first = kind * A_HEADS
            in_specs.append(pl.BlockSpec((None, None, dil, seq // dil, HEAD_DIM),
                                         lambda b, h, first=first: (first + h, b, 0, 0, 0)))
            operands.append(groups[g])
    return pl.pallas_call(
        functools.partial(_attn_a_kernel, seq=seq),
        out_shape=jax.ShapeDtypeStruct((batch * seq, A_HEADS * HEAD_DIM), BF16),
        grid=(batch, A_HEADS),
        in_specs=in_specs,
        out_specs=pl.BlockSpec((seq, HEAD_DIM), lambda b, h: (b, h)),
        scratch_shapes=[pltpu.VMEM((N_DIL, seq, HEAD_DIM), F32), pltpu.VMEM((N_DIL, seq, HEAD_DIM), F32)],
        compiler_params=_params(("arbitrary", "arbitrary")),
    )(*operands)


_BQ = 256
_BK = 512
SCORE_BOUND = 64.0
BF16_SLACK = 1.02


def _attn_b_kernel(q_ref, k_ref, v_ref, o_ref, *, seq):
    rows = B_GROUP * _BQ
    q = q_ref[...].reshape(rows, HEAD_DIM)

    def body(c, carry):
        m, l, acc = carry
        c0 = pl.multiple_of(c * _BK, _BK)
        k = k_ref[pl.ds(c0, _BK), :]
        v = v_ref[pl.ds(c0, _BK), :]
        s = lax.dot_general(q, k, (((1,), (1,)), ((), ())), preferred_element_type=F32)
        m_new = jnp.maximum(m, jnp.max(s, axis=1, keepdims=True))
        alpha = jnp.exp2(m - m_new)
        p = jnp.exp2(s - m_new)
        l = alpha * l + jnp.sum(p, axis=1, keepdims=True)
        acc = alpha * acc + jnp.dot(p.astype(BF16), v, preferred_element_type=F32)
        return m_new, l, acc

    init = (jnp.full((rows, 1), NEG_BIG, F32), jnp.zeros((rows, 1), F32), jnp.zeros((rows, HEAD_DIM), F32))
    _, l, acc = lax.fori_loop(0, seq // _BK, body, init)
    o = acc / l
    for g in range(B_GROUP):
        o_ref[:, g * HEAD_DIM:(g + 1) * HEAD_DIM] = o[g * _BQ:(g + 1) * _BQ].astype(o_ref.dtype)


def _attn_b_bounded_kernel(q_ref, k_ref, v_ref, o_ref, *, seq):
    rows = B_GROUP * _BQ
    q = q_ref[...].reshape(rows, HEAD_DIM)
    ones = jnp.ones((_BK, HEAD_DIM), BF16)
    acc = jnp.zeros((rows, 2 * HEAD_DIM), F32)
    for c in range(seq // _BK):
        k = k_ref[c * _BK:(c + 1) * _BK, :]
        v = v_ref[c * _BK:(c + 1) * _BK, :]
        s = lax.dot_general(q, k, (((1,), (1,)), ((), ())), preferred_element_type=F32)
        p = jnp.exp2(s).astype(BF16)
        acc = acc + jnp.dot(p, jnp.concatenate([v, ones], axis=1), preferred_element_type=F32)
    o = acc[:, :HEAD_DIM] / acc[:, HEAD_DIM:HEAD_DIM + 1]
    for g in range(B_GROUP):
        o_ref[:, g * HEAD_DIM:(g + 1) * HEAD_DIM] = o[g * _BQ:(g + 1) * _BQ].astype(o_ref.dtype)


def _attn_b(slots, batch, seq, bounded):
    n_tok = batch * seq
    k0 = B_Q_HEADS
    v0 = k0 + B_KV_HEADS
    qblocks = seq // _BQ
    return pl.pallas_call(
        functools.partial(_attn_b_bounded_kernel if bounded else _attn_b_kernel, seq=seq),
        out_shape=jax.ShapeDtypeStruct((n_tok, B_Q_HEADS * HEAD_DIM), BF16),
        grid=(batch, B_KV_HEADS, qblocks),
        in_specs=[
            pl.BlockSpec((B_GROUP, _BQ, HEAD_DIM), lambda b, kv, qi: (kv, b * qblocks + qi, 0)),
            pl.BlockSpec((None, seq, HEAD_DIM), lambda b, kv, qi: (k0 + kv, b, 0)),
            pl.BlockSpec((None, seq, HEAD_DIM), lambda b, kv, qi: (v0 + kv, b, 0)),
        ],
        out_specs=pl.BlockSpec((_BQ, B_GROUP * HEAD_DIM), lambda b, kv, qi: (b * qblocks + qi, kv)),
        compiler_params=_params(("arbitrary", "arbitrary", "arbitrary")),
    )(slots, slots, slots)


_MIX_ROWS = 512


def _branch_kernel(oa_ref, ob_ref, ga_ref, gb_ref, wa_ref, wb_ref, out_ref):
    for rt in range(out_ref.shape[0] // SUB_ROWS):
        rows = slice(rt * SUB_ROWS, (rt + 1) * SUB_ROWS)
        y_a = jnp.dot(oa_ref[rows, :], wa_ref[...], preferred_element_type=F32)
        y_b = jnp.dot(ob_ref[rows, :], wb_ref[...], preferred_element_type=F32)
        out_ref[rows, :] = (ga_ref[rows, :].astype(F32) * y_a + gb_ref[rows, :].astype(F32) * y_b).astype(BF16)


def _branch_mix(o_a, o_b, gates, wa, wb):
    n_tok = o_b.shape[0]
    d_model = wa.shape[1]
    tm = _MIX_ROWS
    row = lambda i: (i, 0)
    return pl.pallas_call(
        _branch_kernel,
        out_shape=jax.ShapeDtypeStruct((n_tok, d_model), BF16),
        grid=(n_tok // tm,),
        in_specs=[pl.BlockSpec((tm, o_a.shape[1]), row),
                  pl.BlockSpec((tm, o_b.shape[1]), row),
                  pl.BlockSpec((tm, d_model), lambda i: (i, 0)),
                  pl.BlockSpec((tm, d_model), lambda i: (i, 1)),
                  pl.BlockSpec(wa.shape, lambda i: (0, 0)),
                  pl.BlockSpec(wb.shape, lambda i: (0, 0))],
        out_specs=pl.BlockSpec((tm, d_model), row),
        compiler_params=_params(("arbitrary",)),
    )(o_a, o_b, gates, gates, wa, wb)


def _layer_norm(z, g, b):
    mu = jnp.mean(z, axis=-1, keepdims=True)
    zc = z - mu
    var = jnp.mean(zc * zc, axis=-1, keepdims=True)
    return zc * lax.rsqrt(var + LN_EPS) * g + b


def _to_slabs(slab_ref, value, row0):
    rows, d = value.shape
    chunks = d // LANES
    for c in range(chunks):
        slab_ref[pl.ds(row0 * chunks + c, rows, stride=chunks), :] = value[:, c * LANES:(c + 1) * LANES]


def _from_slabs(slab_ref, rows, chunks, c):
    return slab_ref[pl.ds(c, rows, stride=chunks), :]


def _out_proj_kernel(m_ref, w_ref, x_ref, g_ref, b_ref, rhi_ref, rlo_ref, rb_ref, h_ref, hs_ref, lg_ref, *, alpha):
    for rt in range(h_ref.shape[0] // SUB_ROWS):
        rows = slice(rt * SUB_ROWS, (rt + 1) * SUB_ROWS)
        mix = jnp.dot(m_ref[rows, :], w_ref[...], preferred_element_type=F32)
        h = _layer_norm(alpha * x_ref[rows, :] + mix, g_ref[...], b_ref[...])
        h_ref[rows, :] = h
        _to_slabs(hs_ref, h, rt * SUB_ROWS)
        hi = h.astype(BF16)
        lo = (h - hi.astype(F32)).astype(BF16)
        lg = jnp.dot(hi, rhi_ref[...], preferred_element_type=F32)
        lg = lg + jnp.dot(lo, rhi_ref[...], preferred_element_type=F32)
        lg = lg + jnp.dot(hi, rlo_ref[...], preferred_element_type=F32)
        lg_ref[rows, :] = lg + rb_ref[...]


def _out_proj(merged, w_out, x2, g, b, r_hi, r_lo, r_b, alpha):
    n_tok, d_model = x2.shape
    tm = _MIX_ROWS
    chunks = d_model // LANES
    row = lambda i: (i, 0)
    fix = lambda i: (0, 0)
    return pl.pallas_call(
        functools.partial(_out_proj_kernel, alpha=alpha),
        out_shape=(jax.ShapeDtypeStruct((n_tok, d_model), F32),
                   jax.ShapeDtypeStruct((n_tok * chunks, LANES), F32),
                   jax.ShapeDtypeStruct((n_tok, LANES), F32)),
        grid=(n_tok // tm,),
        in_specs=[pl.BlockSpec((tm, d_model), row),
                  pl.BlockSpec(w_out.shape, fix),
                  pl.BlockSpec((tm, d_model), row),
                  pl.BlockSpec((1, d_model), fix),
                  pl.BlockSpec((1, d_model), fix),
                  pl.BlockSpec(r_hi.shape, fix),
                  pl.BlockSpec(r_lo.shape, fix),
                  pl.BlockSpec((1, LANES), fix)],
        out_specs=(pl.BlockSpec((tm, d_model), row), pl.BlockSpec((tm * chunks, LANES), row),
                   pl.BlockSpec((tm, LANES), row)),
        compiler_params=_params(("arbitrary",)),
    )(merged, w_out, x2, g, b, r_hi, r_lo, r_b)


_BIG_CHUNK = 512


def _moe_kernel(item_expert, item_row0, item_blocks, item_real, src_slab, dst_slab,
                h_hbm, wg_ref, wu_ref, wd_ref, y_hbm,
                stage, xbf, acc, obuf, wgb, wub, wdb, gsem, ssem):
    it = pl.program_id(0)
    f = pl.program_id(1)
    n_items = pl.num_programs(0)
    n_ff = pl.num_programs(1)
    n_blocks = item_blocks[it]
    d_model = xbf.shape[1]
    chunks = d_model // LANES

    def slab(ref, first):
        return ref.at[pl.ds(pl.multiple_of(first, chunks), chunks)]

    def gather_copy(base, r):
        return pltpu.make_async_copy(slab(h_hbm, src_slab[base + r]), slab(stage, r * chunks), gsem.at[0])

    def scatter_copy(base, r):
        return pltpu.make_async_copy(slab(obuf, r * chunks), slab(y_hbm, dst_slab[base + r]), ssem.at[0])

    def for_rows(n_rows, fn):
        def group(t, c):
            for u in range(DMA_UNROLL):
                fn(t * DMA_UNROLL + u)
            return c
        n_groups = n_rows // DMA_UNROLL
        lax.fori_loop(0, n_groups, group, 0)

        def single(r, c):
            fn(r)
            return c
        lax.fori_loop(n_groups * DMA_UNROLL, n_rows, single, 0)

    def gather(item, op):
        base = item_row0[item]
        for_rows(item_blocks[item] * MOE_BLOCK, lambda r: op(gather_copy(base, r)))

    def scatter(item, op):
        base = item_row0[item]
        for_rows(item_real[item], lambda r: op(scatter_copy(base, r)))

    start = lambda cp: cp.start()
    wait = lambda cp: cp.wait()

    @pl.when((it == 0) & (f == 0))
    def _first_gather():
        gather(it, start)

    @pl.when(f == 0)
    def _gathered():
        gather(it, wait)

        def cast_rows(b, carry):
            r0 = pl.multiple_of(b * MOE_BLOCK, MOE_BLOCK)
            for c in range(chunks):
                piece = stage[pl.ds(r0 * chunks + c, MOE_BLOCK, stride=chunks), :]
                xbf[pl.ds(r0, MOE_BLOCK), c * LANES:(c + 1) * LANES] = piece.astype(BF16)
            return carry
        lax.fori_loop(0, n_blocks, cast_rows, 0)

        @pl.when(it + 1 < n_items)
        def _prefetch_next():
            gather(it + 1, start)

    def chunk(r0, rows, mode):
        xb = xbf[pl.ds(r0, rows), :]
        a = jnp.dot(xb, wgb[...], preferred_element_type=F32)
        u = jnp.dot(xb, wub[...], preferred_element_type=F32)
        hid = (a / (1.0 + jnp.exp(-a)) * u).astype(BF16)
        y = jnp.dot(hid, wdb[...], preferred_element_type=F32)
        if mode == "first":
            acc[pl.ds(r0, rows), :] = y
        elif mode == "middle":
            acc[pl.ds(r0, rows), :] += y
        else:
            if mode == "last":
                y = acc[pl.ds(r0, rows), :] + y
            for c in range(chunks):
                obuf[pl.ds(r0 * chunks + c, rows, stride=chunks), :] = y[:, c * LANES:(c + 1) * LANES]

    def run_item(mode):
        per_big = _BIG_CHUNK // MOE_BLOCK

        def big_chunk(t, c):
            chunk(pl.multiple_of(t * _BIG_CHUNK, _BIG_CHUNK), _BIG_CHUNK, mode)
            return c
        n_big = n_blocks // per_big
        lax.fori_loop(0, n_big, big_chunk, 0)
        rest = n_blocks - n_big * per_big
        for blocks in range(1, per_big):
            @pl.when(rest == blocks)
            def _rest(blocks=blocks):
                chunk(pl.multiple_of(n_big * _BIG_CHUNK, _BIG_CHUNK), blocks * MOE_BLOCK, mode)

    @pl.when(n_blocks > 0)
    def _compute():
        wgb[...] = wg_ref[...].astype(BF16)
        wub[...] = wu_ref[...].astype(BF16)
        wdb[...] = wd_ref[...].astype(BF16)

    last = n_ff - 1

    @pl.when((n_blocks > 0) & (f == last))
    def _final():
        @pl.when(it > 0)
        def _drain_prev():
            scatter(it - 1, wait)
        run_item("only" if n_ff == 1 else "last")
        scatter(it, start)

    if n_ff > 1:
        @pl.when((n_blocks > 0) & (f == 0))
        def _first():
            run_item("first")

        @pl.when((n_blocks > 0) & (f > 0) & (f < last))
        def _middle():
            run_item("middle")

    @pl.when((f == last) & (it == n_items - 1))
    def _drain_last():
        scatter(jnp.where(n_blocks > 0, it, _last_live(item_blocks, n_items)), wait)


def _last_live(item_blocks, n_items):
    def body(i, best):
        return jnp.where(item_blocks[i] > 0, i, best)
    return lax.fori_loop(0, n_items, body, 0)


def _moe(h_slabs, w_gate, w_up, w_down, item_expert, item_row0, item_blocks, item_real, src_slab, dst_slab):
    d_model = w_gate.shape[1]
    chunks = d_model // LANES
    n_tok = h_slabs.shape[0] // chunks
    ff = w_gate.shape[2]
    n_ff = ff // FF_TILE
    n_items = item_expert.shape[0]

    def ff_idx(it, f, blocks):
        return jnp.where(blocks[it] > 0, f, n_ff - 1)

    grid_spec = pltpu.PrefetchScalarGridSpec(
        num_scalar_prefetch=6,
        grid=(n_items, n_ff),
        in_specs=[
            pl.BlockSpec(memory_space=pl.ANY),
            pl.BlockSpec((None, d_model, FF_TILE), lambda it, f, ie, ir, ib, nr, st, dr: (ie[it], 0, ff_idx(it, f, ib))),
            pl.BlockSpec((None, d_model, FF_TILE), lambda it, f, ie, ir, ib, nr, st, dr: (ie[it], 0, ff_idx(it, f, ib))),
            pl.BlockSpec((None, FF_TILE, d_model), lambda it, f, ie, ir, ib, nr, st, dr: (ie[it], ff_idx(it, f, ib), 0)),
        ],
        out_specs=pl.BlockSpec(memory_space=pl.ANY),
        scratch_shapes=[
            pltpu.VMEM((ITEM_ROWS * chunks, LANES), F32),
            pltpu.VMEM((ITEM_ROWS, d_model), BF16),
            pltpu.VMEM((ITEM_ROWS, d_model), F32),
            pltpu.VMEM((ITEM_ROWS * chunks, LANES), F32),
            pltpu.VMEM((d_model, FF_TILE), BF16),
            pltpu.VMEM((d_model, FF_TILE), BF16),
            pltpu.VMEM((FF_TILE, d_model), BF16),
            pltpu.SemaphoreType.DMA((1,)),
            pltpu.SemaphoreType.DMA((1,)),
        ],
    )
    return pl.pallas_call(
        _moe_kernel,
        out_shape=jax.ShapeDtypeStruct((n_tok * TOP_K * chunks, LANES), F32),
        grid_spec=grid_spec,
        compiler_params=_params(("arbitrary", "arbitrary")),
    )(item_expert, item_row0, item_blocks, item_real, src_slab, dst_slab, h_slabs, w_gate, w_up, w_down)


_COMBINE_ROWS = 512


def _combine_kernel(y0_ref, y1_ref, wt_ref, h_ref, g_ref, b_ref, out_ref, *, alpha):
    rows, d_model = h_ref.shape
    chunks = d_model // LANES
    wt = wt_ref[...]
    w0, w1 = wt[:, 0:1], wt[:, 1:2]
    for c in range(chunks):
        cols = slice(c * LANES, (c + 1) * LANES)
        ffn = w0 * _from_slabs(y0_ref, rows, chunks, c) + w1 * _from_slabs(y1_ref, rows, chunks, c)
        out_ref[:, cols] = alpha * h_ref[:, cols] + ffn
    out_ref[...] = _layer_norm(out_ref[...], g_ref[...], b_ref[...])


def _combine(y_slabs, weights, h, g, b, alpha):
    n_tok, d_model = h.shape
    tm = _COMBINE_ROWS
    chunks = d_model // LANES
    second = n_tok // tm
    row = lambda i: (i, 0)
    fix = lambda i: (0, 0)
    return pl.pallas_call(
        functools.partial(_combine_kernel, alpha=alpha),
        out_shape=jax.ShapeDtypeStruct((n_tok, d_model), F32),
        grid=(n_tok // tm,),
        in_specs=[pl.BlockSpec((tm * chunks, LANES), row),
                  pl.BlockSpec((tm * chunks, LANES), lambda i: (second + i, 0)),
                  pl.BlockSpec((tm, TOP_K), row),
                  pl.BlockSpec((tm, d_model), row),
                  pl.BlockSpec((1, d_model), fix),
                  pl.BlockSpec((1, d_model), fix)],
        out_specs=pl.BlockSpec((tm, d_model), row),
        compiler_params=_params(("arbitrary",)),
    )(y_slabs, y_slabs, weights, h, g, b)


def _rope_tables(seq):
    half = HEAD_DIM // 2
    inv1 = ROPE_THETA ** (-jnp.arange(half, dtype=F32) / half)
    ang1 = jnp.arange(seq, dtype=F32)[:, None] * inv1[None, :]
    rows = seq // GRID_W
    r, c = jnp.meshgrid(jnp.arange(rows, dtype=F32), jnp.arange(GRID_W, dtype=F32), indexing='ij')
    n_axis = HEAD_DIM // 4
    inv2 = ROPE_THETA ** (-jnp.arange(n_axis, dtype=F32) / n_axis)
    ang2 = jnp.concatenate([r.reshape(-1, 1) * inv2[None, :], c.reshape(-1, 1) * inv2[None, :]], axis=-1)
    q_scale = HEAD_DIM ** -0.5 * LOG2E
    out = []
    for ang in (ang1, ang2):
        cos = jnp.concatenate([jnp.cos(ang), jnp.cos(ang)], axis=-1)
        sin = jnp.concatenate([-jnp.sin(ang), jnp.sin(ang)], axis=-1)
        out += [cos * q_scale, sin * q_scale, cos, sin]
    return jnp.stack(out, axis=0)


def _route(logits, n_tok, chunks):
    g_logits = logits[:, :N_GROUPS]
    g_prob = jax.nn.softmax(g_logits, axis=-1)
    g_idx = jnp.argmax(g_logits, axis=-1)
    g_gate = jnp.take_along_axis(g_prob, g_idx[:, None], axis=1)[:, 0]
    e_logits = logits[:, N_GROUPS:N_GROUPS + N_EXPERTS].reshape(n_tok, N_GROUPS, EXPERTS_PER_GROUP)
    e_logits = jnp.take_along_axis(e_logits, g_idx[:, None, None], axis=1)[:, 0]
    e_prob = jax.nn.softmax(e_logits, axis=-1)
    top_p, top_i = lax.top_k(e_prob, TOP_K)
    top_p = top_p / jnp.sum(top_p, axis=-1, keepdims=True)
    weights = g_gate[:, None] * top_p
    expert = g_idx[:, None] * EXPERTS_PER_GROUP + top_i

    n_slot = n_tok * TOP_K
    e_flat = expert.reshape(n_slot).astype(I32)
    order = jnp.argsort(e_flat).astype(I32)
    e_sorted = e_flat[order]
    counts = jnp.bincount(e_flat, length=N_EXPERTS).astype(I32)
    starts = jnp.cumsum(counts) - counts
    padded = ((counts + MOE_BLOCK - 1) // MOE_BLOCK) * MOE_BLOCK
    p_ends = jnp.cumsum(padded)
    p_starts = p_ends - padded
    dest = p_starts[e_sorted] + (jnp.arange(n_slot, dtype=I32) - starts[e_sorted])
    buf_len = n_slot + N_EXPERTS * MOE_BLOCK
    slot_at = jnp.zeros((buf_len,), I32).at[dest].set(order)
    tok_at, k_at = slot_at // TOP_K, slot_at % TOP_K
    src_slab = tok_at * chunks
    dst_slab = (k_at * n_tok + tok_at) * chunks

    n_items = N_EXPERTS + n_slot // ITEM_ROWS
    per_expert = (padded + ITEM_ROWS - 1) // ITEM_ROWS
    item_ends = jnp.cumsum(per_expert)
    total = item_ends[-1]
    ids = jnp.arange(n_items, dtype=I32)
    last = jnp.maximum(total - 1, 0)
    live = ids < total
    e_of = jnp.clip(jnp.searchsorted(item_ends, jnp.minimum(ids, last), side='right'), 0, N_EXPERTS - 1).astype(I32)
    chunk = jnp.minimum(ids, last) - (item_ends[e_of] - per_expert[e_of])
    item_row0 = jnp.where(live, p_starts[e_of] + chunk * ITEM_ROWS, 0).astype(I32)
    item_blocks = jnp.where(live, jnp.clip(padded[e_of] // MOE_BLOCK - chunk * ITEM_BLOCKS, 0, ITEM_BLOCKS), 0).astype(I32)
    item_real = jnp.where(live, jnp.clip(counts[e_of] - chunk * ITEM_ROWS, 0, ITEM_ROWS), 0).astype(I32)
    return weights, e_of, item_row0, item_blocks, item_real, src_slab, dst_slab


def kernel(x, w_in, q_norm_g, k_norm_g, w_branch_a, w_branch_b, w_out, ln1_g, ln1_b, w_group, b_group, w_router,
           b_router, w_gate, w_up, w_down, ln2_g, ln2_b):
    batch, seq, d_model = x.shape
    depth = w_in.shape[0]
    n_tok = batch * seq
    dn_alpha = (2 * depth) ** 0.25
    tabs = _rope_tables(seq)
    h = x.reshape(n_tok, d_model)
    for layer in range(depth):
        xb = h.astype(BF16)
        groups = [_proj_a(xb, w_in[layer], tabs, g, batch, seq) for g in range(N_DIL)]
        slots_b = _proj_b(xb, w_in[layer], tabs, q_norm_g[layer][None, :], k_norm_g[layer][None, :], seq)
        gates = _proj_gate(xb, w_in[layer])
        o_a = _attn_a(groups, batch, seq)
        score_bound = (jnp.max(jnp.abs(q_norm_g[layer])) * jnp.max(jnp.abs(k_norm_g[layer]))
                       * (HEAD_DIM ** 0.5 * LOG2E * BF16_SLACK))
        o_b = lax.cond(score_bound <= SCORE_BOUND,
                       lambda s: _attn_b(s, batch, seq, True), lambda s: _attn_b(s, batch, seq, False), slots_b)
        merged = _branch_mix(o_a, o_b, gates, w_branch_a[layer].astype(BF16), w_branch_b[layer].astype(BF16))
        w_r = jnp.concatenate([w_group[layer], w_router[layer]], axis=1)
        w_r = jnp.pad(w_r, ((0, 0), (0, LANES - w_r.shape[1])))
        r_hi = w_r.astype(BF16)
        r_lo = (w_r - r_hi.astype(F32)).astype(BF16)
        r_b = jnp.pad(jnp.concatenate([b_group[layer], b_router[layer]]), (0, LANES - N_GROUPS - N_EXPERTS))[None, :]
        h1, h1_slabs, logits = _out_proj(merged, w_out[layer].astype(BF16), h, ln1_g[layer][None, :],
                                         ln1_b[layer][None, :], r_hi, r_lo, r_b, dn_alpha)
        weights, item_expert, item_row0, item_blocks, item_real, src_slab, dst_slab = _route(
            logits, n_tok, d_model // LANES)
        y_slabs = _moe(h1_slabs, w_gate[layer], w_up[layer], w_down[layer], item_expert, item_row0, item_blocks,
                       item_real, src_slab, dst_slab)
        h = _combine(y_slabs, weights, h1, ln2_g[layer][None, :], ln2_b[layer][None, :], dn_alpha)
    return h.reshape(batch, seq, d_model)
```

```python
import functools
import math

import jax
import jax.numpy as jnp
from jax import lax
from jax.experimental import pallas as pl
from jax.experimental.pallas import tpu as pltpu

F32 = jnp.float32
BF16 = jnp.bfloat16
I32 = jnp.int32

HEAD_DIM = 128
ROPE_THETA = 10000.0
GRID_W = 64
DIL_PATTERNS = ((128, 1), (512, 4), (2048, 16))
N_DIL = len(DIL_PATTERNS)
A_HEADS = 8
BAND_HALF = 64
B_Q_HEADS = 16
B_KV_HEADS = 4
B_GROUP = B_Q_HEADS // B_KV_HEADS
N_GROUPS = 4
EXPERTS_PER_GROUP = 8
N_EXPERTS = N_GROUPS * EXPERTS_PER_GROUP
TOP_K = 2
MOE_BLOCK = 128
LN_EPS = 1e-5
RMS_EPS = 1e-6
NEG_BIG = -1e30
LOG2E = math.log2(math.e)

LANES = 128
COL_TILE = 8 * HEAD_DIM
ROW_TILE = 1024
SUB_ROWS = 256
ITEM_ROWS = 1024
ITEM_BLOCKS = ITEM_ROWS // MOE_BLOCK
FF_TILE = 256
DMA_UNROLL = 8
VMEM_LIMIT = 56 * 1024 * 1024


def _params(sem, vmem=VMEM_LIMIT):
    return pltpu.CompilerParams(dimension_semantics=sem, vmem_limit_bytes=vmem)


_T_QA, _T_KA, _T_VA, _T_QB, _T_KVB, _T_GATE = 0, 3, 6, 9, 11, 12
_R1Q, _R1K, _R2Q, _R2K = 0, 2, 4, 6


def _cast_weights_once(w_ref, wbf_ref):
    @pl.when(pl.program_id(1) == 0)
    def _cast():
        wbf_ref[...] = w_ref[...].astype(BF16)


def _for_sub_tiles(x_ref, wbf_ref, emit):
    for rt in range(x_ref.shape[0] // SUB_ROWS):
        rows = slice(rt * SUB_ROWS, (rt + 1) * SUB_ROWS)
        emit(rt, rows, jnp.dot(x_ref[rows, :], wbf_ref[...], preferred_element_type=F32))


def _head(acc, h):
    return acc[:, h * HEAD_DIM:(h + 1) * HEAD_DIM]


def _rope(a, tab_ref, t, rows):
    return a * tab_ref[t, rows, :] + pltpu.roll(a, HEAD_DIM // 2, 1) * tab_ref[t + 1, rows, :]


def _rms(a, g_ref):
    ms = jnp.mean(a * a, axis=-1, keepdims=True)
    return a * lax.rsqrt(ms + RMS_EPS) * g_ref[...]


def _proj_a_kernel(x_ref, w_ref, tab_ref, out_ref, wbf_ref, *perm, dil):
    j = pl.program_id(0)
    _cast_weights_once(w_ref, wbf_ref)
    per_res = SUB_ROWS // dil

    def emit_with(fn):
        def emit(rt, rows, acc):
            if dil == 1:
                for h in range(A_HEADS):
                    out_ref[h, 0, rows, :] = fn(_head(acc, h), rows).astype(BF16)
                return
            perm_ref, = perm
            for h in range(A_HEADS):
                perm_ref[h, rows, :] = fn(_head(acc, h), rows)
            for h in range(A_HEADS):
                for r in range(dil):
                    piece = perm_ref[h, pl.ds(rt * SUB_ROWS + r, per_res, stride=dil), :]
                    out_ref[h, r, rt * per_res:(rt + 1) * per_res, :] = piece.astype(BF16)
        _for_sub_tiles(x_ref, wbf_ref, emit)

    @pl.when(j == 0)
    def _q():
        emit_with(lambda a, rows: _rope(a, tab_ref, _R1Q, rows))

    @pl.when(j == 1)
    def _k():
        emit_with(lambda a, rows: _rope(a, tab_ref, _R1K, rows))

    @pl.when(j == 2)
    def _v():
        emit_with(lambda a, rows: a)


def _proj_b_kernel(x_ref, w_ref, tab_ref, gq_ref, gk_ref, out_ref, wbf_ref, *, q_tiles):
    j = pl.program_id(0)
    _cast_weights_once(w_ref, wbf_ref)
    heads = COL_TILE // HEAD_DIM

    @pl.when(j < q_tiles)
    def _qb():
        def emit(rt, rows, acc):
            for h in range(heads):
                out_ref[h, rows, :] = _rope(_rms(_head(acc, h), gq_ref), tab_ref, _R2Q, rows).astype(BF16)
        _for_sub_tiles(x_ref, wbf_ref, emit)

    @pl.when(j == q_tiles)
    def _kvb():
        def emit(rt, rows, acc):
            for h in range(heads):
                if h < B_KV_HEADS:
                    out_ref[h, rows, :] = _rope(_rms(_head(acc, h), gk_ref), tab_ref, _R2K, rows).astype(BF16)
                else:
                    out_ref[h, rows, :] = _head(acc, h).astype(BF16)
        _for_sub_tiles(x_ref, wbf_ref, emit)


def _proj_gate_kernel(x_ref, w_ref, out_ref, wbf_ref):
    _cast_weights_once(w_ref, wbf_ref)

    def emit(rt, rows, acc):
        out_ref[rows, :] = (1.0 / (1.0 + jnp.exp(-acc))).astype(BF16)
    _for_sub_tiles(x_ref, wbf_ref, emit)


def _proj_specs(d_model, seq, first_tile, tile_step):
    seq_blocks = seq // ROW_TILE
    return [
        pl.BlockSpec((ROW_TILE, d_model), lambda j, i: (i, 0)),
        pl.BlockSpec((d_model, COL_TILE), lambda j, i: (0, first_tile + tile_step * j)),
        pl.BlockSpec((8, ROW_TILE, HEAD_DIM), lambda j, i: (0, i % seq_blocks, 0)),
    ]


def _proj_a(xb, w_in, tabs, group, batch, seq):
    n_tok, d_model = xb.shape
    dil = DIL_PATTERNS[group][1]
    sub_len = seq // dil
    seq_blocks = seq // ROW_TILE
    scratch = [pltpu.VMEM((d_model, COL_TILE), BF16)]
    if dil > 1:
        scratch.append(pltpu.VMEM((A_HEADS, ROW_TILE, HEAD_DIM), F32))
    return pl.pallas_call(
        functools.partial(_proj_a_kernel, dil=dil),
        out_shape=jax.ShapeDtypeStruct((3 * A_HEADS, batch, dil, sub_len, HEAD_DIM), BF16),
        grid=(3, n_tok // ROW_TILE),
        in_specs=_proj_specs(d_model, seq, group, N_DIL),
        out_specs=pl.BlockSpec((A_HEADS, None, dil, ROW_TILE // dil, HEAD_DIM),
                               lambda j, i: (j, i // seq_blocks, 0, i % seq_blocks, 0)),
        scratch_shapes=scratch,
        compiler_params=_params(("arbitrary", "arbitrary")),
    )(xb, w_in, tabs)


def _proj_b(xb, w_in, tabs, gq, gk, seq):
    n_tok, d_model = xb.shape
    q_tiles = B_Q_HEADS * HEAD_DIM // COL_TILE
    heads = COL_TILE // HEAD_DIM
    fix = lambda j, i: (0, 0)
    return pl.pallas_call(
        functools.partial(_proj_b_kernel, q_tiles=q_tiles),
        out_shape=jax.ShapeDtypeStruct(((q_tiles + 1) * heads, n_tok, HEAD_DIM), BF16),
        grid=(q_tiles + 1, n_tok // ROW_TILE),
        in_specs=_proj_specs(d_model, seq, _T_QB, 1) + [pl.BlockSpec((1, HEAD_DIM), fix), pl.BlockSpec((1, HEAD_DIM), fix)],
        out_specs=pl.BlockSpec((heads, ROW_TILE, HEAD_DIM), lambda j, i: (j, i, 0)),
        scratch_shapes=[pltpu.VMEM((d_model, COL_TILE), BF16)],
        compiler_params=_params(("arbitrary", "arbitrary")),
    )(xb, w_in, tabs, gq, gk)


def _proj_gate(xb, w_in):
    n_tok, d_model = xb.shape
    n_ct = w_in.shape[1] // COL_TILE - _T_GATE
    return pl.pallas_call(
        _proj_gate_kernel,
        out_shape=jax.ShapeDtypeStruct((n_tok, n_ct * COL_TILE), BF16),
        grid=(n_ct, n_tok // ROW_TILE),
        in_specs=[pl.BlockSpec((ROW_TILE, d_model), lambda j, i: (i, 0)),
                  pl.BlockSpec((d_model, COL_TILE), lambda j, i: (0, _T_GATE + j))],
        out_specs=pl.BlockSpec((ROW_TILE, COL_TILE), lambda j, i: (i, j)),
        scratch_shapes=[pltpu.VMEM((d_model, COL_TILE), BF16)],
        compiler_params=_params(("arbitrary", "arbitrary")),
    )(xb, w_in)


_SUB = 128
_WIN = _SUB + 2 * BAND_HALF
_MERGE_ROWS = 256
_A_UNROLL = 8


def _attn_a_group(g, dil, seq, q_ref, k_ref, v_ref, og, lg, bias):
    sub_len = seq // dil
    per_seq = sub_len // _SUB
    shift = per_seq.bit_length() - 1

    def body(i, carry):
        r = lax.shift_right_logical(i, shift)
        p0 = pl.multiple_of((i & (per_seq - 1)) * _SUB, _SUB)
        start = pl.multiple_of(jnp.clip(p0 - BAND_HALF, 0, sub_len - _WIN), BAND_HALF)
        q = q_ref[r, pl.ds(p0, _SUB), :]
        k = k_ref[r, pl.ds(start, _WIN), :]
        v = v_ref[r, pl.ds(start, _WIN), :]
        s = lax.dot_general(q, k, (((1,), (1,)), ((), ())), preferred_element_type=F32)
        s = s + bias[lax.shift_right_logical(p0 - start, BAND_HALF.bit_length() - 1)]
        m = jnp.max(s, axis=1, keepdims=True)
        p = jnp.exp2(s - m)
        l = jnp.sum(p, axis=1, keepdims=True)
        o = jnp.dot(p.astype(BF16), v, preferred_element_type=F32) / l
        lse = jnp.broadcast_to(m + jnp.log2(l), (_SUB, HEAD_DIM))
        if dil == 1:
            rows = pl.ds(p0, _SUB)
        else:
            rows = pl.ds(p0 * dil + r, _SUB, stride=dil)
        og[g, rows, :] = o
        lg[g, rows, :] = lse
        return carry

    lax.fori_loop(0, seq // _SUB, body, 0, unroll=_A_UNROLL)


def _attn_a_kernel(*refs, seq):
    qkv, o_ref = refs[:3 * N_DIL], refs[3 * N_DIL]
    og, lg, bias = refs[3 * N_DIL + 1:]
    row = lax.broadcasted_iota(I32, (_SUB, _WIN), 0)
    col = lax.broadcasted_iota(I32, (_SUB, _WIN), 1)
    for case in range(3):
        bias[case] = jnp.where(jnp.abs(col - row - case * BAND_HALF) <= BAND_HALF, 0.0, NEG_BIG)
    for g, (_, dil) in enumerate(DIL_PATTERNS):
        _attn_a_group(g, dil, seq, qkv[3 * g], qkv[3 * g + 1], qkv[3 * g + 2], og, lg, bias)

    def merge(c, carry):
        rows = pl.ds(pl.multiple_of(c * _MERGE_ROWS, _MERGE_ROWS), _MERGE_ROWS)
        ls = [lg[g, rows, :] for g in range(N_DIL)]
        mx = functools.reduce(jnp.maximum, ls)
        ws = [jnp.exp2(l - mx) for l in ls]
        num = functools.reduce(lambda a, b: a + b, [w * og[g, rows, :] for g, w in enumerate(ws)])
        o_ref[rows, :] = (num / functools.reduce(lambda a, b: a + b, ws)).astype(o_ref.dtype)
        return carry

    lax.fori_loop(0, seq // _MERGE_ROWS, merge, 0)


def _attn_a(groups, batch, seq):
    in_specs, operands = [], []
    for g, (_, dil) in enumerate(DIL_PATTERNS):
        for kind in range(3):
            first = kind * A_HEADS
            in_specs.append(pl.BlockSpec((None, None, dil, seq // dil, HEAD_DIM),
                                         lambda b, h, first=first: (first + h, b, 0, 0, 0)))
            operands.append(groups[g])
    return pl.pallas_call(
        functools.partial(_attn_a_kernel, seq=seq),
        out_shape=jax.ShapeDtypeStruct((batch * seq, A_HEADS * HEAD_DIM), BF16),
        grid=(batch, A_HEADS),
        in_specs=in_specs,
        out_specs=pl.BlockSpec((seq, HEAD_DIM), lambda b, h: (b, h)),
        scratch_shapes=[pltpu.VMEM((N_DIL, seq, HEAD_DIM), F32), pltpu.VMEM((N_DIL, seq, HEAD_DIM), F32),
                        pltpu.VMEM((3, _SUB, _WIN), F32)],
        compiler_params=_params(("arbitrary", "arbitrary")),
    )(*operands)


_BQ = 256
_BK = 512
SCORE_BOUND = 64.0
BF16_SLACK = 1.02


def _attn_b_kernel(q_ref, k_ref, v_ref, o_ref, *, seq):
    rows = B_GROUP * _BQ
    q = q_ref[...].reshape(rows, HEAD_DIM)

    def body(c, carry):
        m, l, acc = carry
        c0 = pl.multiple_of(c * _BK, _BK)
        k = k_ref[pl.ds(c0, _BK), :]
        v = v_ref[pl.ds(c0, _BK), :]
        s = lax.dot_general(q, k, (((1,), (1,)), ((), ())), preferred_element_type=F32)
        m_new = jnp.maximum(m, jnp.max(s, axis=1, keepdims=True))
        alpha = jnp.exp2(m - m_new)
        p = jnp.exp2(s - m_new)
        l = alpha * l + jnp.sum(p, axis=1, keepdims=True)
        acc = alpha * acc + jnp.dot(p.astype(BF16), v, preferred_element_type=F32)
        return m_new, l, acc

    init = (jnp.full((rows, 1), NEG_BIG, F32), jnp.zeros((rows, 1), F32), jnp.zeros((rows, HEAD_DIM), F32))
    _, l, acc = lax.fori_loop(0, seq // _BK, body, init)
    o = acc / l
    for g in range(B_GROUP):
        o_ref[:, g * HEAD_DIM:(g + 1) * HEAD_DIM] = o[g * _BQ:(g + 1) * _BQ].astype(o_ref.dtype)


def _attn_b_bounded_kernel(q_ref, k_ref, v_ref, o_ref, *, seq):
    rows = B_GROUP * _BQ
    q = q_ref[...].reshape(rows, HEAD_DIM)
    ones = jnp.ones((_BK, HEAD_DIM), BF16)
    acc = jnp.zeros((rows, 2 * HEAD_DIM), F32)
    for c in range(seq // _BK):
        k = k_ref[c * _BK:(c + 1) * _BK, :]
        v = v_ref[c * _BK:(c + 1) * _BK, :]
        s = lax.dot_general(q, k, (((1,), (1,)), ((), ())), preferred_element_type=F32)
        p = jnp.exp2(s).astype(BF16)
        acc = acc + jnp.dot(p, jnp.concatenate([v, ones], axis=1), preferred_element_type=F32)
    o = acc[:, :HEAD_DIM] / acc[:, HEAD_DIM:HEAD_DIM + 1]
    for g in range(B_GROUP):
        o_ref[:, g * HEAD_DIM:(g + 1) * HEAD_DIM] = o[g * _BQ:(g + 1) * _BQ].astype(o_ref.dtype)


def _attn_b(slots, batch, seq, bounded):
    n_tok = batch * seq
    k0 = B_Q_HEADS
    v0 = k0 + B_KV_HEADS
    qblocks = seq // _BQ
    return pl.pallas_call(
        functools.partial(_attn_b_bounded_kernel if bounded else _attn_b_kernel, seq=seq),
        out_shape=jax.ShapeDtypeStruct((n_tok, B_Q_HEADS * HEAD_DIM), BF16),
        grid=(batch, B_KV_HEADS, qblocks),
        in_specs=[
            pl.BlockSpec((B_GROUP, _BQ, HEAD_DIM), lambda b, kv, qi: (kv, b * qblocks + qi, 0)),
            pl.BlockSpec((None, seq, HEAD_DIM), lambda b, kv, qi: (k0 + kv, b, 0)),
            pl.BlockSpec((None, seq, HEAD_DIM), lambda b, kv, qi: (v0 + kv, b, 0)),
        ],
        out_specs=pl.BlockSpec((_BQ, B_GROUP * HEAD_DIM), lambda b, kv, qi: (b * qblocks + qi, kv)),
        compiler_params=_params(("arbitrary", "arbitrary", "arbitrary")),
    )(slots, slots, slots)


_MIX_ROWS = 512


def _branch_kernel(oa_ref, ob_ref, ga_ref, gb_ref, wa_ref, wb_ref, out_ref):
    for rt in range(out_ref.shape[0] // SUB_ROWS):
        rows = slice(rt * SUB_ROWS, (rt + 1) * SUB_ROWS)
        y_a = jnp.dot(oa_ref[rows, :], wa_ref[...], preferred_element_type=F32)
        y_b = jnp.dot(ob_ref[rows, :], wb_ref[...], preferred_element_type=F32)
        out_ref[rows, :] = (ga_ref[rows, :].astype(F32) * y_a + gb_ref[rows, :].astype(F32) * y_b).astype(BF16)


def _branch_mix(o_a, o_b, gates, wa, wb):
    n_tok = o_b.shape[0]
    d_model = wa.shape[1]
    tm = _MIX_ROWS
    row = lambda i: (i, 0)
    return pl.pallas_call(
        _branch_kernel,
        out_shape=jax.ShapeDtypeStruct((n_tok, d_model), BF16),
        grid=(n_tok // tm,),
        in_specs=[pl.BlockSpec((tm, o_a.shape[1]), row),
                  pl.BlockSpec((tm, o_b.shape[1]), row),
                  pl.BlockSpec((tm, d_model), lambda i: (i, 0)),
                  pl.BlockSpec((tm, d_model), lambda i: (i, 1)),
                  pl.BlockSpec(wa.shape, lambda i: (0, 0)),
                  pl.BlockSpec(wb.shape, lambda i: (0, 0))],
        out_specs=pl.BlockSpec((tm, d_model), row),
        compiler_params=_params(("arbitrary",)),
    )(o_a, o_b, gates, gates, wa, wb)


def _layer_norm(z, g, b):
    mu = jnp.mean(z, axis=-1, keepdims=True)
    zc = z - mu
    var = jnp.mean(zc * zc, axis=-1, keepdims=True)
    return zc * lax.rsqrt(var + LN_EPS) * g + b


def _to_slabs(slab_ref, value, row0):
    rows, d = value.shape
    chunks = d // LANES
    for c in range(chunks):
        slab_ref[pl.ds(row0 * chunks + c, rows, stride=chunks), :] = value[:, c * LANES:(c + 1) * LANES]


def _from_slabs(slab_ref, rows, chunks, c):
    return slab_ref[pl.ds(c, rows, stride=chunks), :]


def _out_proj_kernel(m_ref, w_ref, x_ref, g_ref, b_ref, rhi_ref, rlo_ref, rb_ref, h_ref, hs_ref, lg_ref, *, alpha):
    for rt in range(h_ref.shape[0] // SUB_ROWS):
        rows = slice(rt * SUB_ROWS, (rt + 1) * SUB_ROWS)
        mix = jnp.dot(m_ref[rows, :], w_ref[...], preferred_element_type=F32)
        h = _layer_norm(alpha * x_ref[rows, :] + mix, g_ref[...], b_ref[...])
        h_ref[rows, :] = h
        _to_slabs(hs_ref, h, rt * SUB_ROWS)
        hi = h.astype(BF16)
        lo = (h - hi.astype(F32)).astype(BF16)
        lg = jnp.dot(hi, rhi_ref[...], preferred_element_type=F32)
        lg = lg + jnp.dot(lo, rhi_ref[...], preferred_element_type=F32)
        lg = lg + jnp.dot(hi, rlo_ref[...], preferred_element_type=F32)
        lg_ref[rows, :] = lg + rb_ref[...]


def _out_proj(merged, w_out, x2, g, b, r_hi, r_lo, r_b, alpha):
    n_tok, d_model = x2.shape
    tm = _MIX_ROWS
    chunks = d_model // LANES
    row = lambda i: (i, 0)
    fix = lambda i: (0, 0)
    return pl.pallas_call(
        functools.partial(_out_proj_kernel, alpha=alpha),
        out_shape=(jax.ShapeDtypeStruct((n_tok, d_model), F32),
                   jax.ShapeDtypeStruct((n_tok * chunks, LANES), F32),
                   jax.ShapeDtypeStruct((n_tok, LANES), F32)),
        grid=(n_tok // tm,),
        in_specs=[pl.BlockSpec((tm, d_model), row),
                  pl.BlockSpec(w_out.shape, fix),
                  pl.BlockSpec((tm, d_model), row),
                  pl.BlockSpec((1, d_model), fix),
                  pl.BlockSpec((1, d_model), fix),
                  pl.BlockSpec(r_hi.shape, fix),
                  pl.BlockSpec(r_lo.shape, fix),
                  pl.BlockSpec((1, LANES), fix)],
        out_specs=(pl.BlockSpec((tm, d_model), row), pl.BlockSpec((tm * chunks, LANES), row),
                   pl.BlockSpec((tm, LANES), row)),
        compiler_params=_params(("arbitrary",)),
    )(merged, w_out, x2, g, b, r_hi, r_lo, r_b)


_BIG_CHUNK = 512


def _moe_kernel(item_expert, item_row0, item_blocks, item_real, src_slab, dst_slab,
                h_hbm, wg_ref, wu_ref, wd_ref, y_hbm,
                stage, xbf, acc, obuf, gsem, ssem):
    it = pl.program_id(0)
    f = pl.program_id(1)
    n_items = pl.num_programs(0)
    n_ff = pl.num_programs(1)
    n_blocks = item_blocks[it]
    d_model = xbf.shape[1]
    chunks = d_model // LANES

    def slab(ref, first):
        return ref.at[pl.ds(pl.multiple_of(first, chunks), chunks)]

    def gather_copy(base, r):
        return pltpu.make_async_copy(slab(h_hbm, src_slab[base + r]), slab(stage, r * chunks), gsem.at[0])

    def scatter_copy(base, r):
        return pltpu.make_async_copy(slab(obuf, r * chunks), slab(y_hbm, dst_slab[base + r]), ssem.at[0])

    def for_rows(n_rows, fn):
        def group(t, c):
            for u in range(DMA_UNROLL):
                fn(t * DMA_UNROLL + u)
            return c
        n_groups = n_rows // DMA_UNROLL
        lax.fori_loop(0, n_groups, group, 0)

        def single(r, c):
            fn(r)
            return c
        lax.fori_loop(n_groups * DMA_UNROLL, n_rows, single, 0)

    def gather(item, op):
        base = item_row0[item]
        for_rows(item_blocks[item] * MOE_BLOCK, lambda r: op(gather_copy(base, r)))

    def scatter(item, op):
        base = item_row0[item]
        for_rows(item_real[item], lambda r: op(scatter_copy(base, r)))

    start = lambda cp: cp.start()
    wait = lambda cp: cp.wait()

    @pl.when((it == 0) & (f == 0))
    def _first_gather():
        gather(it, start)

    @pl.when(f == 0)
    def _gathered():
        gather(it, wait)

        def cast_rows(b, carry):
            r0 = pl.multiple_of(b * MOE_BLOCK, MOE_BLOCK)
            for c in range(chunks):
                piece = stage[pl.ds(r0 * chunks + c, MOE_BLOCK, stride=chunks), :]
                xbf[pl.ds(r0, MOE_BLOCK), c * LANES:(c + 1) * LANES] = piece.astype(BF16)
            return carry
        lax.fori_loop(0, n_blocks, cast_rows, 0)

        @pl.when(it + 1 < n_items)
        def _prefetch_next():
            gather(it + 1, start)

    def chunk(r0, rows, mode):
        xb = xbf[pl.ds(r0, rows), :]
        a = jnp.dot(xb, wg_ref[...].astype(BF16), preferred_element_type=F32)
        u = jnp.dot(xb, wu_ref[...].astype(BF16), preferred_element_type=F32)
        hid = (a / (1.0 + jnp.exp(-a)) * u).astype(BF16)
        y = jnp.dot(hid, wd_ref[...].astype(BF16), preferred_element_type=F32)
        if mode == "first":
            acc[pl.ds(r0, rows), :] = y
        elif mode == "middle":
            acc[pl.ds(r0, rows), :] += y
        else:
            if mode == "last":
                y = acc[pl.ds(r0, rows), :] + y
            for c in range(chunks):
                obuf[pl.ds(r0 * chunks + c, rows, stride=chunks), :] = y[:, c * LANES:(c + 1) * LANES]

    def run_item(mode):
        per_big = _BIG_CHUNK // MOE_BLOCK

        def big_chunk(t, c):
            chunk(pl.multiple_of(t * _BIG_CHUNK, _BIG_CHUNK), _BIG_CHUNK, mode)
            return c
        n_big = n_blocks // per_big
        lax.fori_loop(0, n_big, big_chunk, 0)
        rest = n_blocks - n_big * per_big
        for blocks in range(1, per_big):
            @pl.when(rest == blocks)
            def _rest(blocks=blocks):
                chunk(pl.multiple_of(n_big * _BIG_CHUNK, _BIG_CHUNK), blocks * MOE_BLOCK, mode)

    last = n_ff - 1

    @pl.when((n_blocks > 0) & (f == last))
    def _final():
        @pl.when(it > 0)
        def _drain_prev():
            scatter(it - 1, wait)
        run_item("only" if n_ff == 1 else "last")
        scatter(it, start)

    if n_ff > 1:
        @pl.when((n_blocks > 0) & (f == 0))
        def _first():
            run_item("first")

        @pl.when((n_blocks > 0) & (f > 0) & (f < last))
        def _middle():
            run_item("middle")

    @pl.when((f == last) & (it == n_items - 1))
    def _drain_last():
        scatter(jnp.where(n_blocks > 0, it, _last_live(item_blocks, n_items)), wait)


def _last_live(item_blocks, n_items):
    def body(i, best):
        return jnp.where(item_blocks[i] > 0, i, best)
    return lax.fori_loop(0, n_items, body, 0)


def _moe(h_slabs, w_gate, w_up, w_down, item_expert, item_row0, item_blocks, item_real, src_slab, dst_slab):
    d_model = w_gate.shape[1]
    chunks = d_model // LANES
    n_tok = h_slabs.shape[0] // chunks
    ff = w_gate.shape[2]
    n_ff = ff // FF_TILE
    n_items = item_expert.shape[0]

    def ff_idx(it, f, blocks):
        return jnp.where(blocks[it] > 0, f, n_ff - 1)

    grid_spec = pltpu.PrefetchScalarGridSpec(
        num_scalar_prefetch=6,
        grid=(n_items, n_ff),
        in_specs=[
            pl.BlockSpec(memory_space=pl.ANY),
            pl.BlockSpec((None, d_model, FF_TILE), lambda it, f, ie, ir, ib, nr, st, dr: (ie[it], 0, ff_idx(it, f, ib))),
            pl.BlockSpec((None, d_model, FF_TILE), lambda it, f, ie, ir, ib, nr, st, dr: (ie[it], 0, ff_idx(it, f, ib))),
            pl.BlockSpec((None, FF_TILE, d_model), lambda it, f, ie, ir, ib, nr, st, dr: (ie[it], ff_idx(it, f, ib), 0)),
        ],
        out_specs=pl.BlockSpec(memory_space=pl.ANY),
        scratch_shapes=[
            pltpu.VMEM((ITEM_ROWS * chunks, LANES), F32),
            pltpu.VMEM((ITEM_ROWS, d_model), BF16),
            pltpu.VMEM((ITEM_ROWS, d_model), F32),
            pltpu.VMEM((ITEM_ROWS * chunks, LANES), F32),
            pltpu.SemaphoreType.DMA((1,)),
            pltpu.SemaphoreType.DMA((1,)),
        ],
    )
    return pl.pallas_call(
        _moe_kernel,
        out_shape=jax.ShapeDtypeStruct((n_tok * TOP_K * chunks, LANES), F32),
        grid_spec=grid_spec,
        compiler_params=_params(("arbitrary", "arbitrary")),
    )(item_expert, item_row0, item_blocks, item_real, src_slab, dst_slab, h_slabs, w_gate, w_up, w_down)


_COMBINE_ROWS = 512


def _combine_kernel(y0_ref, y1_ref, wt_ref, h_ref, g_ref, b_ref, out_ref, *, alpha):
    rows, d_model = h_ref.shape
    chunks = d_model // LANES
    wt = wt_ref[...]
    w0, w1 = wt[:, 0:1], wt[:, 1:2]
    for c in range(chunks):
        cols = slice(c * LANES, (c + 1) * LANES)
        ffn = w0 * _from_slabs(y0_ref, rows, chunks, c) + w1 * _from_slabs(y1_ref, rows, chunks, c)
        out_ref[:, cols] = alpha * h_ref[:, cols] + ffn
    out_ref[...] = _layer_norm(out_ref[...], g_ref[...], b_ref[...])


def _combine(y_slabs, weights, h, g, b, alpha):
    n_tok, d_model = h.shape
    tm = _COMBINE_ROWS
    chunks = d_model // LANES
    second = n_tok // tm
    row = lambda i: (i, 0)
    fix = lambda i: (0, 0)
    return pl.pallas_call(
        functools.partial(_combine_kernel, alpha=alpha),
        out_shape=jax.ShapeDtypeStruct((n_tok, d_model), F32),
        grid=(n_tok // tm,),
        in_specs=[pl.BlockSpec((tm * chunks, LANES), row),
                  pl.BlockSpec((tm * chunks, LANES), lambda i: (second + i, 0)),
                  pl.BlockSpec((tm, TOP_K), row),
                  pl.BlockSpec((tm, d_model), row),
                  pl.BlockSpec((1, d_model), fix),
                  pl.BlockSpec((1, d_model), fix)],
        out_specs=pl.BlockSpec((tm, d_model), row),
        compiler_params=_params(("arbitrary",)),
    )(y_slabs, y_slabs, weights, h, g, b)


def _rope_tables(seq):
    half = HEAD_DIM // 2
    inv1 = ROPE_THETA ** (-jnp.arange(half, dtype=F32) / half)
    ang1 = jnp.arange(seq, dtype=F32)[:, None] * inv1[None, :]
    rows = seq // GRID_W
    r, c = jnp.meshgrid(jnp.arange(rows, dtype=F32), jnp.arange(GRID_W, dtype=F32), indexing='ij')
    n_axis = HEAD_DIM // 4
    inv2 = ROPE_THETA ** (-jnp.arange(n_axis, dtype=F32) / n_axis)
    ang2 = jnp.concatenate([r.reshape(-1, 1) * inv2[None, :], c.reshape(-1, 1) * inv2[None, :]], axis=-1)
    q_scale = HEAD_DIM ** -0.5 * LOG2E
    out = []
    for ang in (ang1, ang2):
        cos = jnp.concatenate([jnp.cos(ang), jnp.cos(ang)], axis=-1)
        sin = jnp.concatenate([-jnp.sin(ang), jnp.sin(ang)], axis=-1)
        out += [cos * q_scale, sin * q_scale, cos, sin]
    return jnp.stack(out, axis=0)


def _route(logits, n_tok, chunks):
    g_logits = logits[:, :N_GROUPS]
    g_prob = jax.nn.softmax(g_logits, axis=-1)
    g_idx = jnp.argmax(g_logits, axis=-1)
    g_gate = jnp.take_along_axis(g_prob, g_idx[:, None], axis=1)[:, 0]
    e_logits = logits[:, N_GROUPS:N_GROUPS + N_EXPERTS].reshape(n_tok, N_GROUPS, EXPERTS_PER_GROUP)
    e_logits = jnp.take_along_axis(e_logits, g_idx[:, None, None], axis=1)[:, 0]
    e_prob = jax.nn.softmax(e_logits, axis=-1)
    top_p, top_i = lax.top_k(e_prob, TOP_K)
    top_p = top_p / jnp.sum(top_p, axis=-1, keepdims=True)
    weights = g_gate[:, None] * top_p
    expert = g_idx[:, None] * EXPERTS_PER_GROUP + top_i

    n_slot = n_tok * TOP_K
    e_flat = expert.reshape(n_slot).astype(I32)
    order = jnp.argsort(e_flat).astype(I32)
    counts = jnp.bincount(e_flat, length=N_EXPERTS).astype(I32)
    starts = jnp.cumsum(counts) - counts
    padded = ((counts + MOE_BLOCK - 1) // MOE_BLOCK) * MOE_BLOCK
    p_ends = jnp.cumsum(padded)
    p_starts = p_ends - padded
    buf_len = n_slot + N_EXPERTS * MOE_BLOCK
    pos = jnp.arange(buf_len, dtype=I32)
    e_pos = jnp.clip(jnp.searchsorted(p_ends, pos, side='right'), 0, N_EXPERTS - 1)
    idx = pos - p_starts[e_pos]
    slot_at = jnp.where(idx < counts[e_pos], order[jnp.clip(starts[e_pos] + idx, 0, n_slot - 1)], 0)
    tok_at, k_at = slot_at // TOP_K, slot_at % TOP_K
    src_slab = tok_at * chunks
    dst_slab = (k_at * n_tok + tok_at) * chunks

    n_items = N_EXPERTS + n_slot // ITEM_ROWS
    per_expert = (padded + ITEM_ROWS - 1) // ITEM_ROWS
    item_ends = jnp.cumsum(per_expert)
    total = item_ends[-1]
    ids = jnp.arange(n_items, dtype=I32)
    last = jnp.maximum(total - 1, 0)
    live = ids < total
    e_of = jnp.clip(jnp.searchsorted(item_ends, jnp.minimum(ids, last), side='right'), 0, N_EXPERTS - 1).astype(I32)
    chunk = jnp.minimum(ids, last) - (item_ends[e_of] - per_expert[e_of])
    item_row0 = jnp.where(live, p_starts[e_of] + chunk * ITEM_ROWS, 0).astype(I32)
    item_blocks = jnp.where(live, jnp.clip(padded[e_of] // MOE_BLOCK - chunk * ITEM_BLOCKS, 0, ITEM_BLOCKS), 0).astype(I32)
    item_real = jnp.where(live, jnp.clip(counts[e_of] - chunk * ITEM_ROWS, 0, ITEM_ROWS), 0).astype(I32)
    return weights, e_of, item_row0, item_blocks, item_real, src_slab, dst_slab


def kernel(x, w_in, q_norm_g, k_norm_g, w_branch_a, w_branch_b, w_out, ln1_g, ln1_b, w_group, b_group, w_router,
           b_router, w_gate, w_up, w_down, ln2_g, ln2_b):
    batch, seq, d_model = x.shape
    depth = w_in.shape[0]
    n_tok = batch * seq
    dn_alpha = (2 * depth) ** 0.25
    tabs = _rope_tables(seq)
    h = x.reshape(n_tok, d_model)
    for layer in range(depth):
        xb = h.astype(BF16)
        groups = [_proj_a(xb, w_in[layer], tabs, g, batch, seq) for g in range(N_DIL)]
        slots_b = _proj_b(xb, w_in[layer], tabs, q_norm_g[layer][None, :], k_norm_g[layer][None, :], seq)
        gates = _proj_gate(xb, w_in[layer])
        o_a = _attn_a(groups, batch, seq)
        score_bound = (jnp.max(jnp.abs(q_norm_g[layer])) * jnp.max(jnp.abs(k_norm_g[layer]))
                       * (HEAD_DIM ** 0.5 * LOG2E * BF16_SLACK))
        o_b = lax.cond(score_bound <= SCORE_BOUND,
                       lambda s: _attn_b(s, batch, seq, True), lambda s: _attn_b(s, batch, seq, False), slots_b)
        merged = _branch_mix(o_a, o_b, gates, w_branch_a[layer].astype(BF16), w_branch_b[layer].astype(BF16))
        w_r = jnp.concatenate([w_group[layer], w_router[layer]], axis=1)
        w_r = jnp.pad(w_r, ((0, 0), (0, LANES - w_r.shape[1])))
        r_hi = w_r.astype(BF16)
        r_lo = (w_r - r_hi.astype(F32)).astype(BF16)
        r_b = jnp.pad(jnp.concatenate([b_group[layer], b_router[layer]]), (0, LANES - N_GROUPS - N_EXPERTS))[None, :]
        h1, h1_slabs, logits = _out_proj(merged, w_out[layer].astype(BF16), h, ln1_g[layer][None, :],
                                         ln1_b[layer][None, :], r_hi, r_lo, r_b, dn_alpha)
        weights, item_expert, item_row0, item_blocks, item_real, src_slab, dst_slab = _route(
            logits, n_tok, d_model // LANES)
        y_slabs = _moe(h1_slabs, w_gate[layer], w_up[layer], w_down[layer], item_expert, item_row0, item_blocks,
                       item_real, src_slab, dst_slab)
        h = _combine(y_slabs, weights, h1, ln2_g[layer][None, :], ln2_b[layer][None, :], dn_alpha)
    return h.reshape(batch, seq, d_model)
```

```python
import functools
import math

import jax
import jax.numpy as jnp
from jax import lax
from jax.experimental import pallas as pl
from jax.experimental.pallas import tpu as pltpu

F32 = jnp.float32
BF16 = jnp.bfloat16
I32 = jnp.int32

HEAD_DIM = 128
ROPE_THETA = 10000.0
GRID_W = 64
DIL_PATTERNS = ((128, 1), (512, 4), (2048, 16))
N_DIL = len(DIL_PATTERNS)
A_HEADS = 8
BAND_HALF = 64
B_Q_HEADS = 16
B_KV_HEADS = 4
B_GROUP = B_Q_HEADS // B_KV_HEADS
N_GROUPS = 4
EXPERTS_PER_GROUP = 8
N_EXPERTS = N_GROUPS * EXPERTS_PER_GROUP
TOP_K = 2
MOE_BLOCK = 128
LN_EPS = 1e-5
RMS_EPS = 1e-6
NEG_BIG = -1e30
LOG2E = math.log2(math.e)

LANES = 128
COL_TILE = 8 * HEAD_DIM
ROW_TILE = 1024
SUB_ROWS = 256
ITEM_ROWS = 1024
ITEM_BLOCKS = ITEM_ROWS // MOE_BLOCK
FF_TILE = 256
DMA_UNROLL = 8
VMEM_LIMIT = 56 * 1024 * 1024


def _params(sem, vmem=VMEM_LIMIT):
    return pltpu.CompilerParams(dimension_semantics=sem, vmem_limit_bytes=vmem)


_T_QA, _T_KA, _T_VA, _T_QB, _T_KVB, _T_GATE = 0, 3, 6, 9, 11, 12
_R1Q, _R1K, _R2Q, _R2K = 0, 2, 4, 6


def _cast_weights_once(w_ref, wbf_ref):
    @pl.when(pl.program_id(1) == 0)
    def _cast():
        wbf_ref[...] = w_ref[...].astype(BF16)


def _for_sub_tiles(x_ref, wbf_ref, emit):
    for rt in range(x_ref.shape[0] // SUB_ROWS):
        rows = slice(rt * SUB_ROWS, (rt + 1) * SUB_ROWS)
        emit(rt, rows, jnp.dot(x_ref[rows, :], wbf_ref[...], preferred_element_type=F32))


def _head(acc, h):
    return acc[:, h * HEAD_DIM:(h + 1) * HEAD_DIM]


def _rope(a, tab_ref, t, rows):
    return a * tab_ref[t, rows, :] + pltpu.roll(a, HEAD_DIM // 2, 1) * tab_ref[t + 1, rows, :]


def _rms(a, g_ref):
    ms = jnp.mean(a * a, axis=-1, keepdims=True)
    return a * lax.rsqrt(ms + RMS_EPS) * g_ref[...]


def _proj_a_kernel(x_ref, w_ref, tab_ref, out_ref, wbf_ref, *perm, dil):
    j = pl.program_id(0)
    _cast_weights_once(w_ref, wbf_ref)
    per_res = SUB_ROWS // dil

    def emit_with(fn):
        def emit(rt, rows, acc):
            if dil == 1:
                for h in range(A_HEADS):
                    out_ref[h, 0, rows, :] = fn(_head(acc, h), rows).astype(BF16)
                return
            perm_ref, = perm
            for h in range(A_HEADS):
                perm_ref[h, rows, :] = fn(_head(acc, h), rows)
            for h in range(A_HEADS):
                for r in range(dil):
                    piece = perm_ref[h, pl.ds(rt * SUB_ROWS + r, per_res, stride=dil), :]
                    out_ref[h, r, rt * per_res:(rt + 1) * per_res, :] = piece.astype(BF16)
        _for_sub_tiles(x_ref, wbf_ref, emit)

    @pl.when(j == 0)
    def _q():
        emit_with(lambda a, rows: _rope(a, tab_ref, _R1Q, rows))

    @pl.when(j == 1)
    def _k():
        emit_with(lambda a, rows: _rope(a, tab_ref, _R1K, rows))

    @pl.when(j == 2)
    def _v():
        emit_with(lambda a, rows: a)


def _proj_b_kernel(x_ref, w_ref, tab_ref, gq_ref, gk_ref, out_ref, wbf_ref, *, q_tiles):
    j = pl.program_id(0)
    _cast_weights_once(w_ref, wbf_ref)
    heads = COL_TILE // HEAD_DIM

    @pl.when(j < q_tiles)
    def _qb():
        def emit(rt, rows, acc):
            for h in range(heads):
                out_ref[h, rows, :] = _rope(_rms(_head(acc, h), gq_ref), tab_ref, _R2Q, rows).astype(BF16)
        _for_sub_tiles(x_ref, wbf_ref, emit)

    @pl.when(j == q_tiles)
    def _kvb():
        def emit(rt, rows, acc):
            for h in range(heads):
                if h < B_KV_HEADS:
                    out_ref[h, rows, :] = _rope(_rms(_head(acc, h), gk_ref), tab_ref, _R2K, rows).astype(BF16)
                else:
                    out_ref[h, rows, :] = _head(acc, h).astype(BF16)
        _for_sub_tiles(x_ref, wbf_ref, emit)


def _proj_gate_kernel(x_ref, w_ref, out_ref, wbf_ref):
    _cast_weights_once(w_ref, wbf_ref)

    def emit(rt, rows, acc):
        out_ref[rows, :] = (1.0 / (1.0 + jnp.exp(-acc))).astype(BF16)
    _for_sub_tiles(x_ref, wbf_ref, emit)


def _proj_specs(d_model, seq, first_tile, tile_step):
    seq_blocks = seq // ROW_TILE
    return [
        pl.BlockSpec((ROW_TILE, d_model), lambda j, i: (i, 0)),
        pl.BlockSpec((d_model, COL_TILE), lambda j, i: (0, first_tile + tile_step * j)),
        pl.BlockSpec((8, ROW_TILE, HEAD_DIM), lambda j, i: (0, i % seq_blocks, 0)),
    ]


def _proj_a(xb, w_in, tabs, group, batch, seq):
    n_tok, d_model = xb.shape
    dil = DIL_PATTERNS[group][1]
    sub_len = seq // dil
    seq_blocks = seq // ROW_TILE
    scratch = [pltpu.VMEM((d_model, COL_TILE), BF16)]
    if dil > 1:
        scratch.append(pltpu.VMEM((A_HEADS, ROW_TILE, HEAD_DIM), F32))
    return pl.pallas_call(
        functools.partial(_proj_a_kernel, dil=dil),
        out_shape=jax.ShapeDtypeStruct((3 * A_HEADS, batch, dil, sub_len, HEAD_DIM), BF16),
        grid=(3, n_tok // ROW_TILE),
        in_specs=_proj_specs(d_model, seq, group, N_DIL),
        out_specs=pl.BlockSpec((A_HEADS, None, dil, ROW_TILE // dil, HEAD_DIM),
                               lambda j, i: (j, i // seq_blocks, 0, i % seq_blocks, 0)),
        scratch_shapes=scratch,
        compiler_params=_params(("arbitrary", "arbitrary")),
    )(xb, w_in, tabs)


def _proj_b(xb, w_in, tabs, gq, gk, seq):
    n_tok, d_model = xb.shape
    q_tiles = B_Q_HEADS * HEAD_DIM // COL_TILE
    heads = COL_TILE // HEAD_DIM
    fix = lambda j, i: (0, 0)
    return pl.pallas_call(
        functools.partial(_proj_b_kernel, q_tiles=q_tiles),
        out_shape=jax.ShapeDtypeStruct(((q_tiles + 1) * heads, n_tok, HEAD_DIM), BF16),
        grid=(q_tiles + 1, n_tok // ROW_TILE),
        in_specs=_proj_specs(d_model, seq, _T_QB, 1) + [pl.BlockSpec((1, HEAD_DIM), fix), pl.BlockSpec((1, HEAD_DIM), fix)],
        out_specs=pl.BlockSpec((heads, ROW_TILE, HEAD_DIM), lambda j, i: (j, i, 0)),
        scratch_shapes=[pltpu.VMEM((d_model, COL_TILE), BF16)],
        compiler_params=_params(("arbitrary", "arbitrary")),
    )(xb, w_in, tabs, gq, gk)


def _proj_gate(xb, w_in):
    n_tok, d_model = xb.shape
    n_ct = w_in.shape[1] // COL_TILE - _T_GATE
    return pl.pallas_call(
        _proj_gate_kernel,
        out_shape=jax.ShapeDtypeStruct((n_tok, n_ct * COL_TILE), BF16),
        grid=(n_ct, n_tok // ROW_TILE),
        in_specs=[pl.BlockSpec((ROW_TILE, d_model), lambda j, i: (i, 0)),
                  pl.BlockSpec((d_model, COL_TILE), lambda j, i: (0, _T_GATE + j))],
        out_specs=pl.BlockSpec((ROW_TILE, COL_TILE), lambda j, i: (i, j)),
        scratch_shapes=[pltpu.VMEM((d_model, COL_TILE), BF16)],
        compiler_params=_params(("arbitrary", "arbitrary")),
    )(xb, w_in)


_SUB = 128
_WIN = _SUB + 2 * BAND_HALF
_MERGE_ROWS = 256
_A_UNROLL = 8


def _attn_a_group(g, dil, seq, q_ref, k_ref, v_ref, og, lg, bias):
    sub_len = seq // dil
    per_seq = sub_len // _SUB
    shift = per_seq.bit_length() - 1

    def body(i, carry):
        r = lax.shift_right_logical(i, shift)
        p0 = pl.multiple_of((i & (per_seq - 1)) * _SUB, _SUB)
        start = pl.multiple_of(jnp.clip(p0 - BAND_HALF, 0, sub_len - _WIN), BAND_HALF)
        q = q_ref[r, pl.ds(p0, _SUB), :]
        k = k_ref[r, pl.ds(start, _WIN), :]
        v = v_ref[r, pl.ds(start, _WIN), :]
        s = lax.dot_general(q, k, (((1,), (1,)), ((), ())), preferred_element_type=F32)
        s = s + bias[lax.shift_right_logical(p0 - start, BAND_HALF.bit_length() - 1)]
        m = jnp.max(s, axis=1, keepdims=True)
        p = jnp.exp2(s - m)
        l = jnp.sum(p, axis=1, keepdims=True)
        o = jnp.dot(p.astype(BF16), v, preferred_element_type=F32) / l
        lse = jnp.broadcast_to(m + jnp.log2(l), (_SUB, HEAD_DIM))
        if dil == 1:
            rows = pl.ds(p0, _SUB)
        else:
            rows = pl.ds(p0 * dil + r, _SUB, stride=dil)
        og[g, rows, :] = o
        lg[g, rows, :] = lse
        return carry

    lax.fori_loop(0, seq // _SUB, body, 0, unroll=_A_UNROLL)


def _attn_a_kernel(*refs, seq):
    qkv, o_ref = refs[:3 * N_DIL], refs[3 * N_DIL]
    og, lg, bias = refs[3 * N_DIL + 1:]
    row = lax.broadcasted_iota(I32, (_SUB, _WIN), 0)
    col = lax.broadcasted_iota(I32, (_SUB, _WIN), 1)
    for case in range(3):
        bias[case] = jnp.where(jnp.abs(col - row - case * BAND_HALF) <= BAND_HALF, 0.0, NEG_BIG)
    for g, (_, dil) in enumerate(DIL_PATTERNS):
        _attn_a_group(g, dil, seq, qkv[3 * g], qkv[3 * g + 1], qkv[3 * g + 2], og, lg, bias)

    def merge(c, carry):
        rows = pl.ds(pl.multiple_of(c * _MERGE_ROWS, _MERGE_ROWS), _MERGE_ROWS)
        ls = [lg[g, rows, :] for g in range(N_DIL)]
        mx = functools.reduce(jnp.maximum, ls)
        ws = [jnp.exp2(l - mx) for l in ls]
        num = functools.reduce(lambda a, b: a + b, [w * og[g, rows, :] for g, w in enumerate(ws)])
        o_ref[rows, :] = (num / functools.reduce(lambda a, b: a + b, ws)).astype(o_ref.dtype)
        return carry

    lax.fori_loop(0, seq // _MERGE_ROWS, merge, 0)


def _attn_a(groups, batch, seq):
    in_specs, operands = [], []
    for g, (_, dil) in enumerate(DIL_PATTERNS):
        for kind in range(3):
            first = kind * A_HEADS
            in_specs.append(pl.BlockSpec((None, None, dil, seq // dil, HEAD_DIM),
                                         lambda b, h, first=first: (first + h, b, 0, 0, 0)))
            operands.append(groups[g])
    return pl.pallas_call(
        functools.partial(_attn_a_kernel, seq=seq),
        out_shape=jax.ShapeDtypeStruct((batch * seq, A_HEADS * HEAD_DIM), BF16),
        grid=(batch, A_HEADS),
        in_specs=in_specs,
        out_specs=pl.BlockSpec((seq, HEAD_DIM), lambda b, h: (b, h)),
        scratch_shapes=[pltpu.VMEM((N_DIL, seq, HEAD_DIM), F32), pltpu.VMEM((N_DIL, seq, HEAD_DIM), F32),
                        pltpu.VMEM((3, _SUB, _WIN), F32)],
        compiler_params=_params(("arbitrary", "arbitrary")),
    )(*operands)


_BQ = 256
_BK = 512
SCORE_BOUND = 64.0
BF16_SLACK = 1.02


def _attn_b_kernel(q_ref, k_ref, v_ref, o_ref, *, seq):
    rows = B_GROUP * _BQ
    q = q_ref[...].reshape(rows, HEAD_DIM)

    def body(c, carry):
        m, l, acc = carry
        c0 = pl.multiple_of(c * _BK, _BK)
        k = k_ref[pl.ds(c0, _BK), :]
        v = v_ref[pl.ds(c0, _BK), :]
        s = lax.dot_general(q, k, (((1,), (1,)), ((), ())), preferred_element_type=F32)
        m_new = jnp.maximum(m, jnp.max(s, axis=1, keepdims=True))
        alpha = jnp.exp2(m - m_new)
        p = jnp.exp2(s - m_new)
        l = alpha * l + jnp.sum(p, axis=1, keepdims=True)
        acc = alpha * acc + jnp.dot(p.astype(BF16), v, preferred_element_type=F32)
        return m_new, l, acc

    init = (jnp.full((rows, 1), NEG_BIG, F32), jnp.zeros((rows, 1), F32), jnp.zeros((rows, HEAD_DIM), F32))
    _, l, acc = lax.fori_loop(0, seq // _BK, body, init)
    o = acc / l
    for g in range(B_GROUP):
        o_ref[:, g * HEAD_DIM:(g + 1) * HEAD_DIM] = o[g * _BQ:(g + 1) * _BQ].astype(o_ref.dtype)


def _attn_b_bounded_kernel(q_ref, k_ref, v_ref, o_ref, *, seq):
    rows = B_GROUP * _BQ
    q = q_ref[...].reshape(rows, HEAD_DIM)
    ones = jnp.ones((_BK, HEAD_DIM), BF16)
    acc = jnp.zeros((rows, 2 * HEAD_DIM), F32)
    for c in range(seq // _BK):
        k = k_ref[c * _BK:(c + 1) * _BK, :]
        v = v_ref[c * _BK:(c + 1) * _BK, :]
        s = lax.dot_general(q, k, (((1,), (1,)), ((), ())), preferred_element_type=F32)
        p = jnp.exp2(s).astype(BF16)
        acc = acc + jnp.dot(p, jnp.concatenate([v, ones], axis=1), preferred_element_type=F32)
    o = acc[:, :HEAD_DIM] / acc[:, HEAD_DIM:HEAD_DIM + 1]
    for g in range(B_GROUP):
        o_ref[:, g * HEAD_DIM:(g + 1) * HEAD_DIM] = o[g * _BQ:(g + 1) * _BQ].astype(o_ref.dtype)


def _attn_b(slots, batch, seq, bounded):
    n_tok = batch * seq
    k0 = B_Q_HEADS
    v0 = k0 + B_KV_HEADS
    qblocks = seq // _BQ
    return pl.pallas_call(
        functools.partial(_attn_b_bounded_kernel if bounded else _attn_b_kernel, seq=seq),
        out_shape=jax.ShapeDtypeStruct((n_tok, B_Q_HEADS * HEAD_DIM), BF16),
        grid=(batch, B_KV_HEADS, qblocks),
        in_specs=[
            pl.BlockSpec((B_GROUP, _BQ, HEAD_DIM), lambda b, kv, qi: (kv, b * qblocks + qi, 0)),
            pl.BlockSpec((None, seq, HEAD_DIM), lambda b, kv, qi: (k0 + kv, b, 0)),
            pl.BlockSpec((None, seq, HEAD_DIM), lambda b, kv, qi: (v0 + kv, b, 0)),
        ],
        out_specs=pl.BlockSpec((_BQ, B_GROUP * HEAD_DIM), lambda b, kv, qi: (b * qblocks + qi, kv)),
        compiler_params=_params(("arbitrary", "arbitrary", "arbitrary")),
    )(slots, slots, slots)


_MIX_ROWS = 512


def _branch_kernel(oa_ref, ob_ref, ga_ref, gb_ref, wa_ref, wb_ref, out_ref):
    for rt in range(out_ref.shape[0] // SUB_ROWS):
        rows = slice(rt * SUB_ROWS, (rt + 1) * SUB_ROWS)
        y_a = jnp.dot(oa_ref[rows, :], wa_ref[...], preferred_element_type=F32)
        y_b = jnp.dot(ob_ref[rows, :], wb_ref[...], preferred_element_type=F32)
        out_ref[rows, :] = (ga_ref[rows, :].astype(F32) * y_a + gb_ref[rows, :].astype(F32) * y_b).astype(BF16)


def _branch_mix(o_a, o_b, gates, wa, wb):
    n_tok = o_b.shape[0]
    d_model = wa.shape[1]
    tm = _MIX_ROWS
    row = lambda i: (i, 0)
    return pl.pallas_call(
        _branch_kernel,
        out_shape=jax.ShapeDtypeStruct((n_tok, d_model), BF16),
        grid=(n_tok // tm,),
        in_specs=[pl.BlockSpec((tm, o_a.shape[1]), row),
                  pl.BlockSpec((tm, o_b.shape[1]), row),
                  pl.BlockSpec((tm, d_model), lambda i: (i, 0)),
                  pl.BlockSpec((tm, d_model), lambda i: (i, 1)),
                  pl.BlockSpec(wa.shape, lambda i: (0, 0)),
                  pl.BlockSpec(wb.shape, lambda i: (0, 0))],
        out_specs=pl.BlockSpec((tm, d_model), row),
        compiler_params=_params(("arbitrary",)),
    )(o_a, o_b, gates, gates, wa, wb)


def _layer_norm(z, g, b):
    mu = jnp.mean(z, axis=-1, keepdims=True)
    zc = z - mu
    var = jnp.mean(zc * zc, axis=-1, keepdims=True)
    return zc * lax.rsqrt(var + LN_EPS) * g + b


def _to_slabs(slab_ref, value, row0):
    rows, d = value.shape
    chunks = d // LANES
    for c in range(chunks):
        slab_ref[pl.ds(row0 * chunks + c, rows, stride=chunks), :] = value[:, c * LANES:(c + 1) * LANES]


def _from_slabs(slab_ref, rows, chunks, c):
    return slab_ref[pl.ds(c, rows, stride=chunks), :]


def _out_proj_kernel(m_ref, w_ref, x_ref, g_ref, b_ref, rhi_ref, rlo_ref, rb_ref, h_ref, hs_ref, lg_ref, *, alpha):
    for rt in range(h_ref.shape[0] // SUB_ROWS):
        rows = slice(rt * SUB_ROWS, (rt + 1) * SUB_ROWS)
        mix = jnp.dot(m_ref[rows, :], w_ref[...], preferred_element_type=F32)
        h = _layer_norm(alpha * x_ref[rows, :] + mix, g_ref[...], b_ref[...])
        h_ref[rows, :] = h
        _to_slabs(hs_ref, h, rt * SUB_ROWS)
        hi = h.astype(BF16)
        lo = (h - hi.astype(F32)).astype(BF16)
        lg = jnp.dot(hi, rhi_ref[...], preferred_element_type=F32)
        lg = lg + jnp.dot(lo, rhi_ref[...], preferred_element_type=F32)
        lg = lg + jnp.dot(hi, rlo_ref[...], preferred_element_type=F32)
        lg_ref[rows, :] = lg + rb_ref[...]


def _out_proj(merged, w_out, x2, g, b, r_hi, r_lo, r_b, alpha):
    n_tok, d_model = x2.shape
    tm = _MIX_ROWS
    chunks = d_model // LANES
    row = lambda i: (i, 0)
    fix = lambda i: (0, 0)
    return pl.pallas_call(
        functools.partial(_out_proj_kernel, alpha=alpha),
        out_shape=(jax.ShapeDtypeStruct((n_tok, d_model), F32),
                   jax.ShapeDtypeStruct((n_tok * chunks, LANES), F32),
                   jax.ShapeDtypeStruct((n_tok, LANES), F32)),
        grid=(n_tok // tm,),
        in_specs=[pl.BlockSpec((tm, d_model), row),
                  pl.BlockSpec(w_out.shape, fix),
                  pl.BlockSpec((tm, d_model), row),
                  pl.BlockSpec((1, d_model), fix),
                  pl.BlockSpec((1, d_model), fix),
                  pl.BlockSpec(r_hi.shape, fix),
                  pl.BlockSpec(r_lo.shape, fix),
                  pl.BlockSpec((1, LANES), fix)],
        out_specs=(pl.BlockSpec((tm, d_model), row), pl.BlockSpec((tm * chunks, LANES), row),
                   pl.BlockSpec((tm, LANES), row)),
        compiler_params=_params(("arbitrary",)),
    )(merged, w_out, x2, g, b, r_hi, r_lo, r_b)


_BIG_CHUNK = 512


def _moe_kernel(item_expert, item_row0, item_blocks, item_real, src_slab, dst_slab,
                h_hbm, wg_ref, wu_ref, wd_ref, y_hbm,
                stage, xbf, acc, obuf, gsem, ssem):
    it = pl.program_id(0)
    f = pl.program_id(1)
    n_items = pl.num_programs(0)
    n_ff = pl.num_programs(1)
    n_blocks = item_blocks[it]
    d_model = xbf.shape[1]
    chunks = d_model // LANES

    def slab(ref, first):
        return ref.at[pl.ds(pl.multiple_of(first, chunks), chunks)]

    def gather_copy(base, r):
        return pltpu.make_async_copy(slab(h_hbm, src_slab[base + r]), slab(stage, r * chunks), gsem.at[0])

    def scatter_copy(base, r):
        return pltpu.make_async_copy(slab(obuf, r * chunks), slab(y_hbm, dst_slab[base + r]), ssem.at[0])

    def for_rows(n_rows, fn):
        def group(t, c):
            for u in range(DMA_UNROLL):
                fn(t * DMA_UNROLL + u)
            return c
        n_groups = n_rows // DMA_UNROLL
        lax.fori_loop(0, n_groups, group, 0)

        def single(r, c):
            fn(r)
            return c
        lax.fori_loop(n_groups * DMA_UNROLL, n_rows, single, 0)

    def gather(item, op):
        base = item_row0[item]
        for_rows(item_blocks[item] * MOE_BLOCK, lambda r: op(gather_copy(base, r)))

    def scatter(item, op):
        base = item_row0[item]
        for_rows(item_real[item], lambda r: op(scatter_copy(base, r)))

    start = lambda cp: cp.start()
    wait = lambda cp: cp.wait()

    @pl.when((it == 0) & (f == 0))
    def _first_gather():
        gather(it, start)

    @pl.when(f == 0)
    def _gathered():
        gather(it, wait)

        def cast_rows(b, carry):
            r0 = pl.multiple_of(b * MOE_BLOCK, MOE_BLOCK)
            for c in range(chunks):
                piece = stage[pl.ds(r0 * chunks + c, MOE_BLOCK, stride=chunks), :]
                xbf[pl.ds(r0, MOE_BLOCK), c * LANES:(c + 1) * LANES] = piece.astype(BF16)
            return carry
        lax.fori_loop(0, n_blocks, cast_rows, 0)

        @pl.when(it + 1 < n_items)
        def _prefetch_next():
            gather(it + 1, start)

    def chunk(r0, rows, mode):
        xb = xbf[pl.ds(r0, rows), :]
        a = jnp.dot(xb, wg_ref[...].astype(BF16), preferred_element_type=F32)
        u = jnp.dot(xb, wu_ref[...].astype(BF16), preferred_element_type=F32)
        hid = (a / (1.0 + jnp.exp(-a)) * u).astype(BF16)
        y = jnp.dot(hid, wd_ref[...].astype(BF16), preferred_element_type=F32)
        if mode == "first":
            acc[pl.ds(r0, rows), :] = y
        elif mode == "middle":
            acc[pl.ds(r0, rows), :] += y
        else:
            if mode == "last":
                y = acc[pl.ds(r0, rows), :] + y
            for c in range(chunks):
                obuf[pl.ds(r0 * chunks + c, rows, stride=chunks), :] = y[:, c * LANES:(c + 1) * LANES]

    def run_item(mode):
        per_big = _BIG_CHUNK // MOE_BLOCK

        def big_chunk(t, c):
            chunk(pl.multiple_of(t * _BIG_CHUNK, _BIG_CHUNK), _BIG_CHUNK, mode)
            return c
        n_big = n_blocks // per_big
        lax.fori_loop(0, n_big, big_chunk, 0)
        rest = n_blocks - n_big * per_big
        for blocks in range(1, per_big):
            @pl.when(rest == blocks)
            def _rest(blocks=blocks):
                chunk(pl.multiple_of(n_big * _BIG_CHUNK, _BIG_CHUNK), blocks * MOE_BLOCK, mode)

    last = n_ff - 1

    @pl.when((n_blocks > 0) & (f == last))
    def _final():
        @pl.when(it > 0)
        def _drain_prev():
            scatter(it - 1, wait)
        run_item("only" if n_ff == 1 else "last")
        scatter(it, start)

    if n_ff > 1:
        @pl.when((n_blocks > 0) & (f == 0))
        def _first():
            run_item("first")

        @pl.when((n_blocks > 0) & (f > 0) & (f < last))
        def _middle():
            run_item("middle")

    @pl.when((f == last) & (it == n_items - 1))
    def _drain_last():
        scatter(jnp.where(n_blocks > 0, it, _last_live(item_blocks, n_items)), wait)


def _last_live(item_blocks, n_items):
    def body(i, best):
        return jnp.where(item_blocks[i] > 0, i, best)
    return lax.fori_loop(0, n_items, body, 0)


def _moe(h_slabs, w_gate, w_up, w_down, item_expert, item_row0, item_blocks, item_real, src_slab, dst_slab):
    d_model = w_gate.shape[1]
    chunks = d_model // LANES
    n_tok = h_slabs.shape[0] // chunks
    ff = w_gate.shape[2]
    n_ff = ff // FF_TILE
    n_items = item_expert.shape[0]

    def ff_idx(it, f, blocks):
        return jnp.where(blocks[it] > 0, f, n_ff - 1)

    grid_spec = pltpu.PrefetchScalarGridSpec(
        num_scalar_prefetch=6,
        grid=(n_items, n_ff),
        in_specs=[
            pl.BlockSpec(memory_space=pl.ANY),
            pl.BlockSpec((None, d_model, FF_TILE), lambda it, f, ie, ir, ib, nr, st, dr: (ie[it], 0, ff_idx(it, f, ib))),
            pl.BlockSpec((None, d_model, FF_TILE), lambda it, f, ie, ir, ib, nr, st, dr: (ie[it], 0, ff_idx(it, f, ib))),
            pl.BlockSpec((None, FF_TILE, d_model), lambda it, f, ie, ir, ib, nr, st, dr: (ie[it], ff_idx(it, f, ib), 0)),
        ],
        out_specs=pl.BlockSpec(memory_space=pl.ANY),
        scratch_shapes=[
            pltpu.VMEM((ITEM_ROWS * chunks, LANES), F32),
            pltpu.VMEM((ITEM_ROWS, d_model), BF16),
            pltpu.VMEM((ITEM_ROWS, d_model), F32),
            pltpu.VMEM((ITEM_ROWS * chunks, LANES), F32),
            pltpu.SemaphoreType.DMA((1,)),
            pltpu.SemaphoreType.DMA((1,)),
        ],
    )
    return pl.pallas_call(
        _moe_kernel,
        out_shape=jax.ShapeDtypeStruct((n_tok * TOP_K * chunks, LANES), F32),
        grid_spec=grid_spec,
        compiler_params=_params(("arbitrary", "arbitrary")),
    )(item_expert, item_row0, item_blocks, item_real, src_slab, dst_slab, h_slabs, w_gate, w_up, w_down)


_COMBINE_ROWS = 512


def _combine_kernel(y0_ref, y1_ref, wt_ref, h_ref, g_ref, b_ref, out_ref, *, alpha):
    rows, d_model = h_ref.shape
    chunks = d_model // LANES
    wt = wt_ref[...]
    w0, w1 = wt[:, 0:1], wt[:, 1:2]
    for c in range(chunks):
        cols = slice(c * LANES, (c + 1) * LANES)
        ffn = w0 * _from_slabs(y0_ref, rows, chunks, c) + w1 * _from_slabs(y1_ref, rows, chunks, c)
        out_ref[:, cols] = alpha * h_ref[:, cols] + ffn
    out_ref[...] = _layer_norm(out_ref[...], g_ref[...], b_ref[...])


def _combine(y_slabs, weights, h, g, b, alpha):
    n_tok, d_model = h.shape
    tm = _COMBINE_ROWS
    chunks = d_model // LANES
    second = n_tok // tm
    row = lambda i: (i, 0)
    fix = lambda i: (0, 0)
    return pl.pallas_call(
        functools.partial(_combine_kernel, alpha=alpha),
        out_shape=jax.ShapeDtypeStruct((n_tok, d_model), F32),
        grid=(n_tok // tm,),
        in_specs=[pl.BlockSpec((tm * chunks, LANES), row),
                  pl.BlockSpec((tm * chunks, LANES), lambda i: (second + i, 0)),
                  pl.BlockSpec((tm, TOP_K), row),
                  pl.BlockSpec((tm, d_model), row),
                  pl.BlockSpec((1, d_model), fix),
                  pl.BlockSpec((1, d_model), fix)],
        out_specs=pl.BlockSpec((tm, d_model), row),
        compiler_params=_params(("arbitrary",)),
    )(y_slabs, y_slabs, weights, h, g, b)


def _rope_tables(seq):
    half = HEAD_DIM // 2
    inv1 = ROPE_THETA ** (-jnp.arange(half, dtype=F32) / half)
    ang1 = jnp.arange(seq, dtype=F32)[:, None] * inv1[None, :]
    rows = seq // GRID_W
    r, c = jnp.meshgrid(jnp.arange(rows, dtype=F32), jnp.arange(GRID_W, dtype=F32), indexing='ij')
    n_axis = HEAD_DIM // 4
    inv2 = ROPE_THETA ** (-jnp.arange(n_axis, dtype=F32) / n_axis)
    ang2 = jnp.concatenate([r.reshape(-1, 1) * inv2[None, :], c.reshape(-1, 1) * inv2[None, :]], axis=-1)
    q_scale = HEAD_DIM ** -0.5 * LOG2E
    out = []
    for ang in (ang1, ang2):
        cos = jnp.concatenate([jnp.cos(ang), jnp.cos(ang)], axis=-1)
        sin = jnp.concatenate([-jnp.sin(ang), jnp.sin(ang)], axis=-1)
        out += [cos * q_scale, sin * q_scale, cos, sin]
    return jnp.stack(out, axis=0)


def _route(logits, n_tok, chunks):
    g_logits = logits[:, :N_GROUPS]
    g_prob = jax.nn.softmax(g_logits, axis=-1)
    g_idx = jnp.argmax(g_logits, axis=-1)
    g_gate = jnp.take_along_axis(g_prob, g_idx[:, None], axis=1)[:, 0]
    e_logits = logits[:, N_GROUPS:N_GROUPS + N_EXPERTS].reshape(n_tok, N_GROUPS, EXPERTS_PER_GROUP)
    e_logits = jnp.take_along_axis(e_logits, g_idx[:, None, None], axis=1)[:, 0]
    e_prob = jax.nn.softmax(e_logits, axis=-1)
    top_p, top_i = lax.top_k(e_prob, TOP_K)
    top_p = top_p / jnp.sum(top_p, axis=-1, keepdims=True)
    weights = g_gate[:, None] * top_p
    expert = g_idx[:, None] * EXPERTS_PER_GROUP + top_i

    n_slot = n_tok * TOP_K
    e_flat = expert.reshape(n_slot).astype(I32)
    order = jnp.argsort(e_flat).astype(I32)
    counts = jnp.bincount(e_flat, length=N_EXPERTS).astype(I32)
    starts = jnp.cumsum(counts) - counts
    padded = ((counts + MOE_BLOCK - 1) // MOE_BLOCK) * MOE_BLOCK
    p_ends = jnp.cumsum(padded)
    p_starts = p_ends - padded
    buf_len = n_slot + N_EXPERTS * MOE_BLOCK
    pos = jnp.arange(buf_len, dtype=I32)
    e_pos = jnp.minimum(jnp.sum(pos[:, None] >= p_ends[None, :], axis=1, dtype=I32), N_EXPERTS - 1)
    idx = pos - p_starts[e_pos]
    slot_at = jnp.where(idx < counts[e_pos], order[jnp.clip(starts[e_pos] + idx, 0, n_slot - 1)], 0)
    tok_at, k_at = slot_at // TOP_K, slot_at % TOP_K
    src_slab = tok_at * chunks
    dst_slab = (k_at * n_tok + tok_at) * chunks

    n_items = N_EXPERTS + n_slot // ITEM_ROWS
    per_expert = (padded + ITEM_ROWS - 1) // ITEM_ROWS
    item_ends = jnp.cumsum(per_expert)
    total = item_ends[-1]
    ids = jnp.arange(n_items, dtype=I32)
    last = jnp.maximum(total - 1, 0)
    live = ids < total
    e_of = jnp.minimum(jnp.sum(jnp.minimum(ids, last)[:, None] >= item_ends[None, :], axis=1, dtype=I32), N_EXPERTS - 1)
    chunk = jnp.minimum(ids, last) - (item_ends[e_of] - per_expert[e_of])
    item_row0 = jnp.where(live, p_starts[e_of] + chunk * ITEM_ROWS, 0).astype(I32)
    item_blocks = jnp.where(live, jnp.clip(padded[e_of] // MOE_BLOCK - chunk * ITEM_BLOCKS, 0, ITEM_BLOCKS), 0).astype(I32)
    item_real = jnp.where(live, jnp.clip(counts[e_of] - chunk * ITEM_ROWS, 0, ITEM_ROWS), 0).astype(I32)
    return weights, e_of, item_row0, item_blocks, item_real, src_slab, dst_slab


def kernel(x, w_in, q_norm_g, k_norm_g, w_branch_a, w_branch_b, w_out, ln1_g, ln1_b, w_group, b_group, w_router,
           b_router, w_gate, w_up, w_down, ln2_g, ln2_b):
    batch, seq, d_model = x.shape
    depth = w_in.shape[0]
    n_tok = batch * seq
    dn_alpha = (2 * depth) ** 0.25
    tabs = _rope_tables(seq)
    h = x.reshape(n_tok, d_model)
    for layer in range(depth):
        xb = h.astype(BF16)
        groups = [_proj_a(xb, w_in[layer], tabs, g, batch, seq) for g in range(N_DIL)]
        slots_b = _proj_b(xb, w_in[layer], tabs, q_norm_g[layer][None, :], k_norm_g[layer][None, :], seq)
        gates = _proj_gate(xb, w_in[layer])
        o_a = _attn_a(groups, batch, seq)
        score_bound = (jnp.max(jnp.abs(q_norm_g[layer])) * jnp.max(jnp.abs(k_norm_g[layer]))
                       * (HEAD_DIM ** 0.5 * LOG2E * BF16_SLACK))
        o_b = lax.cond(score_bound <= SCORE_BOUND,
                       lambda s: _attn_b(s, batch, seq, True), lambda s: _attn_b(s, batch, seq, False), slots_b)
        merged = _branch_mix(o_a, o_b, gates, w_branch_a[layer].astype(BF16), w_branch_b[layer].astype(BF16))
        w_r = jnp.concatenate([w_group[layer], w_router[layer]], axis=1)
        w_r = jnp.pad(w_r, ((0, 0), (0, LANES - w_r.shape[1])))
        r_hi = w_r.astype(BF16)
        r_lo = (w_r - r_hi.astype(F32)).astype(BF16)
        r_b = jnp.pad(jnp.concatenate([b_group[layer], b_router[layer]]), (0, LANES - N_GROUPS - N_EXPERTS))[None, :]
        h1, h1_slabs, logits = _out_proj(merged, w_out[layer].astype(BF16), h, ln1_g[layer][None, :],
                                         ln1_b[layer][None, :], r_hi, r_lo, r_b, dn_alpha)
        weights, item_expert, item_row0, item_blocks, item_real, src_slab, dst_slab = _route(
            logits, n_tok, d_model // LANES)
        y_slabs = _moe(h1_slabs, w_gate[layer], w_up[layer], w_down[layer], item_expert, item_row0, item_blocks,
                       item_real, src_slab, dst_slab)
        h = _combine(y_slabs, weights, h1, ln2_g[layer][None, :], ln2_b[layer][None, :], dn_alpha)
    return h.reshape(batch, seq, d_model)
```

```python
import functools
import math

import jax
import jax.numpy as jnp
import numpy as np
from jax import lax
from jax.experimental import pallas as pl
from jax.experimental.pallas import tpu as pltpu

F32 = jnp.float32
BF16 = jnp.bfloat16
I32 = jnp.int32

HEAD_DIM = 128
ROPE_THETA = 10000.0
GRID_W = 64
DIL_PATTERNS = ((128, 1), (512, 4), (2048, 16))
N_DIL = len(DIL_PATTERNS)
A_HEADS = 8
BAND_HALF = 64
B_Q_HEADS = 16
B_KV_HEADS = 4
B_GROUP = B_Q_HEADS // B_KV_HEADS
N_GROUPS = 4
EXPERTS_PER_GROUP = 8
N_EXPERTS = N_GROUPS * EXPERTS_PER_GROUP
TOP_K = 2
MOE_BLOCK = 128
LN_EPS = 1e-5
RMS_EPS = 1e-6
NEG_BIG = -1e30
LOG2E = math.log2(math.e)

LANES = 128
COL_TILE = 8 * HEAD_DIM
ROW_TILE = 1024
SUB_ROWS = 256
ITEM_ROWS = 1024
ITEM_BLOCKS = ITEM_ROWS // MOE_BLOCK
FF_TILE = 256
DMA_UNROLL = 8
VMEM_LIMIT = 56 * 1024 * 1024


def _params(sem, vmem=VMEM_LIMIT):
    return pltpu.CompilerParams(dimension_semantics=sem, vmem_limit_bytes=vmem)


_T_QA, _T_KA, _T_VA, _T_QB, _T_KVB, _T_GATE = 0, 3, 6, 9, 11, 12
_R1Q, _R1K, _R2Q, _R2K = 0, 2, 4, 6


def _cast_weights_once(w_ref, wbf_ref):
    @pl.when(pl.program_id(1) == 0)
    def _cast():
        wbf_ref[...] = w_ref[...].astype(BF16)


def _for_sub_tiles(x_ref, wbf_ref, emit):
    for rt in range(x_ref.shape[0] // SUB_ROWS):
        rows = slice(rt * SUB_ROWS, (rt + 1) * SUB_ROWS)
        emit(rt, rows, jnp.dot(x_ref[rows, :], wbf_ref[...], preferred_element_type=F32))


def _head(acc, h):
    return acc[:, h * HEAD_DIM:(h + 1) * HEAD_DIM]


def _rope(a, tab_ref, t, rows):
    return a * tab_ref[t, rows, :] + pltpu.roll(a, HEAD_DIM // 2, 1) * tab_ref[t + 1, rows, :]


def _rms(a, g_ref):
    ms = jnp.mean(a * a, axis=-1, keepdims=True)
    return a * lax.rsqrt(ms + RMS_EPS) * g_ref[...]


def _proj_a_kernel(x_ref, w_ref, tab_ref, out_ref, wbf_ref, *perm, dil):
    j = pl.program_id(0)
    _cast_weights_once(w_ref, wbf_ref)
    per_res = SUB_ROWS // dil

    def emit_with(fn):
        def emit(rt, rows, acc):
            if dil == 1:
                for h in range(A_HEADS):
                    out_ref[h, 0, rows, :] = fn(_head(acc, h), rows).astype(BF16)
                return
            perm_ref, = perm
            for h in range(A_HEADS):
                perm_ref[h, rows, :] = fn(_head(acc, h), rows)
            for h in range(A_HEADS):
                for r in range(dil):
                    piece = perm_ref[h, pl.ds(rt * SUB_ROWS + r, per_res, stride=dil), :]
                    out_ref[h, r, rt * per_res:(rt + 1) * per_res, :] = piece.astype(BF16)
        _for_sub_tiles(x_ref, wbf_ref, emit)

    @pl.when(j == 0)
    def _q():
        emit_with(lambda a, rows: _rope(a, tab_ref, _R1Q, rows))

    @pl.when(j == 1)
    def _k():
        emit_with(lambda a, rows: _rope(a, tab_ref, _R1K, rows))

    @pl.when(j == 2)
    def _v():
        emit_with(lambda a, rows: a)


def _proj_b_kernel(x_ref, w_ref, tab_ref, gq_ref, gk_ref, out_ref, wbf_ref, *, q_tiles):
    j = pl.program_id(0)
    _cast_weights_once(w_ref, wbf_ref)
    heads = COL_TILE // HEAD_DIM

    @pl.when(j < q_tiles)
    def _qb():
        def emit(rt, rows, acc):
            for h in range(heads):
                out_ref[h, rows, :] = _rope(_rms(_head(acc, h), gq_ref), tab_ref, _R2Q, rows).astype(BF16)
        _for_sub_tiles(x_ref, wbf_ref, emit)

    @pl.when(j == q_tiles)
    def _kvb():
        def emit(rt, rows, acc):
            for h in range(heads):
                if h < B_KV_HEADS:
                    out_ref[h, rows, :] = _rope(_rms(_head(acc, h), gk_ref), tab_ref, _R2K, rows).astype(BF16)
                else:
                    out_ref[h, rows, :] = _head(acc, h).astype(BF16)
        _for_sub_tiles(x_ref, wbf_ref, emit)


def _proj_gate_kernel(x_ref, w_ref, out_ref, wbf_ref):
    _cast_weights_once(w_ref, wbf_ref)

    def emit(rt, rows, acc):
        out_ref[rows, :] = (1.0 / (1.0 + jnp.exp(-acc))).astype(BF16)
    _for_sub_tiles(x_ref, wbf_ref, emit)


def _proj_specs(d_model, seq, first_tile, tile_step):
    seq_blocks = seq // ROW_TILE
    return [
        pl.BlockSpec((ROW_TILE, d_model), lambda j, i: (i, 0)),
        pl.BlockSpec((d_model, COL_TILE), lambda j, i: (0, first_tile + tile_step * j)),
        pl.BlockSpec((8, ROW_TILE, HEAD_DIM), lambda j, i: (0, i % seq_blocks, 0)),
    ]


def _proj_a(xb, w_in, tabs, group, batch, seq):
    n_tok, d_model = xb.shape
    dil = DIL_PATTERNS[group][1]
    sub_len = seq // dil
    seq_blocks = seq // ROW_TILE
    scratch = [pltpu.VMEM((d_model, COL_TILE), BF16)]
    if dil > 1:
        scratch.append(pltpu.VMEM((A_HEADS, ROW_TILE, HEAD_DIM), F32))
    return pl.pallas_call(
        functools.partial(_proj_a_kernel, dil=dil),
        out_shape=jax.ShapeDtypeStruct((3 * A_HEADS, batch, dil, sub_len, HEAD_DIM), BF16),
        grid=(3, n_tok // ROW_TILE),
        in_specs=_proj_specs(d_model, seq, group, N_DIL),
        out_specs=pl.BlockSpec((A_HEADS, None, dil, ROW_TILE // dil, HEAD_DIM),
                               lambda j, i: (j, i // seq_blocks, 0, i % seq_blocks, 0)),
        scratch_shapes=scratch,
        compiler_params=_params(("arbitrary", "arbitrary")),
    )(xb, w_in, tabs)


def _proj_b(xb, w_in, tabs, gq, gk, seq):
    n_tok, d_model = xb.shape
    q_tiles = B_Q_HEADS * HEAD_DIM // COL_TILE
    heads = COL_TILE // HEAD_DIM
    fix = lambda j, i: (0, 0)
    return pl.pallas_call(
        functools.partial(_proj_b_kernel, q_tiles=q_tiles),
        out_shape=jax.ShapeDtypeStruct(((q_tiles + 1) * heads, n_tok, HEAD_DIM), BF16),
        grid=(q_tiles + 1, n_tok // ROW_TILE),
        in_specs=_proj_specs(d_model, seq, _T_QB, 1) + [pl.BlockSpec((1, HEAD_DIM), fix), pl.BlockSpec((1, HEAD_DIM), fix)],
        out_specs=pl.BlockSpec((heads, ROW_TILE, HEAD_DIM), lambda j, i: (j, i, 0)),
        scratch_shapes=[pltpu.VMEM((d_model, COL_TILE), BF16)],
        compiler_params=_params(("arbitrary", "arbitrary")),
    )(xb, w_in, tabs, gq, gk)


def _proj_gate(xb, w_in):
    n_tok, d_model = xb.shape
    n_ct = w_in.shape[1] // COL_TILE - _T_GATE
    return pl.pallas_call(
        _proj_gate_kernel,
        out_shape=jax.ShapeDtypeStruct((n_tok, n_ct * COL_TILE), BF16),
        grid=(n_ct, n_tok // ROW_TILE),
        in_specs=[pl.BlockSpec((ROW_TILE, d_model), lambda j, i: (i, 0)),
                  pl.BlockSpec((d_model, COL_TILE), lambda j, i: (0, _T_GATE + j))],
        out_specs=pl.BlockSpec((ROW_TILE, COL_TILE), lambda j, i: (i, j)),
        scratch_shapes=[pltpu.VMEM((d_model, COL_TILE), BF16)],
        compiler_params=_params(("arbitrary", "arbitrary")),
    )(xb, w_in)


_SUB = 128
_WIN = _SUB + 2 * BAND_HALF
_MERGE_ROWS = 256
_A_UNROLL = 8


def _attn_a_group(g, dil, seq, q_ref, k_ref, v_ref, og, lg, bias):
    sub_len = seq // dil
    per_seq = sub_len // _SUB
    shift = per_seq.bit_length() - 1
    ones = jnp.ones((_WIN, HEAD_DIM), BF16)

    def body(i, carry):
        r = lax.shift_right_logical(i, shift)
        p0 = pl.multiple_of((i & (per_seq - 1)) * _SUB, _SUB)
        start = pl.multiple_of(jnp.clip(p0 - BAND_HALF, 0, sub_len - _WIN), BAND_HALF)
        q = q_ref[r, pl.ds(p0, _SUB), :]
        k = k_ref[r, pl.ds(start, _WIN), :]
        v = v_ref[r, pl.ds(start, _WIN), :]
        s = lax.dot_general(q, k, (((1,), (1,)), ((), ())), preferred_element_type=F32)
        s = s + bias[lax.shift_right_logical(p0 - start, BAND_HALF.bit_length() - 1)]
        m = jnp.max(s, axis=1, keepdims=True)
        p = jnp.exp2(s - m).astype(BF16)
        both = jnp.dot(p, jnp.concatenate([v, ones], axis=1), preferred_element_type=F32)
        l = both[:, HEAD_DIM:]
        o = both[:, :HEAD_DIM] / l
        lse = m + jnp.log2(l)
        if dil == 1:
            rows = pl.ds(p0, _SUB)
        else:
            rows = pl.ds(p0 * dil + r, _SUB, stride=dil)
        og[g, rows, :] = o
        lg[g, rows, :] = lse
        return carry

    lax.fori_loop(0, seq // _SUB, body, 0, unroll=_A_UNROLL)


def _attn_a_kernel(*refs, seq):
    qkv, o_ref = refs[:3 * N_DIL], refs[3 * N_DIL]
    og, lg, bias = refs[3 * N_DIL + 1:]
    row = lax.broadcasted_iota(I32, (_SUB, _WIN), 0)
    col = lax.broadcasted_iota(I32, (_SUB, _WIN), 1)
    for case in range(3):
        bias[case] = jnp.where(jnp.abs(col - row - case * BAND_HALF) <= BAND_HALF, 0.0, NEG_BIG)
    for g, (_, dil) in enumerate(DIL_PATTERNS):
        _attn_a_group(g, dil, seq, qkv[3 * g], qkv[3 * g + 1], qkv[3 * g + 2], og, lg, bias)

    def merge(c, carry):
        rows = pl.ds(pl.multiple_of(c * _MERGE_ROWS, _MERGE_ROWS), _MERGE_ROWS)
        ls = [lg[g, rows, :] for g in range(N_DIL)]
        mx = functools.reduce(jnp.maximum, ls)
        ws = [jnp.exp2(l - mx) for l in ls]
        num = functools.reduce(lambda a, b: a + b, [w * og[g, rows, :] for g, w in enumerate(ws)])
        o_ref[rows, :] = (num / functools.reduce(lambda a, b: a + b, ws)).astype(o_ref.dtype)
        return carry

    lax.fori_loop(0, seq // _MERGE_ROWS, merge, 0)


def _attn_a(groups, batch, seq):
    in_specs, operands = [], []
    for g, (_, dil) in enumerate(DIL_PATTERNS):
        for kind in range(3):
            first = kind * A_HEADS
            in_specs.append(pl.BlockSpec((None, None, dil, seq // dil, HEAD_DIM),
                                         lambda b, h, first=first: (first + h, b, 0, 0, 0)))
            operands.append(groups[g])
    return pl.pallas_call(
        functools.partial(_attn_a_kernel, seq=seq),
        out_shape=jax.ShapeDtypeStruct((batch * seq, A_HEADS * HEAD_DIM), BF16),
        grid=(batch, A_HEADS),
        in_specs=in_specs,
        out_specs=pl.BlockSpec((seq, HEAD_DIM), lambda b, h: (b, h)),
        scratch_shapes=[pltpu.VMEM((N_DIL, seq, HEAD_DIM), F32), pltpu.VMEM((N_DIL, seq, HEAD_DIM), F32),
                        pltpu.VMEM((3, _SUB, _WIN), F32)],
        compiler_params=_params(("arbitrary", "arbitrary")),
    )(*operands)


_BQ = 256
_BK = 512
SCORE_BOUND = 64.0
BF16_SLACK = 1.02


def _attn_b_kernel(q_ref, k_ref, v_ref, o_ref, *, seq):
    rows = B_GROUP * _BQ
    q = q_ref[...].reshape(rows, HEAD_DIM)

    def body(c, carry):
        m, l, acc = carry
        c0 = pl.multiple_of(c * _BK, _BK)
        k = k_ref[pl.ds(c0, _BK), :]
        v = v_ref[pl.ds(c0, _BK), :]
        s = lax.dot_general(q, k, (((1,), (1,)), ((), ())), preferred_element_type=F32)
        m_new = jnp.maximum(m, jnp.max(s, axis=1, keepdims=True))
        alpha = jnp.exp2(m - m_new)
        p = jnp.exp2(s - m_new)
        l = alpha * l + jnp.sum(p, axis=1, keepdims=True)
        acc = alpha * acc + jnp.dot(p.astype(BF16), v, preferred_element_type=F32)
        return m_new, l, acc

    init = (jnp.full((rows, 1), NEG_BIG, F32), jnp.zeros((rows, 1), F32), jnp.zeros((rows, HEAD_DIM), F32))
    _, l, acc = lax.fori_loop(0, seq // _BK, body, init)
    o = acc / l
    for g in range(B_GROUP):
        o_ref[:, g * HEAD_DIM:(g + 1) * HEAD_DIM] = o[g * _BQ:(g + 1) * _BQ].astype(o_ref.dtype)


def _attn_b_bounded_kernel(q_ref, k_ref, v_ref, o_ref, *, seq):
    rows = B_GROUP * _BQ
    q = q_ref[...].reshape(rows, HEAD_DIM)
    ones = jnp.ones((_BK, HEAD_DIM), BF16)
    acc = jnp.zeros((rows, 2 * HEAD_DIM), F32)
    for c in range(seq // _BK):
        k = k_ref[c * _BK:(c + 1) * _BK, :]
        v = v_ref[c * _BK:(c + 1) * _BK, :]
        s = lax.dot_general(q, k, (((1,), (1,)), ((), ())), preferred_element_type=F32)
        p = jnp.exp2(s).astype(BF16)
        acc = acc + jnp.dot(p, jnp.concatenate([v, ones], axis=1), preferred_element_type=F32)
    o = acc[:, :HEAD_DIM] / acc[:, HEAD_DIM:HEAD_DIM + 1]
    for g in range(B_GROUP):
        o_ref[:, g * HEAD_DIM:(g + 1) * HEAD_DIM] = o[g * _BQ:(g + 1) * _BQ].astype(o_ref.dtype)


def _attn_b(slots, batch, seq, bounded):
    n_tok = batch * seq
    k0 = B_Q_HEADS
    v0 = k0 + B_KV_HEADS
    qblocks = seq // _BQ
    return pl.pallas_call(
        functools.partial(_attn_b_bounded_kernel if bounded else _attn_b_kernel, seq=seq),
        out_shape=jax.ShapeDtypeStruct((n_tok, B_Q_HEADS * HEAD_DIM), BF16),
        grid=(batch, B_KV_HEADS, qblocks),
        in_specs=[
            pl.BlockSpec((B_GROUP, _BQ, HEAD_DIM), lambda b, kv, qi: (kv, b * qblocks + qi, 0)),
            pl.BlockSpec((None, seq, HEAD_DIM), lambda b, kv, qi: (k0 + kv, b, 0)),
            pl.BlockSpec((None, seq, HEAD_DIM), lambda b, kv, qi: (v0 + kv, b, 0)),
        ],
        out_specs=pl.BlockSpec((_BQ, B_GROUP * HEAD_DIM), lambda b, kv, qi: (b * qblocks + qi, kv)),
        compiler_params=_params(("arbitrary", "arbitrary", "arbitrary")),
    )(slots, slots, slots)


_MIX_ROWS = 512


def _branch_kernel(oa_ref, ob_ref, ga_ref, gb_ref, wa_ref, wb_ref, out_ref):
    for rt in range(out_ref.shape[0] // SUB_ROWS):
        rows = slice(rt * SUB_ROWS, (rt + 1) * SUB_ROWS)
        y_a = jnp.dot(oa_ref[rows, :], wa_ref[...], preferred_element_type=F32)
        y_b = jnp.dot(ob_ref[rows, :], wb_ref[...], preferred_element_type=F32)
        out_ref[rows, :] = (ga_ref[rows, :].astype(F32) * y_a + gb_ref[rows, :].astype(F32) * y_b).astype(BF16)


def _branch_mix(o_a, o_b, gates, wa, wb):
    n_tok = o_b.shape[0]
    d_model = wa.shape[1]
    tm = _MIX_ROWS
    row = lambda i: (i, 0)
    return pl.pallas_call(
        _branch_kernel,
        out_shape=jax.ShapeDtypeStruct((n_tok, d_model), BF16),
        grid=(n_tok // tm,),
        in_specs=[pl.BlockSpec((tm, o_a.shape[1]), row),
                  pl.BlockSpec((tm, o_b.shape[1]), row),
                  pl.BlockSpec((tm, d_model), lambda i: (i, 0)),
                  pl.BlockSpec((tm, d_model), lambda i: (i, 1)),
                  pl.BlockSpec(wa.shape, lambda i: (0, 0)),
                  pl.BlockSpec(wb.shape, lambda i: (0, 0))],
        out_specs=pl.BlockSpec((tm, d_model), row),
        compiler_params=_params(("arbitrary",)),
    )(o_a, o_b, gates, gates, wa, wb)


def _layer_norm(z, g, b):
    mu = jnp.mean(z, axis=-1, keepdims=True)
    zc = z - mu
    var = jnp.mean(zc * zc, axis=-1, keepdims=True)
    return zc * lax.rsqrt(var + LN_EPS) * g + b


def _to_slabs(slab_ref, value, row0):
    rows, d = value.shape
    chunks = d // LANES
    for c in range(chunks):
        slab_ref[pl.ds(row0 * chunks + c, rows, stride=chunks), :] = value[:, c * LANES:(c + 1) * LANES]


def _from_slabs(slab_ref, rows, chunks, c):
    return slab_ref[pl.ds(c, rows, stride=chunks), :]


def _out_proj_kernel(m_ref, w_ref, x_ref, g_ref, b_ref, rcat_ref, rb_ref, h_ref, hs_ref, lg_ref, *, alpha):
    for rt in range(h_ref.shape[0] // SUB_ROWS):
        rows = slice(rt * SUB_ROWS, (rt + 1) * SUB_ROWS)
        mix = jnp.dot(m_ref[rows, :], w_ref[...], preferred_element_type=F32)
        h = _layer_norm(alpha * x_ref[rows, :] + mix, g_ref[...], b_ref[...])
        h_ref[rows, :] = h
        _to_slabs(hs_ref, h, rt * SUB_ROWS)
        hi = h.astype(BF16)
        lo = (h - hi.astype(F32)).astype(BF16)
        both = jnp.dot(hi, rcat_ref[...], preferred_element_type=F32)
        lg = both[:, :LANES] + both[:, LANES:] + jnp.dot(lo, rcat_ref[:, :LANES], preferred_element_type=F32)
        lg_ref[rows, :] = lg + rb_ref[...]


def _out_proj(merged, w_out, x2, g, b, r_cat, r_b, alpha):
    n_tok, d_model = x2.shape
    tm = _MIX_ROWS
    chunks = d_model // LANES
    row = lambda i: (i, 0)
    fix = lambda i: (0, 0)
    return pl.pallas_call(
        functools.partial(_out_proj_kernel, alpha=alpha),
        out_shape=(jax.ShapeDtypeStruct((n_tok, d_model), F32),
                   jax.ShapeDtypeStruct((n_tok * chunks, LANES), F32),
                   jax.ShapeDtypeStruct((n_tok, LANES), F32)),
        grid=(n_tok // tm,),
        in_specs=[pl.BlockSpec((tm, d_model), row),
                  pl.BlockSpec(w_out.shape, fix),
                  pl.BlockSpec((tm, d_model), row),
                  pl.BlockSpec((1, d_model), fix),
                  pl.BlockSpec((1, d_model), fix),
                  pl.BlockSpec(r_cat.shape, fix),
                  pl.BlockSpec((1, LANES), fix)],
        out_specs=(pl.BlockSpec((tm, d_model), row), pl.BlockSpec((tm * chunks, LANES), row),
                   pl.BlockSpec((tm, LANES), row)),
        compiler_params=_params(("arbitrary",)),
    )(merged, w_out, x2, g, b, r_cat, r_b)


_BIG_CHUNK = 512


def _moe_kernel(item_expert, item_row0, item_blocks, item_real, src_slab, dst_slab,
                h_hbm, wg_ref, wu_ref, wd_ref, y_hbm,
                stage, xbf, acc, obuf, gsem, ssem):
    it = pl.program_id(0)
    f = pl.program_id(1)
    n_items = pl.num_programs(0)
    n_ff = pl.num_programs(1)
    n_blocks = item_blocks[it]
    d_model = xbf.shape[1]
    chunks = d_model // LANES

    def slab(ref, first):
        return ref.at[pl.ds(pl.multiple_of(first, chunks), chunks)]

    def gather_copy(base, r):
        return pltpu.make_async_copy(slab(h_hbm, src_slab[base + r]), slab(stage, r * chunks), gsem.at[0])

    def scatter_copy(base, r):
        return pltpu.make_async_copy(slab(obuf, r * chunks), slab(y_hbm, dst_slab[base + r]), ssem.at[0])

    def for_rows(n_rows, fn):
        def group(t, c):
            for u in range(DMA_UNROLL):
                fn(t * DMA_UNROLL + u)
            return c
        n_groups = n_rows // DMA_UNROLL
        lax.fori_loop(0, n_groups, group, 0)

        def single(r, c):
            fn(r)
            return c
        lax.fori_loop(n_groups * DMA_UNROLL, n_rows, single, 0)

    def gather(item, op):
        base = item_row0[item]
        for_rows(item_blocks[item] * MOE_BLOCK, lambda r: op(gather_copy(base, r)))

    def scatter(item, op):
        base = item_row0[item]
        for_rows(item_real[item], lambda r: op(scatter_copy(base, r)))

    start = lambda cp: cp.start()
    wait = lambda cp: cp.wait()

    @pl.when((it == 0) & (f == 0))
    def _first_gather():
        gather(it, start)

    @pl.when(f == 0)
    def _gathered():
        gather(it, wait)

        def cast_rows(b, carry):
            r0 = pl.multiple_of(b * MOE_BLOCK, MOE_BLOCK)
            for c in range(chunks):
                piece = stage[pl.ds(r0 * chunks + c, MOE_BLOCK, stride=chunks), :]
                xbf[pl.ds(r0, MOE_BLOCK), c * LANES:(c + 1) * LANES] = piece.astype(BF16)
            return carry
        lax.fori_loop(0, n_blocks, cast_rows, 0)

        @pl.when(it + 1 < n_items)
        def _prefetch_next():
            gather(it + 1, start)

    def chunk(r0, rows, mode):
        xb = xbf[pl.ds(r0, rows), :]
        a = jnp.dot(xb, wg_ref[...].astype(BF16), preferred_element_type=F32)
        u = jnp.dot(xb, wu_ref[...].astype(BF16), preferred_element_type=F32)
        hid = (a / (1.0 + jnp.exp(-a)) * u).astype(BF16)
        y = jnp.dot(hid, wd_ref[...].astype(BF16), preferred_element_type=F32)
        if mode == "first":
            acc[pl.ds(r0, rows), :] = y
        elif mode == "middle":
            acc[pl.ds(r0, rows), :] += y
        else:
            if mode == "last":
                y = acc[pl.ds(r0, rows), :] + y
            for c in range(chunks):
                obuf[pl.ds(r0 * chunks + c, rows, stride=chunks), :] = y[:, c * LANES:(c + 1) * LANES]

    def run_item(mode):
        per_big = _BIG_CHUNK // MOE_BLOCK

        def big_chunk(t, c):
            chunk(pl.multiple_of(t * _BIG_CHUNK, _BIG_CHUNK), _BIG_CHUNK, mode)
            return c
        n_big = n_blocks // per_big
        lax.fori_loop(0, n_big, big_chunk, 0)
        rest = n_blocks - n_big * per_big
        for blocks in range(1, per_big):
            @pl.when(rest == blocks)
            def _rest(blocks=blocks):
                chunk(pl.multiple_of(n_big * _BIG_CHUNK, _BIG_CHUNK), blocks * MOE_BLOCK, mode)

    last = n_ff - 1

    @pl.when((n_blocks > 0) & (f == last))
    def _final():
        @pl.when(it > 0)
        def _drain_prev():
            scatter(it - 1, wait)
        run_item("only" if n_ff == 1 else "last")
        scatter(it, start)

    if n_ff > 1:
        @pl.when((n_blocks > 0) & (f == 0))
        def _first():
            run_item("first")

        @pl.when((n_blocks > 0) & (f > 0) & (f < last))
        def _middle():
            run_item("middle")

    @pl.when((f == last) & (it == n_items - 1))
    def _drain_last():
        scatter(jnp.where(n_blocks > 0, it, _last_live(item_blocks, n_items)), wait)


def _last_live(item_blocks, n_items):
    def body(i, best):
        return jnp.where(item_blocks[i] > 0, i, best)
    return lax.fori_loop(0, n_items, body, 0)


def _moe(h_slabs, w_gate, w_up, w_down, item_expert, item_row0, item_blocks, item_real, src_slab, dst_slab):
    d_model = w_gate.shape[1]
    chunks = d_model // LANES
    n_tok = h_slabs.shape[0] // chunks
    ff = w_gate.shape[2]
    n_ff = ff // FF_TILE
    n_items = item_expert.shape[0]

    def ff_idx(it, f, blocks):
        return jnp.where(blocks[it] > 0, f, n_ff - 1)

    grid_spec = pltpu.PrefetchScalarGridSpec(
        num_scalar_prefetch=6,
        grid=(n_items, n_ff),
        in_specs=[
            pl.BlockSpec(memory_space=pl.ANY),
            pl.BlockSpec((None, d_model, FF_TILE), lambda it, f, ie, ir, ib, nr, st, dr: (ie[it], 0, ff_idx(it, f, ib))),
            pl.BlockSpec((None, d_model, FF_TILE), lambda it, f, ie, ir, ib, nr, st, dr: (ie[it], 0, ff_idx(it, f, ib))),
            pl.BlockSpec((None, FF_TILE, d_model), lambda it, f, ie, ir, ib, nr, st, dr: (ie[it], ff_idx(it, f, ib), 0)),
        ],
        out_specs=pl.BlockSpec(memory_space=pl.ANY),
        scratch_shapes=[
            pltpu.VMEM((ITEM_ROWS * chunks, LANES), F32),
            pltpu.VMEM((ITEM_ROWS, d_model), BF16),
            pltpu.VMEM((ITEM_ROWS, d_model), F32),
            pltpu.VMEM((ITEM_ROWS * chunks, LANES), F32),
            pltpu.SemaphoreType.DMA((1,)),
            pltpu.SemaphoreType.DMA((1,)),
        ],
    )
    return pl.pallas_call(
        _moe_kernel,
        out_shape=jax.ShapeDtypeStruct((n_tok * TOP_K * chunks, LANES), F32),
        grid_spec=grid_spec,
        compiler_params=_params(("arbitrary", "arbitrary")),
    )(item_expert, item_row0, item_blocks, item_real, src_slab, dst_slab, h_slabs, w_gate, w_up, w_down)


_COMBINE_ROWS = 512


def _combine_kernel(y0_ref, y1_ref, wt_ref, h_ref, g_ref, b_ref, out_ref, *, alpha):
    rows, d_model = h_ref.shape
    chunks = d_model // LANES
    wt = wt_ref[...]
    w0, w1 = wt[:, 0:1], wt[:, 1:2]
    for c in range(chunks):
        cols = slice(c * LANES, (c + 1) * LANES)
        ffn = w0 * _from_slabs(y0_ref, rows, chunks, c) + w1 * _from_slabs(y1_ref, rows, chunks, c)
        out_ref[:, cols] = alpha * h_ref[:, cols] + ffn
    out_ref[...] = _layer_norm(out_ref[...], g_ref[...], b_ref[...])


def _combine(y_slabs, weights, h, g, b, alpha):
    n_tok, d_model = h.shape
    tm = _COMBINE_ROWS
    chunks = d_model // LANES
    second = n_tok // tm
    row = lambda i: (i, 0)
    fix = lambda i: (0, 0)
    return pl.pallas_call(
        functools.partial(_combine_kernel, alpha=alpha),
        out_shape=jax.ShapeDtypeStruct((n_tok, d_model), F32),
        grid=(n_tok // tm,),
        in_specs=[pl.BlockSpec((tm * chunks, LANES), row),
                  pl.BlockSpec((tm * chunks, LANES), lambda i: (second + i, 0)),
                  pl.BlockSpec((tm, TOP_K), row),
                  pl.BlockSpec((tm, d_model), row),
                  pl.BlockSpec((1, d_model), fix),
                  pl.BlockSpec((1, d_model), fix)],
        out_specs=pl.BlockSpec((tm, d_model), row),
        compiler_params=_params(("arbitrary",)),
    )(y_slabs, y_slabs, weights, h, g, b)


def _rope_tables(seq):
    f32 = np.float32
    half = HEAD_DIM // 2
    inv1 = np.power(f32(ROPE_THETA), -(np.arange(half, dtype=f32) / f32(half)))
    ang1 = np.arange(seq, dtype=f32)[:, None] * inv1[None, :]
    t = np.arange(seq)
    n_axis = HEAD_DIM // 4
    inv2 = np.power(f32(ROPE_THETA), -(np.arange(n_axis, dtype=f32) / f32(n_axis)))
    ang2 = np.concatenate([(t // GRID_W).astype(f32)[:, None] * inv2[None, :],
                           (t % GRID_W).astype(f32)[:, None] * inv2[None, :]], axis=-1)
    q_scale = f32(HEAD_DIM ** -0.5 * LOG2E)
    out = []
    for ang in (ang1, ang2):
        cos = np.concatenate([np.cos(ang), np.cos(ang)], axis=-1)
        sin = np.concatenate([-np.sin(ang), np.sin(ang)], axis=-1)
        out += [cos * q_scale, sin * q_scale, cos, sin]
    return jnp.asarray(np.stack(out, axis=0).astype(f32))


def _route(logits, n_tok, chunks):
    g_logits = logits[:, :N_GROUPS]
    g_prob = jax.nn.softmax(g_logits, axis=-1)
    g_idx = jnp.argmax(g_logits, axis=-1)
    g_gate = jnp.take_along_axis(g_prob, g_idx[:, None], axis=1)[:, 0]
    e_logits = logits[:, N_GROUPS:N_GROUPS + N_EXPERTS].reshape(n_tok, N_GROUPS, EXPERTS_PER_GROUP)
    e_logits = jnp.take_along_axis(e_logits, g_idx[:, None, None], axis=1)[:, 0]
    e_prob = jax.nn.softmax(e_logits, axis=-1)
    top_p, top_i = lax.top_k(e_prob, TOP_K)
    top_p = top_p / jnp.sum(top_p, axis=-1, keepdims=True)
    weights = g_gate[:, None] * top_p
    expert = g_idx[:, None] * EXPERTS_PER_GROUP + top_i

    n_slot = n_tok * TOP_K
    e_flat = expert.reshape(n_slot).astype(I32)
    order = jnp.argsort(e_flat).astype(I32)
    counts = jnp.bincount(e_flat, length=N_EXPERTS).astype(I32)
    starts = jnp.cumsum(counts) - counts
    padded = ((counts + MOE_BLOCK - 1) // MOE_BLOCK) * MOE_BLOCK
    p_ends = jnp.cumsum(padded)
    p_starts = p_ends - padded
    buf_len = n_slot + N_EXPERTS * MOE_BLOCK
    pos = jnp.arange(buf_len, dtype=I32)
    e_pos = jnp.minimum(jnp.sum(pos[:, None] >= p_ends[None, :], axis=1, dtype=I32), N_EXPERTS - 1)
    idx = pos - p_starts[e_pos]
    slot_at = jnp.where(idx < counts[e_pos], order[jnp.clip(starts[e_pos] + idx, 0, n_slot - 1)], 0)
    tok_at, k_at = slot_at // TOP_K, slot_at % TOP_K
    src_slab = tok_at * chunks
    dst_slab = (k_at * n_tok + tok_at) * chunks

    n_items = N_EXPERTS + n_slot // ITEM_ROWS
    per_expert = (padded + ITEM_ROWS - 1) // ITEM_ROWS
    item_ends = jnp.cumsum(per_expert)
    total = item_ends[-1]
    ids = jnp.arange(n_items, dtype=I32)
    last = jnp.maximum(total - 1, 0)
    live = ids < total
    e_of = jnp.minimum(jnp.sum(jnp.minimum(ids, last)[:, None] >= item_ends[None, :], axis=1, dtype=I32), N_EXPERTS - 1)
    chunk = jnp.minimum(ids, last) - (item_ends[e_of] - per_expert[e_of])
    item_row0 = jnp.where(live, p_starts[e_of] + chunk * ITEM_ROWS, 0).astype(I32)
    item_blocks = jnp.where(live, jnp.clip(padded[e_of] // MOE_BLOCK - chunk * ITEM_BLOCKS, 0, ITEM_BLOCKS), 0).astype(I32)
    item_real = jnp.where(live, jnp.clip(counts[e_of] - chunk * ITEM_ROWS, 0, ITEM_ROWS), 0).astype(I32)
    return weights, e_of, item_row0, item_blocks, item_real, src_slab, dst_slab


def kernel(x, w_in, q_norm_g, k_norm_g, w_branch_a, w_branch_b, w_out, ln1_g, ln1_b, w_group, b_group, w_router,
           b_router, w_gate, w_up, w_down, ln2_g, ln2_b):
    batch, seq, d_model = x.shape
    depth = w_in.shape[0]
    n_tok = batch * seq
    dn_alpha = (2 * depth) ** 0.25
    tabs = _rope_tables(seq)
    h = x.reshape(n_tok, d_model)
    for layer in range(depth):
        xb = h.astype(BF16)
        groups = [_proj_a(xb, w_in[layer], tabs, g, batch, seq) for g in range(N_DIL)]
        slots_b = _proj_b(xb, w_in[layer], tabs, q_norm_g[layer][None, :], k_norm_g[layer][None, :], seq)
        gates = _proj_gate(xb, w_in[layer])
        o_a = _attn_a(groups, batch, seq)
        score_bound = (jnp.max(jnp.abs(q_norm_g[layer])) * jnp.max(jnp.abs(k_norm_g[layer]))
                       * (HEAD_DIM ** 0.5 * LOG2E * BF16_SLACK))
        o_b = lax.cond(score_bound <= SCORE_BOUND,
                       lambda s: _attn_b(s, batch, seq, True), lambda s: _attn_b(s, batch, seq, False), slots_b)
        merged = _branch_mix(o_a, o_b, gates, w_branch_a[layer].astype(BF16), w_branch_b[layer].astype(BF16))
        w_r = jnp.concatenate([w_group[layer], w_router[layer]], axis=1)
        w_r = jnp.pad(w_r, ((0, 0), (0, LANES - w_r.shape[1])))
        r_hi = w_r.astype(BF16)
        r_lo = (w_r - r_hi.astype(F32)).astype(BF16)
        r_b = jnp.pad(jnp.concatenate([b_group[layer], b_router[layer]]), (0, LANES - N_GROUPS - N_EXPERTS))[None, :]
        h1, h1_slabs, logits = _out_proj(merged, w_out[layer].astype(BF16), h, ln1_g[layer][None, :],
                                         ln1_b[layer][None, :], jnp.concatenate([r_hi, r_lo], axis=1), r_b, dn_alpha)
        weights, item_expert, item_row0, item_blocks, item_real, src_slab, dst_slab = _route(
            logits, n_tok, d_model // LANES)
        y_slabs = _moe(h1_slabs, w_gate[layer], w_up[layer], w_down[layer], item_expert, item_row0, item_blocks,
                       item_real, src_slab, dst_slab)
        h = _combine(y_slabs, weights, h1, ln2_g[layer][None, :], ln2_b[layer][None, :], dn_alpha)
    return h.reshape(batch, seq, d_model)
```

```python
import functools
import math

import jax
import jax.numpy as jnp
import numpy as np
from jax import lax
from jax.experimental import pallas as pl
from jax.experimental.pallas import tpu as pltpu

F32 = jnp.float32
BF16 = jnp.bfloat16
I32 = jnp.int32

HEAD_DIM = 128
ROPE_THETA = 10000.0
GRID_W = 64
DIL_PATTERNS = ((128, 1), (512, 4), (2048, 16))
N_DIL = len(DIL_PATTERNS)
A_HEADS = 8
BAND_HALF = 64
B_Q_HEADS = 16
B_KV_HEADS = 4
B_GROUP = B_Q_HEADS // B_KV_HEADS
N_GROUPS = 4
EXPERTS_PER_GROUP = 8
N_EXPERTS = N_GROUPS * EXPERTS_PER_GROUP
TOP_K = 2
MOE_BLOCK = 128
LN_EPS = 1e-5
RMS_EPS = 1e-6
NEG_BIG = -1e30
LOG2E = math.log2(math.e)

LANES = 128
COL_TILE = 8 * HEAD_DIM
ROW_TILE = 1024
SUB_ROWS = 256
ITEM_ROWS = 1024
ITEM_BLOCKS = ITEM_ROWS // MOE_BLOCK
FF_TILE = 256
DMA_UNROLL = 8
VMEM_LIMIT = 56 * 1024 * 1024


def _params(sem, vmem=VMEM_LIMIT):
    return pltpu.CompilerParams(dimension_semantics=sem, vmem_limit_bytes=vmem)


_T_QA, _T_KA, _T_VA, _T_QB, _T_KVB, _T_GATE = 0, 3, 6, 9, 11, 12
_R1Q, _R1K, _R2Q, _R2K = 0, 2, 4, 6


def _cast_weights_once(w_ref, wbf_ref):
    @pl.when(pl.program_id(1) == 0)
    def _cast():
        wbf_ref[...] = w_ref[...].astype(BF16)


def _for_sub_tiles(x_ref, wbf_ref, emit):
    for rt in range(x_ref.shape[0] // SUB_ROWS):
        rows = slice(rt * SUB_ROWS, (rt + 1) * SUB_ROWS)
        emit(rt, rows, jnp.dot(x_ref[rows, :], wbf_ref[...], preferred_element_type=F32))


def _head(acc, h):
    return acc[:, h * HEAD_DIM:(h + 1) * HEAD_DIM]


def _rope(a, tab_ref, t, rows):
    return a * tab_ref[t, rows, :] + pltpu.roll(a, HEAD_DIM // 2, 1) * tab_ref[t + 1, rows, :]


def _rms(a, g_ref):
    ms = jnp.mean(a * a, axis=-1, keepdims=True)
    return a * lax.rsqrt(ms + RMS_EPS) * g_ref[...]


def _proj_a_kernel(x_ref, w_ref, tab_ref, out_ref, wbf_ref, *perm, dil):
    j = pl.program_id(0)
    _cast_weights_once(w_ref, wbf_ref)
    per_res = SUB_ROWS // dil

    def emit_with(fn):
        def emit(rt, rows, acc):
            if dil == 1:
                for h in range(A_HEADS):
                    out_ref[h, 0, rows, :] = fn(_head(acc, h), rows).astype(BF16)
                return
            perm_ref, = perm
            for h in range(A_HEADS):
                perm_ref[h, rows, :] = fn(_head(acc, h), rows)
            for h in range(A_HEADS):
                for r in range(dil):
                    piece = perm_ref[h, pl.ds(rt * SUB_ROWS + r, per_res, stride=dil), :]
                    out_ref[h, r, rt * per_res:(rt + 1) * per_res, :] = piece.astype(BF16)
        _for_sub_tiles(x_ref, wbf_ref, emit)

    @pl.when(j == 0)
    def _q():
        emit_with(lambda a, rows: _rope(a, tab_ref, _R1Q, rows))

    @pl.when(j == 1)
    def _k():
        emit_with(lambda a, rows: _rope(a, tab_ref, _R1K, rows))

    @pl.when(j == 2)
    def _v():
        emit_with(lambda a, rows: a)


def _proj_b_kernel(x_ref, w_ref, tab_ref, gq_ref, gk_ref, out_ref, wbf_ref, *, q_tiles):
    j = pl.program_id(0)
    _cast_weights_once(w_ref, wbf_ref)
    heads = COL_TILE // HEAD_DIM

    @pl.when(j < q_tiles)
    def _qb():
        def emit(rt, rows, acc):
            for h in range(heads):
                out_ref[h, rows, :] = _rope(_rms(_head(acc, h), gq_ref), tab_ref, _R2Q, rows).astype(BF16)
        _for_sub_tiles(x_ref, wbf_ref, emit)

    @pl.when(j == q_tiles)
    def _kvb():
        def emit(rt, rows, acc):
            for h in range(heads):
                if h < B_KV_HEADS:
                    out_ref[h, rows, :] = _rope(_rms(_head(acc, h), gk_ref), tab_ref, _R2K, rows).astype(BF16)
                else:
                    out_ref[h, rows, :] = _head(acc, h).astype(BF16)
        _for_sub_tiles(x_ref, wbf_ref, emit)


def _proj_gate_kernel(x_ref, w_ref, out_ref, wbf_ref):
    _cast_weights_once(w_ref, wbf_ref)

    def emit(rt, rows, acc):
        out_ref[rows, :] = (1.0 / (1.0 + jnp.exp(-acc))).astype(BF16)
    _for_sub_tiles(x_ref, wbf_ref, emit)


def _proj_specs(d_model, seq, first_tile, tile_step):
    seq_blocks = seq // ROW_TILE
    return [
        pl.BlockSpec((ROW_TILE, d_model), lambda j, i: (i, 0)),
        pl.BlockSpec((d_model, COL_TILE), lambda j, i: (0, first_tile + tile_step * j)),
        pl.BlockSpec((8, ROW_TILE, HEAD_DIM), lambda j, i: (0, i % seq_blocks, 0)),
    ]


def _proj_a(xb, w_in, tabs, group, batch, seq):
    n_tok, d_model = xb.shape
    dil = DIL_PATTERNS[group][1]
    sub_len = seq // dil
    seq_blocks = seq // ROW_TILE
    scratch = [pltpu.VMEM((d_model, COL_TILE), BF16)]
    if dil > 1:
        scratch.append(pltpu.VMEM((A_HEADS, ROW_TILE, HEAD_DIM), F32))
    return pl.pallas_call(
        functools.partial(_proj_a_kernel, dil=dil),
        out_shape=jax.ShapeDtypeStruct((3 * A_HEADS, batch, dil, sub_len, HEAD_DIM), BF16),
        grid=(3, n_tok // ROW_TILE),
        in_specs=_proj_specs(d_model, seq, group, N_DIL),
        out_specs=pl.BlockSpec((A_HEADS, None, dil, ROW_TILE // dil, HEAD_DIM),
                               lambda j, i: (j, i // seq_blocks, 0, i % seq_blocks, 0)),
        scratch_shapes=scratch,
        compiler_params=_params(("arbitrary", "arbitrary")),
    )(xb, w_in, tabs)


def _proj_b(xb, w_in, tabs, gq, gk, seq):
    n_tok, d_model = xb.shape
    q_tiles = B_Q_HEADS * HEAD_DIM // COL_TILE
    heads = COL_TILE // HEAD_DIM
    fix = lambda j, i: (0, 0)
    return pl.pallas_call(
        functools.partial(_proj_b_kernel, q_tiles=q_tiles),
        out_shape=jax.ShapeDtypeStruct(((q_tiles + 1) * heads, n_tok, HEAD_DIM), BF16),
        grid=(q_tiles + 1, n_tok // ROW_TILE),
        in_specs=_proj_specs(d_model, seq, _T_QB, 1) + [pl.BlockSpec((1, HEAD_DIM), fix), pl.BlockSpec((1, HEAD_DIM), fix)],
        out_specs=pl.BlockSpec((heads, ROW_TILE, HEAD_DIM), lambda j, i: (j, i, 0)),
        scratch_shapes=[pltpu.VMEM((d_model, COL_TILE), BF16)],
        compiler_params=_params(("arbitrary", "arbitrary")),
    )(xb, w_in, tabs, gq, gk)


def _proj_gate(xb, w_in):
    n_tok, d_model = xb.shape
    n_ct = w_in.shape[1] // COL_TILE - _T_GATE
    return pl.pallas_call(
        _proj_gate_kernel,
        out_shape=jax.ShapeDtypeStruct((n_tok, n_ct * COL_TILE), BF16),
        grid=(n_ct, n_tok // ROW_TILE),
        in_specs=[pl.BlockSpec((ROW_TILE, d_model), lambda j, i: (i, 0)),
                  pl.BlockSpec((d_model, COL_TILE), lambda j, i: (0, _T_GATE + j))],
        out_specs=pl.BlockSpec((ROW_TILE, COL_TILE), lambda j, i: (i, j)),
        scratch_shapes=[pltpu.VMEM((d_model, COL_TILE), BF16)],
        compiler_params=_params(("arbitrary", "arbitrary")),
    )(xb, w_in)


_SUB = 128
_WIN = _SUB + 2 * BAND_HALF
_MERGE_ROWS = 256
_A_UNROLL = 8


def _attn_a_group(g, dil, seq, q_ref, k_ref, v_ref, og, lg, bias):
    sub_len = seq // dil
    per_seq = sub_len // _SUB
    shift = per_seq.bit_length() - 1
    ones = jnp.ones((_WIN, HEAD_DIM), BF16)

    def body(i, carry):
        r = lax.shift_right_logical(i, shift)
        p0 = pl.multiple_of((i & (per_seq - 1)) * _SUB, _SUB)
        start = pl.multiple_of(jnp.clip(p0 - BAND_HALF, 0, sub_len - _WIN), BAND_HALF)
        q = q_ref[r, pl.ds(p0, _SUB), :]
        k = k_ref[r, pl.ds(start, _WIN), :]
        v = v_ref[r, pl.ds(start, _WIN), :]
        s = lax.dot_general(q, k, (((1,), (1,)), ((), ())), preferred_element_type=F32)
        s = s + bias[lax.shift_right_logical(p0 - start, BAND_HALF.bit_length() - 1)]
        m = jnp.max(s, axis=1, keepdims=True)
        p = jnp.exp2(s - m).astype(BF16)
        both = jnp.dot(p, jnp.concatenate([v, ones], axis=1), preferred_element_type=F32)
        l = both[:, HEAD_DIM:]
        o = both[:, :HEAD_DIM] / l
        lse = m + jnp.log2(l)
        if dil == 1:
            rows = pl.ds(p0, _SUB)
        else:
            rows = pl.ds(p0 * dil + r, _SUB, stride=dil)
        og[g, rows, :] = o
        lg[g, rows, :] = lse
        return carry

    lax.fori_loop(0, seq // _SUB, body, 0, unroll=_A_UNROLL)


def _attn_a_kernel(*refs, seq):
    qkv, o_ref = refs[:3 * N_DIL], refs[3 * N_DIL]
    og, lg, bias = refs[3 * N_DIL + 1:]
    row = lax.broadcasted_iota(I32, (_SUB, _WIN), 0)
    col = lax.broadcasted_iota(I32, (_SUB, _WIN), 1)
    for case in range(3):
        bias[case] = jnp.where(jnp.abs(col - row - case * BAND_HALF) <= BAND_HALF, 0.0, NEG_BIG)
    for g, (_, dil) in enumerate(DIL_PATTERNS):
        _attn_a_group(g, dil, seq, qkv[3 * g], qkv[3 * g + 1], qkv[3 * g + 2], og, lg, bias)

    def merge(c, carry):
        rows = pl.ds(pl.multiple_of(c * _MERGE_ROWS, _MERGE_ROWS), _MERGE_ROWS)
        ls = [lg[g, rows, :] for g in range(N_DIL)]
        mx = functools.reduce(jnp.maximum, ls)
        ws = [jnp.exp2(l - mx) for l in ls]
        num = functools.reduce(lambda a, b: a + b, [w * og[g, rows, :] for g, w in enumerate(ws)])
        o_ref[rows, :] = (num / functools.reduce(lambda a, b: a + b, ws)).astype(o_ref.dtype)
        return carry

    lax.fori_loop(0, seq // _MERGE_ROWS, merge, 0)


def _attn_a(groups, batch, seq):
    in_specs, operands = [], []
    for g, (_, dil) in enumerate(DIL_PATTERNS):
        for kind in range(3):
            first = kind * A_HEADS
            in_specs.append(pl.BlockSpec((None, None, dil, seq // dil, HEAD_DIM),
                                         lambda b, h, first=first: (first + h, b, 0, 0, 0)))
            operands.append(groups[g])
    return pl.pallas_call(
        functools.partial(_attn_a_kernel, seq=seq),
        out_shape=jax.ShapeDtypeStruct((batch * seq, A_HEADS * HEAD_DIM), BF16),
        grid=(batch, A_HEADS),
        in_specs=in_specs,
        out_specs=pl.BlockSpec((seq, HEAD_DIM), lambda b, h: (b, h)),
        scratch_shapes=[pltpu.VMEM((N_DIL, seq, HEAD_DIM), F32), pltpu.VMEM((N_DIL, seq, HEAD_DIM), F32),
                        pltpu.VMEM((3, _SUB, _WIN), F32)],
        compiler_params=_params(("arbitrary", "arbitrary")),
    )(*operands)


_BQ = 256
_BK = 512
SCORE_BOUND = 64.0
BF16_SLACK = 1.02


def _attn_b_kernel(q_ref, k_ref, v_ref, o_ref, *, seq):
    rows = B_GROUP * _BQ
    q = q_ref[...].reshape(rows, HEAD_DIM)

    def body(c, carry):
        m, l, acc = carry
        c0 = pl.multiple_of(c * _BK, _BK)
        k = k_ref[pl.ds(c0, _BK), :]
        v = v_ref[pl.ds(c0, _BK), :]
        s = lax.dot_general(q, k, (((1,), (1,)), ((), ())), preferred_element_type=F32)
        m_new = jnp.maximum(m, jnp.max(s, axis=1, keepdims=True))
        alpha = jnp.exp2(m - m_new)
        p = jnp.exp2(s - m_new)
        l = alpha * l + jnp.sum(p, axis=1, keepdims=True)
        acc = alpha * acc + jnp.dot(p.astype(BF16), v, preferred_element_type=F32)
        return m_new, l, acc

    init = (jnp.full((rows, 1), NEG_BIG, F32), jnp.zeros((rows, 1), F32), jnp.zeros((rows, HEAD_DIM), F32))
    _, l, acc = lax.fori_loop(0, seq // _BK, body, init)
    o = acc / l
    for g in range(B_GROUP):
        o_ref[:, g * HEAD_DIM:(g + 1) * HEAD_DIM] = o[g * _BQ:(g + 1) * _BQ].astype(o_ref.dtype)


def _attn_b_bounded_kernel(q_ref, k_ref, v_ref, o_ref, *, seq):
    rows = B_GROUP * _BQ
    q = q_ref[...].reshape(rows, HEAD_DIM)
    ones = jnp.ones((_BK, HEAD_DIM), BF16)
    acc = jnp.zeros((rows, 2 * HEAD_DIM), F32)
    for c in range(seq // _BK):
        k = k_ref[c * _BK:(c + 1) * _BK, :]
        v = v_ref[c * _BK:(c + 1) * _BK, :]
        s = lax.dot_general(q, k, (((1,), (1,)), ((), ())), preferred_element_type=F32)
        p = jnp.exp2(s).astype(BF16)
        acc = acc + jnp.dot(p, jnp.concatenate([v, ones], axis=1), preferred_element_type=F32)
    o = acc[:, :HEAD_DIM] / acc[:, HEAD_DIM:HEAD_DIM + 1]
    for g in range(B_GROUP):
        o_ref[:, g * HEAD_DIM:(g + 1) * HEAD_DIM] = o[g * _BQ:(g + 1) * _BQ].astype(o_ref.dtype)


def _attn_b(slots, batch, seq, bounded):
    n_tok = batch * seq
    k0 = B_Q_HEADS
    v0 = k0 + B_KV_HEADS
    qblocks = seq // _BQ
    return pl.pallas_call(
        functools.partial(_attn_b_bounded_kernel if bounded else _attn_b_kernel, seq=seq),
        out_shape=jax.ShapeDtypeStruct((n_tok, B_Q_HEADS * HEAD_DIM), BF16),
        grid=(batch, B_KV_HEADS, qblocks),
        in_specs=[
            pl.BlockSpec((B_GROUP, _BQ, HEAD_DIM), lambda b, kv, qi: (kv, b * qblocks + qi, 0)),
            pl.BlockSpec((None, seq, HEAD_DIM), lambda b, kv, qi: (k0 + kv, b, 0)),
            pl.BlockSpec((None, seq, HEAD_DIM), lambda b, kv, qi: (v0 + kv, b, 0)),
        ],
        out_specs=pl.BlockSpec((_BQ, B_GROUP * HEAD_DIM), lambda b, kv, qi: (b * qblocks + qi, kv)),
        compiler_params=_params(("arbitrary", "arbitrary", "arbitrary")),
    )(slots, slots, slots)


_MIX_ROWS = 512


def _branch_kernel(oa_ref, ob_ref, ga_ref, gb_ref, wa_ref, wb_ref, out_ref):
    for rt in range(out_ref.shape[0] // SUB_ROWS):
        rows = slice(rt * SUB_ROWS, (rt + 1) * SUB_ROWS)
        y_a = jnp.dot(oa_ref[rows, :], wa_ref[...], preferred_element_type=F32)
        y_b = jnp.dot(ob_ref[rows, :], wb_ref[...], preferred_element_type=F32)
        out_ref[rows, :] = (ga_ref[rows, :].astype(F32) * y_a + gb_ref[rows, :].astype(F32) * y_b).astype(BF16)


def _branch_mix(o_a, o_b, gates, wa, wb):
    n_tok = o_b.shape[0]
    d_model = wa.shape[1]
    tm = _MIX_ROWS
    row = lambda i: (i, 0)
    return pl.pallas_call(
        _branch_kernel,
        out_shape=jax.ShapeDtypeStruct((n_tok, d_model), BF16),
        grid=(n_tok // tm,),
        in_specs=[pl.BlockSpec((tm, o_a.shape[1]), row),
                  pl.BlockSpec((tm, o_b.shape[1]), row),
                  pl.BlockSpec((tm, d_model), lambda i: (i, 0)),
                  pl.BlockSpec((tm, d_model), lambda i: (i, 1)),
                  pl.BlockSpec(wa.shape, lambda i: (0, 0)),
                  pl.BlockSpec(wb.shape, lambda i: (0, 0))],
        out_specs=pl.BlockSpec((tm, d_model), row),
        compiler_params=_params(("arbitrary",)),
    )(o_a, o_b, gates, gates, wa, wb)


def _layer_norm(z, g, b):
    mu = jnp.mean(z, axis=-1, keepdims=True)
    zc = z - mu
    var = jnp.mean(zc * zc, axis=-1, keepdims=True)
    return zc * lax.rsqrt(var + LN_EPS) * g + b


def _to_slabs(slab_ref, value, row0):
    rows, d = value.shape
    chunks = d // LANES
    for c in range(chunks):
        slab_ref[pl.ds(row0 * chunks + c, rows, stride=chunks), :] = value[:, c * LANES:(c + 1) * LANES]


def _from_slabs(slab_ref, rows, chunks, c):
    return slab_ref[pl.ds(c, rows, stride=chunks), :]


def _out_proj_kernel(m_ref, w_ref, x_ref, g_ref, b_ref, rcat_ref, rb_ref, h_ref, hs_ref, lg_ref, *, alpha):
    for rt in range(h_ref.shape[0] // SUB_ROWS):
        rows = slice(rt * SUB_ROWS, (rt + 1) * SUB_ROWS)
        mix = jnp.dot(m_ref[rows, :], w_ref[...], preferred_element_type=F32)
        h = _layer_norm(alpha * x_ref[rows, :] + mix, g_ref[...], b_ref[...])
        h_ref[rows, :] = h
        _to_slabs(hs_ref, h, rt * SUB_ROWS)
        hi = h.astype(BF16)
        lo = (h - hi.astype(F32)).astype(BF16)
        both = jnp.dot(hi, rcat_ref[...], preferred_element_type=F32)
        lg = both[:, :LANES] + both[:, LANES:] + jnp.dot(lo, rcat_ref[:, :LANES], preferred_element_type=F32)
        lg_ref[rows, :] = lg + rb_ref[...]


def _out_proj(merged, w_out, x2, g, b, r_cat, r_b, alpha):
    n_tok, d_model = x2.shape
    tm = _MIX_ROWS
    chunks = d_model // LANES
    row = lambda i: (i, 0)
    fix = lambda i: (0, 0)
    return pl.pallas_call(
        functools.partial(_out_proj_kernel, alpha=alpha),
        out_shape=(jax.ShapeDtypeStruct((n_tok, d_model), F32),
                   jax.ShapeDtypeStruct((n_tok * chunks, LANES), F32),
                   jax.ShapeDtypeStruct((n_tok, LANES), F32)),
        grid=(n_tok // tm,),
        in_specs=[pl.BlockSpec((tm, d_model), row),
                  pl.BlockSpec(w_out.shape, fix),
                  pl.BlockSpec((tm, d_model), row),
                  pl.BlockSpec((1, d_model), fix),
                  pl.BlockSpec((1, d_model), fix),
                  pl.BlockSpec(r_cat.shape, fix),
                  pl.BlockSpec((1, LANES), fix)],
        out_specs=(pl.BlockSpec((tm, d_model), row), pl.BlockSpec((tm * chunks, LANES), row),
                   pl.BlockSpec((tm, LANES), row)),
        compiler_params=_params(("arbitrary",)),
    )(merged, w_out, x2, g, b, r_cat, r_b)


_BIG_CHUNK = 512


def _moe_kernel(item_expert, item_row0, item_blocks, item_real, src_slab, dst_slab,
                h_hbm, wg_ref, wu_ref, wd_ref, y_hbm,
                stage, xbf, acc, obuf, gsem, ssem):
    it = pl.program_id(0)
    f = pl.program_id(1)
    n_items = pl.num_programs(0)
    n_ff = pl.num_programs(1)
    n_blocks = item_blocks[it]
    d_model = xbf.shape[1]
    chunks = d_model // LANES

    def slab(ref, first):
        return ref.at[pl.ds(pl.multiple_of(first, chunks), chunks)]

    def gather_copy(base, r):
        return pltpu.make_async_copy(slab(h_hbm, src_slab[base + r]), slab(stage, r * chunks), gsem.at[0])

    def scatter_copy(base, r):
        return pltpu.make_async_copy(slab(obuf, r * chunks), slab(y_hbm, dst_slab[base + r]), ssem.at[0])

    def for_rows(n_rows, fn):
        def group(t, c):
            for u in range(DMA_UNROLL):
                fn(t * DMA_UNROLL + u)
            return c
        n_groups = n_rows // DMA_UNROLL
        lax.fori_loop(0, n_groups, group, 0)

        def single(r, c):
            fn(r)
            return c
        lax.fori_loop(n_groups * DMA_UNROLL, n_rows, single, 0)

    def gather(item, op):
        base = item_row0[item]
        for_rows(item_blocks[item] * MOE_BLOCK, lambda r: op(gather_copy(base, r)))

    def scatter(item, op):
        base = item_row0[item]
        for_rows(item_real[item], lambda r: op(scatter_copy(base, r)))

    start = lambda cp: cp.start()
    wait = lambda cp: cp.wait()

    @pl.when((it == 0) & (f == 0))
    def _first_gather():
        gather(it, start)

    @pl.when(f == 0)
    def _gathered():
        gather(it, wait)

        def cast_rows(b, carry):
            r0 = pl.multiple_of(b * MOE_BLOCK, MOE_BLOCK)
            for c in range(chunks):
                piece = stage[pl.ds(r0 * chunks + c, MOE_BLOCK, stride=chunks), :]
                xbf[pl.ds(r0, MOE_BLOCK), c * LANES:(c + 1) * LANES] = piece.astype(BF16)
            return carry
        lax.fori_loop(0, n_blocks, cast_rows, 0)

        @pl.when(it + 1 < n_items)
        def _prefetch_next():
            gather(it + 1, start)

    def chunk(r0, rows, mode):
        xb = xbf[pl.ds(r0, rows), :]
        a = jnp.dot(xb, wg_ref[...].astype(BF16), preferred_element_type=F32)
        u = jnp.dot(xb, wu_ref[...].astype(BF16), preferred_element_type=F32)
        hid = (a / (1.0 + jnp.exp(-a)) * u).astype(BF16)
        y = jnp.dot(hid, wd_ref[...].astype(BF16), preferred_element_type=F32)
        if mode == "first":
            acc[pl.ds(r0, rows), :] = y
        elif mode == "middle":
            acc[pl.ds(r0, rows), :] += y
        else:
            if mode == "last":
                y = acc[pl.ds(r0, rows), :] + y
            for c in range(chunks):
                obuf[pl.ds(r0 * chunks + c, rows, stride=chunks), :] = y[:, c * LANES:(c + 1) * LANES]

    def run_item(mode):
        per_big = _BIG_CHUNK // MOE_BLOCK

        def big_chunk(t, c):
            chunk(pl.multiple_of(t * _BIG_CHUNK, _BIG_CHUNK), _BIG_CHUNK, mode)
            return c
        n_big = n_blocks // per_big
        lax.fori_loop(0, n_big, big_chunk, 0)
        rest = n_blocks - n_big * per_big
        for blocks in range(1, per_big):
            @pl.when(rest == blocks)
            def _rest(blocks=blocks):
                chunk(pl.multiple_of(n_big * _BIG_CHUNK, _BIG_CHUNK), blocks * MOE_BLOCK, mode)

    last = n_ff - 1

    @pl.when((n_blocks > 0) & (f == last))
    def _final():
        @pl.when(it > 0)
        def _drain_prev():
            scatter(it - 1, wait)
        run_item("only" if n_ff == 1 else "last")
        scatter(it, start)

    if n_ff > 1:
        @pl.when((n_blocks > 0) & (f == 0))
        def _first():
            run_item("first")

        @pl.when((n_blocks > 0) & (f > 0) & (f < last))
        def _middle():
            run_item("middle")

    @pl.when((f == last) & (it == n_items - 1))
    def _drain_last():
        scatter(jnp.where(n_blocks > 0, it, _last_live(item_blocks, n_items)), wait)


def _last_live(item_blocks, n_items):
    def body(i, best):
        return jnp.where(item_blocks[i] > 0, i, best)
    return lax.fori_loop(0, n_items, body, 0)


def _moe(h_slabs, w_gate, w_up, w_down, item_expert, item_row0, item_blocks, item_real, src_slab, dst_slab):
    d_model = w_gate.shape[1]
    chunks = d_model // LANES
    n_tok = h_slabs.shape[0] // chunks
    ff = w_gate.shape[2]
    n_ff = ff // FF_TILE
    n_items = item_expert.shape[0]

    def ff_idx(it, f, blocks):
        return jnp.where(blocks[it] > 0, f, n_ff - 1)

    grid_spec = pltpu.PrefetchScalarGridSpec(
        num_scalar_prefetch=6,
        grid=(n_items, n_ff),
        in_specs=[
            pl.BlockSpec(memory_space=pl.ANY),
            pl.BlockSpec((None, d_model, FF_TILE), lambda it, f, ie, ir, ib, nr, st, dr: (ie[it], 0, ff_idx(it, f, ib))),
            pl.BlockSpec((None, d_model, FF_TILE), lambda it, f, ie, ir, ib, nr, st, dr: (ie[it], 0, ff_idx(it, f, ib))),
            pl.BlockSpec((None, FF_TILE, d_model), lambda it, f, ie, ir, ib, nr, st, dr: (ie[it], ff_idx(it, f, ib), 0)),
        ],
        out_specs=pl.BlockSpec(memory_space=pl.ANY),
        scratch_shapes=[
            pltpu.VMEM((ITEM_ROWS * chunks, LANES), F32),
            pltpu.VMEM((ITEM_ROWS, d_model), BF16),
            pltpu.VMEM((ITEM_ROWS, d_model), F32),
            pltpu.VMEM((ITEM_ROWS * chunks, LANES), F32),
            pltpu.SemaphoreType.DMA((1,)),
            pltpu.SemaphoreType.DMA((1,)),
        ],
    )
    return pl.pallas_call(
        _moe_kernel,
        out_shape=jax.ShapeDtypeStruct((n_tok * TOP_K * chunks, LANES), F32),
        grid_spec=grid_spec,
        compiler_params=_params(("arbitrary", "arbitrary")),
    )(item_expert, item_row0, item_blocks, item_real, src_slab, dst_slab, h_slabs, w_gate, w_up, w_down)


_COMBINE_ROWS = 512


def _combine_kernel(y0_ref, y1_ref, wt_ref, h_ref, g_ref, b_ref, out_ref, *, alpha):
    rows, d_model = h_ref.shape
    chunks = d_model // LANES
    wt = wt_ref[...]
    w0, w1 = wt[:, 0:1], wt[:, 1:2]
    for c in range(chunks):
        cols = slice(c * LANES, (c + 1) * LANES)
        ffn = w0 * _from_slabs(y0_ref, rows, chunks, c) + w1 * _from_slabs(y1_ref, rows, chunks, c)
        out_ref[:, cols] = alpha * h_ref[:, cols] + ffn
    out_ref[...] = _layer_norm(out_ref[...], g_ref[...], b_ref[...])


def _combine(y_slabs, weights, h, g, b, alpha):
    n_tok, d_model = h.shape
    tm = _COMBINE_ROWS
    chunks = d_model // LANES
    second = n_tok // tm
    row = lambda i: (i, 0)
    fix = lambda i: (0, 0)
    return pl.pallas_call(
        functools.partial(_combine_kernel, alpha=alpha),
        out_shape=jax.ShapeDtypeStruct((n_tok, d_model), F32),
        grid=(n_tok // tm,),
        in_specs=[pl.BlockSpec((tm * chunks, LANES), row),
                  pl.BlockSpec((tm * chunks, LANES), lambda i: (second + i, 0)),
                  pl.BlockSpec((tm, TOP_K), row),
                  pl.BlockSpec((tm, d_model), row),
                  pl.BlockSpec((1, d_model), fix),
                  pl.BlockSpec((1, d_model), fix)],
        out_specs=pl.BlockSpec((tm, d_model), row),
        compiler_params=_params(("arbitrary",)),
    )(y_slabs, y_slabs, weights, h, g, b)


def _rope_tables(seq):
    f32 = np.float32
    half = HEAD_DIM // 2
    inv1 = np.power(f32(ROPE_THETA), -(np.arange(half, dtype=f32) / f32(half)))
    ang1 = np.arange(seq, dtype=f32)[:, None] * inv1[None, :]
    t = np.arange(seq)
    n_axis = HEAD_DIM // 4
    inv2 = np.power(f32(ROPE_THETA), -(np.arange(n_axis, dtype=f32) / f32(n_axis)))
    ang2 = np.concatenate([(t // GRID_W).astype(f32)[:, None] * inv2[None, :],
                           (t % GRID_W).astype(f32)[:, None] * inv2[None, :]], axis=-1)
    q_scale = f32(HEAD_DIM ** -0.5 * LOG2E)
    out = []
    for ang in (ang1, ang2):
        cos = np.concatenate([np.cos(ang), np.cos(ang)], axis=-1)
        sin = np.concatenate([-np.sin(ang), np.sin(ang)], axis=-1)
        out += [cos * q_scale, sin * q_scale, cos, sin]
    return jnp.asarray(np.stack(out, axis=0).astype(f32))


def _route(logits, n_tok, chunks):
    g_logits = logits[:, :N_GROUPS]
    g_prob = jax.nn.softmax(g_logits, axis=-1)
    g_idx = jnp.argmax(g_logits, axis=-1).astype(I32)
    g_gate = jnp.max(g_prob, axis=-1)
    e_logits = logits[:, N_GROUPS:N_GROUPS + EXPERTS_PER_GROUP]
    for g in range(1, N_GROUPS):
        lo = N_GROUPS + g * EXPERTS_PER_GROUP
        e_logits = jnp.where(g_idx[:, None] == g, logits[:, lo:lo + EXPERTS_PER_GROUP], e_logits)
    e_prob = jax.nn.softmax(e_logits, axis=-1)
    i1 = jnp.argmax(e_prob, axis=-1).astype(I32)
    p1 = jnp.max(e_prob, axis=-1)
    rest = jnp.where(jnp.arange(EXPERTS_PER_GROUP, dtype=I32)[None, :] == i1[:, None], -1.0, e_prob)
    i2 = jnp.argmax(rest, axis=-1).astype(I32)
    p2 = jnp.max(rest, axis=-1)
    top_p = jnp.stack([p1, p2], axis=-1)
    top_p = top_p / jnp.sum(top_p, axis=-1, keepdims=True)
    weights = g_gate[:, None] * top_p
    expert = g_idx[:, None] * EXPERTS_PER_GROUP + jnp.stack([i1, i2], axis=-1)

    n_slot = n_tok * TOP_K
    e_flat = expert.reshape(n_slot).astype(I32)
    order = jnp.argsort(e_flat).astype(I32)
    experts = jnp.arange(N_EXPERTS, dtype=I32)
    counts = jnp.sum(e_flat[:, None] == experts[None, :], axis=0, dtype=I32)
    padded = ((counts + MOE_BLOCK - 1) // MOE_BLOCK) * MOE_BLOCK
    p_ends = jnp.cumsum(padded)
    p_starts = p_ends - padded
    buf_len = n_slot + N_EXPERTS * MOE_BLOCK

    def pick(one_hot, table):
        return jnp.sum(jnp.where(one_hot, table[None, :], 0), axis=1, dtype=I32)

    pos = jnp.arange(buf_len, dtype=I32)[:, None]
    ended = p_ends[None, :] <= pos
    inside = jnp.logical_and(p_starts[None, :] <= pos, pos < p_ends[None, :])
    idx = pos[:, 0] - jnp.max(jnp.where(ended, p_ends[None, :], 0), axis=1)
    sorted_at = jnp.clip(pick(ended, counts) + idx, 0, n_slot - 1)
    slot_at = jnp.where(idx < pick(inside, counts), order[sorted_at], 0)
    tok_at, k_at = slot_at // TOP_K, slot_at % TOP_K
    src_slab = tok_at * chunks
    dst_slab = (k_at * n_tok + tok_at) * chunks

    n_items = N_EXPERTS + n_slot // ITEM_ROWS
    per_expert = (padded + ITEM_ROWS - 1) // ITEM_ROWS
    item_ends = jnp.cumsum(per_expert)
    total = item_ends[-1]
    ids = jnp.arange(n_items, dtype=I32)
    live = ids < total
    item = jnp.minimum(ids, jnp.maximum(total - 1, 0))[:, None]
    owner = jnp.logical_and((item_ends - per_expert)[None, :] <= item, item < item_ends[None, :])
    e_of = pick(owner, experts)
    chunk = item[:, 0] - pick(owner, item_ends - per_expert)
    item_row0 = jnp.where(live, pick(owner, p_starts) + chunk * ITEM_ROWS, 0).astype(I32)
    item_blocks = jnp.where(live, jnp.clip(pick(owner, padded) // MOE_BLOCK - chunk * ITEM_BLOCKS, 0, ITEM_BLOCKS), 0)
    item_real = jnp.where(live, jnp.clip(pick(owner, counts) - chunk * ITEM_ROWS, 0, ITEM_ROWS), 0)
    return weights, e_of, item_row0, item_blocks.astype(I32), item_real.astype(I32), src_slab, dst_slab


def kernel(x, w_in, q_norm_g, k_norm_g, w_branch_a, w_branch_b, w_out, ln1_g, ln1_b, w_group, b_group, w_router,
           b_router, w_gate, w_up, w_down, ln2_g, ln2_b):
    batch, seq, d_model = x.shape
    depth = w_in.shape[0]
    n_tok = batch * seq
    dn_alpha = (2 * depth) ** 0.25
    tabs = _rope_tables(seq)
    h = x.reshape(n_tok, d_model)
    for layer in range(depth):
        xb = h.astype(BF16)
        groups = [_proj_a(xb, w_in[layer], tabs, g, batch, seq) for g in range(N_DIL)]
        slots_b = _proj_b(xb, w_in[layer], tabs, q_norm_g[layer][None, :], k_norm_g[layer][None, :], seq)
        gates = _proj_gate(xb, w_in[layer])
        o_a = _attn_a(groups, batch, seq)
        score_bound = (jnp.max(jnp.abs(q_norm_g[layer])) * jnp.max(jnp.abs(k_norm_g[layer]))
                       * (HEAD_DIM ** 0.5 * LOG2E * BF16_SLACK))
        o_b = lax.cond(score_bound <= SCORE_BOUND,
                       lambda s: _attn_b(s, batch, seq, True), lambda s: _attn_b(s, batch, seq, False), slots_b)
        merged = _branch_mix(o_a, o_b, gates, w_branch_a[layer].astype(BF16), w_branch_b[layer].astype(BF16))
        w_r = jnp.concatenate([w_group[layer], w_router[layer]], axis=1)
        w_r = jnp.pad(w_r, ((0, 0), (0, LANES - w_r.shape[1])))
        r_hi = w_r.astype(BF16)
        r_lo = (w_r - r_hi.astype(F32)).astype(BF16)
        r_b = jnp.pad(jnp.concatenate([b_group[layer], b_router[layer]]), (0, LANES - N_GROUPS - N_EXPERTS))[None, :]
        h1, h1_slabs, logits = _out_proj(merged, w_out[layer].astype(BF16), h, ln1_g[layer][None, :],
                                         ln1_b[layer][None, :], jnp.concatenate([r_hi, r_lo], axis=1), r_b, dn_alpha)
        weights, item_expert, item_row0, item_blocks, item_real, src_slab, dst_slab = _route(
            logits, n_tok, d_model // LANES)
        y_slabs = _moe(h1_slabs, w_gate[layer], w_up[layer], w_down[layer], item_expert, item_row0, item_blocks,
                       item_real, src_slab, dst_slab)
        h = _combine(y_slabs, weights, h1, ln2_g[layer][None, :], ln2_b[layer][None, :], dn_alpha)
    return h.reshape(batch, seq, d_model)
```

```python
import functools
import math

import jax
import jax.numpy as jnp
import numpy as np
from jax import lax
from jax.experimental import pallas as pl
from jax.experimental.pallas import tpu as pltpu

F32 = jnp.float32
BF16 = jnp.bfloat16
I32 = jnp.int32

HEAD_DIM = 128
ROPE_THETA = 10000.0
GRID_W = 64
DIL_PATTERNS = ((128, 1), (512, 4), (2048, 16))
N_DIL = len(DIL_PATTERNS)
A_HEADS = 8
BAND_HALF = 64
B_Q_HEADS = 16
B_KV_HEADS = 4
B_GROUP = B_Q_HEADS // B_KV_HEADS
N_GROUPS = 4
EXPERTS_PER_GROUP = 8
N_EXPERTS = N_GROUPS * EXPERTS_PER_GROUP
TOP_K = 2
MOE_BLOCK = 128
LN_EPS = 1e-5
RMS_EPS = 1e-6
NEG_BIG = -1e30
LOG2E = math.log2(math.e)

LANES = 128
COL_TILE = 8 * HEAD_DIM
ROW_TILE = 1024
SUB_ROWS = 256
ITEM_ROWS = 1024
ITEM_BLOCKS = ITEM_ROWS // MOE_BLOCK
FF_TILE = 256
DMA_UNROLL = 8
VMEM_LIMIT = 56 * 1024 * 1024


def _params(sem, vmem=VMEM_LIMIT):
    return pltpu.CompilerParams(dimension_semantics=sem, vmem_limit_bytes=vmem)


_T_QA, _T_KA, _T_VA, _T_QB, _T_KVB, _T_GATE = 0, 3, 6, 9, 11, 12
_R1Q, _R1K, _R2Q, _R2K = 0, 2, 4, 6


def _cast_weights_once(w_ref, wbf_ref):
    @pl.when(pl.program_id(1) == 0)
    def _cast():
        wbf_ref[...] = w_ref[...].astype(BF16)


def _for_sub_tiles(x_ref, wbf_ref, emit):
    for rt in range(x_ref.shape[0] // SUB_ROWS):
        rows = slice(rt * SUB_ROWS, (rt + 1) * SUB_ROWS)
        emit(rt, rows, jnp.dot(x_ref[rows, :], wbf_ref[...], preferred_element_type=F32))


def _head(acc, h):
    return acc[:, h * HEAD_DIM:(h + 1) * HEAD_DIM]


def _rope(a, tab_ref, t, rows):
    return a * tab_ref[t, rows, :] + pltpu.roll(a, HEAD_DIM // 2, 1) * tab_ref[t + 1, rows, :]


def _rms(a, g_ref):
    ms = jnp.mean(a * a, axis=-1, keepdims=True)
    return a * lax.rsqrt(ms + RMS_EPS) * g_ref[...]


def _proj_a_kernel(x_ref, w_ref, tab_ref, out_ref, wbf_ref, *perm, dil):
    j = pl.program_id(0)
    _cast_weights_once(w_ref, wbf_ref)
    per_res = SUB_ROWS // dil

    def emit_with(fn):
        def emit(rt, rows, acc):
            if dil == 1:
                for h in range(A_HEADS):
                    out_ref[h, 0, rows, :] = fn(_head(acc, h), rows).astype(BF16)
                return
            perm_ref, = perm
            for h in range(A_HEADS):
                perm_ref[h, rows, :] = fn(_head(acc, h), rows)
            for h in range(A_HEADS):
                for r in range(dil):
                    piece = perm_ref[h, pl.ds(rt * SUB_ROWS + r, per_res, stride=dil), :]
                    out_ref[h, r, rt * per_res:(rt + 1) * per_res, :] = piece.astype(BF16)
        _for_sub_tiles(x_ref, wbf_ref, emit)

    @pl.when(j == 0)
    def _q():
        emit_with(lambda a, rows: _rope(a, tab_ref, _R1Q, rows))

    @pl.when(j == 1)
    def _k():
        emit_with(lambda a, rows: _rope(a, tab_ref, _R1K, rows))

    @pl.when(j == 2)
    def _v():
        emit_with(lambda a, rows: a)


def _proj_b_kernel(x_ref, w_ref, tab_ref, gq_ref, gk_ref, out_ref, wbf_ref, *, q_tiles):
    j = pl.program_id(0)
    _cast_weights_once(w_ref, wbf_ref)
    heads = COL_TILE // HEAD_DIM

    @pl.when(j < q_tiles)
    def _qb():
        def emit(rt, rows, acc):
            for h in range(heads):
                out_ref[h, rows, :] = _rope(_rms(_head(acc, h), gq_ref), tab_ref, _R2Q, rows).astype(BF16)
        _for_sub_tiles(x_ref, wbf_ref, emit)

    @pl.when(j == q_tiles)
    def _kvb():
        def emit(rt, rows, acc):
            for h in range(heads):
                if h < B_KV_HEADS:
                    out_ref[h, rows, :] = _rope(_rms(_head(acc, h), gk_ref), tab_ref, _R2K, rows).astype(BF16)
                else:
                    out_ref[h, rows, :] = _head(acc, h).astype(BF16)
        _for_sub_tiles(x_ref, wbf_ref, emit)


def _proj_gate_kernel(x_ref, w_ref, out_ref, wbf_ref):
    _cast_weights_once(w_ref, wbf_ref)

    def emit(rt, rows, acc):
        out_ref[rows, :] = (1.0 / (1.0 + jnp.exp(-acc))).astype(BF16)
    _for_sub_tiles(x_ref, wbf_ref, emit)


def _proj_specs(d_model, seq, first_tile, tile_step):
    seq_blocks = seq // ROW_TILE
    return [
        pl.BlockSpec((ROW_TILE, d_model), lambda j, i: (i, 0)),
        pl.BlockSpec((d_model, COL_TILE), lambda j, i: (0, first_tile + tile_step * j)),
        pl.BlockSpec((8, ROW_TILE, HEAD_DIM), lambda j, i: (0, i % seq_blocks, 0)),
    ]


def _proj_a(xb, w_in, tabs, group, batch, seq):
    n_tok, d_model = xb.shape
    dil = DIL_PATTERNS[group][1]
    sub_len = seq // dil
    seq_blocks = seq // ROW_TILE
    scratch = [pltpu.VMEM((d_model, COL_TILE), BF16)]
    if dil > 1:
        scratch.append(pltpu.VMEM((A_HEADS, ROW_TILE, HEAD_DIM), F32))
    return pl.pallas_call(
        functools.partial(_proj_a_kernel, dil=dil),
        out_shape=jax.ShapeDtypeStruct((3 * A_HEADS, batch, dil, sub_len, HEAD_DIM), BF16),
        grid=(3, n_tok // ROW_TILE),
        in_specs=_proj_specs(d_model, seq, group, N_DIL),
        out_specs=pl.BlockSpec((A_HEADS, None, dil, ROW_TILE // dil, HEAD_DIM),
                               lambda j, i: (j, i // seq_blocks, 0, i % seq_blocks, 0)),
        scratch_shapes=scratch,
        compiler_params=_params(("arbitrary", "arbitrary")),
    )(xb, w_in, tabs)


def _proj_b(xb, w_in, tabs, gq, gk, seq):
    n_tok, d_model = xb.shape
    q_tiles = B_Q_HEADS * HEAD_DIM // COL_TILE
    heads = COL_TILE // HEAD_DIM
    fix = lambda j, i: (0, 0)
    return pl.pallas_call(
        functools.partial(_proj_b_kernel, q_tiles=q_tiles),
        out_shape=jax.ShapeDtypeStruct(((q_tiles + 1) * heads, n_tok, HEAD_DIM), BF16),
        grid=(q_tiles + 1, n_tok // ROW_TILE),
        in_specs=_proj_specs(d_model, seq, _T_QB, 1) + [pl.BlockSpec((1, HEAD_DIM), fix), pl.BlockSpec((1, HEAD_DIM), fix)],
        out_specs=pl.BlockSpec((heads, ROW_TILE, HEAD_DIM), lambda j, i: (j, i, 0)),
        scratch_shapes=[pltpu.VMEM((d_model, COL_TILE), BF16)],
        compiler_params=_params(("arbitrary", "arbitrary")),
    )(xb, w_in, tabs, gq, gk)


def _proj_gate(xb, w_in):
    n_tok, d_model = xb.shape
    n_ct = w_in.shape[1] // COL_TILE - _T_GATE
    return pl.pallas_call(
        _proj_gate_kernel,
        out_shape=jax.ShapeDtypeStruct((n_tok, n_ct * COL_TILE), BF16),
        grid=(n_ct, n_tok // ROW_TILE),
        in_specs=[pl.BlockSpec((ROW_TILE, d_model), lambda j, i: (i, 0)),
                  pl.BlockSpec((d_model, COL_TILE), lambda j, i: (0, _T_GATE + j))],
        out_specs=pl.BlockSpec((ROW_TILE, COL_TILE), lambda j, i: (i, j)),
        scratch_shapes=[pltpu.VMEM((d_model, COL_TILE), BF16)],
        compiler_params=_params(("arbitrary", "arbitrary")),
    )(xb, w_in)


_SUB = 128
_WIN = _SUB + 2 * BAND_HALF
_MERGE_ROWS = 256
_A_UNROLL = 8


def _attn_a_group(g, dil, seq, q_ref, k_ref, v_ref, og, lg, bias):
    sub_len = seq // dil
    per_seq = sub_len // _SUB
    shift = per_seq.bit_length() - 1
    ones = jnp.ones((_WIN, HEAD_DIM), BF16)

    def body(i, carry):
        r = lax.shift_right_logical(i, shift)
        p0 = pl.multiple_of((i & (per_seq - 1)) * _SUB, _SUB)
        start = pl.multiple_of(jnp.clip(p0 - BAND_HALF, 0, sub_len - _WIN), BAND_HALF)
        q = q_ref[r, pl.ds(p0, _SUB), :]
        k = k_ref[r, pl.ds(start, _WIN), :]
        v = v_ref[r, pl.ds(start, _WIN), :]
        s = lax.dot_general(q, k, (((1,), (1,)), ((), ())), preferred_element_type=F32)
        s = s + bias[lax.shift_right_logical(p0 - start, BAND_HALF.bit_length() - 1)]
        m = jnp.max(s, axis=1, keepdims=True)
        p = jnp.exp2(s - m).astype(BF16)
        both = jnp.dot(p, jnp.concatenate([v, ones], axis=1), preferred_element_type=F32)
        l = both[:, HEAD_DIM:]
        o = both[:, :HEAD_DIM] / l
        lse = m + jnp.log2(l)
        if dil == 1:
            rows = pl.ds(p0, _SUB)
        else:
            rows = pl.ds(p0 * dil + r, _SUB, stride=dil)
        og[g, rows, :] = o
        lg[g, rows, :] = lse
        return carry

    lax.fori_loop(0, seq // _SUB, body, 0, unroll=_A_UNROLL)


def _attn_a_kernel(*refs, seq):
    qkv, o_ref = refs[:3 * N_DIL], refs[3 * N_DIL]
    og, lg, bias = refs[3 * N_DIL + 1:]
    row = lax.broadcasted_iota(I32, (_SUB, _WIN), 0)
    col = lax.broadcasted_iota(I32, (_SUB, _WIN), 1)
    for case in range(3):
        bias[case] = jnp.where(jnp.abs(col - row - case * BAND_HALF) <= BAND_HALF, 0.0, NEG_BIG)
    for g, (_, dil) in enumerate(DIL_PATTERNS):
        _attn_a_group(g, dil, seq, qkv[3 * g], qkv[3 * g + 1], qkv[3 * g + 2], og, lg, bias)

    def merge(c, carry):
        rows = pl.ds(pl.multiple_of(c * _MERGE_ROWS, _MERGE_ROWS), _MERGE_ROWS)
        ls = [lg[g, rows, :] for g in range(N_DIL)]
        mx = functools.reduce(jnp.maximum, ls)
        ws = [jnp.exp2(l - mx) for l in ls]
        num = functools.reduce(lambda a, b: a + b, [w * og[g, rows, :] for g, w in enumerate(ws)])
        o_ref[rows, :] = (num / functools.reduce(lambda a, b: a + b, ws)).astype(o_ref.dtype)
        return carry

    lax.fori_loop(0, seq // _MERGE_ROWS, merge, 0)


def _attn_a(groups, batch, seq):
    in_specs, operands = [], []
    for g, (_, dil) in enumerate(DIL_PATTERNS):
        for kind in range(3):
            first = kind * A_HEADS
            in_specs.append(pl.BlockSpec((None, None, dil, seq // dil, HEAD_DIM),
                                         lambda b, h, first=first: (first + h, b, 0, 0, 0)))
            operands.append(groups[g])
    return pl.pallas_call(
        functools.partial(_attn_a_kernel, seq=seq),
        out_shape=jax.ShapeDtypeStruct((batch * seq, A_HEADS * HEAD_DIM), BF16),
        grid=(batch, A_HEADS),
        in_specs=in_specs,
        out_specs=pl.BlockSpec((seq, HEAD_DIM), lambda b, h: (b, h)),
        scratch_shapes=[pltpu.VMEM((N_DIL, seq, HEAD_DIM), F32), pltpu.VMEM((N_DIL, seq, HEAD_DIM), F32),
                        pltpu.VMEM((3, _SUB, _WIN), F32)],
        compiler_params=_params(("arbitrary", "arbitrary")),
    )(*operands)


_BQ = 512
_BK = 512
SCORE_BOUND = 64.0
BF16_SLACK = 1.02


def _attn_b_kernel(q_ref, k_ref, v_ref, o_ref, *, seq):
    rows = B_GROUP * _BQ
    q = q_ref[...].reshape(rows, HEAD_DIM)

    def body(c, carry):
        m, l, acc = carry
        c0 = pl.multiple_of(c * _BK, _BK)
        k = k_ref[pl.ds(c0, _BK), :]
        v = v_ref[pl.ds(c0, _BK), :]
        s = lax.dot_general(q, k, (((1,), (1,)), ((), ())), preferred_element_type=F32)
        m_new = jnp.maximum(m, jnp.max(s, axis=1, keepdims=True))
        alpha = jnp.exp2(m - m_new)
        p = jnp.exp2(s - m_new)
        l = alpha * l + jnp.sum(p, axis=1, keepdims=True)
        acc = alpha * acc + jnp.dot(p.astype(BF16), v, preferred_element_type=F32)
        return m_new, l, acc

    init = (jnp.full((rows, 1), NEG_BIG, F32), jnp.zeros((rows, 1), F32), jnp.zeros((rows, HEAD_DIM), F32))
    _, l, acc = lax.fori_loop(0, seq // _BK, body, init)
    o = acc / l
    for g in range(B_GROUP):
        o_ref[:, g * HEAD_DIM:(g + 1) * HEAD_DIM] = o[g * _BQ:(g + 1) * _BQ].astype(o_ref.dtype)


def _attn_b_bounded_kernel(q_ref, k_ref, v_ref, o_ref, *, seq):
    rows = B_GROUP * _BQ
    q = q_ref[...].reshape(rows, HEAD_DIM)
    ones = jnp.ones((_BK, HEAD_DIM), BF16)
    acc = jnp.zeros((rows, 2 * HEAD_DIM), F32)
    for c in range(seq // _BK):
        k = k_ref[c * _BK:(c + 1) * _BK, :]
        v = v_ref[c * _BK:(c + 1) * _BK, :]
        s = lax.dot_general(q, k, (((1,), (1,)), ((), ())), preferred_element_type=F32)
        p = jnp.exp2(s).astype(BF16)
        acc = acc + jnp.dot(p, jnp.concatenate([v, ones], axis=1), preferred_element_type=F32)
    o = acc[:, :HEAD_DIM] / acc[:, HEAD_DIM:HEAD_DIM + 1]
    for g in range(B_GROUP):
        o_ref[:, g * HEAD_DIM:(g + 1) * HEAD_DIM] = o[g * _BQ:(g + 1) * _BQ].astype(o_ref.dtype)


def _attn_b(slots, batch, seq, bounded):
    n_tok = batch * seq
    k0 = B_Q_HEADS
    v0 = k0 + B_KV_HEADS
    qblocks = seq // _BQ
    return pl.pallas_call(
        functools.partial(_attn_b_bounded_kernel if bounded else _attn_b_kernel, seq=seq),
        out_shape=jax.ShapeDtypeStruct((n_tok, B_Q_HEADS * HEAD_DIM), BF16),
        grid=(batch, B_KV_HEADS, qblocks),
        in_specs=[
            pl.BlockSpec((B_GROUP, _BQ, HEAD_DIM), lambda b, kv, qi: (kv, b * qblocks + qi, 0)),
            pl.BlockSpec((None, seq, HEAD_DIM), lambda b, kv, qi: (k0 + kv, b, 0)),
            pl.BlockSpec((None, seq, HEAD_DIM), lambda b, kv, qi: (v0 + kv, b, 0)),
        ],
        out_specs=pl.BlockSpec((_BQ, B_GROUP * HEAD_DIM), lambda b, kv, qi: (b * qblocks + qi, kv)),
        compiler_params=_params(("arbitrary", "arbitrary", "arbitrary")),
    )(slots, slots, slots)


_MIX_ROWS = 512


def _branch_kernel(oa_ref, ob_ref, ga_ref, gb_ref, wa_ref, wb_ref, out_ref):
    for rt in range(out_ref.shape[0] // SUB_ROWS):
        rows = slice(rt * SUB_ROWS, (rt + 1) * SUB_ROWS)
        y_a = jnp.dot(oa_ref[rows, :], wa_ref[...], preferred_element_type=F32)
        y_b = jnp.dot(ob_ref[rows, :], wb_ref[...], preferred_element_type=F32)
        out_ref[rows, :] = (ga_ref[rows, :].astype(F32) * y_a + gb_ref[rows, :].astype(F32) * y_b).astype(BF16)


def _branch_mix(o_a, o_b, gates, wa, wb):
    n_tok = o_b.shape[0]
    d_model = wa.shape[1]
    tm = _MIX_ROWS
    row = lambda i: (i, 0)
    return pl.pallas_call(
        _branch_kernel,
        out_shape=jax.ShapeDtypeStruct((n_tok, d_model), BF16),
        grid=(n_tok // tm,),
        in_specs=[pl.BlockSpec((tm, o_a.shape[1]), row),
                  pl.BlockSpec((tm, o_b.shape[1]), row),
                  pl.BlockSpec((tm, d_model), lambda i: (i, 0)),
                  pl.BlockSpec((tm, d_model), lambda i: (i, 1)),
                  pl.BlockSpec(wa.shape, lambda i: (0, 0)),
                  pl.BlockSpec(wb.shape, lambda i: (0, 0))],
        out_specs=pl.BlockSpec((tm, d_model), row),
        compiler_params=_params(("arbitrary",)),
    )(o_a, o_b, gates, gates, wa, wb)


def _layer_norm(z, g, b):
    mu = jnp.mean(z, axis=-1, keepdims=True)
    zc = z - mu
    var = jnp.mean(zc * zc, axis=-1, keepdims=True)
    return zc * lax.rsqrt(var + LN_EPS) * g + b


def _to_slabs(slab_ref, value, row0):
    rows, d = value.shape
    chunks = d // LANES
    for c in range(chunks):
        slab_ref[pl.ds(row0 * chunks + c, rows, stride=chunks), :] = value[:, c * LANES:(c + 1) * LANES]


def _from_slabs(slab_ref, rows, chunks, c):
    return slab_ref[pl.ds(c, rows, stride=chunks), :]


def _out_proj_kernel(m_ref, w_ref, x_ref, g_ref, b_ref, rcat_ref, rb_ref, h_ref, hs_ref, lg_ref, *, alpha):
    for rt in range(h_ref.shape[0] // SUB_ROWS):
        rows = slice(rt * SUB_ROWS, (rt + 1) * SUB_ROWS)
        mix = jnp.dot(m_ref[rows, :], w_ref[...], preferred_element_type=F32)
        h = _layer_norm(alpha * x_ref[rows, :] + mix, g_ref[...], b_ref[...])
        h_ref[rows, :] = h
        _to_slabs(hs_ref, h, rt * SUB_ROWS)
        hi = h.astype(BF16)
        lo = (h - hi.astype(F32)).astype(BF16)
        both = jnp.dot(hi, rcat_ref[...], preferred_element_type=F32)
        lg = both[:, :LANES] + both[:, LANES:] + jnp.dot(lo, rcat_ref[:, :LANES], preferred_element_type=F32)
        lg_ref[rows, :] = lg + rb_ref[...]


def _out_proj(merged, w_out, x2, g, b, r_cat, r_b, alpha):
    n_tok, d_model = x2.shape
    tm = _MIX_ROWS
    chunks = d_model // LANES
    row = lambda i: (i, 0)
    fix = lambda i: (0, 0)
    return pl.pallas_call(
        functools.partial(_out_proj_kernel, alpha=alpha),
        out_shape=(jax.ShapeDtypeStruct((n_tok, d_model), F32),
                   jax.ShapeDtypeStruct((n_tok * chunks, LANES), F32),
                   jax.ShapeDtypeStruct((n_tok, LANES), F32)),
        grid=(n_tok // tm,),
        in_specs=[pl.BlockSpec((tm, d_model), row),
                  pl.BlockSpec(w_out.shape, fix),
                  pl.BlockSpec((tm, d_model), row),
                  pl.BlockSpec((1, d_model), fix),
                  pl.BlockSpec((1, d_model), fix),
                  pl.BlockSpec(r_cat.shape, fix),
                  pl.BlockSpec((1, LANES), fix)],
        out_specs=(pl.BlockSpec((tm, d_model), row), pl.BlockSpec((tm * chunks, LANES), row),
                   pl.BlockSpec((tm, LANES), row)),
        compiler_params=_params(("arbitrary",)),
    )(merged, w_out, x2, g, b, r_cat, r_b)


_BIG_CHUNK = 512


def _moe_kernel(item_expert, item_row0, item_blocks, item_real, src_slab, dst_slab,
                h_hbm, wg_ref, wu_ref, wd_ref, y_hbm,
                stage, xbf, acc, obuf, gsem, ssem):
    it = pl.program_id(0)
    f = pl.program_id(1)
    n_items = pl.num_programs(0)
    n_ff = pl.num_programs(1)
    n_blocks = item_blocks[it]
    d_model = xbf.shape[1]
    chunks = d_model // LANES

    def slab(ref, first):
        return ref.at[pl.ds(pl.multiple_of(first, chunks), chunks)]

    def gather_copy(base, r):
        return pltpu.make_async_copy(slab(h_hbm, src_slab[base + r]), slab(stage, r * chunks), gsem.at[0])

    def scatter_copy(base, r):
        return pltpu.make_async_copy(slab(obuf, r * chunks), slab(y_hbm, dst_slab[base + r]), ssem.at[0])

    def for_rows(n_rows, fn):
        def group(t, c):
            for u in range(DMA_UNROLL):
                fn(t * DMA_UNROLL + u)
            return c
        n_groups = n_rows // DMA_UNROLL
        lax.fori_loop(0, n_groups, group, 0)

        def single(r, c):
            fn(r)
            return c
        lax.fori_loop(n_groups * DMA_UNROLL, n_rows, single, 0)

    def gather(item, op):
        base = item_row0[item]
        for_rows(item_blocks[item] * MOE_BLOCK, lambda r: op(gather_copy(base, r)))

    def scatter(item, op):
        base = item_row0[item]
        for_rows(item_real[item], lambda r: op(scatter_copy(base, r)))

    start = lambda cp: cp.start()
    wait = lambda cp: cp.wait()

    @pl.when((it == 0) & (f == 0))
    def _first_gather():
        gather(it, start)

    @pl.when(f == 0)
    def _gathered():
        gather(it, wait)

    def chunk(r0, rows, mode):
        if mode in ("first", "only"):
            xb = jnp.concatenate([stage[pl.ds(r0 * chunks + c, rows, stride=chunks), :].astype(BF16)
                                  for c in range(chunks)], axis=1)
            if mode == "first":
                xbf[pl.ds(r0, rows), :] = xb
        else:
            xb = xbf[pl.ds(r0, rows), :]
        a = jnp.dot(xb, wg_ref[...].astype(BF16), preferred_element_type=F32)
        u = jnp.dot(xb, wu_ref[...].astype(BF16), preferred_element_type=F32)
        hid = (a / (1.0 + jnp.exp(-a)) * u).astype(BF16)
        y = jnp.dot(hid, wd_ref[...].astype(BF16), preferred_element_type=F32)
        if mode == "first":
            acc[pl.ds(r0, rows), :] = y
        elif mode == "middle":
            acc[pl.ds(r0, rows), :] += y
        else:
            if mode == "last":
                y = acc[pl.ds(r0, rows), :] + y
            for c in range(chunks):
                obuf[pl.ds(r0 * chunks + c, rows, stride=chunks), :] = y[:, c * LANES:(c + 1) * LANES]

    def run_item(mode):
        per_big = _BIG_CHUNK // MOE_BLOCK

        def big_chunk(t, c):
            chunk(pl.multiple_of(t * _BIG_CHUNK, _BIG_CHUNK), _BIG_CHUNK, mode)
            return c
        n_big = n_blocks // per_big
        lax.fori_loop(0, n_big, big_chunk, 0)
        rest = n_blocks - n_big * per_big
        for blocks in range(1, per_big):
            @pl.when(rest == blocks)
            def _rest(blocks=blocks):
                chunk(pl.multiple_of(n_big * _BIG_CHUNK, _BIG_CHUNK), blocks * MOE_BLOCK, mode)

    last = n_ff - 1

    @pl.when((n_blocks > 0) & (f == last))
    def _final():
        @pl.when(it > 0)
        def _drain_prev():
            scatter(it - 1, wait)
        run_item("only" if n_ff == 1 else "last")
        scatter(it, start)

    if n_ff > 1:
        @pl.when((n_blocks > 0) & (f == 0))
        def _first():
            run_item("first")

    @pl.when((f == 0) & (it + 1 < n_items))
    def _prefetch_next():
        gather(it + 1, start)

    if n_ff > 1:
        @pl.when((n_blocks > 0) & (f > 0) & (f < last))
        def _middle():
            run_item("middle")

    @pl.when((f == last) & (it == n_items - 1))
    def _drain_last():
        scatter(jnp.where(n_blocks > 0, it, _last_live(item_blocks, n_items)), wait)


def _last_live(item_blocks, n_items):
    def body(i, best):
        return jnp.where(item_blocks[i] > 0, i, best)
    return lax.fori_loop(0, n_items, body, 0)


def _moe(h_slabs, w_gate, w_up, w_down, item_expert, item_row0, item_blocks, item_real, src_slab, dst_slab):
    d_model = w_gate.shape[1]
    chunks = d_model // LANES
    n_tok = h_slabs.shape[0] // chunks
    ff = w_gate.shape[2]
    n_ff = ff // FF_TILE
    n_items = item_expert.shape[0]

    def ff_idx(it, f, blocks):
        return jnp.where(blocks[it] > 0, f, n_ff - 1)

    grid_spec = pltpu.PrefetchScalarGridSpec(
        num_scalar_prefetch=6,
        grid=(n_items, n_ff),
        in_specs=[
            pl.BlockSpec(memory_space=pl.ANY),
            pl.BlockSpec((None, d_model, FF_TILE), lambda it, f, ie, ir, ib, nr, st, dr: (ie[it], 0, ff_idx(it, f, ib))),
            pl.BlockSpec((None, d_model, FF_TILE), lambda it, f, ie, ir, ib, nr, st, dr: (ie[it], 0, ff_idx(it, f, ib))),
            pl.BlockSpec((None, FF_TILE, d_model), lambda it, f, ie, ir, ib, nr, st, dr: (ie[it], ff_idx(it, f, ib), 0)),
        ],
        out_specs=pl.BlockSpec(memory_space=pl.ANY),
        scratch_shapes=[
            pltpu.VMEM((ITEM_ROWS * chunks, LANES), F32),
            pltpu.VMEM((ITEM_ROWS, d_model), BF16),
            pltpu.VMEM((ITEM_ROWS, d_model), F32),
            pltpu.VMEM((ITEM_ROWS * chunks, LANES), F32),
            pltpu.SemaphoreType.DMA((1,)),
            pltpu.SemaphoreType.DMA((1,)),
        ],
    )
    return pl.pallas_call(
        _moe_kernel,
        out_shape=jax.ShapeDtypeStruct((n_tok * TOP_K * chunks, LANES), F32),
        grid_spec=grid_spec,
        compiler_params=_params(("arbitrary", "arbitrary")),
    )(item_expert, item_row0, item_blocks, item_real, src_slab, dst_slab, h_slabs, w_gate, w_up, w_down)


_COMBINE_ROWS = 512


def _combine_kernel(y0_ref, y1_ref, wt_ref, h_ref, g_ref, b_ref, out_ref, *, alpha):
    rows, d_model = h_ref.shape
    chunks = d_model // LANES
    wt = wt_ref[...]
    w0, w1 = wt[:, 0:1], wt[:, 1:2]
    for c in range(chunks):
        cols = slice(c * LANES, (c + 1) * LANES)
        ffn = w0 * _from_slabs(y0_ref, rows, chunks, c) + w1 * _from_slabs(y1_ref, rows, chunks, c)
        out_ref[:, cols] = alpha * h_ref[:, cols] + ffn
    out_ref[...] = _layer_norm(out_ref[...], g_ref[...], b_ref[...])


def _combine(y_slabs, weights, h, g, b, alpha):
    n_tok, d_model = h.shape
    tm = _COMBINE_ROWS
    chunks = d_model // LANES
    second = n_tok // tm
    row = lambda i: (i, 0)
    fix = lambda i: (0, 0)
    return pl.pallas_call(
        functools.partial(_combine_kernel, alpha=alpha),
        out_shape=jax.ShapeDtypeStruct((n_tok, d_model), F32),
        grid=(n_tok // tm,),
        in_specs=[pl.BlockSpec((tm * chunks, LANES), row),
                  pl.BlockSpec((tm * chunks, LANES), lambda i: (second + i, 0)),
                  pl.BlockSpec((tm, TOP_K), row),
                  pl.BlockSpec((tm, d_model), row),
                  pl.BlockSpec((1, d_model), fix),
                  pl.BlockSpec((1, d_model), fix)],
        out_specs=pl.BlockSpec((tm, d_model), row),
        compiler_params=_params(("arbitrary",)),
    )(y_slabs, y_slabs, weights, h, g, b)


def _rope_tables(seq):
    f32 = np.float32
    half = HEAD_DIM // 2
    inv1 = np.power(f32(ROPE_THETA), -(np.arange(half, dtype=f32) / f32(half)))
    ang1 = np.arange(seq, dtype=f32)[:, None] * inv1[None, :]
    t = np.arange(seq)
    n_axis = HEAD_DIM // 4
    inv2 = np.power(f32(ROPE_THETA), -(np.arange(n_axis, dtype=f32) / f32(n_axis)))
    ang2 = np.concatenate([(t // GRID_W).astype(f32)[:, None] * inv2[None, :],
                           (t % GRID_W).astype(f32)[:, None] * inv2[None, :]], axis=-1)
    q_scale = f32(HEAD_DIM ** -0.5 * LOG2E)
    out = []
    for ang in (ang1, ang2):
        cos = np.concatenate([np.cos(ang), np.cos(ang)], axis=-1)
        sin = np.concatenate([-np.sin(ang), np.sin(ang)], axis=-1)
        out += [cos * q_scale, sin * q_scale, cos, sin]
    return jnp.asarray(np.stack(out, axis=0).astype(f32))


def _route(logits, n_tok, chunks):
    g_logits = logits[:, :N_GROUPS]
    g_prob = jax.nn.softmax(g_logits, axis=-1)
    g_idx = jnp.argmax(g_logits, axis=-1).astype(I32)
    g_gate = jnp.max(g_prob, axis=-1)
    e_logits = logits[:, N_GROUPS:N_GROUPS + EXPERTS_PER_GROUP]
    for g in range(1, N_GROUPS):
        lo = N_GROUPS + g * EXPERTS_PER_GROUP
        e_logits = jnp.where(g_idx[:, None] == g, logits[:, lo:lo + EXPERTS_PER_GROUP], e_logits)
    e_prob = jax.nn.softmax(e_logits, axis=-1)
    i1 = jnp.argmax(e_prob, axis=-1).astype(I32)
    p1 = jnp.max(e_prob, axis=-1)
    rest = jnp.where(jnp.arange(EXPERTS_PER_GROUP, dtype=I32)[None, :] == i1[:, None], -1.0, e_prob)
    i2 = jnp.argmax(rest, axis=-1).astype(I32)
    p2 = jnp.max(rest, axis=-1)
    top_p = jnp.stack([p1, p2], axis=-1)
    top_p = top_p / jnp.sum(top_p, axis=-1, keepdims=True)
    weights = g_gate[:, None] * top_p
    expert = g_idx[:, None] * EXPERTS_PER_GROUP + jnp.stack([i1, i2], axis=-1)

    n_slot = n_tok * TOP_K
    e_flat = expert.reshape(n_slot).astype(I32)
    order = jnp.argsort(e_flat).astype(I32)
    experts = jnp.arange(N_EXPERTS, dtype=I32)
    counts = jnp.sum(e_flat[:, None] == experts[None, :], axis=0, dtype=I32)
    padded = ((counts + MOE_BLOCK - 1) // MOE_BLOCK) * MOE_BLOCK
    p_ends = jnp.cumsum(padded)
    p_starts = p_ends - padded
    buf_len = n_slot + N_EXPERTS * MOE_BLOCK

    def pick(one_hot, table):
        return jnp.sum(jnp.where(one_hot, table[None, :], 0), axis=1, dtype=I32)

    pos = jnp.arange(buf_len, dtype=I32)[:, None]
    ended = p_ends[None, :] <= pos
    inside = jnp.logical_and(p_starts[None, :] <= pos, pos < p_ends[None, :])
    idx = pos[:, 0] - jnp.max(jnp.where(ended, p_ends[None, :], 0), axis=1)
    sorted_at = jnp.clip(pick(ended, counts) + idx, 0, n_slot - 1)
    slot_at = jnp.where(idx < pick(inside, counts), order[sorted_at], 0)
    tok_at, k_at = slot_at // TOP_K, slot_at % TOP_K
    src_slab = tok_at * chunks
    dst_slab = (k_at * n_tok + tok_at) * chunks

    n_items = N_EXPERTS + n_slot // ITEM_ROWS
    per_expert = (padded + ITEM_ROWS - 1) // ITEM_ROWS
    item_ends = jnp.cumsum(per_expert)
    total = item_ends[-1]
    ids = jnp.arange(n_items, dtype=I32)
    live = ids < total
    item = jnp.minimum(ids, jnp.maximum(total - 1, 0))[:, None]
    owner = jnp.logical_and((item_ends - per_expert)[None, :] <= item, item < item_ends[None, :])
    e_of = pick(owner, experts)
    chunk = item[:, 0] - pick(owner, item_ends - per_expert)
    item_row0 = jnp.where(live, pick(owner, p_starts) + chunk * ITEM_ROWS, 0).astype(I32)
    item_blocks = jnp.where(live, jnp.clip(pick(owner, padded) // MOE_BLOCK - chunk * ITEM_BLOCKS, 0, ITEM_BLOCKS), 0)
    item_real = jnp.where(live, jnp.clip(pick(owner, counts) - chunk * ITEM_ROWS, 0, ITEM_ROWS), 0)
    return weights, e_of, item_row0, item_blocks.astype(I32), item_real.astype(I32), src_slab, dst_slab


def kernel(x, w_in, q_norm_g, k_norm_g, w_branch_a, w_branch_b, w_out, ln1_g, ln1_b, w_group, b_group, w_router,
           b_router, w_gate, w_up, w_down, ln2_g, ln2_b):
    batch, seq, d_model = x.shape
    depth = w_in.shape[0]
    n_tok = batch * seq
    dn_alpha = (2 * depth) ** 0.25
    tabs = _rope_tables(seq)
    h = x.reshape(n_tok, d_model)
    for layer in range(depth):
        xb = h.astype(BF16)
        groups = [_proj_a(xb, w_in[layer], tabs, g, batch, seq) for g in range(N_DIL)]
        slots_b = _proj_b(xb, w_in[layer], tabs, q_norm_g[layer][None, :], k_norm_g[layer][None, :], seq)
        gates = _proj_gate(xb, w_in[layer])
        o_a = _attn_a(groups, batch, seq)
        score_bound = (jnp.max(jnp.abs(q_norm_g[layer])) * jnp.max(jnp.abs(k_norm_g[layer]))
                       * (HEAD_DIM ** 0.5 * LOG2E * BF16_SLACK))
        o_b = lax.cond(score_bound <= SCORE_BOUND,
                       lambda s: _attn_b(s, batch, seq, True), lambda s: _attn_b(s, batch, seq, False), slots_b)
        merged = _branch_mix(o_a, o_b, gates, w_branch_a[layer].astype(BF16), w_branch_b[layer].astype(BF16))
        w_r = jnp.concatenate([w_group[layer], w_router[layer]], axis=1)
        w_r = jnp.pad(w_r, ((0, 0), (0, LANES - w_r.shape[1])))
        r_hi = w_r.astype(BF16)
        r_lo = (w_r - r_hi.astype(F32)).astype(BF16)
        r_b = jnp.pad(jnp.concatenate([b_group[layer], b_router[layer]]), (0, LANES - N_GROUPS - N_EXPERTS))[None, :]
        h1, h1_slabs, logits = _out_proj(merged, w_out[layer].astype(BF16), h, ln1_g[layer][None, :],
                                         ln1_b[layer][None, :], jnp.concatenate([r_hi, r_lo], axis=1), r_b, dn_alpha)
        weights, item_expert, item_row0, item_blocks, item_real, src_slab, dst_slab = _route(
            logits, n_tok, d_model // LANES)
        y_slabs = _moe(h1_slabs, w_gate[layer], w_up[layer], w_down[layer], item_expert, item_row0, item_blocks,
                       item_real, src_slab, dst_slab)
        h = _combine(y_slabs, weights, h1, ln2_g[layer][None, :], ln2_b[layer][None, :], dn_alpha)
    return h.reshape(batch, seq, d_model)
```

```python
import functools
import math

import jax
import jax.numpy as jnp
import numpy as np
from jax import lax
from jax.experimental import pallas as pl
from jax.experimental.pallas import tpu as pltpu

F32 = jnp.float32
BF16 = jnp.bfloat16
I32 = jnp.int32

HEAD_DIM = 128
ROPE_THETA = 10000.0
GRID_W = 64
DIL_PATTERNS = ((128, 1), (512, 4), (2048, 16))
N_DIL = len(DIL_PATTERNS)
A_HEADS = 8
BAND_HALF = 64
B_Q_HEADS = 16
B_KV_HEADS = 4
B_GROUP = B_Q_HEADS // B_KV_HEADS
N_GROUPS = 4
EXPERTS_PER_GROUP = 8
N_EXPERTS = N_GROUPS * EXPERTS_PER_GROUP
TOP_K = 2
MOE_BLOCK = 128
LN_EPS = 1e-5
RMS_EPS = 1e-6
NEG_BIG = -1e30
LOG2E = math.log2(math.e)

LANES = 128
COL_TILE = 8 * HEAD_DIM
ROW_TILE = 1024
SUB_ROWS = 256
ITEM_ROWS = 1024
ITEM_BLOCKS = ITEM_ROWS // MOE_BLOCK
FF_TILE = 256
DMA_UNROLL = 8
VMEM_LIMIT = 56 * 1024 * 1024


def _params(sem, vmem=VMEM_LIMIT):
    return pltpu.CompilerParams(dimension_semantics=sem, vmem_limit_bytes=vmem)


_T_QA, _T_KA, _T_VA, _T_QB, _T_KVB, _T_GATE = 0, 3, 6, 9, 11, 12
_R1Q, _R1K, _R2Q, _R2K = 0, 2, 4, 6


def _cast_weights_once(w_ref, wbf_ref):
    @pl.when(pl.program_id(1) == 0)
    def _cast():
        wbf_ref[...] = w_ref[...].astype(BF16)


def _for_sub_tiles(x_ref, wbf_ref, emit):
    for rt in range(x_ref.shape[0] // SUB_ROWS):
        rows = slice(rt * SUB_ROWS, (rt + 1) * SUB_ROWS)
        emit(rt, rows, jnp.dot(x_ref[rows, :], wbf_ref[...], preferred_element_type=F32))


def _head(acc, h):
    return acc[:, h * HEAD_DIM:(h + 1) * HEAD_DIM]


def _rope(a, tab_ref, t, rows):
    return a * tab_ref[t, rows, :] + pltpu.roll(a, HEAD_DIM // 2, 1) * tab_ref[t + 1, rows, :]


def _rms(a, g_ref):
    ms = jnp.mean(a * a, axis=-1, keepdims=True)
    return a * lax.rsqrt(ms + RMS_EPS) * g_ref[...]


def _proj_a_kernel(x_ref, w_ref, tab_ref, out_ref, wbf_ref, *, dil):
    j = pl.program_id(0)
    _cast_weights_once(w_ref, wbf_ref)
    per_res = SUB_ROWS // dil

    def emit_with(fn):
        def emit(rt, rows, acc):
            if dil == 1:
                for h in range(A_HEADS):
                    out_ref[h, 0, rows, :] = fn(_head(acc, h), rows).astype(BF16)
                return
            for h in range(A_HEADS):
                by_res = fn(_head(acc, h), rows).reshape(per_res, dil, HEAD_DIM).swapaxes(0, 1)
                for r in range(dil):
                    out_ref[h, r, rt * per_res:(rt + 1) * per_res, :] = by_res[r].astype(BF16)
        _for_sub_tiles(x_ref, wbf_ref, emit)

    @pl.when(j == 0)
    def _q():
        emit_with(lambda a, rows: _rope(a, tab_ref, _R1Q, rows))

    @pl.when(j == 1)
    def _k():
        emit_with(lambda a, rows: _rope(a, tab_ref, _R1K, rows))

    @pl.when(j == 2)
    def _v():
        emit_with(lambda a, rows: a)


def _proj_b_kernel(x_ref, w_ref, tab_ref, gq_ref, gk_ref, out_ref, wbf_ref, *, q_tiles):
    j = pl.program_id(0)
    _cast_weights_once(w_ref, wbf_ref)
    heads = COL_TILE // HEAD_DIM

    @pl.when(j < q_tiles)
    def _qb():
        def emit(rt, rows, acc):
            for h in range(heads):
                out_ref[h, rows, :] = _rope(_rms(_head(acc, h), gq_ref), tab_ref, _R2Q, rows).astype(BF16)
        _for_sub_tiles(x_ref, wbf_ref, emit)

    @pl.when(j == q_tiles)
    def _kvb():
        def emit(rt, rows, acc):
            for h in range(heads):
                if h < B_KV_HEADS:
                    out_ref[h, rows, :] = _rope(_rms(_head(acc, h), gk_ref), tab_ref, _R2K, rows).astype(BF16)
                else:
                    out_ref[h, rows, :] = _head(acc, h).astype(BF16)
        _for_sub_tiles(x_ref, wbf_ref, emit)


def _proj_gate_kernel(x_ref, w_ref, out_ref, wbf_ref):
    _cast_weights_once(w_ref, wbf_ref)

    def emit(rt, rows, acc):
        out_ref[rows, :] = (1.0 / (1.0 + jnp.exp(-acc))).astype(BF16)
    _for_sub_tiles(x_ref, wbf_ref, emit)


def _proj_specs(d_model, seq, first_tile, tile_step):
    seq_blocks = seq // ROW_TILE
    return [
        pl.BlockSpec((ROW_TILE, d_model), lambda j, i: (i, 0)),
        pl.BlockSpec((d_model, COL_TILE), lambda j, i: (0, first_tile + tile_step * j)),
        pl.BlockSpec((8, ROW_TILE, HEAD_DIM), lambda j, i: (0, i % seq_blocks, 0)),
    ]


def _proj_a(xb, w_in, tabs, group, batch, seq):
    n_tok, d_model = xb.shape
    dil = DIL_PATTERNS[group][1]
    sub_len = seq // dil
    seq_blocks = seq // ROW_TILE
    return pl.pallas_call(
        functools.partial(_proj_a_kernel, dil=dil),
        out_shape=jax.ShapeDtypeStruct((3 * A_HEADS, batch, dil, sub_len, HEAD_DIM), BF16),
        grid=(3, n_tok // ROW_TILE),
        in_specs=_proj_specs(d_model, seq, group, N_DIL),
        out_specs=pl.BlockSpec((A_HEADS, None, dil, ROW_TILE // dil, HEAD_DIM),
                               lambda j, i: (j, i // seq_blocks, 0, i % seq_blocks, 0)),
        scratch_shapes=[pltpu.VMEM((d_model, COL_TILE), BF16)],
        compiler_params=_params(("arbitrary", "arbitrary")),
    )(xb, w_in, tabs)


def _proj_b(xb, w_in, tabs, gq, gk, seq):
    n_tok, d_model = xb.shape
    q_tiles = B_Q_HEADS * HEAD_DIM // COL_TILE
    heads = COL_TILE // HEAD_DIM
    fix = lambda j, i: (0, 0)
    return pl.pallas_call(
        functools.partial(_proj_b_kernel, q_tiles=q_tiles),
        out_shape=jax.ShapeDtypeStruct(((q_tiles + 1) * heads, n_tok, HEAD_DIM), BF16),
        grid=(q_tiles + 1, n_tok // ROW_TILE),
        in_specs=_proj_specs(d_model, seq, _T_QB, 1) + [pl.BlockSpec((1, HEAD_DIM), fix), pl.BlockSpec((1, HEAD_DIM), fix)],
        out_specs=pl.BlockSpec((heads, ROW_TILE, HEAD_DIM), lambda j, i: (j, i, 0)),
        scratch_shapes=[pltpu.VMEM((d_model, COL_TILE), BF16)],
        compiler_params=_params(("arbitrary", "arbitrary")),
    )(xb, w_in, tabs, gq, gk)


def _proj_gate(xb, w_in):
    n_tok, d_model = xb.shape
    n_ct = w_in.shape[1] // COL_TILE - _T_GATE
    return pl.pallas_call(
        _proj_gate_kernel,
        out_shape=jax.ShapeDtypeStruct((n_tok, n_ct * COL_TILE), BF16),
        grid=(n_ct, n_tok // ROW_TILE),
        in_specs=[pl.BlockSpec((ROW_TILE, d_model), lambda j, i: (i, 0)),
                  pl.BlockSpec((d_model, COL_TILE), lambda j, i: (0, _T_GATE + j))],
        out_specs=pl.BlockSpec((ROW_TILE, COL_TILE), lambda j, i: (i, j)),
        scratch_shapes=[pltpu.VMEM((d_model, COL_TILE), BF16)],
        compiler_params=_params(("arbitrary", "arbitrary")),
    )(xb, w_in)


_SUB = 128
_WIN = _SUB + 2 * BAND_HALF
_MERGE_ROWS = 256
_A_UNROLL = 8


def _attn_a_group(g, dil, seq, q_ref, k_ref, v_ref, og, lg, bias):
    sub_len = seq // dil
    per_seq = sub_len // _SUB
    shift = per_seq.bit_length() - 1
    ones = jnp.ones((_WIN, HEAD_DIM), BF16)

    def body(i, carry):
        r = lax.shift_right_logical(i, shift)
        p0 = pl.multiple_of((i & (per_seq - 1)) * _SUB, _SUB)
        start = pl.multiple_of(jnp.clip(p0 - BAND_HALF, 0, sub_len - _WIN), BAND_HALF)
        q = q_ref[r, pl.ds(p0, _SUB), :]
        k = k_ref[r, pl.ds(start, _WIN), :]
        v = v_ref[r, pl.ds(start, _WIN), :]
        s = lax.dot_general(q, k, (((1,), (1,)), ((), ())), preferred_element_type=F32)
        s = s + bias[lax.shift_right_logical(p0 - start, BAND_HALF.bit_length() - 1)]
        m = jnp.max(s, axis=1, keepdims=True)
        p = jnp.exp2(s - m).astype(BF16)
        both = jnp.dot(p, jnp.concatenate([v, ones], axis=1), preferred_element_type=F32)
        l = both[:, HEAD_DIM:]
        o = both[:, :HEAD_DIM] / l
        lse = m + jnp.log2(l)
        if dil == 1:
            rows = pl.ds(p0, _SUB)
        else:
            rows = pl.ds(p0 * dil + r, _SUB, stride=dil)
        og[g, rows, :] = o
        lg[g, rows, :] = lse
        return carry

    lax.fori_loop(0, seq // _SUB, body, 0, unroll=_A_UNROLL)


def _attn_a_kernel(*refs, seq):
    qkv, o_ref = refs[:3 * N_DIL], refs[3 * N_DIL]
    og, lg, bias = refs[3 * N_DIL + 1:]
    row = lax.broadcasted_iota(I32, (_SUB, _WIN), 0)
    col = lax.broadcasted_iota(I32, (_SUB, _WIN), 1)
    for case in range(3):
        bias[case] = jnp.where(jnp.abs(col - row - case * BAND_HALF) <= BAND_HALF, 0.0, NEG_BIG)
    for g, (_, dil) in enumerate(DIL_PATTERNS):
        _attn_a_group(g, dil, seq, qkv[3 * g], qkv[3 * g + 1], qkv[3 * g + 2], og, lg, bias)

    def merge(c, carry):
        rows = pl.ds(pl.multiple_of(c * _MERGE_ROWS, _MERGE_ROWS), _MERGE_ROWS)
        ls = [lg[g, rows, :] for g in range(N_DIL)]
        mx = functools.reduce(jnp.maximum, ls)
        ws = [jnp.exp2(l - mx) for l in ls]
        num = functools.reduce(lambda a, b: a + b, [w * og[g, rows, :] for g, w in enumerate(ws)])
        o_ref[rows, :] = (num / functools.reduce(lambda a, b: a + b, ws)).astype(o_ref.dtype)
        return carry

    lax.fori_loop(0, seq // _MERGE_ROWS, merge, 0)


def _attn_a(groups, batch, seq):
    in_specs, operands = [], []
    for g, (_, dil) in enumerate(DIL_PATTERNS):
        for kind in range(3):
            first = kind * A_HEADS
            in_specs.append(pl.BlockSpec((None, None, dil, seq // dil, HEAD_DIM),
                                         lambda b, h, first=first: (first + h, b, 0, 0, 0)))
            operands.append(groups[g])
    return pl.pallas_call(
        functools.partial(_attn_a_kernel, seq=seq),
        out_shape=jax.ShapeDtypeStruct((batch * seq, A_HEADS * HEAD_DIM), BF16),
        grid=(batch, A_HEADS),
        in_specs=in_specs,
        out_specs=pl.BlockSpec((seq, HEAD_DIM), lambda b, h: (b, h)),
        scratch_shapes=[pltpu.VMEM((N_DIL, seq, HEAD_DIM), F32), pltpu.VMEM((N_DIL, seq, HEAD_DIM), F32),
                        pltpu.VMEM((3, _SUB, _WIN), F32)],
        compiler_params=_params(("arbitrary", "arbitrary")),
    )(*operands)


_BQ = 512
_BK = 512
SCORE_BOUND = 64.0
BF16_SLACK = 1.02


def _attn_b_kernel(q_ref, k_ref, v_ref, o_ref, *, seq):
    rows = B_GROUP * _BQ
    q = q_ref[...].reshape(rows, HEAD_DIM)

    def body(c, carry):
        m, l, acc = carry
        c0 = pl.multiple_of(c * _BK, _BK)
        k = k_ref[pl.ds(c0, _BK), :]
        v = v_ref[pl.ds(c0, _BK), :]
        s = lax.dot_general(q, k, (((1,), (1,)), ((), ())), preferred_element_type=F32)
        m_new = jnp.maximum(m, jnp.max(s, axis=1, keepdims=True))
        alpha = jnp.exp2(m - m_new)
        p = jnp.exp2(s - m_new)
        l = alpha * l + jnp.sum(p, axis=1, keepdims=True)
        acc = alpha * acc + jnp.dot(p.astype(BF16), v, preferred_element_type=F32)
        return m_new, l, acc

    init = (jnp.full((rows, 1), NEG_BIG, F32), jnp.zeros((rows, 1), F32), jnp.zeros((rows, HEAD_DIM), F32))
    _, l, acc = lax.fori_loop(0, seq // _BK, body, init)
    o = acc / l
    for g in range(B_GROUP):
        o_ref[:, g * HEAD_DIM:(g + 1) * HEAD_DIM] = o[g * _BQ:(g + 1) * _BQ].astype(o_ref.dtype)


def _attn_b_bounded_kernel(q_ref, k_ref, v_ref, o_ref, *, seq):
    rows = B_GROUP * _BQ
    q = q_ref[...].reshape(rows, HEAD_DIM)
    ones = jnp.ones((_BK, HEAD_DIM), BF16)
    acc = jnp.zeros((rows, 2 * HEAD_DIM), F32)
    for c in range(seq // _BK):
        k = k_ref[c * _BK:(c + 1) * _BK, :]
        v = v_ref[c * _BK:(c + 1) * _BK, :]
        s = lax.dot_general(q, k, (((1,), (1,)), ((), ())), preferred_element_type=F32)
        p = jnp.exp2(s).astype(BF16)
        acc = acc + jnp.dot(p, jnp.concatenate([v, ones], axis=1), preferred_element_type=F32)
    o = acc[:, :HEAD_DIM] / acc[:, HEAD_DIM:HEAD_DIM + 1]
    for g in range(B_GROUP):
        o_ref[:, g * HEAD_DIM:(g + 1) * HEAD_DIM] = o[g * _BQ:(g + 1) * _BQ].astype(o_ref.dtype)


def _attn_b(slots, batch, seq, bounded):
    n_tok = batch * seq
    k0 = B_Q_HEADS
    v0 = k0 + B_KV_HEADS
    qblocks = seq // _BQ
    return pl.pallas_call(
        functools.partial(_attn_b_bounded_kernel if bounded else _attn_b_kernel, seq=seq),
        out_shape=jax.ShapeDtypeStruct((n_tok, B_Q_HEADS * HEAD_DIM), BF16),
        grid=(batch, B_KV_HEADS, qblocks),
        in_specs=[
            pl.BlockSpec((B_GROUP, _BQ, HEAD_DIM), lambda b, kv, qi: (kv, b * qblocks + qi, 0)),
            pl.BlockSpec((None, seq, HEAD_DIM), lambda b, kv, qi: (k0 + kv, b, 0)),
            pl.BlockSpec((None, seq, HEAD_DIM), lambda b, kv, qi: (v0 + kv, b, 0)),
        ],
        out_specs=pl.BlockSpec((_BQ, B_GROUP * HEAD_DIM), lambda b, kv, qi: (b * qblocks + qi, kv)),
        compiler_params=_params(("arbitrary", "arbitrary", "arbitrary")),
    )(slots, slots, slots)


_MIX_ROWS = 512


def _branch_kernel(oa_ref, ob_ref, ga_ref, gb_ref, wa_ref, wb_ref, out_ref):
    for rt in range(out_ref.shape[0] // SUB_ROWS):
        rows = slice(rt * SUB_ROWS, (rt + 1) * SUB_ROWS)
        y_a = jnp.dot(oa_ref[rows, :], wa_ref[...], preferred_element_type=F32)
        y_b = jnp.dot(ob_ref[rows, :], wb_ref[...], preferred_element_type=F32)
        out_ref[rows, :] = (ga_ref[rows, :].astype(F32) * y_a + gb_ref[rows, :].astype(F32) * y_b).astype(BF16)


def _branch_mix(o_a, o_b, gates, wa, wb):
    n_tok = o_b.shape[0]
    d_model = wa.shape[1]
    tm = _MIX_ROWS
    row = lambda i: (i, 0)
    return pl.pallas_call(
        _branch_kernel,
        out_shape=jax.ShapeDtypeStruct((n_tok, d_model), BF16),
        grid=(n_tok // tm,),
        in_specs=[pl.BlockSpec((tm, o_a.shape[1]), row),
                  pl.BlockSpec((tm, o_b.shape[1]), row),
                  pl.BlockSpec((tm, d_model), lambda i: (i, 0)),
                  pl.BlockSpec((tm, d_model), lambda i: (i, 1)),
                  pl.BlockSpec(wa.shape, lambda i: (0, 0)),
                  pl.BlockSpec(wb.shape, lambda i: (0, 0))],
        out_specs=pl.BlockSpec((tm, d_model), row),
        compiler_params=_params(("arbitrary",)),
    )(o_a, o_b, gates, gates, wa, wb)


def _layer_norm(z, g, b):
    mu = jnp.mean(z, axis=-1, keepdims=True)
    zc = z - mu
    var = jnp.mean(zc * zc, axis=-1, keepdims=True)
    return zc * lax.rsqrt(var + LN_EPS) * g + b


def _to_slabs(slab_ref, value, row0):
    rows, d = value.shape
    chunks = d // LANES
    for c in range(chunks):
        slab_ref[pl.ds(row0 * chunks + c, rows, stride=chunks), :] = value[:, c * LANES:(c + 1) * LANES]


def _from_slabs(slab_ref, rows, chunks, c):
    return slab_ref[pl.ds(c, rows, stride=chunks), :]


def _out_proj_kernel(m_ref, w_ref, x_ref, g_ref, b_ref, rcat_ref, rb_ref, hs_ref, lg_ref, *, alpha):
    for rt in range(m_ref.shape[0] // SUB_ROWS):
        rows = slice(rt * SUB_ROWS, (rt + 1) * SUB_ROWS)
        mix = jnp.dot(m_ref[rows, :], w_ref[...], preferred_element_type=F32)
        h = _layer_norm(alpha * x_ref[rows, :] + mix, g_ref[...], b_ref[...])
        _to_slabs(hs_ref, h, rt * SUB_ROWS)
        hi = h.astype(BF16)
        lo = (h - hi.astype(F32)).astype(BF16)
        both = jnp.dot(hi, rcat_ref[...], preferred_element_type=F32)
        lg = both[:, :LANES] + both[:, LANES:] + jnp.dot(lo, rcat_ref[:, :LANES], preferred_element_type=F32)
        lg_ref[rows, :] = lg + rb_ref[...]


def _out_proj(merged, w_out, x2, g, b, r_cat, r_b, alpha):
    n_tok, d_model = x2.shape
    tm = _MIX_ROWS
    chunks = d_model // LANES
    row = lambda i: (i, 0)
    fix = lambda i: (0, 0)
    return pl.pallas_call(
        functools.partial(_out_proj_kernel, alpha=alpha),
        out_shape=(jax.ShapeDtypeStruct((n_tok * chunks, LANES), F32),
                   jax.ShapeDtypeStruct((n_tok, LANES), F32)),
        grid=(n_tok // tm,),
        in_specs=[pl.BlockSpec((tm, d_model), row),
                  pl.BlockSpec(w_out.shape, fix),
                  pl.BlockSpec((tm, d_model), row),
                  pl.BlockSpec((1, d_model), fix),
                  pl.BlockSpec((1, d_model), fix),
                  pl.BlockSpec(r_cat.shape, fix),
                  pl.BlockSpec((1, LANES), fix)],
        out_specs=(pl.BlockSpec((tm * chunks, LANES), row), pl.BlockSpec((tm, LANES), row)),
        compiler_params=_params(("arbitrary",)),
    )(merged, w_out, x2, g, b, r_cat, r_b)


_BIG_CHUNK = 512


def _moe_kernel(item_expert, item_row0, item_blocks, item_real, src_slab, dst_slab,
                h_hbm, wg_ref, wu_ref, wd_ref, y_hbm,
                stage, xbf, acc, obuf, gsem, ssem):
    it = pl.program_id(0)
    f = pl.program_id(1)
    n_items = pl.num_programs(0)
    n_ff = pl.num_programs(1)
    n_blocks = item_blocks[it]
    d_model = xbf.shape[1]
    chunks = d_model // LANES

    def slab(ref, first):
        return ref.at[pl.ds(pl.multiple_of(first, chunks), chunks)]

    def gather_copy(base, r):
        return pltpu.make_async_copy(slab(h_hbm, src_slab[base + r]), slab(stage, r * chunks), gsem.at[0])

    def scatter_copy(base, r):
        return pltpu.make_async_copy(slab(obuf, r * chunks), slab(y_hbm, dst_slab[base + r]), ssem.at[0])

    def for_rows(n_rows, fn):
        def group(t, c):
            for u in range(DMA_UNROLL):
                fn(t * DMA_UNROLL + u)
            return c
        n_groups = n_rows // DMA_UNROLL
        lax.fori_loop(0, n_groups, group, 0)

        def single(r, c):
            fn(r)
            return c
        lax.fori_loop(n_groups * DMA_UNROLL, n_rows, single, 0)

    def gather(item, op):
        base = item_row0[item]
        for_rows(item_blocks[item] * MOE_BLOCK, lambda r: op(gather_copy(base, r)))

    def scatter(item, op):
        base = item_row0[item]
        for_rows(item_real[item], lambda r: op(scatter_copy(base, r)))

    start = lambda cp: cp.start()
    wait = lambda cp: cp.wait()

    @pl.when((it == 0) & (f == 0))
    def _first_gather():
        gather(it, start)

    @pl.when(f == 0)
    def _gathered():
        gather(it, wait)

        def cast_rows(b, carry):
            r0 = pl.multiple_of(b * MOE_BLOCK, MOE_BLOCK)
            for c in range(chunks):
                piece = stage[pl.ds(r0 * chunks + c, MOE_BLOCK, stride=chunks), :]
                xbf[pl.ds(r0, MOE_BLOCK), c * LANES:(c + 1) * LANES] = piece.astype(BF16)
            return carry
        lax.fori_loop(0, n_blocks, cast_rows, 0)

        @pl.when(it + 1 < n_items)
        def _prefetch_next():
            gather(it + 1, start)

    def chunk(r0, rows, mode):
        xb = xbf[pl.ds(r0, rows), :]
        a = jnp.dot(xb, wg_ref[...].astype(BF16), preferred_element_type=F32)
        u = jnp.dot(xb, wu_ref[...].astype(BF16), preferred_element_type=F32)
        hid = (a / (1.0 + jnp.exp(-a)) * u).astype(BF16)
        y = jnp.dot(hid, wd_ref[...].astype(BF16), preferred_element_type=F32)
        if mode == "first":
            acc[pl.ds(r0, rows), :] = y
        elif mode == "middle":
            acc[pl.ds(r0, rows), :] += y
        else:
            if mode == "last":
                y = acc[pl.ds(r0, rows), :] + y
            for c in range(chunks):
                obuf[pl.ds(r0 * chunks + c, rows, stride=chunks), :] = y[:, c * LANES:(c + 1) * LANES]

    def run_item(mode):
        per_big = _BIG_CHUNK // MOE_BLOCK

        def big_chunk(t, c):
            chunk(pl.multiple_of(t * _BIG_CHUNK, _BIG_CHUNK), _BIG_CHUNK, mode)
            return c
        n_big = n_blocks // per_big
        lax.fori_loop(0, n_big, big_chunk, 0)
        rest = n_blocks - n_big * per_big
        for blocks in range(1, per_big):
            @pl.when(rest == blocks)
            def _rest(blocks=blocks):
                chunk(pl.multiple_of(n_big * _BIG_CHUNK, _BIG_CHUNK), blocks * MOE_BLOCK, mode)

    last = n_ff - 1

    @pl.when((n_blocks > 0) & (f == last))
    def _final():
        @pl.when(it > 0)
        def _drain_prev():
            scatter(it - 1, wait)
        run_item("only" if n_ff == 1 else "last")
        scatter(it, start)

    if n_ff > 1:
        @pl.when((n_blocks > 0) & (f == 0))
        def _first():
            run_item("first")

        @pl.when((n_blocks > 0) & (f > 0) & (f < last))
        def _middle():
            run_item("middle")

    @pl.when((f == last) & (it == n_items - 1))
    def _drain_last():
        scatter(jnp.where(n_blocks > 0, it, _last_live(item_blocks, n_items)), wait)


def _last_live(item_blocks, n_items):
    def body(i, best):
        return jnp.where(item_blocks[i] > 0, i, best)
    return lax.fori_loop(0, n_items, body, 0)


def _moe(h_slabs, w_gate, w_up, w_down, item_expert, item_row0, item_blocks, item_real, src_slab, dst_slab):
    d_model = w_gate.shape[1]
    chunks = d_model // LANES
    n_tok = h_slabs.shape[0] // chunks
    ff = w_gate.shape[2]
    n_ff = ff // FF_TILE
    n_items = item_expert.shape[0]

    def ff_idx(it, f, blocks):
        return jnp.where(blocks[it] > 0, f, n_ff - 1)

    grid_spec = pltpu.PrefetchScalarGridSpec(
        num_scalar_prefetch=6,
        grid=(n_items, n_ff),
        in_specs=[
            pl.BlockSpec(memory_space=pl.ANY),
            pl.BlockSpec((None, d_model, FF_TILE), lambda it, f, ie, ir, ib, nr, st, dr: (ie[it], 0, ff_idx(it, f, ib))),
            pl.BlockSpec((None, d_model, FF_TILE), lambda it, f, ie, ir, ib, nr, st, dr: (ie[it], 0, ff_idx(it, f, ib))),
            pl.BlockSpec((None, FF_TILE, d_model), lambda it, f, ie, ir, ib, nr, st, dr: (ie[it], ff_idx(it, f, ib), 0)),
        ],
        out_specs=pl.BlockSpec(memory_space=pl.ANY),
        scratch_shapes=[
            pltpu.VMEM((ITEM_ROWS * chunks, LANES), F32),
            pltpu.VMEM((ITEM_ROWS, d_model), BF16),
            pltpu.VMEM((ITEM_ROWS, d_model), F32),
            pltpu.VMEM((ITEM_ROWS * chunks, LANES), F32),
            pltpu.SemaphoreType.DMA((1,)),
            pltpu.SemaphoreType.DMA((1,)),
        ],
    )
    return pl.pallas_call(
        _moe_kernel,
        out_shape=jax.ShapeDtypeStruct((n_tok * TOP_K * chunks, LANES), F32),
        grid_spec=grid_spec,
        compiler_params=_params(("arbitrary", "arbitrary")),
    )(item_expert, item_row0, item_blocks, item_real, src_slab, dst_slab, h_slabs, w_gate, w_up, w_down)


_COMBINE_ROWS = 512


def _combine_kernel(y0_ref, y1_ref, wt_ref, hs_ref, g_ref, b_ref, out_ref, *, alpha):
    rows, d_model = out_ref.shape
    chunks = d_model // LANES
    wt = wt_ref[...]
    w0, w1 = wt[:, 0:1], wt[:, 1:2]
    for c in range(chunks):
        cols = slice(c * LANES, (c + 1) * LANES)
        ffn = w0 * _from_slabs(y0_ref, rows, chunks, c) + w1 * _from_slabs(y1_ref, rows, chunks, c)
        out_ref[:, cols] = alpha * _from_slabs(hs_ref, rows, chunks, c) + ffn
    out_ref[...] = _layer_norm(out_ref[...], g_ref[...], b_ref[...])


def _combine(y_slabs, weights, h_slabs, g, b, alpha):
    d_model = g.shape[1]
    n_tok = h_slabs.shape[0] * LANES // d_model
    tm = _COMBINE_ROWS
    chunks = d_model // LANES
    second = n_tok // tm
    row = lambda i: (i, 0)
    fix = lambda i: (0, 0)
    return pl.pallas_call(
        functools.partial(_combine_kernel, alpha=alpha),
        out_shape=jax.ShapeDtypeStruct((n_tok, d_model), F32),
        grid=(n_tok // tm,),
        in_specs=[pl.BlockSpec((tm * chunks, LANES), row),
                  pl.BlockSpec((tm * chunks, LANES), lambda i: (second + i, 0)),
                  pl.BlockSpec((tm, TOP_K), row),
                  pl.BlockSpec((tm * chunks, LANES), row),
                  pl.BlockSpec((1, d_model), fix),
                  pl.BlockSpec((1, d_model), fix)],
        out_specs=pl.BlockSpec((tm, d_model), row),
        compiler_params=_params(("arbitrary",)),
    )(y_slabs, y_slabs, weights, h_slabs, g, b)


def _rope_tables(seq):
    f32 = np.float32
    half = HEAD_DIM // 2
    inv1 = np.power(f32(ROPE_THETA), -(np.arange(half, dtype=f32) / f32(half)))
    ang1 = np.arange(seq, dtype=f32)[:, None] * inv1[None, :]
    t = np.arange(seq)
    n_axis = HEAD_DIM // 4
    inv2 = np.power(f32(ROPE_THETA), -(np.arange(n_axis, dtype=f32) / f32(n_axis)))
    ang2 = np.concatenate([(t // GRID_W).astype(f32)[:, None] * inv2[None, :],
                           (t % GRID_W).astype(f32)[:, None] * inv2[None, :]], axis=-1)
    q_scale = f32(HEAD_DIM ** -0.5 * LOG2E)
    out = []
    for ang in (ang1, ang2):
        cos = np.concatenate([np.cos(ang), np.cos(ang)], axis=-1)
        sin = np.concatenate([-np.sin(ang), np.sin(ang)], axis=-1)
        out += [cos * q_scale, sin * q_scale, cos, sin]
    return jnp.asarray(np.stack(out, axis=0).astype(f32))


def _route(logits, n_tok, chunks):
    g_logits = logits[:, :N_GROUPS]
    g_prob = jax.nn.softmax(g_logits, axis=-1)
    g_idx = jnp.argmax(g_logits, axis=-1).astype(I32)
    g_gate = jnp.max(g_prob, axis=-1)
    e_logits = logits[:, N_GROUPS:N_GROUPS + EXPERTS_PER_GROUP]
    for g in range(1, N_GROUPS):
        lo = N_GROUPS + g * EXPERTS_PER_GROUP
        e_logits = jnp.where(g_idx[:, None] == g, logits[:, lo:lo + EXPERTS_PER_GROUP], e_logits)
    e_prob = jax.nn.softmax(e_logits, axis=-1)
    i1 = jnp.argmax(e_prob, axis=-1).astype(I32)
    p1 = jnp.max(e_prob, axis=-1)
    rest = jnp.where(jnp.arange(EXPERTS_PER_GROUP, dtype=I32)[None, :] == i1[:, None], -1.0, e_prob)
    i2 = jnp.argmax(rest, axis=-1).astype(I32)
    p2 = jnp.max(rest, axis=-1)
    top_p = jnp.stack([p1, p2], axis=-1)
    top_p = top_p / jnp.sum(top_p, axis=-1, keepdims=True)
    weights = g_gate[:, None] * top_p
    expert = g_idx[:, None] * EXPERTS_PER_GROUP + jnp.stack([i1, i2], axis=-1)

    n_slot = n_tok * TOP_K
    e_flat = expert.reshape(n_slot).astype(I32)
    order = jnp.argsort(e_flat).astype(I32)
    experts = jnp.arange(N_EXPERTS, dtype=I32)
    counts = jnp.sum(e_flat[:, None] == experts[None, :], axis=0, dtype=I32)
    padded = ((counts + MOE_BLOCK - 1) // MOE_BLOCK) * MOE_BLOCK
    p_ends = jnp.cumsum(padded)
    p_starts = p_ends - padded
    buf_len = n_slot + N_EXPERTS * MOE_BLOCK

    def pick(one_hot, table):
        return jnp.sum(jnp.where(one_hot, table[None, :], 0), axis=1, dtype=I32)

    pos = jnp.arange(buf_len, dtype=I32)[:, None]
    ended = p_ends[None, :] <= pos
    inside = jnp.logical_and(p_starts[None, :] <= pos, pos < p_ends[None, :])
    idx = pos[:, 0] - jnp.max(jnp.where(ended, p_ends[None, :], 0), axis=1)
    sorted_at = jnp.clip(pick(ended, counts) + idx, 0, n_slot - 1)
    slot_at = jnp.where(idx < pick(inside, counts), order[sorted_at], 0)
    tok_at, k_at = slot_at // TOP_K, slot_at % TOP_K
    src_slab = tok_at * chunks
    dst_slab = (k_at * n_tok + tok_at) * chunks

    n_items = N_EXPERTS + n_slot // ITEM_ROWS
    per_expert = (padded + ITEM_ROWS - 1) // ITEM_ROWS
    item_ends = jnp.cumsum(per_expert)
    total = item_ends[-1]
    ids = jnp.arange(n_items, dtype=I32)
    live = ids < total
    item = jnp.minimum(ids, jnp.maximum(total - 1, 0))[:, None]
    owner = jnp.logical_and((item_ends - per_expert)[None, :] <= item, item < item_ends[None, :])
    e_of = pick(owner, experts)
    chunk = item[:, 0] - pick(owner, item_ends - per_expert)
    item_row0 = jnp.where(live, pick(owner, p_starts) + chunk * ITEM_ROWS, 0).astype(I32)
    item_blocks = jnp.where(live, jnp.clip(pick(owner, padded) // MOE_BLOCK - chunk * ITEM_BLOCKS, 0, ITEM_BLOCKS), 0)
    item_real = jnp.where(live, jnp.clip(pick(owner, counts) - chunk * ITEM_ROWS, 0, ITEM_ROWS), 0)
    return weights, e_of, item_row0, item_blocks.astype(I32), item_real.astype(I32), src_slab, dst_slab


def kernel(x, w_in, q_norm_g, k_norm_g, w_branch_a, w_branch_b, w_out, ln1_g, ln1_b, w_group, b_group, w_router,
           b_router, w_gate, w_up, w_down, ln2_g, ln2_b):
    batch, seq, d_model = x.shape
    depth = w_in.shape[0]
    n_tok = batch * seq
    dn_alpha = (2 * depth) ** 0.25
    tabs = _rope_tables(seq)
    h = x.reshape(n_tok, d_model)
    for layer in range(depth):
        xb = h.astype(BF16)
        groups = [_proj_a(xb, w_in[layer], tabs, g, batch, seq) for g in range(N_DIL)]
        slots_b = _proj_b(xb, w_in[layer], tabs, q_norm_g[layer][None, :], k_norm_g[layer][None, :], seq)
        gates = _proj_gate(xb, w_in[layer])
        o_a = _attn_a(groups, batch, seq)
        score_bound = (jnp.max(jnp.abs(q_norm_g[layer])) * jnp.max(jnp.abs(k_norm_g[layer]))
                       * (HEAD_DIM ** 0.5 * LOG2E * BF16_SLACK))
        o_b = lax.cond(score_bound <= SCORE_BOUND,
                       lambda s: _attn_b(s, batch, seq, True), lambda s: _attn_b(s, batch, seq, False), slots_b)
        merged = _branch_mix(o_a, o_b, gates, w_branch_a[layer].astype(BF16), w_branch_b[layer].astype(BF16))
        w_r = jnp.concatenate([w_group[layer], w_router[layer]], axis=1)
        w_r = jnp.pad(w_r, ((0, 0), (0, LANES - w_r.shape[1])))
        r_hi = w_r.astype(BF16)
        r_lo = (w_r - r_hi.astype(F32)).astype(BF16)
        r_b = jnp.pad(jnp.concatenate([b_group[layer], b_router[layer]]), (0, LANES - N_GROUPS - N_EXPERTS))[None, :]
        h1_slabs, logits = _out_proj(merged, w_out[layer].astype(BF16), h, ln1_g[layer][None, :],
                                         ln1_b[layer][None, :], jnp.concatenate([r_hi, r_lo], axis=1), r_b, dn_alpha)
        weights, item_expert, item_row0, item_blocks, item_real, src_slab, dst_slab = _route(
            logits, n_tok, d_model // LANES)
        y_slabs = _moe(h1_slabs, w_gate[layer], w_up[layer], w_down[layer], item_expert, item_row0, item_blocks,
                       item_real, src_slab, dst_slab)
        h = _combine(y_slabs, weights, h1_slabs, ln2_g[layer][None, :], ln2_b[layer][None, :], dn_alpha)
    return h.reshape(batch, seq, d_model)
```

```python
import functools
import math

import jax
import jax.numpy as jnp
import numpy as np
from jax import lax
from jax.experimental import pallas as pl
from jax.experimental.pallas import tpu as pltpu

F32 = jnp.float32
BF16 = jnp.bfloat16
I32 = jnp.int32

HEAD_DIM = 128
ROPE_THETA = 10000.0
GRID_W = 64
DIL_PATTERNS = ((128, 1), (512, 4), (2048, 16))
N_DIL = len(DIL_PATTERNS)
A_HEADS = 8
BAND_HALF = 64
B_Q_HEADS = 16
B_KV_HEADS = 4
B_GROUP = B_Q_HEADS // B_KV_HEADS
N_GROUPS = 4
EXPERTS_PER_GROUP = 8
N_EXPERTS = N_GROUPS * EXPERTS_PER_GROUP
TOP_K = 2
MOE_BLOCK = 128
LN_EPS = 1e-5
RMS_EPS = 1e-6
NEG_BIG = -1e30
LOG2E = math.log2(math.e)

LANES = 128
COL_TILE = 8 * HEAD_DIM
ROW_TILE = 1024
SUB_ROWS = 256
ITEM_ROWS = 1024
ITEM_BLOCKS = ITEM_ROWS // MOE_BLOCK
FF_TILE = 256
DMA_UNROLL = 8
VMEM_LIMIT = 56 * 1024 * 1024


def _params(sem, vmem=VMEM_LIMIT):
    return pltpu.CompilerParams(dimension_semantics=sem, vmem_limit_bytes=vmem)


_T_QA, _T_KA, _T_VA, _T_QB, _T_KVB, _T_GATE = 0, 3, 6, 9, 11, 12
_TAB_1D, _TAB_AXIAL = 0, 1
_RQ, _RK = 0, 2


def _cast_weights_once(w_ref, wbf_ref):
    @pl.when(pl.program_id(1) == 0)
    def _cast():
        wbf_ref[...] = w_ref[...].astype(BF16)


def _for_sub_tiles(x_ref, wbf_ref, emit):
    for rt in range(x_ref.shape[0] // SUB_ROWS):
        rows = slice(rt * SUB_ROWS, (rt + 1) * SUB_ROWS)
        emit(rt, rows, jnp.dot(x_ref[rows, :], wbf_ref[...], preferred_element_type=F32))


def _head(acc, h):
    return acc[:, h * HEAD_DIM:(h + 1) * HEAD_DIM]


def _rope(a, tab_ref, t, rows):
    return a * tab_ref[t, rows, :] + pltpu.roll(a, HEAD_DIM // 2, 1) * tab_ref[t + 1, rows, :]


def _rms(a, g_ref):
    ms = jnp.mean(a * a, axis=-1, keepdims=True)
    return a * lax.rsqrt(ms + RMS_EPS) * g_ref[...]


def _proj_a_kernel(x_ref, w_ref, tab_ref, out_ref, wbf_ref, *, dil):
    j = pl.program_id(0)
    _cast_weights_once(w_ref, wbf_ref)
    per_res = SUB_ROWS // dil

    def emit_with(fn):
        def emit(rt, rows, acc):
            if dil == 1:
                for h in range(A_HEADS):
                    out_ref[h, 0, rows, :] = fn(_head(acc, h), rows).astype(BF16)
                return
            for h in range(A_HEADS):
                by_res = fn(_head(acc, h), rows).reshape(per_res, dil, HEAD_DIM).swapaxes(0, 1)
                for r in range(dil):
                    out_ref[h, r, rt * per_res:(rt + 1) * per_res, :] = by_res[r].astype(BF16)
        _for_sub_tiles(x_ref, wbf_ref, emit)

    @pl.when(j == 0)
    def _q():
        emit_with(lambda a, rows: _rope(a, tab_ref, _RQ, rows))

    @pl.when(j == 1)
    def _k():
        emit_with(lambda a, rows: _rope(a, tab_ref, _RK, rows))

    @pl.when(j == 2)
    def _v():
        emit_with(lambda a, rows: a)


def _proj_b_kernel(x_ref, w_ref, tab_ref, gq_ref, gk_ref, out_ref, wbf_ref, *, q_tiles):
    j = pl.program_id(0)
    _cast_weights_once(w_ref, wbf_ref)
    heads = COL_TILE // HEAD_DIM

    @pl.when(j < q_tiles)
    def _qb():
        def emit(rt, rows, acc):
            for h in range(heads):
                out_ref[h, rows, :] = _rope(_rms(_head(acc, h), gq_ref), tab_ref, _RQ, rows).astype(BF16)
        _for_sub_tiles(x_ref, wbf_ref, emit)

    @pl.when(j == q_tiles)
    def _kvb():
        def emit(rt, rows, acc):
            for h in range(heads):
                if h < B_KV_HEADS:
                    out_ref[h, rows, :] = _rope(_rms(_head(acc, h), gk_ref), tab_ref, _RK, rows).astype(BF16)
                else:
                    out_ref[h, rows, :] = _head(acc, h).astype(BF16)
        _for_sub_tiles(x_ref, wbf_ref, emit)


def _proj_gate_kernel(x_ref, w_ref, out_ref, wbf_ref):
    _cast_weights_once(w_ref, wbf_ref)

    def emit(rt, rows, acc):
        out_ref[rows, :] = (1.0 / (1.0 + jnp.exp(-acc))).astype(BF16)
    _for_sub_tiles(x_ref, wbf_ref, emit)


def _proj_specs(d_model, seq, first_tile, tile_step, table_half):
    seq_blocks = seq // ROW_TILE
    return [
        pl.BlockSpec((ROW_TILE, d_model), lambda j, i: (i, 0)),
        pl.BlockSpec((d_model, COL_TILE), lambda j, i: (0, first_tile + tile_step * j)),
        pl.BlockSpec((4, ROW_TILE, HEAD_DIM), lambda j, i: (table_half, i % seq_blocks, 0)),
    ]


def _proj_a(xb, w_in, tabs, group, batch, seq):
    n_tok, d_model = xb.shape
    dil = DIL_PATTERNS[group][1]
    sub_len = seq // dil
    seq_blocks = seq // ROW_TILE
    return pl.pallas_call(
        functools.partial(_proj_a_kernel, dil=dil),
        out_shape=jax.ShapeDtypeStruct((3 * A_HEADS, batch, dil, sub_len, HEAD_DIM), BF16),
        grid=(3, n_tok // ROW_TILE),
        in_specs=_proj_specs(d_model, seq, group, N_DIL, _TAB_1D),
        out_specs=pl.BlockSpec((A_HEADS, None, dil, ROW_TILE // dil, HEAD_DIM),
                               lambda j, i: (j, i // seq_blocks, 0, i % seq_blocks, 0)),
        scratch_shapes=[pltpu.VMEM((d_model, COL_TILE), BF16)],
        compiler_params=_params(("arbitrary", "arbitrary")),
    )(xb, w_in, tabs)


def _proj_b(xb, w_in, tabs, gq, gk, seq):
    n_tok, d_model = xb.shape
    q_tiles = B_Q_HEADS * HEAD_DIM // COL_TILE
    heads = COL_TILE // HEAD_DIM
    fix = lambda j, i: (0, 0)
    return pl.pallas_call(
        functools.partial(_proj_b_kernel, q_tiles=q_tiles),
        out_shape=jax.ShapeDtypeStruct(((q_tiles + 1) * heads, n_tok, HEAD_DIM), BF16),
        grid=(q_tiles + 1, n_tok // ROW_TILE),
        in_specs=_proj_specs(d_model, seq, _T_QB, 1, _TAB_AXIAL) + [pl.BlockSpec((1, HEAD_DIM), fix), pl.BlockSpec((1, HEAD_DIM), fix)],
        out_specs=pl.BlockSpec((heads, ROW_TILE, HEAD_DIM), lambda j, i: (j, i, 0)),
        scratch_shapes=[pltpu.VMEM((d_model, COL_TILE), BF16)],
        compiler_params=_params(("arbitrary", "arbitrary")),
    )(xb, w_in, tabs, gq, gk)


def _proj_gate(xb, w_in):
    n_tok, d_model = xb.shape
    n_ct = w_in.shape[1] // COL_TILE - _T_GATE
    return pl.pallas_call(
        _proj_gate_kernel,
        out_shape=jax.ShapeDtypeStruct((n_tok, n_ct * COL_TILE), BF16),
        grid=(n_ct, n_tok // ROW_TILE),
        in_specs=[pl.BlockSpec((ROW_TILE, d_model), lambda j, i: (i, 0)),
                  pl.BlockSpec((d_model, COL_TILE), lambda j, i: (0, _T_GATE + j))],
        out_specs=pl.BlockSpec((ROW_TILE, COL_TILE), lambda j, i: (i, j)),
        scratch_shapes=[pltpu.VMEM((d_model, COL_TILE), BF16)],
        compiler_params=_params(("arbitrary", "arbitrary")),
    )(xb, w_in)


_SUB = 128
_WIN = _SUB + 2 * BAND_HALF
_MERGE_ROWS = 256
_A_UNROLL = 16


def _attn_a_group(g, dil, seq, q_ref, k_ref, v_ref, og, lg, bias):
    sub_len = seq // dil
    per_seq = sub_len // _SUB
    shift = per_seq.bit_length() - 1
    ones = jnp.ones((_WIN, HEAD_DIM), BF16)

    def body(i, carry):
        r = lax.shift_right_logical(i, shift)
        p0 = pl.multiple_of((i & (per_seq - 1)) * _SUB, _SUB)
        start = pl.multiple_of(jnp.clip(p0 - BAND_HALF, 0, sub_len - _WIN), BAND_HALF)
        q = q_ref[r, pl.ds(p0, _SUB), :]
        k = k_ref[r, pl.ds(start, _WIN), :]
        v = v_ref[r, pl.ds(start, _WIN), :]
        s = lax.dot_general(q, k, (((1,), (1,)), ((), ())), preferred_element_type=F32)
        s = s + bias[lax.shift_right_logical(p0 - start, BAND_HALF.bit_length() - 1)]
        m = jnp.max(s, axis=1, keepdims=True)
        p = jnp.exp2(s - m).astype(BF16)
        both = jnp.dot(p, jnp.concatenate([v, ones], axis=1), preferred_element_type=F32)
        l = both[:, HEAD_DIM:]
        o = both[:, :HEAD_DIM] / l
        lse = m + jnp.log2(l)
        if dil == 1:
            rows = pl.ds(p0, _SUB)
        else:
            rows = pl.ds(p0 * dil + r, _SUB, stride=dil)
        og[g, rows, :] = o
        lg[g, rows, :] = lse
        return carry

    lax.fori_loop(0, seq // _SUB, body, 0, unroll=_A_UNROLL)


def _attn_a_kernel(*refs, seq):
    qkv, o_ref = refs[:3 * N_DIL], refs[3 * N_DIL]
    og, lg, bias = refs[3 * N_DIL + 1:]
    row = lax.broadcasted_iota(I32, (_SUB, _WIN), 0)
    col = lax.broadcasted_iota(I32, (_SUB, _WIN), 1)
    for case in range(3):
        bias[case] = jnp.where(jnp.abs(col - row - case * BAND_HALF) <= BAND_HALF, 0.0, NEG_BIG)
    for g, (_, dil) in enumerate(DIL_PATTERNS):
        _attn_a_group(g, dil, seq, qkv[3 * g], qkv[3 * g + 1], qkv[3 * g + 2], og, lg, bias)

    def merge(c, carry):
        rows = pl.ds(pl.multiple_of(c * _MERGE_ROWS, _MERGE_ROWS), _MERGE_ROWS)
        ls = [lg[g, rows, :] for g in range(N_DIL)]
        mx = functools.reduce(jnp.maximum, ls)
        ws = [jnp.exp2(l - mx) for l in ls]
        num = functools.reduce(lambda a, b: a + b, [w * og[g, rows, :] for g, w in enumerate(ws)])
        o_ref[rows, :] = (num / functools.reduce(lambda a, b: a + b, ws)).astype(o_ref.dtype)
        return carry

    lax.fori_loop(0, seq // _MERGE_ROWS, merge, 0)


def _attn_a(groups, batch, seq):
    in_specs, operands = [], []
    for g, (_, dil) in enumerate(DIL_PATTERNS):
        for kind in range(3):
            first = kind * A_HEADS
            in_specs.append(pl.BlockSpec((None, None, dil, seq // dil, HEAD_DIM),
                                         lambda b, h, first=first: (first + h, b, 0, 0, 0)))
            operands.append(groups[g])
    return pl.pallas_call(
        functools.partial(_attn_a_kernel, seq=seq),
        out_shape=jax.ShapeDtypeStruct((batch * seq, A_HEADS * HEAD_DIM), BF16),
        grid=(batch, A_HEADS),
        in_specs=in_specs,
        out_specs=pl.BlockSpec((seq, HEAD_DIM), lambda b, h: (b, h)),
        scratch_shapes=[pltpu.VMEM((N_DIL, seq, HEAD_DIM), F32), pltpu.VMEM((N_DIL, seq, HEAD_DIM), F32),
                        pltpu.VMEM((3, _SUB, _WIN), F32)],
        compiler_params=_params(("arbitrary", "arbitrary")),
    )(*operands)


_BQ = 512
_BK = 512
SCORE_BOUND = 64.0
BF16_SLACK = 1.02


def _attn_b_kernel(q_ref, k_ref, v_ref, o_ref, *, seq):
    rows = B_GROUP * _BQ
    q = q_ref[...].reshape(rows, HEAD_DIM)

    def body(c, carry):
        m, l, acc = carry
        c0 = pl.multiple_of(c * _BK, _BK)
        k = k_ref[pl.ds(c0, _BK), :]
        v = v_ref[pl.ds(c0, _BK), :]
        s = lax.dot_general(q, k, (((1,), (1,)), ((), ())), preferred_element_type=F32)
        m_new = jnp.maximum(m, jnp.max(s, axis=1, keepdims=True))
        alpha = jnp.exp2(m - m_new)
        p = jnp.exp2(s - m_new)
        l = alpha * l + jnp.sum(p, axis=1, keepdims=True)
        acc = alpha * acc + jnp.dot(p.astype(BF16), v, preferred_element_type=F32)
        return m_new, l, acc

    init = (jnp.full((rows, 1), NEG_BIG, F32), jnp.zeros((rows, 1), F32), jnp.zeros((rows, HEAD_DIM), F32))
    _, l, acc = lax.fori_loop(0, seq // _BK, body, init)
    o = acc / l
    for g in range(B_GROUP):
        o_ref[:, g * HEAD_DIM:(g + 1) * HEAD_DIM] = o[g * _BQ:(g + 1) * _BQ].astype(o_ref.dtype)


def _attn_b_bounded_kernel(q_ref, k_ref, v_ref, o_ref, *, seq):
    rows = B_GROUP * _BQ
    q = q_ref[...].reshape(rows, HEAD_DIM)
    ones = jnp.ones((_BK, HEAD_DIM), BF16)
    acc = jnp.zeros((rows, 2 * HEAD_DIM), F32)
    for c in range(seq // _BK):
        k = k_ref[c * _BK:(c + 1) * _BK, :]
        v = v_ref[c * _BK:(c + 1) * _BK, :]
        s = lax.dot_general(q, k, (((1,), (1,)), ((), ())), preferred_element_type=F32)
        p = jnp.exp2(s).astype(BF16)
        acc = acc + jnp.dot(p, jnp.concatenate([v, ones], axis=1), preferred_element_type=F32)
    o = acc[:, :HEAD_DIM] / acc[:, HEAD_DIM:HEAD_DIM + 1]
    for g in range(B_GROUP):
        o_ref[:, g * HEAD_DIM:(g + 1) * HEAD_DIM] = o[g * _BQ:(g + 1) * _BQ].astype(o_ref.dtype)


def _attn_b(slots, batch, seq, bounded):
    n_tok = batch * seq
    k0 = B_Q_HEADS
    v0 = k0 + B_KV_HEADS
    qblocks = seq // _BQ
    return pl.pallas_call(
        functools.partial(_attn_b_bounded_kernel if bounded else _attn_b_kernel, seq=seq),
        out_shape=jax.ShapeDtypeStruct((n_tok, B_Q_HEADS * HEAD_DIM), BF16),
        grid=(batch, B_KV_HEADS, qblocks),
        in_specs=[
            pl.BlockSpec((B_GROUP, _BQ, HEAD_DIM), lambda b, kv, qi: (kv, b * qblocks + qi, 0)),
            pl.BlockSpec((None, seq, HEAD_DIM), lambda b, kv, qi: (k0 + kv, b, 0)),
            pl.BlockSpec((None, seq, HEAD_DIM), lambda b, kv, qi: (v0 + kv, b, 0)),
        ],
        out_specs=pl.BlockSpec((_BQ, B_GROUP * HEAD_DIM), lambda b, kv, qi: (b * qblocks + qi, kv)),
        compiler_params=_params(("arbitrary", "arbitrary", "arbitrary")),
    )(slots, slots, slots)


_MIX_ROWS = 512


def _branch_kernel(oa_ref, ob_ref, ga_ref, gb_ref, wa_ref, wb_ref, out_ref):
    for rt in range(out_ref.shape[0] // SUB_ROWS):
        rows = slice(rt * SUB_ROWS, (rt + 1) * SUB_ROWS)
        y_a = jnp.dot(oa_ref[rows, :], wa_ref[...], preferred_element_type=F32)
        y_b = jnp.dot(ob_ref[rows, :], wb_ref[...], preferred_element_type=F32)
        out_ref[rows, :] = (ga_ref[rows, :].astype(F32) * y_a + gb_ref[rows, :].astype(F32) * y_b).astype(BF16)


def _branch_mix(o_a, o_b, gates, wa, wb):
    n_tok = o_b.shape[0]
    d_model = wa.shape[1]
    tm = _MIX_ROWS
    row = lambda i: (i, 0)
    return pl.pallas_call(
        _branch_kernel,
        out_shape=jax.ShapeDtypeStruct((n_tok, d_model), BF16),
        grid=(n_tok // tm,),
        in_specs=[pl.BlockSpec((tm, o_a.shape[1]), row),
                  pl.BlockSpec((tm, o_b.shape[1]), row),
                  pl.BlockSpec((tm, d_model), lambda i: (i, 0)),
                  pl.BlockSpec((tm, d_model), lambda i: (i, 1)),
                  pl.BlockSpec(wa.shape, lambda i: (0, 0)),
                  pl.BlockSpec(wb.shape, lambda i: (0, 0))],
        out_specs=pl.BlockSpec((tm, d_model), row),
        compiler_params=_params(("arbitrary",)),
    )(o_a, o_b, gates, gates, wa, wb)


def _layer_norm(z, g, b):
    mu = jnp.mean(z, axis=-1, keepdims=True)
    zc = z - mu
    var = jnp.mean(zc * zc, axis=-1, keepdims=True)
    return zc * lax.rsqrt(var + LN_EPS) * g + b


def _to_slabs(slab_ref, value, row0):
    rows, d = value.shape
    chunks = d // LANES
    for c in range(chunks):
        slab_ref[pl.ds(row0 * chunks + c, rows, stride=chunks), :] = value[:, c * LANES:(c + 1) * LANES]


def _from_slabs(slab_ref, rows, chunks, c):
    return slab_ref[pl.ds(c, rows, stride=chunks), :]


def _out_proj_kernel(m_ref, w_ref, x_ref, g_ref, b_ref, rcat_ref, rb_ref, hs_ref, lg_ref, *, alpha):
    for rt in range(m_ref.shape[0] // SUB_ROWS):
        rows = slice(rt * SUB_ROWS, (rt + 1) * SUB_ROWS)
        mix = jnp.dot(m_ref[rows, :], w_ref[...], preferred_element_type=F32)
        h = _layer_norm(alpha * x_ref[rows, :] + mix, g_ref[...], b_ref[...])
        _to_slabs(hs_ref, h, rt * SUB_ROWS)
        hi = h.astype(BF16)
        lo = (h - hi.astype(F32)).astype(BF16)
        both = jnp.dot(hi, rcat_ref[...], preferred_element_type=F32)
        lg = both[:, :LANES] + both[:, LANES:] + jnp.dot(lo, rcat_ref[:, :LANES], preferred_element_type=F32)
        lg_ref[rows, :] = lg + rb_ref[...]


def _out_proj(merged, w_out, x2, g, b, r_cat, r_b, alpha):
    n_tok, d_model = x2.shape
    tm = _MIX_ROWS
    chunks = d_model // LANES
    row = lambda i: (i, 0)
    fix = lambda i: (0, 0)
    return pl.pallas_call(
        functools.partial(_out_proj_kernel, alpha=alpha),
        out_shape=(jax.ShapeDtypeStruct((n_tok * chunks, LANES), F32),
                   jax.ShapeDtypeStruct((n_tok, LANES), F32)),
        grid=(n_tok // tm,),
        in_specs=[pl.BlockSpec((tm, d_model), row),
                  pl.BlockSpec(w_out.shape, fix),
                  pl.BlockSpec((tm, d_model), row),
                  pl.BlockSpec((1, d_model), fix),
                  pl.BlockSpec((1, d_model), fix),
                  pl.BlockSpec(r_cat.shape, fix),
                  pl.BlockSpec((1, LANES), fix)],
        out_specs=(pl.BlockSpec((tm * chunks, LANES), row), pl.BlockSpec((tm, LANES), row)),
        compiler_params=_params(("arbitrary",)),
    )(merged, w_out, x2, g, b, r_cat, r_b)


_BIG_CHUNK = 512


def _moe_kernel(item_expert, item_row0, item_blocks, item_real, src_slab, dst_slab,
                h_hbm, wg_ref, wu_ref, wd_ref, y_hbm,
                stage, xbf, acc, obuf, gsem, ssem):
    it = pl.program_id(0)
    f = pl.program_id(1)
    n_items = pl.num_programs(0)
    n_ff = pl.num_programs(1)
    n_blocks = item_blocks[it]
    d_model = xbf.shape[1]
    chunks = d_model // LANES

    def slab(ref, first):
        return ref.at[pl.ds(pl.multiple_of(first, chunks), chunks)]

    def gather_copy(base, r):
        return pltpu.make_async_copy(slab(h_hbm, src_slab[base + r]), slab(stage, r * chunks), gsem.at[0])

    def scatter_copy(base, r):
        return pltpu.make_async_copy(slab(obuf, r * chunks), slab(y_hbm, dst_slab[base + r]), ssem.at[0])

    def for_rows(n_rows, fn):
        def group(t, c):
            for u in range(DMA_UNROLL):
                fn(t * DMA_UNROLL + u)
            return c
        n_groups = n_rows // DMA_UNROLL
        lax.fori_loop(0, n_groups, group, 0)

        def single(r, c):
            fn(r)
            return c
        lax.fori_loop(n_groups * DMA_UNROLL, n_rows, single, 0)

    def gather(item, op):
        base = item_row0[item]
        for_rows(item_blocks[item] * MOE_BLOCK, lambda r: op(gather_copy(base, r)))

    def scatter(item, op):
        base = item_row0[item]
        for_rows(item_real[item], lambda r: op(scatter_copy(base, r)))

    start = lambda cp: cp.start()
    wait = lambda cp: cp.wait()

    @pl.when((it == 0) & (f == 0))
    def _first_gather():
        gather(it, start)

    @pl.when(f == 0)
    def _gathered():
        gather(it, wait)

        def cast_rows(b, carry):
            r0 = pl.multiple_of(b * MOE_BLOCK, MOE_BLOCK)
            for c in range(chunks):
                piece = stage[pl.ds(r0 * chunks + c, MOE_BLOCK, stride=chunks), :]
                xbf[pl.ds(r0, MOE_BLOCK), c * LANES:(c + 1) * LANES] = piece.astype(BF16)
            return carry
        lax.fori_loop(0, n_blocks, cast_rows, 0)

        @pl.when(it + 1 < n_items)
        def _prefetch_next():
            gather(it + 1, start)

    def chunk(r0, rows, mode):
        xb = xbf[pl.ds(r0, rows), :]
        a = jnp.dot(xb, wg_ref[...].astype(BF16), preferred_element_type=F32)
        u = jnp.dot(xb, wu_ref[...].astype(BF16), preferred_element_type=F32)
        hid = (a / (1.0 + jnp.exp(-a)) * u).astype(BF16)
        y = jnp.dot(hid, wd_ref[...].astype(BF16), preferred_element_type=F32)
        if mode == "first":
            acc[pl.ds(r0, rows), :] = y
        elif mode == "middle":
            acc[pl.ds(r0, rows), :] += y
        else:
            if mode == "last":
                y = acc[pl.ds(r0, rows), :] + y
            for c in range(chunks):
                obuf[pl.ds(r0 * chunks + c, rows, stride=chunks), :] = y[:, c * LANES:(c + 1) * LANES]

    def run_item(mode):
        per_big = _BIG_CHUNK // MOE_BLOCK

        def big_chunk(t, c):
            chunk(pl.multiple_of(t * _BIG_CHUNK, _BIG_CHUNK), _BIG_CHUNK, mode)
            return c
        n_big = n_blocks // per_big
        lax.fori_loop(0, n_big, big_chunk, 0)
        rest = n_blocks - n_big * per_big
        for blocks in range(1, per_big):
            @pl.when(rest == blocks)
            def _rest(blocks=blocks):
                chunk(pl.multiple_of(n_big * _BIG_CHUNK, _BIG_CHUNK), blocks * MOE_BLOCK, mode)

    last = n_ff - 1

    @pl.when((n_blocks > 0) & (f == last))
    def _final():
        @pl.when(it > 0)
        def _drain_prev():
            scatter(it - 1, wait)
        run_item("only" if n_ff == 1 else "last")
        scatter(it, start)

    if n_ff > 1:
        @pl.when((n_blocks > 0) & (f == 0))
        def _first():
            run_item("first")

        @pl.when((n_blocks > 0) & (f > 0) & (f < last))
        def _middle():
            run_item("middle")

    @pl.when((f == last) & (it == n_items - 1))
    def _drain_last():
        scatter(jnp.where(n_blocks > 0, it, _last_live(item_blocks, n_items)), wait)


def _last_live(item_blocks, n_items):
    def body(i, best):
        return jnp.where(item_blocks[i] > 0, i, best)
    return lax.fori_loop(0, n_items, body, 0)


def _moe(h_slabs, w_gate, w_up, w_down, item_expert, item_row0, item_blocks, item_real, src_slab, dst_slab):
    d_model = w_gate.shape[1]
    chunks = d_model // LANES
    n_tok = h_slabs.shape[0] // chunks
    ff = w_gate.shape[2]
    n_ff = ff // FF_TILE
    n_items = item_expert.shape[0]

    def ff_idx(it, f, blocks):
        return jnp.where(blocks[it] > 0, f, n_ff - 1)

    grid_spec = pltpu.PrefetchScalarGridSpec(
        num_scalar_prefetch=6,
        grid=(n_items, n_ff),
        in_specs=[
            pl.BlockSpec(memory_space=pl.ANY),
            pl.BlockSpec((None, d_model, FF_TILE), lambda it, f, ie, ir, ib, nr, st, dr: (ie[it], 0, ff_idx(it, f, ib))),
            pl.BlockSpec((None, d_model, FF_TILE), lambda it, f, ie, ir, ib, nr, st, dr: (ie[it], 0, ff_idx(it, f, ib))),
            pl.BlockSpec((None, FF_TILE, d_model), lambda it, f, ie, ir, ib, nr, st, dr: (ie[it], ff_idx(it, f, ib), 0)),
        ],
        out_specs=pl.BlockSpec(memory_space=pl.ANY),
        scratch_shapes=[
            pltpu.VMEM((ITEM_ROWS * chunks, LANES), F32),
            pltpu.VMEM((ITEM_ROWS, d_model), BF16),
            pltpu.VMEM((ITEM_ROWS, d_model), F32),
            pltpu.VMEM((ITEM_ROWS * chunks, LANES), F32),
            pltpu.SemaphoreType.DMA((1,)),
            pltpu.SemaphoreType.DMA((1,)),
        ],
    )
    return pl.pallas_call(
        _moe_kernel,
        out_shape=jax.ShapeDtypeStruct((n_tok * TOP_K * chunks, LANES), F32),
        grid_spec=grid_spec,
        compiler_params=_params(("arbitrary", "arbitrary")),
    )(item_expert, item_row0, item_blocks, item_real, src_slab, dst_slab, h_slabs, w_gate, w_up, w_down)


_COMBINE_ROWS = 512


def _combine_kernel(y0_ref, y1_ref, wt_ref, hs_ref, g_ref, b_ref, out_ref, *, alpha):
    rows, d_model = out_ref.shape
    chunks = d_model // LANES
    wt = wt_ref[...]
    w0, w1 = wt[:, 0:1], wt[:, 1:2]
    for c in range(chunks):
        cols = slice(c * LANES, (c + 1) * LANES)
        ffn = w0 * _from_slabs(y0_ref, rows, chunks, c) + w1 * _from_slabs(y1_ref, rows, chunks, c)
        out_ref[:, cols] = alpha * _from_slabs(hs_ref, rows, chunks, c) + ffn
    out_ref[...] = _layer_norm(out_ref[...], g_ref[...], b_ref[...])


def _combine(y_slabs, weights, h_slabs, g, b, alpha):
    d_model = g.shape[1]
    n_tok = h_slabs.shape[0] * LANES // d_model
    tm = _COMBINE_ROWS
    chunks = d_model // LANES
    second = n_tok // tm
    row = lambda i: (i, 0)
    fix = lambda i: (0, 0)
    return pl.pallas_call(
        functools.partial(_combine_kernel, alpha=alpha),
        out_shape=jax.ShapeDtypeStruct((n_tok, d_model), F32),
        grid=(n_tok // tm,),
        in_specs=[pl.BlockSpec((tm * chunks, LANES), row),
                  pl.BlockSpec((tm * chunks, LANES), lambda i: (second + i, 0)),
                  pl.BlockSpec((tm, TOP_K), row),
                  pl.BlockSpec((tm * chunks, LANES), row),
                  pl.BlockSpec((1, d_model), fix),
                  pl.BlockSpec((1, d_model), fix)],
        out_specs=pl.BlockSpec((tm, d_model), row),
        compiler_params=_params(("arbitrary",)),
    )(y_slabs, y_slabs, weights, h_slabs, g, b)


def _rope_tables(seq):
    f32 = np.float32
    half = HEAD_DIM // 2
    inv1 = np.power(f32(ROPE_THETA), -(np.arange(half, dtype=f32) / f32(half)))
    ang1 = np.arange(seq, dtype=f32)[:, None] * inv1[None, :]
    t = np.arange(seq)
    n_axis = HEAD_DIM // 4
    inv2 = np.power(f32(ROPE_THETA), -(np.arange(n_axis, dtype=f32) / f32(n_axis)))
    ang2 = np.concatenate([(t // GRID_W).astype(f32)[:, None] * inv2[None, :],
                           (t % GRID_W).astype(f32)[:, None] * inv2[None, :]], axis=-1)
    q_scale = f32(HEAD_DIM ** -0.5 * LOG2E)
    out = []
    for ang in (ang1, ang2):
        cos = np.concatenate([np.cos(ang), np.cos(ang)], axis=-1)
        sin = np.concatenate([-np.sin(ang), np.sin(ang)], axis=-1)
        out += [cos * q_scale, sin * q_scale, cos, sin]
    return jnp.asarray(np.stack(out, axis=0).astype(f32))


def _route(logits, n_tok, chunks):
    g_logits = logits[:, :N_GROUPS]
    g_prob = jax.nn.softmax(g_logits, axis=-1)
    g_idx = jnp.argmax(g_logits, axis=-1).astype(I32)
    g_gate = jnp.max(g_prob, axis=-1)
    e_logits = logits[:, N_GROUPS:N_GROUPS + EXPERTS_PER_GROUP]
    for g in range(1, N_GROUPS):
        lo = N_GROUPS + g * EXPERTS_PER_GROUP
        e_logits = jnp.where(g_idx[:, None] == g, logits[:, lo:lo + EXPERTS_PER_GROUP], e_logits)
    e_prob = jax.nn.softmax(e_logits, axis=-1)
    i1 = jnp.argmax(e_prob, axis=-1).astype(I32)
    p1 = jnp.max(e_prob, axis=-1)
    rest = jnp.where(jnp.arange(EXPERTS_PER_GROUP, dtype=I32)[None, :] == i1[:, None], -1.0, e_prob)
    i2 = jnp.argmax(rest, axis=-1).astype(I32)
    p2 = jnp.max(rest, axis=-1)
    top_p = jnp.stack([p1, p2], axis=-1)
    top_p = top_p / jnp.sum(top_p, axis=-1, keepdims=True)
    weights = g_gate[:, None] * top_p
    expert = g_idx[:, None] * EXPERTS_PER_GROUP + jnp.stack([i1, i2], axis=-1)

    n_slot = n_tok * TOP_K
    e_flat = expert.reshape(n_slot).astype(I32)
    order = jnp.argsort(e_flat).astype(I32)
    experts = jnp.arange(N_EXPERTS, dtype=I32)
    counts = jnp.sum(e_flat[:, None] == experts[None, :], axis=0, dtype=I32)
    padded = ((counts + MOE_BLOCK - 1) // MOE_BLOCK) * MOE_BLOCK
    p_ends = jnp.cumsum(padded)
    p_starts = p_ends - padded
    buf_len = n_slot + N_EXPERTS * MOE_BLOCK

    def pick(one_hot, table):
        return jnp.sum(jnp.where(one_hot, table[None, :], 0), axis=1, dtype=I32)

    pos = jnp.arange(buf_len, dtype=I32)[:, None]
    ended = p_ends[None, :] <= pos
    inside = jnp.logical_and(p_starts[None, :] <= pos, pos < p_ends[None, :])
    idx = pos[:, 0] - jnp.max(jnp.where(ended, p_ends[None, :], 0), axis=1)
    sorted_at = jnp.clip(pick(ended, counts) + idx, 0, n_slot - 1)
    slot_at = jnp.where(idx < pick(inside, counts), order[sorted_at], 0)
    tok_at, k_at = slot_at // TOP_K, slot_at % TOP_K
    src_slab = tok_at * chunks
    dst_slab = (k_at * n_tok + tok_at) * chunks

    n_items = N_EXPERTS + n_slot // ITEM_ROWS
    per_expert = (padded + ITEM_ROWS - 1) // ITEM_ROWS
    item_ends = jnp.cumsum(per_expert)
    total = item_ends[-1]
    ids = jnp.arange(n_items, dtype=I32)
    live = ids < total
    item = jnp.minimum(ids, jnp.maximum(total - 1, 0))[:, None]
    owner = jnp.logical_and((item_ends - per_expert)[None, :] <= item, item < item_ends[None, :])
    e_of = pick(owner, experts)
    chunk = item[:, 0] - pick(owner, item_ends - per_expert)
    item_row0 = jnp.where(live, pick(owner, p_starts) + chunk * ITEM_ROWS, 0).astype(I32)
    item_blocks = jnp.where(live, jnp.clip(pick(owner, padded) // MOE_BLOCK - chunk * ITEM_BLOCKS, 0, ITEM_BLOCKS), 0)
    item_real = jnp.where(live, jnp.clip(pick(owner, counts) - chunk * ITEM_ROWS, 0, ITEM_ROWS), 0)
    return weights, e_of, item_row0, item_blocks.astype(I32), item_real.astype(I32), src_slab, dst_slab


def kernel(x, w_in, q_norm_g, k_norm_g, w_branch_a, w_branch_b, w_out, ln1_g, ln1_b, w_group, b_group, w_router,
           b_router, w_gate, w_up, w_down, ln2_g, ln2_b):
    batch, seq, d_model = x.shape
    depth = w_in.shape[0]
    n_tok = batch * seq
    dn_alpha = (2 * depth) ** 0.25
    tabs = _rope_tables(seq)
    h = x.reshape(n_tok, d_model)
    for layer in range(depth):
        xb = h.astype(BF16)
        groups = [_proj_a(xb, w_in[layer], tabs, g, batch, seq) for g in range(N_DIL)]
        slots_b = _proj_b(xb, w_in[layer], tabs, q_norm_g[layer][None, :], k_norm_g[layer][None, :], seq)
        gates = _proj_gate(xb, w_in[layer])
        o_a = _attn_a(groups, batch, seq)
        score_bound = (jnp.max(jnp.abs(q_norm_g[layer])) * jnp.max(jnp.abs(k_norm_g[layer]))
                       * (HEAD_DIM ** 0.5 * LOG2E * BF16_SLACK))
        o_b = lax.cond(score_bound <= SCORE_BOUND,
                       lambda s: _attn_b(s, batch, seq, True), lambda s: _attn_b(s, batch, seq, False), slots_b)
        merged = _branch_mix(o_a, o_b, gates, w_branch_a[layer].astype(BF16), w_branch_b[layer].astype(BF16))
        w_r = jnp.concatenate([w_group[layer], w_router[layer]], axis=1)
        w_r = jnp.pad(w_r, ((0, 0), (0, LANES - w_r.shape[1])))
        r_hi = w_r.astype(BF16)
        r_lo = (w_r - r_hi.astype(F32)).astype(BF16)
        r_b = jnp.pad(jnp.concatenate([b_group[layer], b_router[layer]]), (0, LANES - N_GROUPS - N_EXPERTS))[None, :]
        h1_slabs, logits = _out_proj(merged, w_out[layer].astype(BF16), h, ln1_g[layer][None, :],
                                         ln1_b[layer][None, :], jnp.concatenate([r_hi, r_lo], axis=1), r_b, dn_alpha)
        weights, item_expert, item_row0, item_blocks, item_real, src_slab, dst_slab = _route(
            logits, n_tok, d_model // LANES)
        y_slabs = _moe(h1_slabs, w_gate[layer], w_up[layer], w_down[layer], item_expert, item_row0, item_blocks,
                       item_real, src_slab, dst_slab)
        h = _combine(y_slabs, weights, h1_slabs, ln2_g[layer][None, :], ln2_b[layer][None, :], dn_alpha)
    return h.reshape(batch, seq, d_model)
```

```python
import functools
import math

import jax
import jax.numpy as jnp
import numpy as np
from jax import lax
from jax.experimental import pallas as pl
from jax.experimental.pallas import tpu as pltpu

F32 = jnp.float32
BF16 = jnp.bfloat16
I32 = jnp.int32

HEAD_DIM = 128
ROPE_THETA = 10000.0
GRID_W = 64
DIL_PATTERNS = ((128, 1), (512, 4), (2048, 16))
N_DIL = len(DIL_PATTERNS)
A_HEADS = 8
BAND_HALF = 64
B_Q_HEADS = 16
B_KV_HEADS = 4
B_GROUP = B_Q_HEADS // B_KV_HEADS
N_GROUPS = 4
EXPERTS_PER_GROUP = 8
N_EXPERTS = N_GROUPS * EXPERTS_PER_GROUP
TOP_K = 2
MOE_BLOCK = 128
LN_EPS = 1e-5
RMS_EPS = 1e-6
NEG_BIG = -1e30
LOG2E = math.log2(math.e)

LANES = 128
COL_TILE = 8 * HEAD_DIM
ROW_TILE = 1024
SUB_ROWS = 256
ITEM_ROWS = 768
ITEM_BLOCKS = ITEM_ROWS // MOE_BLOCK
FF_TILE = 512
DMA_UNROLL = 8
VMEM_LIMIT = 56 * 1024 * 1024


def _params(sem, vmem=VMEM_LIMIT):
    return pltpu.CompilerParams(dimension_semantics=sem, vmem_limit_bytes=vmem)


_T_QA, _T_KA, _T_VA, _T_QB, _T_KVB, _T_GATE = 0, 3, 6, 9, 11, 12
_TAB_1D, _TAB_AXIAL = 0, 1
_RQ, _RK = 0, 2


def _cast_weights_once(w_ref, wbf_ref):
    @pl.when(pl.program_id(1) == 0)
    def _cast():
        wbf_ref[...] = w_ref[...].astype(BF16)


def _for_sub_tiles(x_ref, wbf_ref, emit):
    for rt in range(x_ref.shape[0] // SUB_ROWS):
        rows = slice(rt * SUB_ROWS, (rt + 1) * SUB_ROWS)
        emit(rt, rows, jnp.dot(x_ref[rows, :], wbf_ref[...], preferred_element_type=F32))


def _head(acc, h):
    return acc[:, h * HEAD_DIM:(h + 1) * HEAD_DIM]


def _rope(a, tab_ref, t, rows):
    return a * tab_ref[t, rows, :] + pltpu.roll(a, HEAD_DIM // 2, 1) * tab_ref[t + 1, rows, :]


def _rms(a, g_ref):
    ms = jnp.mean(a * a, axis=-1, keepdims=True)
    return a * lax.rsqrt(ms + RMS_EPS) * g_ref[...]


def _proj_a_kernel(x_ref, w_ref, tab_ref, out_ref, wbf_ref, *, dil):
    j = pl.program_id(0)
    _cast_weights_once(w_ref, wbf_ref)
    per_res = SUB_ROWS // dil

    def emit_with(fn):
        def emit(rt, rows, acc):
            if dil == 1:
                for h in range(A_HEADS):
                    out_ref[h, 0, rows, :] = fn(_head(acc, h), rows).astype(BF16)
                return
            for h in range(A_HEADS):
                by_res = fn(_head(acc, h), rows).reshape(per_res, dil, HEAD_DIM).swapaxes(0, 1)
                for r in range(dil):
                    out_ref[h, r, rt * per_res:(rt + 1) * per_res, :] = by_res[r].astype(BF16)
        _for_sub_tiles(x_ref, wbf_ref, emit)

    @pl.when(j == 0)
    def _q():
        emit_with(lambda a, rows: _rope(a, tab_ref, _RQ, rows))

    @pl.when(j == 1)
    def _k():
        emit_with(lambda a, rows: _rope(a, tab_ref, _RK, rows))

    @pl.when(j == 2)
    def _v():
        emit_with(lambda a, rows: a)


def _proj_b_kernel(x_ref, w_ref, tab_ref, gq_ref, gk_ref, out_ref, wbf_ref, *, q_tiles):
    j = pl.program_id(0)
    _cast_weights_once(w_ref, wbf_ref)
    heads = COL_TILE // HEAD_DIM

    @pl.when(j < q_tiles)
    def _qb():
        def emit(rt, rows, acc):
            for h in range(heads):
                out_ref[h, rows, :] = _rope(_rms(_head(acc, h), gq_ref), tab_ref, _RQ, rows).astype(BF16)
        _for_sub_tiles(x_ref, wbf_ref, emit)

    @pl.when(j == q_tiles)
    def _kvb():
        def emit(rt, rows, acc):
            for h in range(heads):
                if h < B_KV_HEADS:
                    out_ref[h, rows, :] = _rope(_rms(_head(acc, h), gk_ref), tab_ref, _RK, rows).astype(BF16)
                else:
                    out_ref[h, rows, :] = _head(acc, h).astype(BF16)
        _for_sub_tiles(x_ref, wbf_ref, emit)


def _proj_gate_kernel(x_ref, w_ref, out_ref, wbf_ref):
    _cast_weights_once(w_ref, wbf_ref)

    def emit(rt, rows, acc):
        out_ref[rows, :] = (1.0 / (1.0 + jnp.exp(-acc))).astype(BF16)
    _for_sub_tiles(x_ref, wbf_ref, emit)


def _proj_specs(d_model, seq, first_tile, tile_step, table_half):
    seq_blocks = seq // ROW_TILE
    return [
        pl.BlockSpec((ROW_TILE, d_model), lambda j, i: (i, 0)),
        pl.BlockSpec((d_model, COL_TILE), lambda j, i: (0, first_tile + tile_step * j)),
        pl.BlockSpec((4, ROW_TILE, HEAD_DIM), lambda j, i: (table_half, i % seq_blocks, 0)),
    ]


def _proj_a(xb, w_in, tabs, group, batch, seq):
    n_tok, d_model = xb.shape
    dil = DIL_PATTERNS[group][1]
    sub_len = seq // dil
    seq_blocks = seq // ROW_TILE
    return pl.pallas_call(
        functools.partial(_proj_a_kernel, dil=dil),
        out_shape=jax.ShapeDtypeStruct((3 * A_HEADS, batch, dil, sub_len, HEAD_DIM), BF16),
        grid=(3, n_tok // ROW_TILE),
        in_specs=_proj_specs(d_model, seq, group, N_DIL, _TAB_1D),
        out_specs=pl.BlockSpec((A_HEADS, None, dil, ROW_TILE // dil, HEAD_DIM),
                               lambda j, i: (j, i // seq_blocks, 0, i % seq_blocks, 0)),
        scratch_shapes=[pltpu.VMEM((d_model, COL_TILE), BF16)],
        compiler_params=_params(("arbitrary", "arbitrary")),
    )(xb, w_in, tabs)


def _proj_b(xb, w_in, tabs, gq, gk, seq):
    n_tok, d_model = xb.shape
    q_tiles = B_Q_HEADS * HEAD_DIM // COL_TILE
    heads = COL_TILE // HEAD_DIM
    fix = lambda j, i: (0, 0)
    return pl.pallas_call(
        functools.partial(_proj_b_kernel, q_tiles=q_tiles),
        out_shape=jax.ShapeDtypeStruct(((q_tiles + 1) * heads, n_tok, HEAD_DIM), BF16),
        grid=(q_tiles + 1, n_tok // ROW_TILE),
        in_specs=_proj_specs(d_model, seq, _T_QB, 1, _TAB_AXIAL) + [pl.BlockSpec((1, HEAD_DIM), fix), pl.BlockSpec((1, HEAD_DIM), fix)],
        out_specs=pl.BlockSpec((heads, ROW_TILE, HEAD_DIM), lambda j, i: (j, i, 0)),
        scratch_shapes=[pltpu.VMEM((d_model, COL_TILE), BF16)],
        compiler_params=_params(("arbitrary", "arbitrary")),
    )(xb, w_in, tabs, gq, gk)


def _proj_gate(xb, w_in):
    n_tok, d_model = xb.shape
    n_ct = w_in.shape[1] // COL_TILE - _T_GATE
    return pl.pallas_call(
        _proj_gate_kernel,
        out_shape=jax.ShapeDtypeStruct((n_tok, n_ct * COL_TILE), BF16),
        grid=(n_ct, n_tok // ROW_TILE),
        in_specs=[pl.BlockSpec((ROW_TILE, d_model), lambda j, i: (i, 0)),
                  pl.BlockSpec((d_model, COL_TILE), lambda j, i: (0, _T_GATE + j))],
        out_specs=pl.BlockSpec((ROW_TILE, COL_TILE), lambda j, i: (i, j)),
        scratch_shapes=[pltpu.VMEM((d_model, COL_TILE), BF16)],
        compiler_params=_params(("arbitrary", "arbitrary")),
    )(xb, w_in)


_SUB = 128
_WIN = _SUB + 2 * BAND_HALF
_MERGE_ROWS = 256
_A_UNROLL = 16


def _attn_a_group(g, dil, seq, q_ref, k_ref, v_ref, og, lg, bias):
    sub_len = seq // dil
    per_seq = sub_len // _SUB
    shift = per_seq.bit_length() - 1
    ones = jnp.ones((_WIN, HEAD_DIM), BF16)

    def body(i, carry):
        r = lax.shift_right_logical(i, shift)
        p0 = pl.multiple_of((i & (per_seq - 1)) * _SUB, _SUB)
        start = pl.multiple_of(jnp.clip(p0 - BAND_HALF, 0, sub_len - _WIN), BAND_HALF)
        q = q_ref[r, pl.ds(p0, _SUB), :]
        k = k_ref[r, pl.ds(start, _WIN), :]
        v = v_ref[r, pl.ds(start, _WIN), :]
        s = lax.dot_general(q, k, (((1,), (1,)), ((), ())), preferred_element_type=F32)
        s = s + bias[lax.shift_right_logical(p0 - start, BAND_HALF.bit_length() - 1)]
        m = jnp.max(s, axis=1, keepdims=True)
        p = jnp.exp2(s - m).astype(BF16)
        both = jnp.dot(p, jnp.concatenate([v, ones], axis=1), preferred_element_type=F32)
        l = both[:, HEAD_DIM:]
        o = both[:, :HEAD_DIM] / l
        lse = m + jnp.log2(l)
        if dil == 1:
            rows = pl.ds(p0, _SUB)
        else:
            rows = pl.ds(p0 * dil + r, _SUB, stride=dil)
        og[g, rows, :] = o
        lg[g, rows, :] = lse
        return carry

    lax.fori_loop(0, seq // _SUB, body, 0, unroll=_A_UNROLL)


def _attn_a_kernel(*refs, seq):
    qkv, o_ref = refs[:3 * N_DIL], refs[3 * N_DIL]
    og, lg, bias = refs[3 * N_DIL + 1:]
    row = lax.broadcasted_iota(I32, (_SUB, _WIN), 0)
    col = lax.broadcasted_iota(I32, (_SUB, _WIN), 1)
    for case in range(3):
        bias[case] = jnp.where(jnp.abs(col - row - case * BAND_HALF) <= BAND_HALF, 0.0, NEG_BIG)
    for g, (_, dil) in enumerate(DIL_PATTERNS):
        _attn_a_group(g, dil, seq, qkv[3 * g], qkv[3 * g + 1], qkv[3 * g + 2], og, lg, bias)

    def merge(c, carry):
        rows = pl.ds(pl.multiple_of(c * _MERGE_ROWS, _MERGE_ROWS), _MERGE_ROWS)
        ls = [lg[g, rows, :] for g in range(N_DIL)]
        mx = functools.reduce(jnp.maximum, ls)
        ws = [jnp.exp2(l - mx) for l in ls]
        num = functools.reduce(lambda a, b: a + b, [w * og[g, rows, :] for g, w in enumerate(ws)])
        o_ref[rows, :] = (num / functools.reduce(lambda a, b: a + b, ws)).astype(o_ref.dtype)
        return carry

    lax.fori_loop(0, seq // _MERGE_ROWS, merge, 0)


def _attn_a(groups, batch, seq):
    in_specs, operands = [], []
    for g, (_, dil) in enumerate(DIL_PATTERNS):
        for kind in range(3):
            first = kind * A_HEADS
            in_specs.append(pl.BlockSpec((None, None, dil, seq // dil, HEAD_DIM),
                                         lambda b, h, first=first: (first + h, b, 0, 0, 0)))
            operands.append(groups[g])
    return pl.pallas_call(
        functools.partial(_attn_a_kernel, seq=seq),
        out_shape=jax.ShapeDtypeStruct((batch * seq, A_HEADS * HEAD_DIM), BF16),
        grid=(batch, A_HEADS),
        in_specs=in_specs,
        out_specs=pl.BlockSpec((seq, HEAD_DIM), lambda b, h: (b, h)),
        scratch_shapes=[pltpu.VMEM((N_DIL, seq, HEAD_DIM), F32), pltpu.VMEM((N_DIL, seq, HEAD_DIM), F32),
                        pltpu.VMEM((3, _SUB, _WIN), F32)],
        compiler_params=_params(("arbitrary", "arbitrary")),
    )(*operands)


_BQ = 512
_BK = 512
SCORE_BOUND = 64.0
BF16_SLACK = 1.02


def _attn_b_kernel(q_ref, k_ref, v_ref, o_ref, *, seq):
    rows = B_GROUP * _BQ
    q = q_ref[...].reshape(rows, HEAD_DIM)

    def body(c, carry):
        m, l, acc = carry
        c0 = pl.multiple_of(c * _BK, _BK)
        k = k_ref[pl.ds(c0, _BK), :]
        v = v_ref[pl.ds(c0, _BK), :]
        s = lax.dot_general(q, k, (((1,), (1,)), ((), ())), preferred_element_type=F32)
        m_new = jnp.maximum(m, jnp.max(s, axis=1, keepdims=True))
        alpha = jnp.exp2(m - m_new)
        p = jnp.exp2(s - m_new)
        l = alpha * l + jnp.sum(p, axis=1, keepdims=True)
        acc = alpha * acc + jnp.dot(p.astype(BF16), v, preferred_element_type=F32)
        return m_new, l, acc

    init = (jnp.full((rows, 1), NEG_BIG, F32), jnp.zeros((rows, 1), F32), jnp.zeros((rows, HEAD_DIM), F32))
    _, l, acc = lax.fori_loop(0, seq // _BK, body, init)
    o = acc / l
    for g in range(B_GROUP):
        o_ref[:, g * HEAD_DIM:(g + 1) * HEAD_DIM] = o[g * _BQ:(g + 1) * _BQ].astype(o_ref.dtype)


def _attn_b_bounded_kernel(q_ref, k_ref, v_ref, o_ref, *, seq):
    rows = B_GROUP * _BQ
    q = q_ref[...].reshape(rows, HEAD_DIM)
    ones = jnp.ones((_BK, HEAD_DIM), BF16)
    acc = jnp.zeros((rows, 2 * HEAD_DIM), F32)
    for c in range(seq // _BK):
        k = k_ref[c * _BK:(c + 1) * _BK, :]
        v = v_ref[c * _BK:(c + 1) * _BK, :]
        s = lax.dot_general(q, k, (((1,), (1,)), ((), ())), preferred_element_type=F32)
        p = jnp.exp2(s).astype(BF16)
        acc = acc + jnp.dot(p, jnp.concatenate([v, ones], axis=1), preferred_element_type=F32)
    o = acc[:, :HEAD_DIM] / acc[:, HEAD_DIM:HEAD_DIM + 1]
    for g in range(B_GROUP):
        o_ref[:, g * HEAD_DIM:(g + 1) * HEAD_DIM] = o[g * _BQ:(g + 1) * _BQ].astype(o_ref.dtype)


def _attn_b(slots, batch, seq, bounded):
    n_tok = batch * seq
    k0 = B_Q_HEADS
    v0 = k0 + B_KV_HEADS
    qblocks = seq // _BQ
    return pl.pallas_call(
        functools.partial(_attn_b_bounded_kernel if bounded else _attn_b_kernel, seq=seq),
        out_shape=jax.ShapeDtypeStruct((n_tok, B_Q_HEADS * HEAD_DIM), BF16),
        grid=(batch, B_KV_HEADS, qblocks),
        in_specs=[
            pl.BlockSpec((B_GROUP, _BQ, HEAD_DIM), lambda b, kv, qi: (kv, b * qblocks + qi, 0)),
            pl.BlockSpec((None, seq, HEAD_DIM), lambda b, kv, qi: (k0 + kv, b, 0)),
            pl.BlockSpec((None, seq, HEAD_DIM), lambda b, kv, qi: (v0 + kv, b, 0)),
        ],
        out_specs=pl.BlockSpec((_BQ, B_GROUP * HEAD_DIM), lambda b, kv, qi: (b * qblocks + qi, kv)),
        compiler_params=_params(("arbitrary", "arbitrary", "arbitrary")),
    )(slots, slots, slots)


_MIX_ROWS = 512


def _branch_kernel(oa_ref, ob_ref, ga_ref, gb_ref, wa_ref, wb_ref, out_ref):
    for rt in range(out_ref.shape[0] // SUB_ROWS):
        rows = slice(rt * SUB_ROWS, (rt + 1) * SUB_ROWS)
        y_a = jnp.dot(oa_ref[rows, :], wa_ref[...], preferred_element_type=F32)
        y_b = jnp.dot(ob_ref[rows, :], wb_ref[...], preferred_element_type=F32)
        out_ref[rows, :] = (ga_ref[rows, :].astype(F32) * y_a + gb_ref[rows, :].astype(F32) * y_b).astype(BF16)


def _branch_mix(o_a, o_b, gates, wa, wb):
    n_tok = o_b.shape[0]
    d_model = wa.shape[1]
    tm = _MIX_ROWS
    row = lambda i: (i, 0)
    return pl.pallas_call(
        _branch_kernel,
        out_shape=jax.ShapeDtypeStruct((n_tok, d_model), BF16),
        grid=(n_tok // tm,),
        in_specs=[pl.BlockSpec((tm, o_a.shape[1]), row),
                  pl.BlockSpec((tm, o_b.shape[1]), row),
                  pl.BlockSpec((tm, d_model), lambda i: (i, 0)),
                  pl.BlockSpec((tm, d_model), lambda i: (i, 1)),
                  pl.BlockSpec(wa.shape, lambda i: (0, 0)),
                  pl.BlockSpec(wb.shape, lambda i: (0, 0))],
        out_specs=pl.BlockSpec((tm, d_model), row),
        compiler_params=_params(("arbitrary",)),
    )(o_a, o_b, gates, gates, wa, wb)


def _layer_norm(z, g, b):
    mu = jnp.mean(z, axis=-1, keepdims=True)
    zc = z - mu
    var = jnp.mean(zc * zc, axis=-1, keepdims=True)
    return zc * lax.rsqrt(var + LN_EPS) * g + b


def _to_slabs(slab_ref, value, row0):
    rows, d = value.shape
    chunks = d // LANES
    for c in range(chunks):
        slab_ref[pl.ds(row0 * chunks + c, rows, stride=chunks), :] = value[:, c * LANES:(c + 1) * LANES]


def _from_slabs(slab_ref, rows, chunks, c):
    return slab_ref[pl.ds(c, rows, stride=chunks), :]


def _out_proj_kernel(m_ref, w_ref, x_ref, g_ref, b_ref, rcat_ref, rb_ref, hs_ref, lg_ref, *, alpha):
    for rt in range(m_ref.shape[0] // SUB_ROWS):
        rows = slice(rt * SUB_ROWS, (rt + 1) * SUB_ROWS)
        mix = jnp.dot(m_ref[rows, :], w_ref[...], preferred_element_type=F32)
        h = _layer_norm(alpha * x_ref[rows, :] + mix, g_ref[...], b_ref[...])
        _to_slabs(hs_ref, h, rt * SUB_ROWS)
        hi = h.astype(BF16)
        lo = (h - hi.astype(F32)).astype(BF16)
        both = jnp.dot(hi, rcat_ref[...], preferred_element_type=F32)
        lg = both[:, :LANES] + both[:, LANES:] + jnp.dot(lo, rcat_ref[:, :LANES], preferred_element_type=F32)
        lg_ref[rows, :] = lg + rb_ref[...]


def _out_proj(merged, w_out, x2, g, b, r_cat, r_b, alpha):
    n_tok, d_model = x2.shape
    tm = _MIX_ROWS
    chunks = d_model // LANES
    row = lambda i: (i, 0)
    fix = lambda i: (0, 0)
    return pl.pallas_call(
        functools.partial(_out_proj_kernel, alpha=alpha),
        out_shape=(jax.ShapeDtypeStruct((n_tok * chunks, LANES), F32),
                   jax.ShapeDtypeStruct((n_tok, LANES), F32)),
        grid=(n_tok // tm,),
        in_specs=[pl.BlockSpec((tm, d_model), row),
                  pl.BlockSpec(w_out.shape, fix),
                  pl.BlockSpec((tm, d_model), row),
                  pl.BlockSpec((1, d_model), fix),
                  pl.BlockSpec((1, d_model), fix),
                  pl.BlockSpec(r_cat.shape, fix),
                  pl.BlockSpec((1, LANES), fix)],
        out_specs=(pl.BlockSpec((tm * chunks, LANES), row), pl.BlockSpec((tm, LANES), row)),
        compiler_params=_params(("arbitrary",)),
    )(merged, w_out, x2, g, b, r_cat, r_b)


def _moe_kernel(item_expert, item_row0, item_blocks, item_real, src_slab, dst_slab,
                h_hbm, wg_ref, wu_ref, wd_ref, y_hbm,
                stage, xbf, acc, obuf, gsem, ssem):
    it = pl.program_id(0)
    f = pl.program_id(1)
    n_items = pl.num_programs(0)
    n_ff = pl.num_programs(1)
    n_blocks = item_blocks[it]
    d_model = xbf.shape[1]
    chunks = d_model // LANES

    def slab(ref, first):
        return ref.at[pl.ds(pl.multiple_of(first, chunks), chunks)]

    def gather_copy(base, r):
        return pltpu.make_async_copy(slab(h_hbm, src_slab[base + r]), slab(stage, r * chunks), gsem.at[0])

    def scatter_copy(base, r):
        return pltpu.make_async_copy(slab(obuf, r * chunks), slab(y_hbm, dst_slab[base + r]), ssem.at[0])

    def for_rows(n_rows, fn):
        def group(t, c):
            for u in range(DMA_UNROLL):
                fn(t * DMA_UNROLL + u)
            return c
        n_groups = n_rows // DMA_UNROLL
        lax.fori_loop(0, n_groups, group, 0)

        def single(r, c):
            fn(r)
            return c
        lax.fori_loop(n_groups * DMA_UNROLL, n_rows, single, 0)

    def gather(item, op):
        base = item_row0[item]
        for_rows(item_blocks[item] * MOE_BLOCK, lambda r: op(gather_copy(base, r)))

    def scatter(item, op):
        base = item_row0[item]
        for_rows(item_real[item], lambda r: op(scatter_copy(base, r)))

    start = lambda cp: cp.start()
    wait = lambda cp: cp.wait()

    @pl.when((it == 0) & (f == 0))
    def _first_gather():
        gather(it, start)

    @pl.when(f == 0)
    def _gathered():
        gather(it, wait)

        def cast_rows(b, carry):
            r0 = pl.multiple_of(b * MOE_BLOCK, MOE_BLOCK)
            for c in range(chunks):
                piece = stage[pl.ds(r0 * chunks + c, MOE_BLOCK, stride=chunks), :]
                xbf[pl.ds(r0, MOE_BLOCK), c * LANES:(c + 1) * LANES] = piece.astype(BF16)
            return carry
        lax.fori_loop(0, n_blocks, cast_rows, 0)

        @pl.when(it + 1 < n_items)
        def _prefetch_next():
            gather(it + 1, start)

    def chunk(r0, rows, mode):
        xb = xbf[pl.ds(r0, rows), :]
        a = jnp.dot(xb, wg_ref[...].astype(BF16), preferred_element_type=F32)
        u = jnp.dot(xb, wu_ref[...].astype(BF16), preferred_element_type=F32)
        hid = (a / (1.0 + jnp.exp(-a)) * u).astype(BF16)
        y = jnp.dot(hid, wd_ref[...].astype(BF16), preferred_element_type=F32)
        if mode == "first":
            acc[pl.ds(r0, rows), :] = y
        elif mode == "middle":
            acc[pl.ds(r0, rows), :] += y
        else:
            if mode == "last":
                y = acc[pl.ds(r0, rows), :] + y
            for c in range(chunks):
                obuf[pl.ds(r0 * chunks + c, rows, stride=chunks), :] = y[:, c * LANES:(c + 1) * LANES]

    def run_item(mode):
        for blocks in range(1, ITEM_BLOCKS + 1):
            @pl.when(n_blocks == blocks)
            def _rows(blocks=blocks):
                chunk(0, blocks * MOE_BLOCK, mode)

    last = n_ff - 1

    @pl.when((n_blocks > 0) & (f == last))
    def _final():
        @pl.when(it > 0)
        def _drain_prev():
            scatter(it - 1, wait)
        run_item("only" if n_ff == 1 else "last")
        scatter(it, start)

    if n_ff > 1:
        @pl.when((n_blocks > 0) & (f == 0))
        def _first():
            run_item("first")

        @pl.when((n_blocks > 0) & (f > 0) & (f < last))
        def _middle():
            run_item("middle")

    @pl.when((f == last) & (it == n_items - 1))
    def _drain_last():
        scatter(jnp.where(n_blocks > 0, it, _last_live(item_blocks, n_items)), wait)


def _last_live(item_blocks, n_items):
    def body(i, best):
        return jnp.where(item_blocks[i] > 0, i, best)
    return lax.fori_loop(0, n_items, body, 0)


def _moe(h_slabs, w_gate, w_up, w_down, item_expert, item_row0, item_blocks, item_real, src_slab, dst_slab):
    d_model = w_gate.shape[1]
    chunks = d_model // LANES
    n_tok = h_slabs.shape[0] // chunks
    ff = w_gate.shape[2]
    n_ff = ff // FF_TILE
    n_items = item_expert.shape[0]

    def ff_idx(it, f, blocks):
        return jnp.where(blocks[it] > 0, f, n_ff - 1)

    grid_spec = pltpu.PrefetchScalarGridSpec(
        num_scalar_prefetch=6,
        grid=(n_items, n_ff),
        in_specs=[
            pl.BlockSpec(memory_space=pl.ANY),
            pl.BlockSpec((None, d_model, FF_TILE), lambda it, f, ie, ir, ib, nr, st, dr: (ie[it], 0, ff_idx(it, f, ib))),
            pl.BlockSpec((None, d_model, FF_TILE), lambda it, f, ie, ir, ib, nr, st, dr: (ie[it], 0, ff_idx(it, f, ib))),
            pl.BlockSpec((None, FF_TILE, d_model), lambda it, f, ie, ir, ib, nr, st, dr: (ie[it], ff_idx(it, f, ib), 0)),
        ],
        out_specs=pl.BlockSpec(memory_space=pl.ANY),
        scratch_shapes=[
            pltpu.VMEM((ITEM_ROWS * chunks, LANES), F32),
            pltpu.VMEM((ITEM_ROWS, d_model), BF16),
            pltpu.VMEM((ITEM_ROWS, d_model), F32),
            pltpu.VMEM((ITEM_ROWS * chunks, LANES), F32),
            pltpu.SemaphoreType.DMA((1,)),
            pltpu.SemaphoreType.DMA((1,)),
        ],
    )
    return pl.pallas_call(
        _moe_kernel,
        out_shape=jax.ShapeDtypeStruct((n_tok * TOP_K * chunks, LANES), F32),
        grid_spec=grid_spec,
        compiler_params=_params(("arbitrary", "arbitrary")),
    )(item_expert, item_row0, item_blocks, item_real, src_slab, dst_slab, h_slabs, w_gate, w_up, w_down)


_COMBINE_ROWS = 512


def _combine_kernel(y0_ref, y1_ref, wt_ref, hs_ref, g_ref, b_ref, out_ref, *, alpha):
    rows, d_model = out_ref.shape
    chunks = d_model // LANES
    wt = wt_ref[...]
    w0, w1 = wt[:, 0:1], wt[:, 1:2]
    for c in range(chunks):
        cols = slice(c * LANES, (c + 1) * LANES)
        ffn = w0 * _from_slabs(y0_ref, rows, chunks, c) + w1 * _from_slabs(y1_ref, rows, chunks, c)
        out_ref[:, cols] = alpha * _from_slabs(hs_ref, rows, chunks, c) + ffn
    out_ref[...] = _layer_norm(out_ref[...], g_ref[...], b_ref[...])


def _combine(y_slabs, weights, h_slabs, g, b, alpha):
    d_model = g.shape[1]
    n_tok = h_slabs.shape[0] * LANES // d_model
    tm = _COMBINE_ROWS
    chunks = d_model // LANES
    second = n_tok // tm
    row = lambda i: (i, 0)
    fix = lambda i: (0, 0)
    return pl.pallas_call(
        functools.partial(_combine_kernel, alpha=alpha),
        out_shape=jax.ShapeDtypeStruct((n_tok, d_model), F32),
        grid=(n_tok // tm,),
        in_specs=[pl.BlockSpec((tm * chunks, LANES), row),
                  pl.BlockSpec((tm * chunks, LANES), lambda i: (second + i, 0)),
                  pl.BlockSpec((tm, TOP_K), row),
                  pl.BlockSpec((tm * chunks, LANES), row),
                  pl.BlockSpec((1, d_model), fix),
                  pl.BlockSpec((1, d_model), fix)],
        out_specs=pl.BlockSpec((tm, d_model), row),
        compiler_params=_params(("arbitrary",)),
    )(y_slabs, y_slabs, weights, h_slabs, g, b)


def _rope_tables(seq):
    f32 = np.float32
    half = HEAD_DIM // 2
    inv1 = np.power(f32(ROPE_THETA), -(np.arange(half, dtype=f32) / f32(half)))
    ang1 = np.arange(seq, dtype=f32)[:, None] * inv1[None, :]
    t = np.arange(seq)
    n_axis = HEAD_DIM // 4
    inv2 = np.power(f32(ROPE_THETA), -(np.arange(n_axis, dtype=f32) / f32(n_axis)))
    ang2 = np.concatenate([(t // GRID_W).astype(f32)[:, None] * inv2[None, :],
                           (t % GRID_W).astype(f32)[:, None] * inv2[None, :]], axis=-1)
    q_scale = f32(HEAD_DIM ** -0.5 * LOG2E)
    out = []
    for ang in (ang1, ang2):
        cos = np.concatenate([np.cos(ang), np.cos(ang)], axis=-1)
        sin = np.concatenate([-np.sin(ang), np.sin(ang)], axis=-1)
        out += [cos * q_scale, sin * q_scale, cos, sin]
    return jnp.asarray(np.stack(out, axis=0).astype(f32))


def _route(logits, n_tok, chunks):
    g_logits = logits[:, :N_GROUPS]
    g_prob = jax.nn.softmax(g_logits, axis=-1)
    g_idx = jnp.argmax(g_logits, axis=-1).astype(I32)
    g_gate = jnp.max(g_prob, axis=-1)
    e_logits = logits[:, N_GROUPS:N_GROUPS + EXPERTS_PER_GROUP]
    for g in range(1, N_GROUPS):
        lo = N_GROUPS + g * EXPERTS_PER_GROUP
        e_logits = jnp.where(g_idx[:, None] == g, logits[:, lo:lo + EXPERTS_PER_GROUP], e_logits)
    e_prob = jax.nn.softmax(e_logits, axis=-1)
    i1 = jnp.argmax(e_prob, axis=-1).astype(I32)
    p1 = jnp.max(e_prob, axis=-1)
    rest = jnp.where(jnp.arange(EXPERTS_PER_GROUP, dtype=I32)[None, :] == i1[:, None], -1.0, e_prob)
    i2 = jnp.argmax(rest, axis=-1).astype(I32)
    p2 = jnp.max(rest, axis=-1)
    top_p = jnp.stack([p1, p2], axis=-1)
    top_p = top_p / jnp.sum(top_p, axis=-1, keepdims=True)
    weights = g_gate[:, None] * top_p
    expert = g_idx[:, None] * EXPERTS_PER_GROUP + jnp.stack([i1, i2], axis=-1)

    n_slot = n_tok * TOP_K
    e_flat = expert.reshape(n_slot).astype(I32)
    order = jnp.argsort(e_flat).astype(I32)
    experts = jnp.arange(N_EXPERTS, dtype=I32)
    counts = jnp.sum(e_flat[:, None] == experts[None, :], axis=0, dtype=I32)
    padded = ((counts + MOE_BLOCK - 1) // MOE_BLOCK) * MOE_BLOCK
    p_ends = jnp.cumsum(padded)
    p_starts = p_ends - padded
    buf_len = n_slot + N_EXPERTS * MOE_BLOCK

    def pick(one_hot, table):
        return jnp.sum(jnp.where(one_hot, table[None, :], 0), axis=1, dtype=I32)

    pos = jnp.arange(buf_len, dtype=I32)[:, None]
    ended = p_ends[None, :] <= pos
    inside = jnp.logical_and(p_starts[None, :] <= pos, pos < p_ends[None, :])
    idx = pos[:, 0] - jnp.max(jnp.where(ended, p_ends[None, :], 0), axis=1)
    sorted_at = jnp.clip(pick(ended, counts) + idx, 0, n_slot - 1)
    slot_at = jnp.where(idx < pick(inside, counts), order[sorted_at], 0)
    tok_at, k_at = slot_at // TOP_K, slot_at % TOP_K
    src_slab = tok_at * chunks
    dst_slab = (k_at * n_tok + tok_at) * chunks

    n_items = N_EXPERTS + n_slot // ITEM_ROWS
    per_expert = (padded + ITEM_ROWS - 1) // ITEM_ROWS
    item_ends = jnp.cumsum(per_expert)
    total = item_ends[-1]
    ids = jnp.arange(n_items, dtype=I32)
    live = ids < total
    item = jnp.minimum(ids, jnp.maximum(total - 1, 0))[:, None]
    owner = jnp.logical_and((item_ends - per_expert)[None, :] <= item, item < item_ends[None, :])
    e_of = pick(owner, experts)
    chunk = item[:, 0] - pick(owner, item_ends - per_expert)
    item_row0 = jnp.where(live, pick(owner, p_starts) + chunk * ITEM_ROWS, 0).astype(I32)
    item_blocks = jnp.where(live, jnp.clip(pick(owner, padded) // MOE_BLOCK - chunk * ITEM_BLOCKS, 0, ITEM_BLOCKS), 0)
    item_real = jnp.where(live, jnp.clip(pick(owner, counts) - chunk * ITEM_ROWS, 0, ITEM_ROWS), 0)
    return weights, e_of, item_row0, item_blocks.astype(I32), item_real.astype(I32), src_slab, dst_slab


def kernel(x, w_in, q_norm_g, k_norm_g, w_branch_a, w_branch_b, w_out, ln1_g, ln1_b, w_group, b_group, w_router,
           b_router, w_gate, w_up, w_down, ln2_g, ln2_b):
    batch, seq, d_model = x.shape
    depth = w_in.shape[0]
    n_tok = batch * seq
    dn_alpha = (2 * depth) ** 0.25
    tabs = _rope_tables(seq)
    h = x.reshape(n_tok, d_model)
    for layer in range(depth):
        xb = h.astype(BF16)
        groups = [_proj_a(xb, w_in[layer], tabs, g, batch, seq) for g in range(N_DIL)]
        slots_b = _proj_b(xb, w_in[layer], tabs, q_norm_g[layer][None, :], k_norm_g[layer][None, :], seq)
        gates = _proj_gate(xb, w_in[layer])
        o_a = _attn_a(groups, batch, seq)
        score_bound = (jnp.max(jnp.abs(q_norm_g[layer])) * jnp.max(jnp.abs(k_norm_g[layer]))
                       * (HEAD_DIM ** 0.5 * LOG2E * BF16_SLACK))
        o_b = lax.cond(score_bound <= SCORE_BOUND,
                       lambda s: _attn_b(s, batch, seq, True), lambda s: _attn_b(s, batch, seq, False), slots_b)
        merged = _branch_mix(o_a, o_b, gates, w_branch_a[layer].astype(BF16), w_branch_b[layer].astype(BF16))
        w_r = jnp.concatenate([w_group[layer], w_router[layer]], axis=1)
        w_r = jnp.pad(w_r, ((0, 0), (0, LANES - w_r.shape[1])))
        r_hi = w_r.astype(BF16)
        r_lo = (w_r - r_hi.astype(F32)).astype(BF16)
        r_b = jnp.pad(jnp.concatenate([b_group[layer], b_router[layer]]), (0, LANES - N_GROUPS - N_EXPERTS))[None, :]
        h1_slabs, logits = _out_proj(merged, w_out[layer].astype(BF16), h, ln1_g[layer][None, :],
                                         ln1_b[layer][None, :], jnp.concatenate([r_hi, r_lo], axis=1), r_b, dn_alpha)
        weights, item_expert, item_row0, item_blocks, item_real, src_slab, dst_slab = _route(
            logits, n_tok, d_model // LANES)
        y_slabs = _moe(h1_slabs, w_gate[layer], w_up[layer], w_down[layer], item_expert, item_row0, item_blocks,
                       item_real, src_slab, dst_slab)
        h = _combine(y_slabs, weights, h1_slabs, ln2_g[layer][None, :], ln2_b[layer][None, :], dn_alpha)
    return h.reshape(batch, seq, d_model)
```

```python
import functools
import math

import jax
import jax.numpy as jnp
import numpy as np
from jax import lax
from jax.experimental import pallas as pl
from jax.experimental.pallas import tpu as pltpu

F32 = jnp.float32
BF16 = jnp.bfloat16
I32 = jnp.int32

HEAD_DIM = 128
ROPE_THETA = 10000.0
GRID_W = 64
DIL_PATTERNS = ((128, 1), (512, 4), (2048, 16))
N_DIL = len(DIL_PATTERNS)
A_HEADS = 8
BAND_HALF = 64
B_Q_HEADS = 16
B_KV_HEADS = 4
B_GROUP = B_Q_HEADS // B_KV_HEADS
N_GROUPS = 4
EXPERTS_PER_GROUP = 8
N_EXPERTS = N_GROUPS * EXPERTS_PER_GROUP
TOP_K = 2
MOE_BLOCK = 128
LN_EPS = 1e-5
RMS_EPS = 1e-6
NEG_BIG = -1e30
LOG2E = math.log2(math.e)

LANES = 128
COL_TILE = 8 * HEAD_DIM
ROW_TILE = 1024
SUB_ROWS = 256
ITEM_ROWS = 768
ITEM_BLOCKS = ITEM_ROWS // MOE_BLOCK
FF_TILE = 512
DMA_UNROLL = 8
VMEM_LIMIT = 56 * 1024 * 1024


def _params(sem, vmem=VMEM_LIMIT):
    return pltpu.CompilerParams(dimension_semantics=sem, vmem_limit_bytes=vmem)


_T_QA, _T_KA, _T_VA, _T_QB, _T_KVB, _T_GATE = 0, 3, 6, 9, 11, 12
_TAB_1D, _TAB_AXIAL = 0, 1
_RQ, _RK = 0, 2


def _cast_weights_once(w_ref, wbf_ref):
    @pl.when(pl.program_id(1) == 0)
    def _cast():
        wbf_ref[...] = w_ref[...].astype(BF16)


def _for_sub_tiles(x_ref, wbf_ref, emit):
    for rt in range(x_ref.shape[0] // SUB_ROWS):
        rows = slice(rt * SUB_ROWS, (rt + 1) * SUB_ROWS)
        emit(rt, rows, jnp.dot(x_ref[rows, :], wbf_ref[...], preferred_element_type=F32))


def _head(acc, h):
    return acc[:, h * HEAD_DIM:(h + 1) * HEAD_DIM]


def _rope(a, tab_ref, t, rows):
    return a * tab_ref[t, rows, :] + pltpu.roll(a, HEAD_DIM // 2, 1) * tab_ref[t + 1, rows, :]


def _rms(a, g_ref):
    ms = jnp.mean(a * a, axis=-1, keepdims=True)
    return a * lax.rsqrt(ms + RMS_EPS) * g_ref[...]


def _proj_a_kernel(x_ref, w_ref, tab_ref, out_ref, wbf_ref, *, dil):
    j = pl.program_id(0)
    _cast_weights_once(w_ref, wbf_ref)
    per_res = SUB_ROWS // dil

    def emit_with(fn):
        def emit(rt, rows, acc):
            if dil == 1:
                for h in range(A_HEADS):
                    out_ref[h, 0, rows, :] = fn(_head(acc, h), rows).astype(BF16)
                return
            for h in range(A_HEADS):
                by_res = fn(_head(acc, h), rows).reshape(per_res, dil, HEAD_DIM).swapaxes(0, 1)
                for r in range(dil):
                    out_ref[h, r, rt * per_res:(rt + 1) * per_res, :] = by_res[r].astype(BF16)
        _for_sub_tiles(x_ref, wbf_ref, emit)

    @pl.when(j == 0)
    def _q():
        emit_with(lambda a, rows: _rope(a, tab_ref, _RQ, rows))

    @pl.when(j == 1)
    def _k():
        emit_with(lambda a, rows: _rope(a, tab_ref, _RK, rows))

    @pl.when(j == 2)
    def _v():
        emit_with(lambda a, rows: a)


def _proj_b_kernel(x_ref, w_ref, tab_ref, gq_ref, gk_ref, out_ref, wbf_ref, *, q_tiles):
    j = pl.program_id(0)
    _cast_weights_once(w_ref, wbf_ref)
    heads = COL_TILE // HEAD_DIM

    @pl.when(j < q_tiles)
    def _qb():
        def emit(rt, rows, acc):
            for h in range(heads):
                out_ref[h, rows, :] = _rope(_rms(_head(acc, h), gq_ref), tab_ref, _RQ, rows).astype(BF16)
        _for_sub_tiles(x_ref, wbf_ref, emit)

    @pl.when(j == q_tiles)
    def _kvb():
        def emit(rt, rows, acc):
            for h in range(heads):
                if h < B_KV_HEADS:
                    out_ref[h, rows, :] = _rope(_rms(_head(acc, h), gk_ref), tab_ref, _RK, rows).astype(BF16)
                else:
                    out_ref[h, rows, :] = _head(acc, h).astype(BF16)
        _for_sub_tiles(x_ref, wbf_ref, emit)


def _proj_gate_kernel(x_ref, w_ref, out_ref, wbf_ref):
    _cast_weights_once(w_ref, wbf_ref)

    def emit(rt, rows, acc):
        out_ref[rows, :] = (1.0 / (1.0 + jnp.exp(-acc))).astype(BF16)
    _for_sub_tiles(x_ref, wbf_ref, emit)


def _proj_specs(d_model, seq, first_tile, tile_step, table_half):
    seq_blocks = seq // ROW_TILE
    return [
        pl.BlockSpec((ROW_TILE, d_model), lambda j, i: (i, 0)),
        pl.BlockSpec((d_model, COL_TILE), lambda j, i: (0, first_tile + tile_step * j)),
        pl.BlockSpec((4, ROW_TILE, HEAD_DIM), lambda j, i: (table_half, i % seq_blocks, 0)),
    ]


def _proj_a(xb, w_in, tabs, group, batch, seq):
    n_tok, d_model = xb.shape
    dil = DIL_PATTERNS[group][1]
    sub_len = seq // dil
    seq_blocks = seq // ROW_TILE
    return pl.pallas_call(
        functools.partial(_proj_a_kernel, dil=dil),
        out_shape=jax.ShapeDtypeStruct((3 * A_HEADS, batch, dil, sub_len, HEAD_DIM), BF16),
        grid=(3, n_tok // ROW_TILE),
        in_specs=_proj_specs(d_model, seq, group, N_DIL, _TAB_1D),
        out_specs=pl.BlockSpec((A_HEADS, None, dil, ROW_TILE // dil, HEAD_DIM),
                               lambda j, i: (j, i // seq_blocks, 0, i % seq_blocks, 0)),
        scratch_shapes=[pltpu.VMEM((d_model, COL_TILE), BF16)],
        compiler_params=_params(("arbitrary", "arbitrary")),
    )(xb, w_in, tabs)


def _proj_b(xb, w_in, tabs, gq, gk, seq):
    n_tok, d_model = xb.shape
    q_tiles = B_Q_HEADS * HEAD_DIM // COL_TILE
    heads = COL_TILE // HEAD_DIM
    fix = lambda j, i: (0, 0)
    return pl.pallas_call(
        functools.partial(_proj_b_kernel, q_tiles=q_tiles),
        out_shape=jax.ShapeDtypeStruct(((q_tiles + 1) * heads, n_tok, HEAD_DIM), BF16),
        grid=(q_tiles + 1, n_tok // ROW_TILE),
        in_specs=_proj_specs(d_model, seq, _T_QB, 1, _TAB_AXIAL) + [pl.BlockSpec((1, HEAD_DIM), fix), pl.BlockSpec((1, HEAD_DIM), fix)],
        out_specs=pl.BlockSpec((heads, ROW_TILE, HEAD_DIM), lambda j, i: (j, i, 0)),
        scratch_shapes=[pltpu.VMEM((d_model, COL_TILE), BF16)],
        compiler_params=_params(("arbitrary", "arbitrary")),
    )(xb, w_in, tabs, gq, gk)


def _proj_gate(xb, w_in):
    n_tok, d_model = xb.shape
    n_ct = w_in.shape[1] // COL_TILE - _T_GATE
    return pl.pallas_call(
        _proj_gate_kernel,
        out_shape=jax.ShapeDtypeStruct((n_tok, n_ct * COL_TILE), BF16),
        grid=(n_ct, n_tok // ROW_TILE),
        in_specs=[pl.BlockSpec((ROW_TILE, d_model), lambda j, i: (i, 0)),
                  pl.BlockSpec((d_model, COL_TILE), lambda j, i: (0, _T_GATE + j))],
        out_specs=pl.BlockSpec((ROW_TILE, COL_TILE), lambda j, i: (i, j)),
        scratch_shapes=[pltpu.VMEM((d_model, COL_TILE), BF16)],
        compiler_params=_params(("arbitrary", "arbitrary")),
    )(xb, w_in)


_SUB = 128
_WIN = _SUB + 2 * BAND_HALF
_MERGE_ROWS = 256
_A_UNROLL = 16


def _attn_a_group(g, dil, seq, q_ref, k_ref, v_ref, og, lg, bias):
    sub_len = seq // dil
    per_seq = sub_len // _SUB
    shift = per_seq.bit_length() - 1
    ones = jnp.ones((_WIN, HEAD_DIM), BF16)

    def body(i, carry):
        r = lax.shift_right_logical(i, shift)
        p0 = pl.multiple_of((i & (per_seq - 1)) * _SUB, _SUB)
        start = pl.multiple_of(jnp.clip(p0 - BAND_HALF, 0, sub_len - _WIN), BAND_HALF)
        q = q_ref[r, pl.ds(p0, _SUB), :]
        k = k_ref[r, pl.ds(start, _WIN), :]
        v = v_ref[r, pl.ds(start, _WIN), :]
        s = lax.dot_general(q, k, (((1,), (1,)), ((), ())), preferred_element_type=F32)
        s = s + bias[lax.shift_right_logical(p0 - start, BAND_HALF.bit_length() - 1)]
        m = jnp.max(s, axis=1, keepdims=True)
        p = jnp.exp2(s - m).astype(BF16)
        both = jnp.dot(p, jnp.concatenate([v, ones], axis=1), preferred_element_type=F32)
        l = both[:, HEAD_DIM:]
        o = both[:, :HEAD_DIM] / l
        lse = m + jnp.log2(l)
        if dil == 1:
            rows = pl.ds(p0, _SUB)
        else:
            rows = pl.ds(p0 * dil + r, _SUB, stride=dil)
        og[g, rows, :] = o
        lg[g, rows, :] = lse
        return carry

    lax.fori_loop(0, seq // _SUB, body, 0, unroll=_A_UNROLL)


def _attn_a_kernel(*refs, seq):
    qkv, o_ref = refs[:3 * N_DIL], refs[3 * N_DIL]
    og, lg, bias = refs[3 * N_DIL + 1:]
    row = lax.broadcasted_iota(I32, (_SUB, _WIN), 0)
    col = lax.broadcasted_iota(I32, (_SUB, _WIN), 1)
    for case in range(3):
        bias[case] = jnp.where(jnp.abs(col - row - case * BAND_HALF) <= BAND_HALF, 0.0, NEG_BIG)
    for g, (_, dil) in enumerate(DIL_PATTERNS):
        _attn_a_group(g, dil, seq, qkv[3 * g], qkv[3 * g + 1], qkv[3 * g + 2], og, lg, bias)

    def merge(c, carry):
        rows = pl.ds(pl.multiple_of(c * _MERGE_ROWS, _MERGE_ROWS), _MERGE_ROWS)
        ls = [lg[g, rows, :] for g in range(N_DIL)]
        mx = functools.reduce(jnp.maximum, ls)
        ws = [jnp.exp2(l - mx) for l in ls]
        num = functools.reduce(lambda a, b: a + b, [w * og[g, rows, :] for g, w in enumerate(ws)])
        o_ref[rows, :] = (num / functools.reduce(lambda a, b: a + b, ws)).astype(o_ref.dtype)
        return carry

    lax.fori_loop(0, seq // _MERGE_ROWS, merge, 0)


def _attn_a(groups, batch, seq):
    in_specs, operands = [], []
    for g, (_, dil) in enumerate(DIL_PATTERNS):
        for kind in range(3):
            first = kind * A_HEADS
            in_specs.append(pl.BlockSpec((None, None, dil, seq // dil, HEAD_DIM),
                                         lambda b, h, first=first: (first + h, b, 0, 0, 0)))
            operands.append(groups[g])
    return pl.pallas_call(
        functools.partial(_attn_a_kernel, seq=seq),
        out_shape=jax.ShapeDtypeStruct((batch * seq, A_HEADS * HEAD_DIM), BF16),
        grid=(batch, A_HEADS),
        in_specs=in_specs,
        out_specs=pl.BlockSpec((seq, HEAD_DIM), lambda b, h: (b, h)),
        scratch_shapes=[pltpu.VMEM((N_DIL, seq, HEAD_DIM), F32), pltpu.VMEM((N_DIL, seq, HEAD_DIM), F32),
                        pltpu.VMEM((3, _SUB, _WIN), F32)],
        compiler_params=_params(("arbitrary", "arbitrary")),
    )(*operands)


_BQ = 512
_BK = 512
SCORE_BOUND = 64.0
BF16_SLACK = 1.02


def _attn_b_kernel(q_ref, k_ref, v_ref, o_ref, *, seq):
    rows = B_GROUP * _BQ
    q = q_ref[...].reshape(rows, HEAD_DIM)

    def body(c, carry):
        m, l, acc = carry
        c0 = pl.multiple_of(c * _BK, _BK)
        k = k_ref[pl.ds(c0, _BK), :]
        v = v_ref[pl.ds(c0, _BK), :]
        s = lax.dot_general(q, k, (((1,), (1,)), ((), ())), preferred_element_type=F32)
        m_new = jnp.maximum(m, jnp.max(s, axis=1, keepdims=True))
        alpha = jnp.exp2(m - m_new)
        p = jnp.exp2(s - m_new)
        l = alpha * l + jnp.sum(p, axis=1, keepdims=True)
        acc = alpha * acc + jnp.dot(p.astype(BF16), v, preferred_element_type=F32)
        return m_new, l, acc

    init = (jnp.full((rows, 1), NEG_BIG, F32), jnp.zeros((rows, 1), F32), jnp.zeros((rows, HEAD_DIM), F32))
    _, l, acc = lax.fori_loop(0, seq // _BK, body, init)
    o = acc / l
    for g in range(B_GROUP):
        o_ref[:, g * HEAD_DIM:(g + 1) * HEAD_DIM] = o[g * _BQ:(g + 1) * _BQ].astype(o_ref.dtype)


def _attn_b_bounded_kernel(q_ref, k_ref, v_ref, o_ref, *, seq):
    rows = B_GROUP * _BQ
    q = q_ref[...].reshape(rows, HEAD_DIM)
    ones = jnp.ones((_BK, HEAD_DIM), BF16)
    acc = jnp.zeros((rows, 2 * HEAD_DIM), F32)
    for c in range(seq // _BK):
        k = k_ref[c * _BK:(c + 1) * _BK, :]
        v = v_ref[c * _BK:(c + 1) * _BK, :]
        s = lax.dot_general(q, k, (((1,), (1,)), ((), ())), preferred_element_type=F32)
        p = jnp.exp2(s).astype(BF16)
        acc = acc + jnp.dot(p, jnp.concatenate([v, ones], axis=1), preferred_element_type=F32)
    o = acc[:, :HEAD_DIM] / acc[:, HEAD_DIM:HEAD_DIM + 1]
    for g in range(B_GROUP):
        o_ref[:, g * HEAD_DIM:(g + 1) * HEAD_DIM] = o[g * _BQ:(g + 1) * _BQ].astype(o_ref.dtype)


def _attn_b(slots, batch, seq, bounded):
    n_tok = batch * seq
    k0 = B_Q_HEADS
    v0 = k0 + B_KV_HEADS
    qblocks = seq // _BQ
    return pl.pallas_call(
        functools.partial(_attn_b_bounded_kernel if bounded else _attn_b_kernel, seq=seq),
        out_shape=jax.ShapeDtypeStruct((n_tok, B_Q_HEADS * HEAD_DIM), BF16),
        grid=(batch, B_KV_HEADS, qblocks),
        in_specs=[
            pl.BlockSpec((B_GROUP, _BQ, HEAD_DIM), lambda b, kv, qi: (kv, b * qblocks + qi, 0)),
            pl.BlockSpec((None, seq, HEAD_DIM), lambda b, kv, qi: (k0 + kv, b, 0)),
            pl.BlockSpec((None, seq, HEAD_DIM), lambda b, kv, qi: (v0 + kv, b, 0)),
        ],
        out_specs=pl.BlockSpec((_BQ, B_GROUP * HEAD_DIM), lambda b, kv, qi: (b * qblocks + qi, kv)),
        compiler_params=_params(("arbitrary", "arbitrary", "arbitrary")),
    )(slots, slots, slots)


_MIX_ROWS = 512


def _branch_kernel(oa_ref, ob_ref, ga_ref, gb_ref, wa_ref, wb_ref, out_ref):
    for rt in range(out_ref.shape[0] // SUB_ROWS):
        rows = slice(rt * SUB_ROWS, (rt + 1) * SUB_ROWS)
        y_a = jnp.dot(oa_ref[rows, :], wa_ref[...], preferred_element_type=F32)
        y_b = jnp.dot(ob_ref[rows, :], wb_ref[...], preferred_element_type=F32)
        out_ref[rows, :] = (ga_ref[rows, :].astype(F32) * y_a + gb_ref[rows, :].astype(F32) * y_b).astype(BF16)


def _branch_mix(o_a, o_b, gates, wa, wb):
    n_tok = o_b.shape[0]
    d_model = wa.shape[1]
    tm = _MIX_ROWS
    row = lambda i: (i, 0)
    return pl.pallas_call(
        _branch_kernel,
        out_shape=jax.ShapeDtypeStruct((n_tok, d_model), BF16),
        grid=(n_tok // tm,),
        in_specs=[pl.BlockSpec((tm, o_a.shape[1]), row),
                  pl.BlockSpec((tm, o_b.shape[1]), row),
                  pl.BlockSpec((tm, d_model), lambda i: (i, 0)),
                  pl.BlockSpec((tm, d_model), lambda i: (i, 1)),
                  pl.BlockSpec(wa.shape, lambda i: (0, 0)),
                  pl.BlockSpec(wb.shape, lambda i: (0, 0))],
        out_specs=pl.BlockSpec((tm, d_model), row),
        compiler_params=_params(("arbitrary",)),
    )(o_a, o_b, gates, gates, wa, wb)


def _layer_norm(z, g, b):
    mu = jnp.mean(z, axis=-1, keepdims=True)
    zc = z - mu
    var = jnp.mean(zc * zc, axis=-1, keepdims=True)
    return zc * lax.rsqrt(var + LN_EPS) * g + b


def _to_slabs(slab_ref, value, row0):
    rows, d = value.shape
    chunks = d // LANES
    for c in range(chunks):
        slab_ref[pl.ds(row0 * chunks + c, rows, stride=chunks), :] = value[:, c * LANES:(c + 1) * LANES]


def _from_slabs(slab_ref, rows, chunks, c):
    return slab_ref[pl.ds(c, rows, stride=chunks), :]


def _out_proj_kernel(m_ref, w_ref, x_ref, g_ref, b_ref, rcat_ref, rb_ref, hs_ref, lg_ref, *, alpha):
    for rt in range(m_ref.shape[0] // SUB_ROWS):
        rows = slice(rt * SUB_ROWS, (rt + 1) * SUB_ROWS)
        mix = jnp.dot(m_ref[rows, :], w_ref[...], preferred_element_type=F32)
        h = _layer_norm(alpha * x_ref[rows, :] + mix, g_ref[...], b_ref[...])
        _to_slabs(hs_ref, h, rt * SUB_ROWS)
        hi = h.astype(BF16)
        lo = (h - hi.astype(F32)).astype(BF16)
        both = jnp.dot(hi, rcat_ref[...], preferred_element_type=F32)
        lg = both[:, :LANES] + both[:, LANES:] + jnp.dot(lo, rcat_ref[:, :LANES], preferred_element_type=F32)
        lg_ref[rows, :] = lg + rb_ref[...]


def _out_proj(merged, w_out, x2, g, b, r_cat, r_b, alpha):
    n_tok, d_model = x2.shape
    tm = _MIX_ROWS
    chunks = d_model // LANES
    row = lambda i: (i, 0)
    fix = lambda i: (0, 0)
    return pl.pallas_call(
        functools.partial(_out_proj_kernel, alpha=alpha),
        out_shape=(jax.ShapeDtypeStruct((n_tok * chunks, LANES), F32),
                   jax.ShapeDtypeStruct((n_tok, LANES), F32)),
        grid=(n_tok // tm,),
        in_specs=[pl.BlockSpec((tm, d_model), row),
                  pl.BlockSpec(w_out.shape, fix),
                  pl.BlockSpec((tm, d_model), row),
                  pl.BlockSpec((1, d_model), fix),
                  pl.BlockSpec((1, d_model), fix),
                  pl.BlockSpec(r_cat.shape, fix),
                  pl.BlockSpec((1, LANES), fix)],
        out_specs=(pl.BlockSpec((tm * chunks, LANES), row), pl.BlockSpec((tm, LANES), row)),
        compiler_params=_params(("arbitrary",)),
    )(merged, w_out, x2, g, b, r_cat, r_b)


_BIG_CHUNK = 512


def _moe_kernel(item_expert, item_row0, item_blocks, item_real, src_slab, dst_slab,
                h_hbm, wg_ref, wu_ref, wd_ref, y_hbm,
                stage, xbf, acc, obuf, gsem, ssem):
    it = pl.program_id(0)
    f = pl.program_id(1)
    n_items = pl.num_programs(0)
    n_ff = pl.num_programs(1)
    n_blocks = item_blocks[it]
    d_model = xbf.shape[1]
    chunks = d_model // LANES

    def slab(ref, first):
        return ref.at[pl.ds(pl.multiple_of(first, chunks), chunks)]

    def gather_copy(base, r):
        return pltpu.make_async_copy(slab(h_hbm, src_slab[base + r]), slab(stage, r * chunks), gsem.at[0])

    def scatter_copy(base, r):
        return pltpu.make_async_copy(slab(obuf, r * chunks), slab(y_hbm, dst_slab[base + r]), ssem.at[0])

    def for_rows(n_rows, fn):
        def group(t, c):
            for u in range(DMA_UNROLL):
                fn(t * DMA_UNROLL + u)
            return c
        n_groups = n_rows // DMA_UNROLL
        lax.fori_loop(0, n_groups, group, 0)

        def single(r, c):
            fn(r)
            return c
        lax.fori_loop(n_groups * DMA_UNROLL, n_rows, single, 0)

    def gather(item, op):
        base = item_row0[item]
        for_rows(item_blocks[item] * MOE_BLOCK, lambda r: op(gather_copy(base, r)))

    def scatter(item, op):
        base = item_row0[item]
        for_rows(item_real[item], lambda r: op(scatter_copy(base, r)))

    start = lambda cp: cp.start()
    wait = lambda cp: cp.wait()

    @pl.when((it == 0) & (f == 0))
    def _first_gather():
        gather(it, start)

    @pl.when(f == 0)
    def _gathered():
        gather(it, wait)

        def cast_rows(b, carry):
            r0 = pl.multiple_of(b * MOE_BLOCK, MOE_BLOCK)
            for c in range(chunks):
                piece = stage[pl.ds(r0 * chunks + c, MOE_BLOCK, stride=chunks), :]
                xbf[pl.ds(r0, MOE_BLOCK), c * LANES:(c + 1) * LANES] = piece.astype(BF16)
            return carry
        lax.fori_loop(0, n_blocks, cast_rows, 0)

        @pl.when(it + 1 < n_items)
        def _prefetch_next():
            gather(it + 1, start)

    def chunk(r0, rows, mode):
        xb = xbf[pl.ds(r0, rows), :]
        a = jnp.dot(xb, wg_ref[...].astype(BF16), preferred_element_type=F32)
        u = jnp.dot(xb, wu_ref[...].astype(BF16), preferred_element_type=F32)
        hid = (a / (1.0 + jnp.exp(-a)) * u).astype(BF16)
        y = jnp.dot(hid, wd_ref[...].astype(BF16), preferred_element_type=F32)
        if mode == "first":
            acc[pl.ds(r0, rows), :] = y
        elif mode == "middle":
            acc[pl.ds(r0, rows), :] += y
        else:
            if mode == "last":
                y = acc[pl.ds(r0, rows), :] + y
            for c in range(chunks):
                obuf[pl.ds(r0 * chunks + c, rows, stride=chunks), :] = y[:, c * LANES:(c + 1) * LANES]

    def run_item(mode):
        per_big = _BIG_CHUNK // MOE_BLOCK

        def big_chunk(t, c):
            chunk(pl.multiple_of(t * _BIG_CHUNK, _BIG_CHUNK), _BIG_CHUNK, mode)
            return c
        n_big = n_blocks // per_big
        lax.fori_loop(0, n_big, big_chunk, 0)

        def small_chunk(b, c):
            chunk(pl.multiple_of(b * MOE_BLOCK, MOE_BLOCK), MOE_BLOCK, mode)
            return c
        lax.fori_loop(n_big * per_big, n_blocks, small_chunk, 0)

    last = n_ff - 1

    @pl.when((n_blocks > 0) & (f == last))
    def _final():
        @pl.when(it > 0)
        def _drain_prev():
            scatter(it - 1, wait)
        run_item("only" if n_ff == 1 else "last")
        scatter(it, start)

    if n_ff > 1:
        @pl.when((n_blocks > 0) & (f == 0))
        def _first():
            run_item("first")

        @pl.when((n_blocks > 0) & (f > 0) & (f < last))
        def _middle():
            run_item("middle")

    @pl.when((f == last) & (it == n_items - 1))
    def _drain_last():
        scatter(jnp.where(n_blocks > 0, it, _last_live(item_blocks, n_items)), wait)


def _last_live(item_blocks, n_items):
    def body(i, best):
        return jnp.where(item_blocks[i] > 0, i, best)
    return lax.fori_loop(0, n_items, body, 0)


def _moe(h_slabs, w_gate, w_up, w_down, item_expert, item_row0, item_blocks, item_real, src_slab, dst_slab):
    d_model = w_gate.shape[1]
    chunks = d_model // LANES
    n_tok = h_slabs.shape[0] // chunks
    ff = w_gate.shape[2]
    n_ff = ff // FF_TILE
    n_items = item_expert.shape[0]

    def ff_idx(it, f, blocks):
        return jnp.where(blocks[it] > 0, f, n_ff - 1)

    grid_spec = pltpu.PrefetchScalarGridSpec(
        num_scalar_prefetch=6,
        grid=(n_items, n_ff),
        in_specs=[
            pl.BlockSpec(memory_space=pl.ANY),
            pl.BlockSpec((None, d_model, FF_TILE), lambda it, f, ie, ir, ib, nr, st, dr: (ie[it], 0, ff_idx(it, f, ib))),
            pl.BlockSpec((None, d_model, FF_TILE), lambda it, f, ie, ir, ib, nr, st, dr: (ie[it], 0, ff_idx(it, f, ib))),
            pl.BlockSpec((None, FF_TILE, d_model), lambda it, f, ie, ir, ib, nr, st, dr: (ie[it], ff_idx(it, f, ib), 0)),
        ],
        out_specs=pl.BlockSpec(memory_space=pl.ANY),
        scratch_shapes=[
            pltpu.VMEM((ITEM_ROWS * chunks, LANES), F32),
            pltpu.VMEM((ITEM_ROWS, d_model), BF16),
            pltpu.VMEM((ITEM_ROWS, d_model), F32),
            pltpu.VMEM((ITEM_ROWS * chunks, LANES), F32),
            pltpu.SemaphoreType.DMA((1,)),
            pltpu.SemaphoreType.DMA((1,)),
        ],
    )
    return pl.pallas_call(
        _moe_kernel,
        out_shape=jax.ShapeDtypeStruct((n_tok * TOP_K * chunks, LANES), F32),
        grid_spec=grid_spec,
        compiler_params=_params(("arbitrary", "arbitrary")),
    )(item_expert, item_row0, item_blocks, item_real, src_slab, dst_slab, h_slabs, w_gate, w_up, w_down)


_COMBINE_ROWS = 512


def _combine_kernel(y0_ref, y1_ref, wt_ref, hs_ref, g_ref, b_ref, out_ref, *, alpha):
    rows, d_model = out_ref.shape
    chunks = d_model // LANES
    wt = wt_ref[...]
    w0, w1 = wt[:, 0:1], wt[:, 1:2]
    for c in range(chunks):
        cols = slice(c * LANES, (c + 1) * LANES)
        ffn = w0 * _from_slabs(y0_ref, rows, chunks, c) + w1 * _from_slabs(y1_ref, rows, chunks, c)
        out_ref[:, cols] = alpha * _from_slabs(hs_ref, rows, chunks, c) + ffn
    out_ref[...] = _layer_norm(out_ref[...], g_ref[...], b_ref[...])


def _combine(y_slabs, weights, h_slabs, g, b, alpha):
    d_model = g.shape[1]
    n_tok = h_slabs.shape[0] * LANES // d_model
    tm = _COMBINE_ROWS
    chunks = d_model // LANES
    second = n_tok // tm
    row = lambda i: (i, 0)
    fix = lambda i: (0, 0)
    return pl.pallas_call(
        functools.partial(_combine_kernel, alpha=alpha),
        out_shape=jax.ShapeDtypeStruct((n_tok, d_model), F32),
        grid=(n_tok // tm,),
        in_specs=[pl.BlockSpec((tm * chunks, LANES), row),
                  pl.BlockSpec((tm * chunks, LANES), lambda i: (second + i, 0)),
                  pl.BlockSpec((tm, TOP_K), row),
                  pl.BlockSpec((tm * chunks, LANES), row),
                  pl.BlockSpec((1, d_model), fix),
                  pl.BlockSpec((1, d_model), fix)],
        out_specs=pl.BlockSpec((tm, d_model), row),
        compiler_params=_params(("arbitrary",)),
    )(y_slabs, y_slabs, weights, h_slabs, g, b)


def _rope_tables(seq):
    f32 = np.float32
    half = HEAD_DIM // 2
    inv1 = np.power(f32(ROPE_THETA), -(np.arange(half, dtype=f32) / f32(half)))
    ang1 = np.arange(seq, dtype=f32)[:, None] * inv1[None, :]
    t = np.arange(seq)
    n_axis = HEAD_DIM // 4
    inv2 = np.power(f32(ROPE_THETA), -(np.arange(n_axis, dtype=f32) / f32(n_axis)))
    ang2 = np.concatenate([(t // GRID_W).astype(f32)[:, None] * inv2[None, :],
                           (t % GRID_W).astype(f32)[:, None] * inv2[None, :]], axis=-1)
    q_scale = f32(HEAD_DIM ** -0.5 * LOG2E)
    out = []
    for ang in (ang1, ang2):
        cos = np.concatenate([np.cos(ang), np.cos(ang)], axis=-1)
        sin = np.concatenate([-np.sin(ang), np.sin(ang)], axis=-1)
        out += [cos * q_scale, sin * q_scale, cos, sin]
    return jnp.asarray(np.stack(out, axis=0).astype(f32))


def _route(logits, n_tok, chunks):
    g_logits = logits[:, :N_GROUPS]
    g_prob = jax.nn.softmax(g_logits, axis=-1)
    g_idx = jnp.argmax(g_logits, axis=-1).astype(I32)
    g_gate = jnp.max(g_prob, axis=-1)
    e_logits = logits[:, N_GROUPS:N_GROUPS + EXPERTS_PER_GROUP]
    for g in range(1, N_GROUPS):
        lo = N_GROUPS + g * EXPERTS_PER_GROUP
        e_logits = jnp.where(g_idx[:, None] == g, logits[:, lo:lo + EXPERTS_PER_GROUP], e_logits)
    e_prob = jax.nn.softmax(e_logits, axis=-1)
    i1 = jnp.argmax(e_prob, axis=-1).astype(I32)
    p1 = jnp.max(e_prob, axis=-1)
    rest = jnp.where(jnp.arange(EXPERTS_PER_GROUP, dtype=I32)[None, :] == i1[:, None], -1.0, e_prob)
    i2 = jnp.argmax(rest, axis=-1).astype(I32)
    p2 = jnp.max(rest, axis=-1)
    top_p = jnp.stack([p1, p2], axis=-1)
    top_p = top_p / jnp.sum(top_p, axis=-1, keepdims=True)
    weights = g_gate[:, None] * top_p
    expert = g_idx[:, None] * EXPERTS_PER_GROUP + jnp.stack([i1, i2], axis=-1)

    n_slot = n_tok * TOP_K
    e_flat = expert.reshape(n_slot).astype(I32)
    order = jnp.argsort(e_flat).astype(I32)
    experts = jnp.arange(N_EXPERTS, dtype=I32)
    counts = jnp.sum(e_flat[:, None] == experts[None, :], axis=0, dtype=I32)
    padded = ((counts + MOE_BLOCK - 1) // MOE_BLOCK) * MOE_BLOCK
    p_ends = jnp.cumsum(padded)
    p_starts = p_ends - padded
    buf_len = n_slot + N_EXPERTS * MOE_BLOCK

    def pick(one_hot, table):
        return jnp.sum(jnp.where(one_hot, table[None, :], 0), axis=1, dtype=I32)

    pos = jnp.arange(buf_len, dtype=I32)[:, None]
    ended = p_ends[None, :] <= pos
    inside = jnp.logical_and(p_starts[None, :] <= pos, pos < p_ends[None, :])
    idx = pos[:, 0] - jnp.max(jnp.where(ended, p_ends[None, :], 0), axis=1)
    sorted_at = jnp.clip(pick(ended, counts) + idx, 0, n_slot - 1)
    slot_at = jnp.where(idx < pick(inside, counts), order[sorted_at], 0)
    tok_at, k_at = slot_at // TOP_K, slot_at % TOP_K
    src_slab = tok_at * chunks
    dst_slab = (k_at * n_tok + tok_at) * chunks

    n_items = N_EXPERTS + n_slot // ITEM_ROWS
    per_expert = (padded + ITEM_ROWS - 1) // ITEM_ROWS
    item_ends = jnp.cumsum(per_expert)
    total = item_ends[-1]
    ids = jnp.arange(n_items, dtype=I32)
    live = ids < total
    item = jnp.minimum(ids, jnp.maximum(total - 1, 0))[:, None]
    owner = jnp.logical_and((item_ends - per_expert)[None, :] <= item, item < item_ends[None, :])
    e_of = pick(owner, experts)
    chunk = item[:, 0] - pick(owner, item_ends - per_expert)
    item_row0 = jnp.where(live, pick(owner, p_starts) + chunk * ITEM_ROWS, 0).astype(I32)
    item_blocks = jnp.where(live, jnp.clip(pick(owner, padded) // MOE_BLOCK - chunk * ITEM_BLOCKS, 0, ITEM_BLOCKS), 0)
    item_real = jnp.where(live, jnp.clip(pick(owner, counts) - chunk * ITEM_ROWS, 0, ITEM_ROWS), 0)
    return weights, e_of, item_row0, item_blocks.astype(I32), item_real.astype(I32), src_slab, dst_slab


def kernel(x, w_in, q_norm_g, k_norm_g, w_branch_a, w_branch_b, w_out, ln1_g, ln1_b, w_group, b_group, w_router,
           b_router, w_gate, w_up, w_down, ln2_g, ln2_b):
    batch, seq, d_model = x.shape
    depth = w_in.shape[0]
    n_tok = batch * seq
    dn_alpha = (2 * depth) ** 0.25
    tabs = _rope_tables(seq)
    h = x.reshape(n_tok, d_model)
    for layer in range(depth):
        xb = h.astype(BF16)
        groups = [_proj_a(xb, w_in[layer], tabs, g, batch, seq) for g in range(N_DIL)]
        slots_b = _proj_b(xb, w_in[layer], tabs, q_norm_g[layer][None, :], k_norm_g[layer][None, :], seq)
        gates = _proj_gate(xb, w_in[layer])
        o_a = _attn_a(groups, batch, seq)
        score_bound = (jnp.max(jnp.abs(q_norm_g[layer])) * jnp.max(jnp.abs(k_norm_g[layer]))
                       * (HEAD_DIM ** 0.5 * LOG2E * BF16_SLACK))
        o_b = lax.cond(score_bound <= SCORE_BOUND,
                       lambda s: _attn_b(s, batch, seq, True), lambda s: _attn_b(s, batch, seq, False), slots_b)
        merged = _branch_mix(o_a, o_b, gates, w_branch_a[layer].astype(BF16), w_branch_b[layer].astype(BF16))
        w_r = jnp.concatenate([w_group[layer], w_router[layer]], axis=1)
        w_r = jnp.pad(w_r, ((0, 0), (0, LANES - w_r.shape[1])))
        r_hi = w_r.astype(BF16)
        r_lo = (w_r - r_hi.astype(F32)).astype(BF16)
        r_b = jnp.pad(jnp.concatenate([b_group[layer], b_router[layer]]), (0, LANES - N_GROUPS - N_EXPERTS))[None, :]
        h1_slabs, logits = _out_proj(merged, w_out[layer].astype(BF16), h, ln1_g[layer][None, :],
                                         ln1_b[layer][None, :], jnp.concatenate([r_hi, r_lo], axis=1), r_b, dn_alpha)
        weights, item_expert, item_row0, item_blocks, item_real, src_slab, dst_slab = _route(
            logits, n_tok, d_model // LANES)
        y_slabs = _moe(h1_slabs, w_gate[layer], w_up[layer], w_down[layer], item_expert, item_row0, item_blocks,
                       item_real, src_slab, dst_slab)
        h = _combine(y_slabs, weights, h1_slabs, ln2_g[layer][None, :], ln2_b[layer][None, :], dn_alpha)
    return h.reshape(batch, seq, d_model)
```

```python
import functools
import math

import jax
import jax.numpy as jnp
import numpy as np
from jax import lax
from jax.experimental import pallas as pl
from jax.experimental.pallas import tpu as pltpu

F32 = jnp.float32
BF16 = jnp.bfloat16
I32 = jnp.int32

HEAD_DIM = 128
ROPE_THETA = 10000.0
GRID_W = 64
DIL_PATTERNS = ((128, 1), (512, 4), (2048, 16))
N_DIL = len(DIL_PATTERNS)
A_HEADS = 8
BAND_HALF = 64
B_Q_HEADS = 16
B_KV_HEADS = 4
B_GROUP = B_Q_HEADS // B_KV_HEADS
N_GROUPS = 4
EXPERTS_PER_GROUP = 8
N_EXPERTS = N_GROUPS * EXPERTS_PER_GROUP
TOP_K = 2
MOE_BLOCK = 128
LN_EPS = 1e-5
RMS_EPS = 1e-6
NEG_BIG = -1e30
LOG2E = math.log2(math.e)

LANES = 128
COL_TILE = 8 * HEAD_DIM
ROW_TILE = 1024
SUB_ROWS = 256
ITEM_ROWS = 768
ITEM_BLOCKS = ITEM_ROWS // MOE_BLOCK
FF_TILE = 512
DMA_UNROLL = 8
VMEM_LIMIT = 56 * 1024 * 1024


def _params(sem, vmem=VMEM_LIMIT):
    return pltpu.CompilerParams(dimension_semantics=sem, vmem_limit_bytes=vmem)


_T_QA, _T_KA, _T_VA, _T_QB, _T_KVB, _T_GATE = 0, 3, 6, 9, 11, 12
_TAB_1D, _TAB_AXIAL = 0, 1
_RQ, _RK = 0, 2


def _cast_weights_once(w_ref, wbf_ref):
    @pl.when(pl.program_id(1) == 0)
    def _cast():
        wbf_ref[...] = w_ref[...].astype(BF16)


def _for_sub_tiles(x_ref, wbf_ref, emit):
    for rt in range(x_ref.shape[0] // SUB_ROWS):
        rows = slice(rt * SUB_ROWS, (rt + 1) * SUB_ROWS)
        emit(rt, rows, jnp.dot(x_ref[rows, :], wbf_ref[...], preferred_element_type=F32))


def _head(acc, h):
    return acc[:, h * HEAD_DIM:(h + 1) * HEAD_DIM]


def _table_rows(tab_ref, n_rows):
    return (pl.program_id(1) % (tab_ref.shape[1] // n_rows)) * n_rows


def _rope(a, tab_ref, t, rows, base):
    pos = pl.ds(pl.multiple_of(base + rows.start, SUB_ROWS), SUB_ROWS)
    return a * tab_ref[t, pos, :] + pltpu.roll(a, HEAD_DIM // 2, 1) * tab_ref[t + 1, pos, :]


def _rms(a, g_ref):
    ms = jnp.mean(a * a, axis=-1, keepdims=True)
    return a * lax.rsqrt(ms + RMS_EPS) * g_ref[...]


def _proj_a_kernel(x_ref, w_ref, tab_ref, out_ref, wbf_ref, *, dil):
    j = pl.program_id(0)
    _cast_weights_once(w_ref, wbf_ref)
    per_res = SUB_ROWS // dil
    base = _table_rows(tab_ref, x_ref.shape[0])

    def emit_with(fn):
        def emit(rt, rows, acc):
            if dil == 1:
                for h in range(A_HEADS):
                    out_ref[h, 0, rows, :] = fn(_head(acc, h), rows).astype(BF16)
                return
            for h in range(A_HEADS):
                by_res = fn(_head(acc, h), rows).reshape(per_res, dil, HEAD_DIM).swapaxes(0, 1)
                for r in range(dil):
                    out_ref[h, r, rt * per_res:(rt + 1) * per_res, :] = by_res[r].astype(BF16)
        _for_sub_tiles(x_ref, wbf_ref, emit)

    @pl.when(j == 0)
    def _q():
        emit_with(lambda a, rows: _rope(a, tab_ref, _RQ, rows, base))

    @pl.when(j == 1)
    def _k():
        emit_with(lambda a, rows: _rope(a, tab_ref, _RK, rows, base))

    @pl.when(j == 2)
    def _v():
        emit_with(lambda a, rows: a)


def _proj_b_kernel(x_ref, w_ref, tab_ref, gq_ref, gk_ref, out_ref, wbf_ref, *, q_tiles):
    j = pl.program_id(0)
    _cast_weights_once(w_ref, wbf_ref)
    heads = COL_TILE // HEAD_DIM
    base = _table_rows(tab_ref, x_ref.shape[0])

    @pl.when(j < q_tiles)
    def _qb():
        def emit(rt, rows, acc):
            for h in range(heads):
                out_ref[h, rows, :] = _rope(_rms(_head(acc, h), gq_ref), tab_ref, _RQ, rows, base).astype(BF16)
        _for_sub_tiles(x_ref, wbf_ref, emit)

    @pl.when(j == q_tiles)
    def _kvb():
        def emit(rt, rows, acc):
            for h in range(heads):
                if h < B_KV_HEADS:
                    out_ref[h, rows, :] = _rope(_rms(_head(acc, h), gk_ref), tab_ref, _RK, rows, base).astype(BF16)
                else:
                    out_ref[h, rows, :] = _head(acc, h).astype(BF16)
        _for_sub_tiles(x_ref, wbf_ref, emit)


def _proj_gate_kernel(x_ref, w_ref, out_ref, wbf_ref):
    _cast_weights_once(w_ref, wbf_ref)

    def emit(rt, rows, acc):
        out_ref[rows, :] = (1.0 / (1.0 + jnp.exp(-acc))).astype(BF16)
    _for_sub_tiles(x_ref, wbf_ref, emit)


def _proj_specs(d_model, seq, first_tile, tile_step, table_half):
    return [
        pl.BlockSpec((ROW_TILE, d_model), lambda j, i: (i, 0)),
        pl.BlockSpec((d_model, COL_TILE), lambda j, i: (0, first_tile + tile_step * j)),
        pl.BlockSpec((4, seq, HEAD_DIM), lambda j, i: (table_half, 0, 0), pipeline_mode=pl.Buffered(1)),
    ]


def _proj_a(xb, w_in, tabs, group, batch, seq):
    n_tok, d_model = xb.shape
    dil = DIL_PATTERNS[group][1]
    sub_len = seq // dil
    seq_blocks = seq // ROW_TILE
    return pl.pallas_call(
        functools.partial(_proj_a_kernel, dil=dil),
        out_shape=jax.ShapeDtypeStruct((3 * A_HEADS, batch, dil, sub_len, HEAD_DIM), BF16),
        grid=(3, n_tok // ROW_TILE),
        in_specs=_proj_specs(d_model, seq, group, N_DIL, _TAB_1D),
        out_specs=pl.BlockSpec((A_HEADS, None, dil, ROW_TILE // dil, HEAD_DIM),
                               lambda j, i: (j, i // seq_blocks, 0, i % seq_blocks, 0)),
        scratch_shapes=[pltpu.VMEM((d_model, COL_TILE), BF16)],
        compiler_params=_params(("arbitrary", "arbitrary")),
    )(xb, w_in, tabs)


def _proj_b(xb, w_in, tabs, gq, gk, seq):
    n_tok, d_model = xb.shape
    q_tiles = B_Q_HEADS * HEAD_DIM // COL_TILE
    heads = COL_TILE // HEAD_DIM
    fix = lambda j, i: (0, 0)
    return pl.pallas_call(
        functools.partial(_proj_b_kernel, q_tiles=q_tiles),
        out_shape=jax.ShapeDtypeStruct(((q_tiles + 1) * heads, n_tok, HEAD_DIM), BF16),
        grid=(q_tiles + 1, n_tok // ROW_TILE),
        in_specs=_proj_specs(d_model, seq, _T_QB, 1, _TAB_AXIAL) + [pl.BlockSpec((1, HEAD_DIM), fix), pl.BlockSpec((1, HEAD_DIM), fix)],
        out_specs=pl.BlockSpec((heads, ROW_TILE, HEAD_DIM), lambda j, i: (j, i, 0)),
        scratch_shapes=[pltpu.VMEM((d_model, COL_TILE), BF16)],
        compiler_params=_params(("arbitrary", "arbitrary")),
    )(xb, w_in, tabs, gq, gk)


def _proj_gate(xb, w_in):
    n_tok, d_model = xb.shape
    n_ct = w_in.shape[1] // COL_TILE - _T_GATE
    return pl.pallas_call(
        _proj_gate_kernel,
        out_shape=jax.ShapeDtypeStruct((n_tok, n_ct * COL_TILE), BF16),
        grid=(n_ct, n_tok // ROW_TILE),
        in_specs=[pl.BlockSpec((ROW_TILE, d_model), lambda j, i: (i, 0)),
                  pl.BlockSpec((d_model, COL_TILE), lambda j, i: (0, _T_GATE + j))],
        out_specs=pl.BlockSpec((ROW_TILE, COL_TILE), lambda j, i: (i, j)),
        scratch_shapes=[pltpu.VMEM((d_model, COL_TILE), BF16)],
        compiler_params=_params(("arbitrary", "arbitrary")),
    )(xb, w_in)


_SUB = 128
_WIN = _SUB + 2 * BAND_HALF
_MERGE_ROWS = 256
_A_UNROLL = 16


def _attn_a_group(g, dil, seq, q_ref, k_ref, v_ref, og, lg, bias):
    sub_len = seq // dil
    per_seq = sub_len // _SUB
    shift = per_seq.bit_length() - 1
    ones = jnp.ones((_WIN, HEAD_DIM), BF16)

    def body(i, carry):
        r = lax.shift_right_logical(i, shift)
        p0 = pl.multiple_of((i & (per_seq - 1)) * _SUB, _SUB)
        start = pl.multiple_of(jnp.clip(p0 - BAND_HALF, 0, sub_len - _WIN), BAND_HALF)
        q = q_ref[r, pl.ds(p0, _SUB), :]
        k = k_ref[r, pl.ds(start, _WIN), :]
        v = v_ref[r, pl.ds(start, _WIN), :]
        s = lax.dot_general(q, k, (((1,), (1,)), ((), ())), preferred_element_type=F32)
        s = s + bias[lax.shift_right_logical(p0 - start, BAND_HALF.bit_length() - 1)]
        m = jnp.max(s, axis=1, keepdims=True)
        p = jnp.exp2(s - m).astype(BF16)
        both = jnp.dot(p, jnp.concatenate([v, ones], axis=1), preferred_element_type=F32)
        l = both[:, HEAD_DIM:]
        o = both[:, :HEAD_DIM] / l
        lse = m + jnp.log2(l)
        if dil == 1:
            rows = pl.ds(p0, _SUB)
        else:
            rows = pl.ds(p0 * dil + r, _SUB, stride=dil)
        og[g, rows, :] = o
        lg[g, rows, :] = lse
        return carry

    lax.fori_loop(0, seq // _SUB, body, 0, unroll=_A_UNROLL)


def _attn_a_kernel(*refs, seq):
    qkv, o_ref = refs[:3 * N_DIL], refs[3 * N_DIL]
    og, lg, bias = refs[3 * N_DIL + 1:]
    row = lax.broadcasted_iota(I32, (_SUB, _WIN), 0)
    col = lax.broadcasted_iota(I32, (_SUB, _WIN), 1)
    for case in range(3):
        bias[case] = jnp.where(jnp.abs(col - row - case * BAND_HALF) <= BAND_HALF, 0.0, NEG_BIG)
    for g, (_, dil) in enumerate(DIL_PATTERNS):
        _attn_a_group(g, dil, seq, qkv[3 * g], qkv[3 * g + 1], qkv[3 * g + 2], og, lg, bias)

    def merge(c, carry):
        rows = pl.ds(pl.multiple_of(c * _MERGE_ROWS, _MERGE_ROWS), _MERGE_ROWS)
        ls = [lg[g, rows, :] for g in range(N_DIL)]
        mx = functools.reduce(jnp.maximum, ls)
        ws = [jnp.exp2(l - mx) for l in ls]
        num = functools.reduce(lambda a, b: a + b, [w * og[g, rows, :] for g, w in enumerate(ws)])
        o_ref[rows, :] = (num / functools.reduce(lambda a, b: a + b, ws)).astype(o_ref.dtype)
        return carry

    lax.fori_loop(0, seq // _MERGE_ROWS, merge, 0)


def _attn_a(groups, batch, seq):
    in_specs, operands = [], []
    for g, (_, dil) in enumerate(DIL_PATTERNS):
        for kind in range(3):
            first = kind * A_HEADS
            in_specs.append(pl.BlockSpec((None, None, dil, seq // dil, HEAD_DIM),
                                         lambda b, h, first=first: (first + h, b, 0, 0, 0)))
            operands.append(groups[g])
    return pl.pallas_call(
        functools.partial(_attn_a_kernel, seq=seq),
        out_shape=jax.ShapeDtypeStruct((batch * seq, A_HEADS * HEAD_DIM), BF16),
        grid=(batch, A_HEADS),
        in_specs=in_specs,
        out_specs=pl.BlockSpec((seq, HEAD_DIM), lambda b, h: (b, h)),
        scratch_shapes=[pltpu.VMEM((N_DIL, seq, HEAD_DIM), F32), pltpu.VMEM((N_DIL, seq, HEAD_DIM), F32),
                        pltpu.VMEM((3, _SUB, _WIN), F32)],
        compiler_params=_params(("arbitrary", "arbitrary")),
    )(*operands)


_BQ = 512
_BK = 512
SCORE_BOUND = 64.0
BF16_SLACK = 1.02


def _attn_b_kernel(q_ref, k_ref, v_ref, o_ref, *, seq):
    rows = B_GROUP * _BQ
    q = q_ref[...].reshape(rows, HEAD_DIM)

    def body(c, carry):
        m, l, acc = carry
        c0 = pl.multiple_of(c * _BK, _BK)
        k = k_ref[pl.ds(c0, _BK), :]
        v = v_ref[pl.ds(c0, _BK), :]
        s = lax.dot_general(q, k, (((1,), (1,)), ((), ())), preferred_element_type=F32)
        m_new = jnp.maximum(m, jnp.max(s, axis=1, keepdims=True))
        alpha = jnp.exp2(m - m_new)
        p = jnp.exp2(s - m_new)
        l = alpha * l + jnp.sum(p, axis=1, keepdims=True)
        acc = alpha * acc + jnp.dot(p.astype(BF16), v, preferred_element_type=F32)
        return m_new, l, acc

    init = (jnp.full((rows, 1), NEG_BIG, F32), jnp.zeros((rows, 1), F32), jnp.zeros((rows, HEAD_DIM), F32))
    _, l, acc = lax.fori_loop(0, seq // _BK, body, init)
    o = acc / l
    for g in range(B_GROUP):
        o_ref[:, g * HEAD_DIM:(g + 1) * HEAD_DIM] = o[g * _BQ:(g + 1) * _BQ].astype(o_ref.dtype)


def _attn_b_bounded_kernel(q_ref, k_ref, v_ref, o_ref, *, seq):
    rows = B_GROUP * _BQ
    q = q_ref[...].reshape(rows, HEAD_DIM)
    ones = jnp.ones((_BK, HEAD_DIM), BF16)
    acc = jnp.zeros((rows, 2 * HEAD_DIM), F32)
    for c in range(seq // _BK):
        k = k_ref[c * _BK:(c + 1) * _BK, :]
        v = v_ref[c * _BK:(c + 1) * _BK, :]
        s = lax.dot_general(q, k, (((1,), (1,)), ((), ())), preferred_element_type=F32)
        p = jnp.exp2(s).astype(BF16)
        acc = acc + jnp.dot(p, jnp.concatenate([v, ones], axis=1), preferred_element_type=F32)
    o = acc[:, :HEAD_DIM] / acc[:, HEAD_DIM:HEAD_DIM + 1]
    for g in range(B_GROUP):
        o_ref[:, g * HEAD_DIM:(g + 1) * HEAD_DIM] = o[g * _BQ:(g + 1) * _BQ].astype(o_ref.dtype)


def _attn_b(slots, batch, seq, bounded):
    n_tok = batch * seq
    k0 = B_Q_HEADS
    v0 = k0 + B_KV_HEADS
    qblocks = seq // _BQ
    return pl.pallas_call(
        functools.partial(_attn_b_bounded_kernel if bounded else _attn_b_kernel, seq=seq),
        out_shape=jax.ShapeDtypeStruct((n_tok, B_Q_HEADS * HEAD_DIM), BF16),
        grid=(batch, B_KV_HEADS, qblocks),
        in_specs=[
            pl.BlockSpec((B_GROUP, _BQ, HEAD_DIM), lambda b, kv, qi: (kv, b * qblocks + qi, 0)),
            pl.BlockSpec((None, seq, HEAD_DIM), lambda b, kv, qi: (k0 + kv, b, 0)),
            pl.BlockSpec((None, seq, HEAD_DIM), lambda b, kv, qi: (v0 + kv, b, 0)),
        ],
        out_specs=pl.BlockSpec((_BQ, B_GROUP * HEAD_DIM), lambda b, kv, qi: (b * qblocks + qi, kv)),
        compiler_params=_params(("arbitrary", "arbitrary", "arbitrary")),
    )(slots, slots, slots)


_MIX_ROWS = 512


def _branch_kernel(oa_ref, ob_ref, ga_ref, gb_ref, wa_ref, wb_ref, out_ref):
    for rt in range(out_ref.shape[0] // SUB_ROWS):
        rows = slice(rt * SUB_ROWS, (rt + 1) * SUB_ROWS)
        y_a = jnp.dot(oa_ref[rows, :], wa_ref[...], preferred_element_type=F32)
        y_b = jnp.dot(ob_ref[rows, :], wb_ref[...], preferred_element_type=F32)
        out_ref[rows, :] = (ga_ref[rows, :].astype(F32) * y_a + gb_ref[rows, :].astype(F32) * y_b).astype(BF16)


def _branch_mix(o_a, o_b, gates, wa, wb):
    n_tok = o_b.shape[0]
    d_model = wa.shape[1]
    tm = _MIX_ROWS
    row = lambda i: (i, 0)
    return pl.pallas_call(
        _branch_kernel,
        out_shape=jax.ShapeDtypeStruct((n_tok, d_model), BF16),
        grid=(n_tok // tm,),
        in_specs=[pl.BlockSpec((tm, o_a.shape[1]), row),
                  pl.BlockSpec((tm, o_b.shape[1]), row),
                  pl.BlockSpec((tm, d_model), lambda i: (i, 0)),
                  pl.BlockSpec((tm, d_model), lambda i: (i, 1)),
                  pl.BlockSpec(wa.shape, lambda i: (0, 0)),
                  pl.BlockSpec(wb.shape, lambda i: (0, 0))],
        out_specs=pl.BlockSpec((tm, d_model), row),
        compiler_params=_params(("arbitrary",)),
    )(o_a, o_b, gates, gates, wa, wb)


def _layer_norm(z, g, b):
    mu = jnp.mean(z, axis=-1, keepdims=True)
    zc = z - mu
    var = jnp.mean(zc * zc, axis=-1, keepdims=True)
    return zc * lax.rsqrt(var + LN_EPS) * g + b


def _to_slabs(slab_ref, value, row0):
    rows, d = value.shape
    chunks = d // LANES
    for c in range(chunks):
        slab_ref[pl.ds(row0 * chunks + c, rows, stride=chunks), :] = value[:, c * LANES:(c + 1) * LANES]


def _from_slabs(slab_ref, rows, chunks, c):
    return slab_ref[pl.ds(c, rows, stride=chunks), :]


def _out_proj_kernel(m_ref, w_ref, x_ref, g_ref, b_ref, rcat_ref, rb_ref, hs_ref, lg_ref, *, alpha):
    for rt in range(m_ref.shape[0] // SUB_ROWS):
        rows = slice(rt * SUB_ROWS, (rt + 1) * SUB_ROWS)
        mix = jnp.dot(m_ref[rows, :], w_ref[...], preferred_element_type=F32)
        h = _layer_norm(alpha * x_ref[rows, :] + mix, g_ref[...], b_ref[...])
        _to_slabs(hs_ref, h, rt * SUB_ROWS)
        hi = h.astype(BF16)
        lo = (h - hi.astype(F32)).astype(BF16)
        both = jnp.dot(hi, rcat_ref[...], preferred_element_type=F32)
        lg = both[:, :LANES] + both[:, LANES:] + jnp.dot(lo, rcat_ref[:, :LANES], preferred_element_type=F32)
        lg_ref[rows, :] = lg + rb_ref[...]


def _out_proj(merged, w_out, x2, g, b, r_cat, r_b, alpha):
    n_tok, d_model = x2.shape
    tm = _MIX_ROWS
    chunks = d_model // LANES
    row = lambda i: (i, 0)
    fix = lambda i: (0, 0)
    return pl.pallas_call(
        functools.partial(_out_proj_kernel, alpha=alpha),
        out_shape=(jax.ShapeDtypeStruct((n_tok * chunks, LANES), F32),
                   jax.ShapeDtypeStruct((n_tok, LANES), F32)),
        grid=(n_tok // tm,),
        in_specs=[pl.BlockSpec((tm, d_model), row),
                  pl.BlockSpec(w_out.shape, fix),
                  pl.BlockSpec((tm, d_model), row),
                  pl.BlockSpec((1, d_model), fix),
                  pl.BlockSpec((1, d_model), fix),
                  pl.BlockSpec(r_cat.shape, fix),
                  pl.BlockSpec((1, LANES), fix)],
        out_specs=(pl.BlockSpec((tm * chunks, LANES), row), pl.BlockSpec((tm, LANES), row)),
        compiler_params=_params(("arbitrary",)),
    )(merged, w_out, x2, g, b, r_cat, r_b)


_BIG_CHUNK = 512


def _moe_kernel(item_expert, item_row0, item_blocks, item_real, src_slab, dst_slab,
                h_hbm, wg_ref, wu_ref, wd_ref, y_hbm,
                stage, xbf, acc, obuf, gsem, ssem):
    it = pl.program_id(0)
    f = pl.program_id(1)
    n_items = pl.num_programs(0)
    n_ff = pl.num_programs(1)
    n_blocks = item_blocks[it]
    d_model = xbf.shape[1]
    chunks = d_model // LANES

    def slab(ref, first):
        return ref.at[pl.ds(pl.multiple_of(first, chunks), chunks)]

    def gather_copy(base, r):
        return pltpu.make_async_copy(slab(h_hbm, src_slab[base + r]), slab(stage, r * chunks), gsem.at[0])

    def scatter_copy(base, r):
        return pltpu.make_async_copy(slab(obuf, r * chunks), slab(y_hbm, dst_slab[base + r]), ssem.at[0])

    def for_rows(n_rows, fn):
        def group(t, c):
            for u in range(DMA_UNROLL):
                fn(t * DMA_UNROLL + u)
            return c
        n_groups = n_rows // DMA_UNROLL
        lax.fori_loop(0, n_groups, group, 0)

        def single(r, c):
            fn(r)
            return c
        lax.fori_loop(n_groups * DMA_UNROLL, n_rows, single, 0)

    def gather(item, op):
        base = item_row0[item]
        for_rows(item_blocks[item] * MOE_BLOCK, lambda r: op(gather_copy(base, r)))

    def scatter(item, op):
        base = item_row0[item]
        for_rows(item_real[item], lambda r: op(scatter_copy(base, r)))

    start = lambda cp: cp.start()
    wait = lambda cp: cp.wait()

    @pl.when((it == 0) & (f == 0))
    def _first_gather():
        gather(it, start)

    @pl.when(f == 0)
    def _gathered():
        gather(it, wait)

        def cast_rows(b, carry):
            r0 = pl.multiple_of(b * MOE_BLOCK, MOE_BLOCK)
            for c in range(chunks):
                piece = stage[pl.ds(r0 * chunks + c, MOE_BLOCK, stride=chunks), :]
                xbf[pl.ds(r0, MOE_BLOCK), c * LANES:(c + 1) * LANES] = piece.astype(BF16)
            return carry
        lax.fori_loop(0, n_blocks, cast_rows, 0)

        @pl.when(it + 1 < n_items)
        def _prefetch_next():
            gather(it + 1, start)

    def chunk(r0, rows, mode):
        xb = xbf[pl.ds(r0, rows), :]
        a = jnp.dot(xb, wg_ref[...].astype(BF16), preferred_element_type=F32)
        u = jnp.dot(xb, wu_ref[...].astype(BF16), preferred_element_type=F32)
        hid = (a / (1.0 + jnp.exp(-a)) * u).astype(BF16)
        y = jnp.dot(hid, wd_ref[...].astype(BF16), preferred_element_type=F32)
        if mode == "first":
            acc[pl.ds(r0, rows), :] = y
        elif mode == "middle":
            acc[pl.ds(r0, rows), :] += y
        else:
            if mode == "last":
                y = acc[pl.ds(r0, rows), :] + y
            for c in range(chunks):
                obuf[pl.ds(r0 * chunks + c, rows, stride=chunks), :] = y[:, c * LANES:(c + 1) * LANES]

    def run_item(mode):
        per_big = _BIG_CHUNK // MOE_BLOCK

        def big_chunk(t, c):
            chunk(pl.multiple_of(t * _BIG_CHUNK, _BIG_CHUNK), _BIG_CHUNK, mode)
            return c
        n_big = n_blocks // per_big
        lax.fori_loop(0, n_big, big_chunk, 0)

        def small_chunk(b, c):
            chunk(pl.multiple_of(b * MOE_BLOCK, MOE_BLOCK), MOE_BLOCK, mode)
            return c
        lax.fori_loop(n_big * per_big, n_blocks, small_chunk, 0)

    last = n_ff - 1

    @pl.when((n_blocks > 0) & (f == last))
    def _final():
        @pl.when(it > 0)
        def _drain_prev():
            scatter(it - 1, wait)
        run_item("only" if n_ff == 1 else "last")
        scatter(it, start)

    if n_ff > 1:
        @pl.when((n_blocks > 0) & (f == 0))
        def _first():
            run_item("first")

    if n_ff > 2:
        @pl.when((n_blocks > 0) & (f > 0) & (f < last))
        def _middle():
            run_item("middle")

    @pl.when((f == last) & (it == n_items - 1))
    def _drain_last():
        scatter(jnp.where(n_blocks > 0, it, _last_live(item_blocks, n_items)), wait)


def _last_live(item_blocks, n_items):
    def body(i, best):
        return jnp.where(item_blocks[i] > 0, i, best)
    return lax.fori_loop(0, n_items, body, 0)


def _moe(h_slabs, w_gate, w_up, w_down, item_expert, item_row0, item_blocks, item_real, src_slab, dst_slab):
    d_model = w_gate.shape[1]
    chunks = d_model // LANES
    n_tok = h_slabs.shape[0] // chunks
    ff = w_gate.shape[2]
    n_ff = ff // FF_TILE
    n_items = item_expert.shape[0]

    def ff_idx(it, f, blocks):
        return jnp.where(blocks[it] > 0, f, n_ff - 1)

    grid_spec = pltpu.PrefetchScalarGridSpec(
        num_scalar_prefetch=6,
        grid=(n_items, n_ff),
        in_specs=[
            pl.BlockSpec(memory_space=pl.ANY),
            pl.BlockSpec((None, d_model, FF_TILE), lambda it, f, ie, ir, ib, nr, st, dr: (ie[it], 0, ff_idx(it, f, ib))),
            pl.BlockSpec((None, d_model, FF_TILE), lambda it, f, ie, ir, ib, nr, st, dr: (ie[it], 0, ff_idx(it, f, ib))),
            pl.BlockSpec((None, FF_TILE, d_model), lambda it, f, ie, ir, ib, nr, st, dr: (ie[it], ff_idx(it, f, ib), 0)),
        ],
        out_specs=pl.BlockSpec(memory_space=pl.ANY),
        scratch_shapes=[
            pltpu.VMEM((ITEM_ROWS * chunks, LANES), F32),
            pltpu.VMEM((ITEM_ROWS, d_model), BF16),
            pltpu.VMEM((ITEM_ROWS, d_model), F32),
            pltpu.VMEM((ITEM_ROWS * chunks, LANES), F32),
            pltpu.SemaphoreType.DMA((1,)),
            pltpu.SemaphoreType.DMA((1,)),
        ],
    )
    return pl.pallas_call(
        _moe_kernel,
        out_shape=jax.ShapeDtypeStruct((n_tok * TOP_K * chunks, LANES), F32),
        grid_spec=grid_spec,
        compiler_params=_params(("arbitrary", "arbitrary")),
    )(item_expert, item_row0, item_blocks, item_real, src_slab, dst_slab, h_slabs, w_gate, w_up, w_down)


_COMBINE_ROWS = 512


def _combine_kernel(y0_ref, y1_ref, wt_ref, hs_ref, g_ref, b_ref, out_ref, *, alpha):
    rows, d_model = out_ref.shape
    chunks = d_model // LANES
    wt = wt_ref[...]
    w0, w1 = wt[:, 0:1], wt[:, 1:2]
    for c in range(chunks):
        cols = slice(c * LANES, (c + 1) * LANES)
        ffn = w0 * _from_slabs(y0_ref, rows, chunks, c) + w1 * _from_slabs(y1_ref, rows, chunks, c)
        out_ref[:, cols] = alpha * _from_slabs(hs_ref, rows, chunks, c) + ffn
    out_ref[...] = _layer_norm(out_ref[...], g_ref[...], b_ref[...])


def _combine(y_slabs, weights, h_slabs, g, b, alpha):
    d_model = g.shape[1]
    n_tok = h_slabs.shape[0] * LANES // d_model
    tm = _COMBINE_ROWS
    chunks = d_model // LANES
    second = n_tok // tm
    row = lambda i: (i, 0)
    fix = lambda i: (0, 0)
    return pl.pallas_call(
        functools.partial(_combine_kernel, alpha=alpha),
        out_shape=jax.ShapeDtypeStruct((n_tok, d_model), F32),
        grid=(n_tok // tm,),
        in_specs=[pl.BlockSpec((tm * chunks, LANES), row),
                  pl.BlockSpec((tm * chunks, LANES), lambda i: (second + i, 0)),
                  pl.BlockSpec((tm, TOP_K), row),
                  pl.BlockSpec((tm * chunks, LANES), row),
                  pl.BlockSpec((1, d_model), fix),
                  pl.BlockSpec((1, d_model), fix)],
        out_specs=pl.BlockSpec((tm, d_model), row),
        compiler_params=_params(("arbitrary",)),
    )(y_slabs, y_slabs, weights, h_slabs, g, b)


def _rope_tables(seq):
    f32 = np.float32
    half = HEAD_DIM // 2
    inv1 = np.power(f32(ROPE_THETA), -(np.arange(half, dtype=f32) / f32(half)))
    ang1 = np.arange(seq, dtype=f32)[:, None] * inv1[None, :]
    t = np.arange(seq)
    n_axis = HEAD_DIM // 4
    inv2 = np.power(f32(ROPE_THETA), -(np.arange(n_axis, dtype=f32) / f32(n_axis)))
    ang2 = np.concatenate([(t // GRID_W).astype(f32)[:, None] * inv2[None, :],
                           (t % GRID_W).astype(f32)[:, None] * inv2[None, :]], axis=-1)
    q_scale = f32(HEAD_DIM ** -0.5 * LOG2E)
    out = []
    for ang in (ang1, ang2):
        cos = np.concatenate([np.cos(ang), np.cos(ang)], axis=-1)
        sin = np.concatenate([-np.sin(ang), np.sin(ang)], axis=-1)
        out += [cos * q_scale, sin * q_scale, cos, sin]
    return jnp.asarray(np.stack(out, axis=0).astype(f32))


def _route(logits, n_tok, chunks):
    g_logits = logits[:, :N_GROUPS]
    g_prob = jax.nn.softmax(g_logits, axis=-1)
    g_idx = jnp.argmax(g_logits, axis=-1).astype(I32)
    g_gate = jnp.max(g_prob, axis=-1)
    e_logits = logits[:, N_GROUPS:N_GROUPS + EXPERTS_PER_GROUP]
    for g in range(1, N_GROUPS):
        lo = N_GROUPS + g * EXPERTS_PER_GROUP
        e_logits = jnp.where(g_idx[:, None] == g, logits[:, lo:lo + EXPERTS_PER_GROUP], e_logits)
    e_prob = jax.nn.softmax(e_logits, axis=-1)
    i1 = jnp.argmax(e_prob, axis=-1).astype(I32)
    p1 = jnp.max(e_prob, axis=-1)
    rest = jnp.where(jnp.arange(EXPERTS_PER_GROUP, dtype=I32)[None, :] == i1[:, None], -1.0, e_prob)
    i2 = jnp.argmax(rest, axis=-1).astype(I32)
    p2 = jnp.max(rest, axis=-1)
    top_p = jnp.stack([p1, p2], axis=-1)
    top_p = top_p / jnp.sum(top_p, axis=-1, keepdims=True)
    weights = g_gate[:, None] * top_p
    expert = g_idx[:, None] * EXPERTS_PER_GROUP + jnp.stack([i1, i2], axis=-1)

    n_slot = n_tok * TOP_K
    e_flat = expert.reshape(n_slot).astype(I32)
    order = jnp.argsort(e_flat).astype(I32)
    experts = jnp.arange(N_EXPERTS, dtype=I32)
    counts = jnp.sum(e_flat[:, None] == experts[None, :], axis=0, dtype=I32)
    padded = ((counts + MOE_BLOCK - 1) // MOE_BLOCK) * MOE_BLOCK
    p_ends = jnp.cumsum(padded)
    p_starts = p_ends - padded
    buf_len = n_slot + N_EXPERTS * MOE_BLOCK

    def pick(one_hot, table):
        return jnp.sum(jnp.where(one_hot, table[None, :], 0), axis=1, dtype=I32)

    pos = jnp.arange(buf_len, dtype=I32)[:, None]
    ended = p_ends[None, :] <= pos
    inside = jnp.logical_and(p_starts[None, :] <= pos, pos < p_ends[None, :])
    idx = pos[:, 0] - jnp.max(jnp.where(ended, p_ends[None, :], 0), axis=1)
    sorted_at = jnp.clip(pick(ended, counts) + idx, 0, n_slot - 1)
    slot_at = jnp.where(idx < pick(inside, counts), order[sorted_at], 0)
    tok_at, k_at = slot_at // TOP_K, slot_at % TOP_K
    src_slab = tok_at * chunks
    dst_slab = (k_at * n_tok + tok_at) * chunks

    n_items = N_EXPERTS + n_slot // ITEM_ROWS
    per_expert = (padded + ITEM_ROWS - 1) // ITEM_ROWS
    item_ends = jnp.cumsum(per_expert)
    total = item_ends[-1]
    ids = jnp.arange(n_items, dtype=I32)
    live = ids < total
    item = jnp.minimum(ids, jnp.maximum(total - 1, 0))[:, None]
    owner = jnp.logical_and((item_ends - per_expert)[None, :] <= item, item < item_ends[None, :])
    e_of = pick(owner, experts)
    chunk = item[:, 0] - pick(owner, item_ends - per_expert)
    item_row0 = jnp.where(live, pick(owner, p_starts) + chunk * ITEM_ROWS, 0).astype(I32)
    item_blocks = jnp.where(live, jnp.clip(pick(owner, padded) // MOE_BLOCK - chunk * ITEM_BLOCKS, 0, ITEM_BLOCKS), 0)
    item_real = jnp.where(live, jnp.clip(pick(owner, counts) - chunk * ITEM_ROWS, 0, ITEM_ROWS), 0)
    return weights, e_of, item_row0, item_blocks.astype(I32), item_real.astype(I32), src_slab, dst_slab


def kernel(x, w_in, q_norm_g, k_norm_g, w_branch_a, w_branch_b, w_out, ln1_g, ln1_b, w_group, b_group, w_router,
           b_router, w_gate, w_up, w_down, ln2_g, ln2_b):
    batch, seq, d_model = x.shape
    depth = w_in.shape[0]
    n_tok = batch * seq
    dn_alpha = (2 * depth) ** 0.25
    tabs = _rope_tables(seq)
    h = x.reshape(n_tok, d_model)
    for layer in range(depth):
        xb = h.astype(BF16)
        groups = [_proj_a(xb, w_in[layer], tabs, g, batch, seq) for g in range(N_DIL)]
        slots_b = _proj_b(xb, w_in[layer], tabs, q_norm_g[layer][None, :], k_norm_g[layer][None, :], seq)
        gates = _proj_gate(xb, w_in[layer])
        o_a = _attn_a(groups, batch, seq)
        score_bound = (jnp.max(jnp.abs(q_norm_g[layer])) * jnp.max(jnp.abs(k_norm_g[layer]))
                       * (HEAD_DIM ** 0.5 * LOG2E * BF16_SLACK))
        o_b = lax.cond(score_bound <= SCORE_BOUND,
                       lambda s: _attn_b(s, batch, seq, True), lambda s: _attn_b(s, batch, seq, False), slots_b)
        merged = _branch_mix(o_a, o_b, gates, w_branch_a[layer].astype(BF16), w_branch_b[layer].astype(BF16))
        w_r = jnp.concatenate([w_group[layer], w_router[layer]], axis=1)
        w_r = jnp.pad(w_r, ((0, 0), (0, LANES - w_r.shape[1])))
        r_hi = w_r.astype(BF16)
        r_lo = (w_r - r_hi.astype(F32)).astype(BF16)
        r_b = jnp.pad(jnp.concatenate([b_group[layer], b_router[layer]]), (0, LANES - N_GROUPS - N_EXPERTS))[None, :]
        h1_slabs, logits = _out_proj(merged, w_out[layer].astype(BF16), h, ln1_g[layer][None, :],
                                         ln1_b[layer][None, :], jnp.concatenate([r_hi, r_lo], axis=1), r_b, dn_alpha)
        weights, item_expert, item_row0, item_blocks, item_real, src_slab, dst_slab = _route(
            logits, n_tok, d_model // LANES)
        y_slabs = _moe(h1_slabs, w_gate[layer], w_up[layer], w_down[layer], item_expert, item_row0, item_blocks,
                       item_real, src_slab, dst_slab)
        h = _combine(y_slabs, weights, h1_slabs, ln2_g[layer][None, :], ln2_b[layer][None, :], dn_alpha)
    return h.reshape(batch, seq, d_model)
```

```python
import functools
import math

import jax
import jax.numpy as jnp
import numpy as np
from jax import lax
from jax.experimental import pallas as pl
from jax.experimental.pallas import tpu as pltpu

F32 = jnp.float32
BF16 = jnp.bfloat16
I32 = jnp.int32

HEAD_DIM = 128
ROPE_THETA = 10000.0
GRID_W = 64
DIL_PATTERNS = ((128, 1), (512, 4), (2048, 16))
N_DIL = len(DIL_PATTERNS)
A_HEADS = 8
BAND_HALF = 64
B_Q_HEADS = 16
B_KV_HEADS = 4
B_GROUP = B_Q_HEADS // B_KV_HEADS
N_GROUPS = 4
EXPERTS_PER_GROUP = 8
N_EXPERTS = N_GROUPS * EXPERTS_PER_GROUP
TOP_K = 2
MOE_BLOCK = 128
LN_EPS = 1e-5
RMS_EPS = 1e-6
NEG_BIG = -1e30
LOG2E = math.log2(math.e)

LANES = 128
COL_TILE = 8 * HEAD_DIM
ROW_TILE = 1024
SUB_ROWS = 256
ITEM_ROWS = 768
ITEM_BLOCKS = ITEM_ROWS // MOE_BLOCK
FF_TILE = 512
DMA_UNROLL = 8
VMEM_LIMIT = 56 * 1024 * 1024


def _params(sem, vmem=VMEM_LIMIT):
    return pltpu.CompilerParams(dimension_semantics=sem, vmem_limit_bytes=vmem)


_T_QA, _T_KA, _T_VA, _T_QB, _T_KVB, _T_GATE = 0, 3, 6, 9, 11, 12
_TAB_1D, _TAB_AXIAL = 0, 1
_RQ, _RK = 0, 2


def _cast_weights_once(w_ref, wbf_ref):
    @pl.when(pl.program_id(1) == 0)
    def _cast():
        wbf_ref[...] = w_ref[...].astype(BF16)


def _for_sub_tiles(x_ref, wbf_ref, emit):
    for rt in range(x_ref.shape[0] // SUB_ROWS):
        rows = slice(rt * SUB_ROWS, (rt + 1) * SUB_ROWS)
        emit(rt, rows, jnp.dot(x_ref[rows, :], wbf_ref[...], preferred_element_type=F32))


def _head(acc, h):
    return acc[:, h * HEAD_DIM:(h + 1) * HEAD_DIM]


def _rope(a, tab_ref, t, rows):
    return a * tab_ref[t, rows, :] + pltpu.roll(a, HEAD_DIM // 2, 1) * tab_ref[t + 1, rows, :]


def _rms(a, g_ref):
    ms = jnp.mean(a * a, axis=-1, keepdims=True)
    return a * lax.rsqrt(ms + RMS_EPS) * g_ref[...]


def _proj_a_kernel(x_ref, w_ref, tab_ref, out_ref, wbf_ref, *, dil):
    j = pl.program_id(0)
    _cast_weights_once(w_ref, wbf_ref)
    per_res = SUB_ROWS // dil

    def emit_with(fn):
        def emit(rt, rows, acc):
            if dil == 1:
                for h in range(A_HEADS):
                    out_ref[h, 0, rows, :] = fn(_head(acc, h), rows).astype(BF16)
                return
            for h in range(A_HEADS):
                by_res = fn(_head(acc, h), rows).reshape(per_res, dil, HEAD_DIM).swapaxes(0, 1)
                for r in range(dil):
                    out_ref[h, r, rt * per_res:(rt + 1) * per_res, :] = by_res[r].astype(BF16)
        _for_sub_tiles(x_ref, wbf_ref, emit)

    @pl.when(j == 0)
    def _q():
        emit_with(lambda a, rows: _rope(a, tab_ref, _RQ, rows))

    @pl.when(j == 1)
    def _k():
        emit_with(lambda a, rows: _rope(a, tab_ref, _RK, rows))

    @pl.when(j == 2)
    def _v():
        emit_with(lambda a, rows: a)


def _proj_b_kernel(x_ref, w_ref, tab_ref, gq_ref, gk_ref, out_ref, wbf_ref, *, q_tiles):
    j = pl.program_id(0)
    _cast_weights_once(w_ref, wbf_ref)
    heads = COL_TILE // HEAD_DIM

    @pl.when(j < q_tiles)
    def _qb():
        def emit(rt, rows, acc):
            for h in range(heads):
                out_ref[h, rows, :] = _rope(_rms(_head(acc, h), gq_ref), tab_ref, _RQ, rows).astype(BF16)
        _for_sub_tiles(x_ref, wbf_ref, emit)

    @pl.when(j == q_tiles)
    def _kvb():
        def emit(rt, rows, acc):
            for h in range(heads):
                if h < B_KV_HEADS:
                    out_ref[h, rows, :] = _rope(_rms(_head(acc, h), gk_ref), tab_ref, _RK, rows).astype(BF16)
                else:
                    out_ref[h, rows, :] = _head(acc, h).astype(BF16)
        _for_sub_tiles(x_ref, wbf_ref, emit)


def _proj_gate_kernel(x_ref, w_ref, out_ref, wbf_ref):
    _cast_weights_once(w_ref, wbf_ref)

    def emit(rt, rows, acc):
        out_ref[rows, :] = (1.0 / (1.0 + jnp.exp(-acc))).astype(BF16)
    _for_sub_tiles(x_ref, wbf_ref, emit)


def _proj_specs(d_model, seq, first_tile, tile_step, table_half):
    seq_blocks = seq // ROW_TILE
    return [
        pl.BlockSpec((ROW_TILE, d_model), lambda j, i: (i, 0)),
        pl.BlockSpec((d_model, COL_TILE), lambda j, i: (0, first_tile + tile_step * j)),
        pl.BlockSpec((4, ROW_TILE, HEAD_DIM), lambda j, i: (table_half, i % seq_blocks, 0)),
    ]


def _proj_a(xb, w_in, tabs, group, batch, seq):
    n_tok, d_model = xb.shape
    dil = DIL_PATTERNS[group][1]
    sub_len = seq // dil
    seq_blocks = seq // ROW_TILE
    return pl.pallas_call(
        functools.partial(_proj_a_kernel, dil=dil),
        out_shape=jax.ShapeDtypeStruct((3 * A_HEADS, batch, dil, sub_len, HEAD_DIM), BF16),
        grid=(3, n_tok // ROW_TILE),
        in_specs=_proj_specs(d_model, seq, group, N_DIL, _TAB_1D),
        out_specs=pl.BlockSpec((A_HEADS, None, dil, ROW_TILE // dil, HEAD_DIM),
                               lambda j, i: (j, i // seq_blocks, 0, i % seq_blocks, 0)),
        scratch_shapes=[pltpu.VMEM((d_model, COL_TILE), BF16)],
        compiler_params=_params(("arbitrary", "arbitrary")),
    )(xb, w_in, tabs)


def _proj_b(xb, w_in, tabs, gq, gk, seq):
    n_tok, d_model = xb.shape
    q_tiles = B_Q_HEADS * HEAD_DIM // COL_TILE
    heads = COL_TILE // HEAD_DIM
    fix = lambda j, i: (0, 0)
    return pl.pallas_call(
        functools.partial(_proj_b_kernel, q_tiles=q_tiles),
        out_shape=jax.ShapeDtypeStruct(((q_tiles + 1) * heads, n_tok, HEAD_DIM), BF16),
        grid=(q_tiles + 1, n_tok // ROW_TILE),
        in_specs=_proj_specs(d_model, seq, _T_QB, 1, _TAB_AXIAL) + [pl.BlockSpec((1, HEAD_DIM), fix), pl.BlockSpec((1, HEAD_DIM), fix)],
        out_specs=pl.BlockSpec((heads, ROW_TILE, HEAD_DIM), lambda j, i: (j, i, 0)),
        scratch_shapes=[pltpu.VMEM((d_model, COL_TILE), BF16)],
        compiler_params=_params(("arbitrary", "arbitrary")),
    )(xb, w_in, tabs, gq, gk)


def _proj_gate(xb, w_in):
    n_tok, d_model = xb.shape
    n_ct = w_in.shape[1] // COL_TILE - _T_GATE
    return pl.pallas_call(
        _proj_gate_kernel,
        out_shape=jax.ShapeDtypeStruct((n_tok, n_ct * COL_TILE), BF16),
        grid=(n_ct, n_tok // ROW_TILE),
        in_specs=[pl.BlockSpec((ROW_TILE, d_model), lambda j, i: (i, 0)),
                  pl.BlockSpec((d_model, COL_TILE), lambda j, i: (0, _T_GATE + j))],
        out_specs=pl.BlockSpec((ROW_TILE, COL_TILE), lambda j, i: (i, j)),
        scratch_shapes=[pltpu.VMEM((d_model, COL_TILE), BF16)],
        compiler_params=_params(("arbitrary", "arbitrary")),
    )(xb, w_in)


_SUB = 128
_WIN = _SUB + 2 * BAND_HALF
_MERGE_ROWS = 256
_A_UNROLL = 16


def _attn_a_group(g, dil, seq, q_ref, k_ref, v_ref, og, lg, bias):
    sub_len = seq // dil
    per_seq = sub_len // _SUB
    shift = per_seq.bit_length() - 1
    ones = jnp.ones((_WIN, HEAD_DIM), BF16)

    def body(i, carry):
        r = lax.shift_right_logical(i, shift)
        p0 = pl.multiple_of((i & (per_seq - 1)) * _SUB, _SUB)
        start = pl.multiple_of(jnp.clip(p0 - BAND_HALF, 0, sub_len - _WIN), BAND_HALF)
        q = q_ref[r, pl.ds(p0, _SUB), :]
        k = k_ref[r, pl.ds(start, _WIN), :]
        v = v_ref[r, pl.ds(start, _WIN), :]
        s = lax.dot_general(q, k, (((1,), (1,)), ((), ())), preferred_element_type=F32)
        s = s + bias[lax.shift_right_logical(p0 - start, BAND_HALF.bit_length() - 1)]
        m = jnp.max(s, axis=1, keepdims=True)
        p = jnp.exp2(s - m).astype(BF16)
        both = jnp.dot(p, jnp.concatenate([v, ones], axis=1), preferred_element_type=F32)
        l = both[:, HEAD_DIM:]
        o = both[:, :HEAD_DIM] / l
        lse = m + jnp.log2(l)
        if dil == 1:
            rows = pl.ds(p0, _SUB)
        else:
            rows = pl.ds(p0 * dil + r, _SUB, stride=dil)
        og[g, rows, :] = o
        lg[g, rows, :] = lse
        return carry

    lax.fori_loop(0, seq // _SUB, body, 0, unroll=_A_UNROLL)


def _attn_a_kernel(*refs, seq):
    qkv, o_ref = refs[:3 * N_DIL], refs[3 * N_DIL]
    og, lg, bias = refs[3 * N_DIL + 1:]
    row = lax.broadcasted_iota(I32, (_SUB, _WIN), 0)
    col = lax.broadcasted_iota(I32, (_SUB, _WIN), 1)
    for case in range(3):
        bias[case] = jnp.where(jnp.abs(col - row - case * BAND_HALF) <= BAND_HALF, 0.0, NEG_BIG)
    for g, (_, dil) in enumerate(DIL_PATTERNS):
        _attn_a_group(g, dil, seq, qkv[3 * g], qkv[3 * g + 1], qkv[3 * g + 2], og, lg, bias)

    def merge(c, carry):
        rows = pl.ds(pl.multiple_of(c * _MERGE_ROWS, _MERGE_ROWS), _MERGE_ROWS)
        ls = [lg[g, rows, :] for g in range(N_DIL)]
        mx = functools.reduce(jnp.maximum, ls)
        ws = [jnp.exp2(l - mx) for l in ls]
        num = functools.reduce(lambda a, b: a + b, [w * og[g, rows, :] for g, w in enumerate(ws)])
        o_ref[rows, :] = (num / functools.reduce(lambda a, b: a + b, ws)).astype(o_ref.dtype)
        return carry

    lax.fori_loop(0, seq // _MERGE_ROWS, merge, 0)


def _attn_a(groups, batch, seq):
    in_specs, operands = [], []
    for g, (_, dil) in enumerate(DIL_PATTERNS):
        for kind in range(3):
            first = kind * A_HEADS
            in_specs.append(pl.BlockSpec((None, None, dil, seq // dil, HEAD_DIM),
                                         lambda b, h, first=first: (first + h, b, 0, 0, 0)))
            operands.append(groups[g])
    return pl.pallas_call(
        functools.partial(_attn_a_kernel, seq=seq),
        out_shape=jax.ShapeDtypeStruct((batch * seq, A_HEADS * HEAD_DIM), BF16),
        grid=(batch, A_HEADS),
        in_specs=in_specs,
        out_specs=pl.BlockSpec((seq, HEAD_DIM), lambda b, h: (b, h)),
        scratch_shapes=[pltpu.VMEM((N_DIL, seq, HEAD_DIM), F32), pltpu.VMEM((N_DIL, seq, HEAD_DIM), F32),
                        pltpu.VMEM((3, _SUB, _WIN), F32)],
        compiler_params=_params(("arbitrary", "arbitrary")),
    )(*operands)


_BQ = 1024
_BK = 512
SCORE_BOUND = 64.0
BF16_SLACK = 1.02


def _attn_b_kernel(q_ref, k_ref, v_ref, o_ref, *, seq):
    rows = B_GROUP * _BQ
    q = q_ref[...].reshape(rows, HEAD_DIM)

    def body(c, carry):
        m, l, acc = carry
        c0 = pl.multiple_of(c * _BK, _BK)
        k = k_ref[pl.ds(c0, _BK), :]
        v = v_ref[pl.ds(c0, _BK), :]
        s = lax.dot_general(q, k, (((1,), (1,)), ((), ())), preferred_element_type=F32)
        m_new = jnp.maximum(m, jnp.max(s, axis=1, keepdims=True))
        alpha = jnp.exp2(m - m_new)
        p = jnp.exp2(s - m_new)
        l = alpha * l + jnp.sum(p, axis=1, keepdims=True)
        acc = alpha * acc + jnp.dot(p.astype(BF16), v, preferred_element_type=F32)
        return m_new, l, acc

    init = (jnp.full((rows, 1), NEG_BIG, F32), jnp.zeros((rows, 1), F32), jnp.zeros((rows, HEAD_DIM), F32))
    _, l, acc = lax.fori_loop(0, seq // _BK, body, init)
    o = acc / l
    for g in range(B_GROUP):
        o_ref[:, g * HEAD_DIM:(g + 1) * HEAD_DIM] = o[g * _BQ:(g + 1) * _BQ].astype(o_ref.dtype)


def _attn_b_bounded_kernel(q_ref, k_ref, v_ref, o_ref, *, seq):
    rows = B_GROUP * _BQ
    q = q_ref[...].reshape(rows, HEAD_DIM)
    ones = jnp.ones((_BK, HEAD_DIM), BF16)
    acc = jnp.zeros((rows, 2 * HEAD_DIM), F32)
    for c in range(seq // _BK):
        k = k_ref[c * _BK:(c + 1) * _BK, :]
        v = v_ref[c * _BK:(c + 1) * _BK, :]
        s = lax.dot_general(q, k, (((1,), (1,)), ((), ())), preferred_element_type=F32)
        p = jnp.exp2(s).astype(BF16)
        acc = acc + jnp.dot(p, jnp.concatenate([v, ones], axis=1), preferred_element_type=F32)
    o = acc[:, :HEAD_DIM] / acc[:, HEAD_DIM:HEAD_DIM + 1]
    for g in range(B_GROUP):
        o_ref[:, g * HEAD_DIM:(g + 1) * HEAD_DIM] = o[g * _BQ:(g + 1) * _BQ].astype(o_ref.dtype)


def _attn_b(slots, batch, seq, bounded):
    n_tok = batch * seq
    k0 = B_Q_HEADS
    v0 = k0 + B_KV_HEADS
    qblocks = seq // _BQ
    return pl.pallas_call(
        functools.partial(_attn_b_bounded_kernel if bounded else _attn_b_kernel, seq=seq),
        out_shape=jax.ShapeDtypeStruct((n_tok, B_Q_HEADS * HEAD_DIM), BF16),
        grid=(batch, B_KV_HEADS, qblocks),
        in_specs=[
            pl.BlockSpec((B_GROUP, _BQ, HEAD_DIM), lambda b, kv, qi: (kv, b * qblocks + qi, 0)),
            pl.BlockSpec((None, seq, HEAD_DIM), lambda b, kv, qi: (k0 + kv, b, 0)),
            pl.BlockSpec((None, seq, HEAD_DIM), lambda b, kv, qi: (v0 + kv, b, 0)),
        ],
        out_specs=pl.BlockSpec((_BQ, B_GROUP * HEAD_DIM), lambda b, kv, qi: (b * qblocks + qi, kv)),
        compiler_params=_params(("arbitrary", "arbitrary", "arbitrary")),
    )(slots, slots, slots)


_MIX_ROWS = 512


def _branch_kernel(oa_ref, ob_ref, ga_ref, gb_ref, wa_ref, wb_ref, out_ref):
    for rt in range(out_ref.shape[0] // SUB_ROWS):
        rows = slice(rt * SUB_ROWS, (rt + 1) * SUB_ROWS)
        y_a = jnp.dot(oa_ref[rows, :], wa_ref[...], preferred_element_type=F32)
        y_b = jnp.dot(ob_ref[rows, :], wb_ref[...], preferred_element_type=F32)
        out_ref[rows, :] = (ga_ref[rows, :].astype(F32) * y_a + gb_ref[rows, :].astype(F32) * y_b).astype(BF16)


def _branch_mix(o_a, o_b, gates, wa, wb):
    n_tok = o_b.shape[0]
    d_model = wa.shape[1]
    tm = _MIX_ROWS
    row = lambda i: (i, 0)
    return pl.pallas_call(
        _branch_kernel,
        out_shape=jax.ShapeDtypeStruct((n_tok, d_model), BF16),
        grid=(n_tok // tm,),
        in_specs=[pl.BlockSpec((tm, o_a.shape[1]), row),
                  pl.BlockSpec((tm, o_b.shape[1]), row),
                  pl.BlockSpec((tm, d_model), lambda i: (i, 0)),
                  pl.BlockSpec((tm, d_model), lambda i: (i, 1)),
                  pl.BlockSpec(wa.shape, lambda i: (0, 0)),
                  pl.BlockSpec(wb.shape, lambda i: (0, 0))],
        out_specs=pl.BlockSpec((tm, d_model), row),
        compiler_params=_params(("arbitrary",)),
    )(o_a, o_b, gates, gates, wa, wb)


def _layer_norm(z, g, b):
    mu = jnp.mean(z, axis=-1, keepdims=True)
    zc = z - mu
    var = jnp.mean(zc * zc, axis=-1, keepdims=True)
    return zc * lax.rsqrt(var + LN_EPS) * g + b


def _to_slabs(slab_ref, value, row0):
    rows, d = value.shape
    chunks = d // LANES
    for c in range(chunks):
        slab_ref[pl.ds(row0 * chunks + c, rows, stride=chunks), :] = value[:, c * LANES:(c + 1) * LANES]


def _from_slabs(slab_ref, rows, chunks, c):
    return slab_ref[pl.ds(c, rows, stride=chunks), :]


def _out_proj_kernel(m_ref, w_ref, x_ref, g_ref, b_ref, rcat_ref, rb_ref, hs_ref, lg_ref, *, alpha):
    for rt in range(m_ref.shape[0] // SUB_ROWS):
        rows = slice(rt * SUB_ROWS, (rt + 1) * SUB_ROWS)
        mix = jnp.dot(m_ref[rows, :], w_ref[...], preferred_element_type=F32)
        h = _layer_norm(alpha * x_ref[rows, :] + mix, g_ref[...], b_ref[...])
        _to_slabs(hs_ref, h, rt * SUB_ROWS)
        hi = h.astype(BF16)
        lo = (h - hi.astype(F32)).astype(BF16)
        both = jnp.dot(hi, rcat_ref[...], preferred_element_type=F32)
        lg = both[:, :LANES] + both[:, LANES:] + jnp.dot(lo, rcat_ref[:, :LANES], preferred_element_type=F32)
        lg_ref[rows, :] = lg + rb_ref[...]


def _out_proj(merged, w_out, x2, g, b, r_cat, r_b, alpha):
    n_tok, d_model = x2.shape
    tm = _MIX_ROWS
    chunks = d_model // LANES
    row = lambda i: (i, 0)
    fix = lambda i: (0, 0)
    return pl.pallas_call(
        functools.partial(_out_proj_kernel, alpha=alpha),
        out_shape=(jax.ShapeDtypeStruct((n_tok * chunks, LANES), F32),
                   jax.ShapeDtypeStruct((n_tok, LANES), F32)),
        grid=(n_tok // tm,),
        in_specs=[pl.BlockSpec((tm, d_model), row),
                  pl.BlockSpec(w_out.shape, fix),
                  pl.BlockSpec((tm, d_model), row),
                  pl.BlockSpec((1, d_model), fix),
                  pl.BlockSpec((1, d_model), fix),
                  pl.BlockSpec(r_cat.shape, fix),
                  pl.BlockSpec((1, LANES), fix)],
        out_specs=(pl.BlockSpec((tm * chunks, LANES), row), pl.BlockSpec((tm, LANES), row)),
        compiler_params=_params(("arbitrary",)),
    )(merged, w_out, x2, g, b, r_cat, r_b)


_BIG_CHUNK = 512


def _moe_kernel(item_expert, item_row0, item_blocks, item_real, src_slab, dst_slab,
                h_hbm, wg_ref, wu_ref, wd_ref, y_hbm,
                stage, xbf, acc, obuf, gsem, ssem):
    it = pl.program_id(0)
    f = pl.program_id(1)
    n_items = pl.num_programs(0)
    n_ff = pl.num_programs(1)
    n_blocks = item_blocks[it]
    d_model = xbf.shape[1]
    chunks = d_model // LANES

    def slab(ref, first):
        return ref.at[pl.ds(pl.multiple_of(first, chunks), chunks)]

    def gather_copy(base, r):
        return pltpu.make_async_copy(slab(h_hbm, src_slab[base + r]), slab(stage, r * chunks), gsem.at[0])

    def scatter_copy(base, r):
        return pltpu.make_async_copy(slab(obuf, r * chunks), slab(y_hbm, dst_slab[base + r]), ssem.at[0])

    def for_rows(n_rows, fn):
        def group(t, c):
            for u in range(DMA_UNROLL):
                fn(t * DMA_UNROLL + u)
            return c
        n_groups = n_rows // DMA_UNROLL
        lax.fori_loop(0, n_groups, group, 0)

        def single(r, c):
            fn(r)
            return c
        lax.fori_loop(n_groups * DMA_UNROLL, n_rows, single, 0)

    def gather(item, op):
        base = item_row0[item]
        for_rows(item_blocks[item] * MOE_BLOCK, lambda r: op(gather_copy(base, r)))

    def scatter(item, op):
        base = item_row0[item]
        for_rows(item_real[item], lambda r: op(scatter_copy(base, r)))

    start = lambda cp: cp.start()
    wait = lambda cp: cp.wait()

    @pl.when((it == 0) & (f == 0))
    def _first_gather():
        gather(it, start)

    @pl.when(f == 0)
    def _gathered():
        gather(it, wait)

        def cast_rows(b, carry):
            r0 = pl.multiple_of(b * MOE_BLOCK, MOE_BLOCK)
            for c in range(chunks):
                piece = stage[pl.ds(r0 * chunks + c, MOE_BLOCK, stride=chunks), :]
                xbf[pl.ds(r0, MOE_BLOCK), c * LANES:(c + 1) * LANES] = piece.astype(BF16)
            return carry
        lax.fori_loop(0, n_blocks, cast_rows, 0)

        @pl.when(it + 1 < n_items)
        def _prefetch_next():
            gather(it + 1, start)

    def chunk(r0, rows, mode):
        xb = xbf[pl.ds(r0, rows), :]
        a = jnp.dot(xb, wg_ref[...].astype(BF16), preferred_element_type=F32)
        u = jnp.dot(xb, wu_ref[...].astype(BF16), preferred_element_type=F32)
        hid = (a / (1.0 + jnp.exp(-a)) * u).astype(BF16)
        y = jnp.dot(hid, wd_ref[...].astype(BF16), preferred_element_type=F32)
        if mode == "first":
            acc[pl.ds(r0, rows), :] = y
        elif mode == "middle":
            acc[pl.ds(r0, rows), :] += y
        else:
            if mode == "last":
                y = acc[pl.ds(r0, rows), :] + y
            for c in range(chunks):
                obuf[pl.ds(r0 * chunks + c, rows, stride=chunks), :] = y[:, c * LANES:(c + 1) * LANES]

    def run_item(mode):
        per_big = _BIG_CHUNK // MOE_BLOCK

        def big_chunk(t, c):
            chunk(pl.multiple_of(t * _BIG_CHUNK, _BIG_CHUNK), _BIG_CHUNK, mode)
            return c
        n_big = n_blocks // per_big
        lax.fori_loop(0, n_big, big_chunk, 0)

        def small_chunk(b, c):
            chunk(pl.multiple_of(b * MOE_BLOCK, MOE_BLOCK), MOE_BLOCK, mode)
            return c
        lax.fori_loop(n_big * per_big, n_blocks, small_chunk, 0)

    last = n_ff - 1

    @pl.when((n_blocks > 0) & (f == last))
    def _final():
        @pl.when(it > 0)
        def _drain_prev():
            scatter(it - 1, wait)
        run_item("only" if n_ff == 1 else "last")
        scatter(it, start)

    if n_ff > 1:
        @pl.when((n_blocks > 0) & (f == 0))
        def _first():
            run_item("first")

    if n_ff > 2:
        @pl.when((n_blocks > 0) & (f > 0) & (f < last))
        def _middle():
            run_item("middle")

    @pl.when((f == last) & (it == n_items - 1))
    def _drain_last():
        scatter(jnp.where(n_blocks > 0, it, _last_live(item_blocks, n_items)), wait)


def _last_live(item_blocks, n_items):
    def body(i, best):
        return jnp.where(item_blocks[i] > 0, i, best)
    return lax.fori_loop(0, n_items, body, 0)


def _moe(h_slabs, w_gate, w_up, w_down, item_expert, item_row0, item_blocks, item_real, src_slab, dst_slab):
    d_model = w_gate.shape[1]
    chunks = d_model // LANES
    n_tok = h_slabs.shape[0] // chunks
    ff = w_gate.shape[2]
    n_ff = ff // FF_TILE
    n_items = item_expert.shape[0]

    def ff_idx(it, f, blocks):
        return jnp.where(blocks[it] > 0, f, n_ff - 1)

    grid_spec = pltpu.PrefetchScalarGridSpec(
        num_scalar_prefetch=6,
        grid=(n_items, n_ff),
        in_specs=[
            pl.BlockSpec(memory_space=pl.ANY),
            pl.BlockSpec((None, d_model, FF_TILE), lambda it, f, ie, ir, ib, nr, st, dr: (ie[it], 0, ff_idx(it, f, ib))),
            pl.BlockSpec((None, d_model, FF_TILE), lambda it, f, ie, ir, ib, nr, st, dr: (ie[it], 0, ff_idx(it, f, ib))),
            pl.BlockSpec((None, FF_TILE, d_model), lambda it, f, ie, ir, ib, nr, st, dr: (ie[it], ff_idx(it, f, ib), 0)),
        ],
        out_specs=pl.BlockSpec(memory_space=pl.ANY),
        scratch_shapes=[
            pltpu.VMEM((ITEM_ROWS * chunks, LANES), F32),
            pltpu.VMEM((ITEM_ROWS, d_model), BF16),
            pltpu.VMEM((ITEM_ROWS, d_model), F32),
            pltpu.VMEM((ITEM_ROWS * chunks, LANES), F32),
            pltpu.SemaphoreType.DMA((1,)),
            pltpu.SemaphoreType.DMA((1,)),
        ],
    )
    return pl.pallas_call(
        _moe_kernel,
        out_shape=jax.ShapeDtypeStruct((n_tok * TOP_K * chunks, LANES), F32),
        grid_spec=grid_spec,
        compiler_params=_params(("arbitrary", "arbitrary")),
    )(item_expert, item_row0, item_blocks, item_real, src_slab, dst_slab, h_slabs, w_gate, w_up, w_down)


_COMBINE_ROWS = 512


def _combine_kernel(y0_ref, y1_ref, wt_ref, hs_ref, g_ref, b_ref, out_ref, *, alpha):
    rows, d_model = out_ref.shape
    chunks = d_model // LANES
    wt = wt_ref[...]
    w0, w1 = wt[:, 0:1], wt[:, 1:2]
    for c in range(chunks):
        cols = slice(c * LANES, (c + 1) * LANES)
        ffn = w0 * _from_slabs(y0_ref, rows, chunks, c) + w1 * _from_slabs(y1_ref, rows, chunks, c)
        out_ref[:, cols] = alpha * _from_slabs(hs_ref, rows, chunks, c) + ffn
    out_ref[...] = _layer_norm(out_ref[...], g_ref[...], b_ref[...])


def _combine(y_slabs, weights, h_slabs, g, b, alpha):
    d_model = g.shape[1]
    n_tok = h_slabs.shape[0] * LANES // d_model
    tm = _COMBINE_ROWS
    chunks = d_model // LANES
    second = n_tok // tm
    row = lambda i: (i, 0)
    fix = lambda i: (0, 0)
    return pl.pallas_call(
        functools.partial(_combine_kernel, alpha=alpha),
        out_shape=jax.ShapeDtypeStruct((n_tok, d_model), F32),
        grid=(n_tok // tm,),
        in_specs=[pl.BlockSpec((tm * chunks, LANES), row),
                  pl.BlockSpec((tm * chunks, LANES), lambda i: (second + i, 0)),
                  pl.BlockSpec((tm, TOP_K), row),
                  pl.BlockSpec((tm * chunks, LANES), row),
                  pl.BlockSpec((1, d_model), fix),
                  pl.BlockSpec((1, d_model), fix)],
        out_specs=pl.BlockSpec((tm, d_model), row),
        compiler_params=_params(("arbitrary",)),
    )(y_slabs, y_slabs, weights, h_slabs, g, b)


def _rope_tables(seq):
    f32 = np.float32
    half = HEAD_DIM // 2
    inv1 = np.power(f32(ROPE_THETA), -(np.arange(half, dtype=f32) / f32(half)))
    ang1 = np.arange(seq, dtype=f32)[:, None] * inv1[None, :]
    t = np.arange(seq)
    n_axis = HEAD_DIM // 4
    inv2 = np.power(f32(ROPE_THETA), -(np.arange(n_axis, dtype=f32) / f32(n_axis)))
    ang2 = np.concatenate([(t // GRID_W).astype(f32)[:, None] * inv2[None, :],
                           (t % GRID_W).astype(f32)[:, None] * inv2[None, :]], axis=-1)
    q_scale = f32(HEAD_DIM ** -0.5 * LOG2E)
    out = []
    for ang in (ang1, ang2):
        cos = np.concatenate([np.cos(ang), np.cos(ang)], axis=-1)
        sin = np.concatenate([-np.sin(ang), np.sin(ang)], axis=-1)
        out += [cos * q_scale, sin * q_scale, cos, sin]
    return jnp.asarray(np.stack(out, axis=0).astype(f32))


def _route(logits, n_tok, chunks):
    g_logits = logits[:, :N_GROUPS]
    g_prob = jax.nn.softmax(g_logits, axis=-1)
    g_idx = jnp.argmax(g_logits, axis=-1).astype(I32)
    g_gate = jnp.max(g_prob, axis=-1)
    e_logits = logits[:, N_GROUPS:N_GROUPS + EXPERTS_PER_GROUP]
    for g in range(1, N_GROUPS):
        lo = N_GROUPS + g * EXPERTS_PER_GROUP
        e_logits = jnp.where(g_idx[:, None] == g, logits[:, lo:lo + EXPERTS_PER_GROUP], e_logits)
    e_prob = jax.nn.softmax(e_logits, axis=-1)
    i1 = jnp.argmax(e_prob, axis=-1).astype(I32)
    p1 = jnp.max(e_prob, axis=-1)
    rest = jnp.where(jnp.arange(EXPERTS_PER_GROUP, dtype=I32)[None, :] == i1[:, None], -1.0, e_prob)
    i2 = jnp.argmax(rest, axis=-1).astype(I32)
    p2 = jnp.max(rest, axis=-1)
    top_p = jnp.stack([p1, p2], axis=-1)
    top_p = top_p / jnp.sum(top_p, axis=-1, keepdims=True)
    weights = g_gate[:, None] * top_p
    expert = g_idx[:, None] * EXPERTS_PER_GROUP + jnp.stack([i1, i2], axis=-1)

    n_slot = n_tok * TOP_K
    e_flat = expert.reshape(n_slot).astype(I32)
    order = jnp.argsort(e_flat).astype(I32)
    experts = jnp.arange(N_EXPERTS, dtype=I32)
    counts = jnp.sum(e_flat[:, None] == experts[None, :], axis=0, dtype=I32)
    padded = ((counts + MOE_BLOCK - 1) // MOE_BLOCK) * MOE_BLOCK
    p_ends = jnp.cumsum(padded)
    p_starts = p_ends - padded
    buf_len = n_slot + N_EXPERTS * MOE_BLOCK

    def pick(one_hot, table):
        return jnp.sum(jnp.where(one_hot, table[None, :], 0), axis=1, dtype=I32)

    pos = jnp.arange(buf_len, dtype=I32)[:, None]
    ended = p_ends[None, :] <= pos
    inside = jnp.logical_and(p_starts[None, :] <= pos, pos < p_ends[None, :])
    idx = pos[:, 0] - jnp.max(jnp.where(ended, p_ends[None, :], 0), axis=1)
    sorted_at = jnp.clip(pick(ended, counts) + idx, 0, n_slot - 1)
    slot_at = jnp.where(idx < pick(inside, counts), order[sorted_at], 0)
    tok_at, k_at = slot_at // TOP_K, slot_at % TOP_K
    src_slab = tok_at * chunks
    dst_slab = (k_at * n_tok + tok_at) * chunks

    n_items = N_EXPERTS + n_slot // ITEM_ROWS
    per_expert = (padded + ITEM_ROWS - 1) // ITEM_ROWS
    item_ends = jnp.cumsum(per_expert)
    total = item_ends[-1]
    ids = jnp.arange(n_items, dtype=I32)
    live = ids < total
    item = jnp.minimum(ids, jnp.maximum(total - 1, 0))[:, None]
    owner = jnp.logical_and((item_ends - per_expert)[None, :] <= item, item < item_ends[None, :])
    e_of = pick(owner, experts)
    chunk = item[:, 0] - pick(owner, item_ends - per_expert)
    item_row0 = jnp.where(live, pick(owner, p_starts) + chunk * ITEM_ROWS, 0).astype(I32)
    item_blocks = jnp.where(live, jnp.clip(pick(owner, padded) // MOE_BLOCK - chunk * ITEM_BLOCKS, 0, ITEM_BLOCKS), 0)
    item_real = jnp.where(live, jnp.clip(pick(owner, counts) - chunk * ITEM_ROWS, 0, ITEM_ROWS), 0)
    return weights, e_of, item_row0, item_blocks.astype(I32), item_real.astype(I32), src_slab, dst_slab


def kernel(x, w_in, q_norm_g, k_norm_g, w_branch_a, w_branch_b, w_out, ln1_g, ln1_b, w_group, b_group, w_router,
           b_router, w_gate, w_up, w_down, ln2_g, ln2_b):
    batch, seq, d_model = x.shape
    depth = w_in.shape[0]
    n_tok = batch * seq
    dn_alpha = (2 * depth) ** 0.25
    tabs = _rope_tables(seq)
    h = x.reshape(n_tok, d_model)
    for layer in range(depth):
        xb = h.astype(BF16)
        groups = [_proj_a(xb, w_in[layer], tabs, g, batch, seq) for g in range(N_DIL)]
        slots_b = _proj_b(xb, w_in[layer], tabs, q_norm_g[layer][None, :], k_norm_g[layer][None, :], seq)
        gates = _proj_gate(xb, w_in[layer])
        o_a = _attn_a(groups, batch, seq)
        score_bound = (jnp.max(jnp.abs(q_norm_g[layer])) * jnp.max(jnp.abs(k_norm_g[layer]))
                       * (HEAD_DIM ** 0.5 * LOG2E * BF16_SLACK))
        o_b = lax.cond(score_bound <= SCORE_BOUND,
                       lambda s: _attn_b(s, batch, seq, True), lambda s: _attn_b(s, batch, seq, False), slots_b)
        merged = _branch_mix(o_a, o_b, gates, w_branch_a[layer].astype(BF16), w_branch_b[layer].astype(BF16))
        w_r = jnp.concatenate([w_group[layer], w_router[layer]], axis=1)
        w_r = jnp.pad(w_r, ((0, 0), (0, LANES - w_r.shape[1])))
        r_hi = w_r.astype(BF16)
        r_lo = (w_r - r_hi.astype(F32)).astype(BF16)
        r_b = jnp.pad(jnp.concatenate([b_group[layer], b_router[layer]]), (0, LANES - N_GROUPS - N_EXPERTS))[None, :]
        h1_slabs, logits = _out_proj(merged, w_out[layer].astype(BF16), h, ln1_g[layer][None, :],
                                         ln1_b[layer][None, :], jnp.concatenate([r_hi, r_lo], axis=1), r_b, dn_alpha)
        weights, item_expert, item_row0, item_blocks, item_real, src_slab, dst_slab = _route(
            logits, n_tok, d_model // LANES)
        y_slabs = _moe(h1_slabs, w_gate[layer], w_up[layer], w_down[layer], item_expert, item_row0, item_blocks,
                       item_real, src_slab, dst_slab)
        h = _combine(y_slabs, weights, h1_slabs, ln2_g[layer][None, :], ln2_b[layer][None, :], dn_alpha)
    return h.reshape(batch, seq, d_model)
```

```python
import functools
import math

import jax
import jax.numpy as jnp
import numpy as np
from jax import lax
from jax.experimental import pallas as pl
from jax.experimental.pallas import tpu as pltpu

F32 = jnp.float32
BF16 = jnp.bfloat16
I32 = jnp.int32

HEAD_DIM = 128
ROPE_THETA = 10000.0
GRID_W = 64
DIL_PATTERNS = ((128, 1), (512, 4), (2048, 16))
N_DIL = len(DIL_PATTERNS)
A_HEADS = 8
BAND_HALF = 64
B_Q_HEADS = 16
B_KV_HEADS = 4
B_GROUP = B_Q_HEADS // B_KV_HEADS
N_GROUPS = 4
EXPERTS_PER_GROUP = 8
N_EXPERTS = N_GROUPS * EXPERTS_PER_GROUP
TOP_K = 2
MOE_BLOCK = 128
LN_EPS = 1e-5
RMS_EPS = 1e-6
NEG_BIG = -1e30
LOG2E = math.log2(math.e)

LANES = 128
COL_TILE = 8 * HEAD_DIM
ROW_TILE = 1024
SUB_ROWS = 256
ITEM_ROWS = 768
ITEM_BLOCKS = ITEM_ROWS // MOE_BLOCK
FF_TILE = 512
DMA_UNROLL = 8
VMEM_LIMIT = 56 * 1024 * 1024


def _params(sem, vmem=VMEM_LIMIT):
    return pltpu.CompilerParams(dimension_semantics=sem, vmem_limit_bytes=vmem)


_T_QA, _T_KA, _T_VA, _T_QB, _T_KVB, _T_GATE = 0, 3, 6, 9, 11, 12
_TAB_1D, _TAB_AXIAL = 0, 1
_RQ, _RK = 0, 2


def _cast_weights_once(w_ref, wbf_ref):
    @pl.when(pl.program_id(1) == 0)
    def _cast():
        wbf_ref[...] = w_ref[...].astype(BF16)


def _for_sub_tiles(x_ref, wbf_ref, emit):
    for rt in range(x_ref.shape[0] // SUB_ROWS):
        rows = slice(rt * SUB_ROWS, (rt + 1) * SUB_ROWS)
        emit(rt, rows, jnp.dot(x_ref[rows, :], wbf_ref[...], preferred_element_type=F32))


def _head(acc, h):
    return acc[:, h * HEAD_DIM:(h + 1) * HEAD_DIM]


def _rope(a, tab_ref, t, rows):
    return a * tab_ref[t, rows, :] + pltpu.roll(a, HEAD_DIM // 2, 1) * tab_ref[t + 1, rows, :]


def _rms(a, g_ref):
    ms = jnp.mean(a * a, axis=-1, keepdims=True)
    return a * lax.rsqrt(ms + RMS_EPS) * g_ref[...]


def _proj_a_kernel(x_ref, w_ref, tab_ref, out_ref, wbf_ref, *, dil):
    j = pl.program_id(0)
    _cast_weights_once(w_ref, wbf_ref)
    per_res = SUB_ROWS // dil

    def emit_with(fn):
        def emit(rt, rows, acc):
            if dil == 1:
                for h in range(A_HEADS):
                    out_ref[h, 0, rows, :] = fn(_head(acc, h), rows).astype(BF16)
                return
            for h in range(A_HEADS):
                by_res = fn(_head(acc, h), rows).reshape(per_res, dil, HEAD_DIM).swapaxes(0, 1)
                for r in range(dil):
                    out_ref[h, r, rt * per_res:(rt + 1) * per_res, :] = by_res[r].astype(BF16)
        _for_sub_tiles(x_ref, wbf_ref, emit)

    @pl.when(j == 0)
    def _q():
        emit_with(lambda a, rows: _rope(a, tab_ref, _RQ, rows))

    @pl.when(j == 1)
    def _k():
        emit_with(lambda a, rows: _rope(a, tab_ref, _RK, rows))

    @pl.when(j == 2)
    def _v():
        emit_with(lambda a, rows: a)


def _proj_b_kernel(x_ref, w_ref, tab_ref, gq_ref, gk_ref, out_ref, wbf_ref, *, q_tiles):
    j = pl.program_id(0)
    _cast_weights_once(w_ref, wbf_ref)
    heads = COL_TILE // HEAD_DIM

    @pl.when(j < q_tiles)
    def _qb():
        def emit(rt, rows, acc):
            for h in range(heads):
                out_ref[h, rows, :] = _rope(_rms(_head(acc, h), gq_ref), tab_ref, _RQ, rows).astype(BF16)
        _for_sub_tiles(x_ref, wbf_ref, emit)

    @pl.when(j == q_tiles)
    def _kvb():
        def emit(rt, rows, acc):
            for h in range(heads):
                if h < B_KV_HEADS:
                    out_ref[h, rows, :] = _rope(_rms(_head(acc, h), gk_ref), tab_ref, _RK, rows).astype(BF16)
                else:
                    out_ref[h, rows, :] = _head(acc, h).astype(BF16)
        _for_sub_tiles(x_ref, wbf_ref, emit)


def _proj_gate_kernel(x_ref, w_ref, out_ref, wbf_ref):
    _cast_weights_once(w_ref, wbf_ref)

    def emit(rt, rows, acc):
        out_ref[rows, :] = (1.0 / (1.0 + jnp.exp(-acc))).astype(BF16)
    _for_sub_tiles(x_ref, wbf_ref, emit)


def _proj_specs(d_model, seq, first_tile, tile_step, table_half):
    seq_blocks = seq // ROW_TILE
    return [
        pl.BlockSpec((ROW_TILE, d_model), lambda j, i: (i, 0)),
        pl.BlockSpec((d_model, COL_TILE), lambda j, i: (0, first_tile + tile_step * j)),
        pl.BlockSpec((4, ROW_TILE, HEAD_DIM), lambda j, i: (table_half, i % seq_blocks, 0)),
    ]


def _proj_a(xb, w_in, tabs, group, batch, seq):
    n_tok, d_model = xb.shape
    dil = DIL_PATTERNS[group][1]
    sub_len = seq // dil
    seq_blocks = seq // ROW_TILE
    return pl.pallas_call(
        functools.partial(_proj_a_kernel, dil=dil),
        out_shape=jax.ShapeDtypeStruct((3 * A_HEADS, batch, dil, sub_len, HEAD_DIM), BF16),
        grid=(3, n_tok // ROW_TILE),
        in_specs=_proj_specs(d_model, seq, group, N_DIL, _TAB_1D),
        out_specs=pl.BlockSpec((A_HEADS, None, dil, ROW_TILE // dil, HEAD_DIM),
                               lambda j, i: (j, i // seq_blocks, 0, i % seq_blocks, 0)),
        scratch_shapes=[pltpu.VMEM((d_model, COL_TILE), BF16)],
        compiler_params=_params(("arbitrary", "arbitrary")),
    )(xb, w_in, tabs)


def _proj_b(xb, w_in, tabs, gq, gk, seq):
    n_tok, d_model = xb.shape
    q_tiles = B_Q_HEADS * HEAD_DIM // COL_TILE
    heads = COL_TILE // HEAD_DIM
    fix = lambda j, i: (0, 0)
    return pl.pallas_call(
        functools.partial(_proj_b_kernel, q_tiles=q_tiles),
        out_shape=jax.ShapeDtypeStruct(((q_tiles + 1) * heads, n_tok, HEAD_DIM), BF16),
        grid=(q_tiles + 1, n_tok // ROW_TILE),
        in_specs=_proj_specs(d_model, seq, _T_QB, 1, _TAB_AXIAL) + [pl.BlockSpec((1, HEAD_DIM), fix), pl.BlockSpec((1, HEAD_DIM), fix)],
        out_specs=pl.BlockSpec((heads, ROW_TILE, HEAD_DIM), lambda j, i: (j, i, 0)),
        scratch_shapes=[pltpu.VMEM((d_model, COL_TILE), BF16)],
        compiler_params=_params(("arbitrary", "arbitrary")),
    )(xb, w_in, tabs, gq, gk)


def _proj_gate(xb, w_in):
    n_tok, d_model = xb.shape
    n_ct = w_in.shape[1] // COL_TILE - _T_GATE
    return pl.pallas_call(
        _proj_gate_kernel,
        out_shape=jax.ShapeDtypeStruct((n_tok, n_ct * COL_TILE), BF16),
        grid=(n_ct, n_tok // ROW_TILE),
        in_specs=[pl.BlockSpec((ROW_TILE, d_model), lambda j, i: (i, 0)),
                  pl.BlockSpec((d_model, COL_TILE), lambda j, i: (0, _T_GATE + j))],
        out_specs=pl.BlockSpec((ROW_TILE, COL_TILE), lambda j, i: (i, j)),
        scratch_shapes=[pltpu.VMEM((d_model, COL_TILE), BF16)],
        compiler_params=_params(("arbitrary", "arbitrary")),
    )(xb, w_in)


_SUB = 128
_WIN = _SUB + 2 * BAND_HALF
_MERGE_ROWS = 256
_A_UNROLL = 32


def _attn_a_group(g, dil, seq, q_ref, k_ref, v_ref, og, lg, bias):
    sub_len = seq // dil
    per_seq = sub_len // _SUB
    shift = per_seq.bit_length() - 1
    ones = jnp.ones((_WIN, HEAD_DIM), BF16)

    def body(i, carry):
        r = lax.shift_right_logical(i, shift)
        p0 = pl.multiple_of((i & (per_seq - 1)) * _SUB, _SUB)
        start = pl.multiple_of(jnp.clip(p0 - BAND_HALF, 0, sub_len - _WIN), BAND_HALF)
        q = q_ref[r, pl.ds(p0, _SUB), :]
        k = k_ref[r, pl.ds(start, _WIN), :]
        v = v_ref[r, pl.ds(start, _WIN), :]
        s = lax.dot_general(q, k, (((1,), (1,)), ((), ())), preferred_element_type=F32)
        s = s + bias[lax.shift_right_logical(p0 - start, BAND_HALF.bit_length() - 1)]
        m = jnp.max(s, axis=1, keepdims=True)
        p = jnp.exp2(s - m).astype(BF16)
        both = jnp.dot(p, jnp.concatenate([v, ones], axis=1), preferred_element_type=F32)
        l = both[:, HEAD_DIM:]
        o = both[:, :HEAD_DIM] / l
        lse = m + jnp.log2(l)
        if dil == 1:
            rows = pl.ds(p0, _SUB)
        else:
            rows = pl.ds(p0 * dil + r, _SUB, stride=dil)
        og[g, rows, :] = o
        lg[g, rows, :] = lse
        return carry

    lax.fori_loop(0, seq // _SUB, body, 0, unroll=_A_UNROLL)


def _attn_a_kernel(*refs, seq):
    qkv, o_ref = refs[:3 * N_DIL], refs[3 * N_DIL]
    og, lg, bias = refs[3 * N_DIL + 1:]
    row = lax.broadcasted_iota(I32, (_SUB, _WIN), 0)
    col = lax.broadcasted_iota(I32, (_SUB, _WIN), 1)
    for case in range(3):
        bias[case] = jnp.where(jnp.abs(col - row - case * BAND_HALF) <= BAND_HALF, 0.0, NEG_BIG)
    for g, (_, dil) in enumerate(DIL_PATTERNS):
        _attn_a_group(g, dil, seq, qkv[3 * g], qkv[3 * g + 1], qkv[3 * g + 2], og, lg, bias)

    def merge(c, carry):
        rows = pl.ds(pl.multiple_of(c * _MERGE_ROWS, _MERGE_ROWS), _MERGE_ROWS)
        ls = [lg[g, rows, :] for g in range(N_DIL)]
        mx = functools.reduce(jnp.maximum, ls)
        ws = [jnp.exp2(l - mx) for l in ls]
        num = functools.reduce(lambda a, b: a + b, [w * og[g, rows, :] for g, w in enumerate(ws)])
        o_ref[rows, :] = (num / functools.reduce(lambda a, b: a + b, ws)).astype(o_ref.dtype)
        return carry

    lax.fori_loop(0, seq // _MERGE_ROWS, merge, 0)


def _attn_a(groups, batch, seq):
    in_specs, operands = [], []
    for g, (_, dil) in enumerate(DIL_PATTERNS):
        for kind in range(3):
            first = kind * A_HEADS
            in_specs.append(pl.BlockSpec((None, None, dil, seq // dil, HEAD_DIM),
                                         lambda b, h, first=first: (first + h, b, 0, 0, 0)))
            operands.append(groups[g])
    return pl.pallas_call(
        functools.partial(_attn_a_kernel, seq=seq),
        out_shape=jax.ShapeDtypeStruct((batch * seq, A_HEADS * HEAD_DIM), BF16),
        grid=(batch, A_HEADS),
        in_specs=in_specs,
        out_specs=pl.BlockSpec((seq, HEAD_DIM), lambda b, h: (b, h)),
        scratch_shapes=[pltpu.VMEM((N_DIL, seq, HEAD_DIM), F32), pltpu.VMEM((N_DIL, seq, HEAD_DIM), F32),
                        pltpu.VMEM((3, _SUB, _WIN), F32)],
        compiler_params=_params(("arbitrary", "arbitrary")),
    )(*operands)


_BQ = 1024
_BK = 512
SCORE_BOUND = 64.0
BF16_SLACK = 1.02


def _attn_b_kernel(q_ref, k_ref, v_ref, o_ref, *, seq):
    rows = B_GROUP * _BQ
    q = q_ref[...].reshape(rows, HEAD_DIM)

    def body(c, carry):
        m, l, acc = carry
        c0 = pl.multiple_of(c * _BK, _BK)
        k = k_ref[pl.ds(c0, _BK), :]
        v = v_ref[pl.ds(c0, _BK), :]
        s = lax.dot_general(q, k, (((1,), (1,)), ((), ())), preferred_element_type=F32)
        m_new = jnp.maximum(m, jnp.max(s, axis=1, keepdims=True))
        alpha = jnp.exp2(m - m_new)
        p = jnp.exp2(s - m_new)
        l = alpha * l + jnp.sum(p, axis=1, keepdims=True)
        acc = alpha * acc + jnp.dot(p.astype(BF16), v, preferred_element_type=F32)
        return m_new, l, acc

    init = (jnp.full((rows, 1), NEG_BIG, F32), jnp.zeros((rows, 1), F32), jnp.zeros((rows, HEAD_DIM), F32))
    _, l, acc = lax.fori_loop(0, seq // _BK, body, init)
    o = acc / l
    for g in range(B_GROUP):
        o_ref[:, g * HEAD_DIM:(g + 1) * HEAD_DIM] = o[g * _BQ:(g + 1) * _BQ].astype(o_ref.dtype)


def _attn_b_bounded_kernel(q_ref, k_ref, v_ref, o_ref, *, seq):
    rows = B_GROUP * _BQ
    q = q_ref[...].reshape(rows, HEAD_DIM)
    ones = jnp.ones((_BK, HEAD_DIM), BF16)
    acc = jnp.zeros((rows, 2 * HEAD_DIM), F32)
    for c in range(seq // _BK):
        k = k_ref[c * _BK:(c + 1) * _BK, :]
        v = v_ref[c * _BK:(c + 1) * _BK, :]
        s = lax.dot_general(q, k, (((1,), (1,)), ((), ())), preferred_element_type=F32)
        p = jnp.exp2(s).astype(BF16)
        acc = acc + jnp.dot(p, jnp.concatenate([v, ones], axis=1), preferred_element_type=F32)
    o = acc[:, :HEAD_DIM] / acc[:, HEAD_DIM:HEAD_DIM + 1]
    for g in range(B_GROUP):
        o_ref[:, g * HEAD_DIM:(g + 1) * HEAD_DIM] = o[g * _BQ:(g + 1) * _BQ].astype(o_ref.dtype)


def _attn_b(slots, batch, seq, bounded):
    n_tok = batch * seq
    k0 = B_Q_HEADS
    v0 = k0 + B_KV_HEADS
    qblocks = seq // _BQ
    return pl.pallas_call(
        functools.partial(_attn_b_bounded_kernel if bounded else _attn_b_kernel, seq=seq),
        out_shape=jax.ShapeDtypeStruct((n_tok, B_Q_HEADS * HEAD_DIM), BF16),
        grid=(batch, B_KV_HEADS, qblocks),
        in_specs=[
            pl.BlockSpec((B_GROUP, _BQ, HEAD_DIM), lambda b, kv, qi: (kv, b * qblocks + qi, 0)),
            pl.BlockSpec((None, seq, HEAD_DIM), lambda b, kv, qi: (k0 + kv, b, 0)),
            pl.BlockSpec((None, seq, HEAD_DIM), lambda b, kv, qi: (v0 + kv, b, 0)),
        ],
        out_specs=pl.BlockSpec((_BQ, B_GROUP * HEAD_DIM), lambda b, kv, qi: (b * qblocks + qi, kv)),
        compiler_params=_params(("arbitrary", "arbitrary", "arbitrary")),
    )(slots, slots, slots)


_MIX_ROWS = 512


def _branch_kernel(oa_ref, ob_ref, ga_ref, gb_ref, wa_ref, wb_ref, out_ref):
    for rt in range(out_ref.shape[0] // SUB_ROWS):
        rows = slice(rt * SUB_ROWS, (rt + 1) * SUB_ROWS)
        y_a = jnp.dot(oa_ref[rows, :], wa_ref[...], preferred_element_type=F32)
        y_b = jnp.dot(ob_ref[rows, :], wb_ref[...], preferred_element_type=F32)
        out_ref[rows, :] = (ga_ref[rows, :].astype(F32) * y_a + gb_ref[rows, :].astype(F32) * y_b).astype(BF16)


def _branch_mix(o_a, o_b, gates, wa, wb):
    n_tok = o_b.shape[0]
    d_model = wa.shape[1]
    tm = _MIX_ROWS
    row = lambda i: (i, 0)
    return pl.pallas_call(
        _branch_kernel,
        out_shape=jax.ShapeDtypeStruct((n_tok, d_model), BF16),
        grid=(n_tok // tm,),
        in_specs=[pl.BlockSpec((tm, o_a.shape[1]), row),
                  pl.BlockSpec((tm, o_b.shape[1]), row),
                  pl.BlockSpec((tm, d_model), lambda i: (i, 0)),
                  pl.BlockSpec((tm, d_model), lambda i: (i, 1)),
                  pl.BlockSpec(wa.shape, lambda i: (0, 0)),
                  pl.BlockSpec(wb.shape, lambda i: (0, 0))],
        out_specs=pl.BlockSpec((tm, d_model), row),
        compiler_params=_params(("arbitrary",)),
    )(o_a, o_b, gates, gates, wa, wb)


def _layer_norm(z, g, b):
    mu = jnp.mean(z, axis=-1, keepdims=True)
    zc = z - mu
    var = jnp.mean(zc * zc, axis=-1, keepdims=True)
    return zc * lax.rsqrt(var + LN_EPS) * g + b


def _to_slabs(slab_ref, value, row0):
    rows, d = value.shape
    chunks = d // LANES
    for c in range(chunks):
        slab_ref[pl.ds(row0 * chunks + c, rows, stride=chunks), :] = value[:, c * LANES:(c + 1) * LANES]


def _from_slabs(slab_ref, rows, chunks, c):
    return slab_ref[pl.ds(c, rows, stride=chunks), :]


def _out_proj_kernel(m_ref, w_ref, x_ref, g_ref, b_ref, rcat_ref, rb_ref, hs_ref, lg_ref, *, alpha):
    for rt in range(m_ref.shape[0] // SUB_ROWS):
        rows = slice(rt * SUB_ROWS, (rt + 1) * SUB_ROWS)
        mix = jnp.dot(m_ref[rows, :], w_ref[...], preferred_element_type=F32)
        h = _layer_norm(alpha * x_ref[rows, :] + mix, g_ref[...], b_ref[...])
        _to_slabs(hs_ref, h, rt * SUB_ROWS)
        hi = h.astype(BF16)
        lo = (h - hi.astype(F32)).astype(BF16)
        both = jnp.dot(hi, rcat_ref[...], preferred_element_type=F32)
        lg = both[:, :LANES] + both[:, LANES:] + jnp.dot(lo, rcat_ref[:, :LANES], preferred_element_type=F32)
        lg_ref[rows, :] = lg + rb_ref[...]


def _out_proj(merged, w_out, x2, g, b, r_cat, r_b, alpha):
    n_tok, d_model = x2.shape
    tm = _MIX_ROWS
    chunks = d_model // LANES
    row = lambda i: (i, 0)
    fix = lambda i: (0, 0)
    return pl.pallas_call(
        functools.partial(_out_proj_kernel, alpha=alpha),
        out_shape=(jax.ShapeDtypeStruct((n_tok * chunks, LANES), F32),
                   jax.ShapeDtypeStruct((n_tok, LANES), F32)),
        grid=(n_tok // tm,),
        in_specs=[pl.BlockSpec((tm, d_model), row),
                  pl.BlockSpec(w_out.shape, fix),
                  pl.BlockSpec((tm, d_model), row),
                  pl.BlockSpec((1, d_model), fix),
                  pl.BlockSpec((1, d_model), fix),
                  pl.BlockSpec(r_cat.shape, fix),
                  pl.BlockSpec((1, LANES), fix)],
        out_specs=(pl.BlockSpec((tm * chunks, LANES), row), pl.BlockSpec((tm, LANES), row)),
        compiler_params=_params(("arbitrary",)),
    )(merged, w_out, x2, g, b, r_cat, r_b)


_BIG_CHUNK = 512


def _moe_kernel(item_expert, item_row0, item_blocks, item_real, src_slab, dst_slab,
                h_hbm, wg_ref, wu_ref, wd_ref, y_hbm,
                stage, xbf, acc, obuf, gsem, ssem):
    it = pl.program_id(0)
    f = pl.program_id(1)
    n_items = pl.num_programs(0)
    n_ff = pl.num_programs(1)
    n_blocks = item_blocks[it]
    d_model = xbf.shape[1]
    chunks = d_model // LANES

    def slab(ref, first):
        return ref.at[pl.ds(pl.multiple_of(first, chunks), chunks)]

    def gather_copy(base, r):
        return pltpu.make_async_copy(slab(h_hbm, src_slab[base + r]), slab(stage, r * chunks), gsem.at[0])

    def scatter_copy(base, r):
        return pltpu.make_async_copy(slab(obuf, r * chunks), slab(y_hbm, dst_slab[base + r]), ssem.at[0])

    def for_rows(n_rows, fn):
        def group(t, c):
            for u in range(DMA_UNROLL):
                fn(t * DMA_UNROLL + u)
            return c
        n_groups = n_rows // DMA_UNROLL
        lax.fori_loop(0, n_groups, group, 0)

        def single(r, c):
            fn(r)
            return c
        lax.fori_loop(n_groups * DMA_UNROLL, n_rows, single, 0)

    def gather(item, op):
        base = item_row0[item]
        for_rows(item_blocks[item] * MOE_BLOCK, lambda r: op(gather_copy(base, r)))

    def scatter(item, op):
        base = item_row0[item]
        for_rows(item_real[item], lambda r: op(scatter_copy(base, r)))

    start = lambda cp: cp.start()
    wait = lambda cp: cp.wait()

    @pl.when((it == 0) & (f == 0))
    def _first_gather():
        gather(it, start)

    @pl.when(f == 0)
    def _gathered():
        gather(it, wait)

        def cast_rows(b, carry):
            r0 = pl.multiple_of(b * MOE_BLOCK, MOE_BLOCK)
            for c in range(chunks):
                piece = stage[pl.ds(r0 * chunks + c, MOE_BLOCK, stride=chunks), :]
                xbf[pl.ds(r0, MOE_BLOCK), c * LANES:(c + 1) * LANES] = piece.astype(BF16)
            return carry
        lax.fori_loop(0, n_blocks, cast_rows, 0)

        @pl.when(it + 1 < n_items)
        def _prefetch_next():
            gather(it + 1, start)

    def chunk(r0, rows, mode):
        xb = xbf[pl.ds(r0, rows), :]
        a = jnp.dot(xb, wg_ref[...].astype(BF16), preferred_element_type=F32)
        u = jnp.dot(xb, wu_ref[...].astype(BF16), preferred_element_type=F32)
        hid = (a / (1.0 + jnp.exp(-a)) * u).astype(BF16)
        y = jnp.dot(hid, wd_ref[...].astype(BF16), preferred_element_type=F32)
        if mode == "first":
            acc[pl.ds(r0, rows), :] = y
        elif mode == "middle":
            acc[pl.ds(r0, rows), :] += y
        else:
            if mode == "last":
                y = acc[pl.ds(r0, rows), :] + y
            for c in range(chunks):
                obuf[pl.ds(r0 * chunks + c, rows, stride=chunks), :] = y[:, c * LANES:(c + 1) * LANES]

    def run_item(mode):
        per_big = _BIG_CHUNK // MOE_BLOCK

        def big_chunk(t, c):
            chunk(pl.multiple_of(t * _BIG_CHUNK, _BIG_CHUNK), _BIG_CHUNK, mode)
            return c
        n_big = n_blocks // per_big
        lax.fori_loop(0, n_big, big_chunk, 0)

        def small_chunk(b, c):
            chunk(pl.multiple_of(b * MOE_BLOCK, MOE_BLOCK), MOE_BLOCK, mode)
            return c
        lax.fori_loop(n_big * per_big, n_blocks, small_chunk, 0)

    last = n_ff - 1

    @pl.when((n_blocks > 0) & (f == last))
    def _final():
        @pl.when(it > 0)
        def _drain_prev():
            scatter(it - 1, wait)
        run_item("only" if n_ff == 1 else "last")
        scatter(it, start)

    if n_ff > 1:
        @pl.when((n_blocks > 0) & (f == 0))
        def _first():
            run_item("first")

    if n_ff > 2:
        @pl.when((n_blocks > 0) & (f > 0) & (f < last))
        def _middle():
            run_item("middle")

    @pl.when((f == last) & (it == n_items - 1))
    def _drain_last():
        scatter(jnp.where(n_blocks > 0, it, _last_live(item_blocks, n_items)), wait)


def _last_live(item_blocks, n_items):
    def body(i, best):
        return jnp.where(item_blocks[i] > 0, i, best)
    return lax.fori_loop(0, n_items, body, 0)


def _moe(h_slabs, w_gate, w_up, w_down, item_expert, item_row0, item_blocks, item_real, src_slab, dst_slab):
    d_model = w_gate.shape[1]
    chunks = d_model // LANES
    n_tok = h_slabs.shape[0] // chunks
    ff = w_gate.shape[2]
    n_ff = ff // FF_TILE
    n_items = item_expert.shape[0]

    def ff_idx(it, f, blocks):
        return jnp.where(blocks[it] > 0, f, n_ff - 1)

    grid_spec = pltpu.PrefetchScalarGridSpec(
        num_scalar_prefetch=6,
        grid=(n_items, n_ff),
        in_specs=[
            pl.BlockSpec(memory_space=pl.ANY),
            pl.BlockSpec((None, d_model, FF_TILE), lambda it, f, ie, ir, ib, nr, st, dr: (ie[it], 0, ff_idx(it, f, ib))),
            pl.BlockSpec((None, d_model, FF_TILE), lambda it, f, ie, ir, ib, nr, st, dr: (ie[it], 0, ff_idx(it, f, ib))),
            pl.BlockSpec((None, FF_TILE, d_model), lambda it, f, ie, ir, ib, nr, st, dr: (ie[it], ff_idx(it, f, ib), 0)),
        ],
        out_specs=pl.BlockSpec(memory_space=pl.ANY),
        scratch_shapes=[
            pltpu.VMEM((ITEM_ROWS * chunks, LANES), F32),
            pltpu.VMEM((ITEM_ROWS, d_model), BF16),
            pltpu.VMEM((ITEM_ROWS, d_model), F32),
            pltpu.VMEM((ITEM_ROWS * chunks, LANES), F32),
            pltpu.SemaphoreType.DMA((1,)),
            pltpu.SemaphoreType.DMA((1,)),
        ],
    )
    return pl.pallas_call(
        _moe_kernel,
        out_shape=jax.ShapeDtypeStruct((n_tok * TOP_K * chunks, LANES), F32),
        grid_spec=grid_spec,
        compiler_params=_params(("arbitrary", "arbitrary")),
    )(item_expert, item_row0, item_blocks, item_real, src_slab, dst_slab, h_slabs, w_gate, w_up, w_down)


_COMBINE_ROWS = 512


def _combine_kernel(y0_ref, y1_ref, wt_ref, hs_ref, g_ref, b_ref, out_ref, *, alpha):
    rows, d_model = out_ref.shape
    chunks = d_model // LANES
    wt = wt_ref[...]
    w0, w1 = wt[:, 0:1], wt[:, 1:2]
    for c in range(chunks):
        cols = slice(c * LANES, (c + 1) * LANES)
        ffn = w0 * _from_slabs(y0_ref, rows, chunks, c) + w1 * _from_slabs(y1_ref, rows, chunks, c)
        out_ref[:, cols] = alpha * _from_slabs(hs_ref, rows, chunks, c) + ffn
    out_ref[...] = _layer_norm(out_ref[...], g_ref[...], b_ref[...])


def _combine(y_slabs, weights, h_slabs, g, b, alpha):
    d_model = g.shape[1]
    n_tok = h_slabs.shape[0] * LANES // d_model
    tm = _COMBINE_ROWS
    chunks = d_model // LANES
    second = n_tok // tm
    row = lambda i: (i, 0)
    fix = lambda i: (0, 0)
    return pl.pallas_call(
        functools.partial(_combine_kernel, alpha=alpha),
        out_shape=jax.ShapeDtypeStruct((n_tok, d_model), F32),
        grid=(n_tok // tm,),
        in_specs=[pl.BlockSpec((tm * chunks, LANES), row),
                  pl.BlockSpec((tm * chunks, LANES), lambda i: (second + i, 0)),
                  pl.BlockSpec((tm, TOP_K), row),
                  pl.BlockSpec((tm * chunks, LANES), row),
                  pl.BlockSpec((1, d_model), fix),
                  pl.BlockSpec((1, d_model), fix)],
        out_specs=pl.BlockSpec((tm, d_model), row),
        compiler_params=_params(("arbitrary",)),
    )(y_slabs, y_slabs, weights, h_slabs, g, b)


def _rope_tables(seq):
    f32 = np.float32
    half = HEAD_DIM // 2
    inv1 = np.power(f32(ROPE_THETA), -(np.arange(half, dtype=f32) / f32(half)))
    ang1 = np.arange(seq, dtype=f32)[:, None] * inv1[None, :]
    t = np.arange(seq)
    n_axis = HEAD_DIM // 4
    inv2 = np.power(f32(ROPE_THETA), -(np.arange(n_axis, dtype=f32) / f32(n_axis)))
    ang2 = np.concatenate([(t // GRID_W).astype(f32)[:, None] * inv2[None, :],
                           (t % GRID_W).astype(f32)[:, None] * inv2[None, :]], axis=-1)
    q_scale = f32(HEAD_DIM ** -0.5 * LOG2E)
    out = []
    for ang in (ang1, ang2):
        cos = np.concatenate([np.cos(ang), np.cos(ang)], axis=-1)
        sin = np.concatenate([-np.sin(ang), np.sin(ang)], axis=-1)
        out += [cos * q_scale, sin * q_scale, cos, sin]
    return jnp.asarray(np.stack(out, axis=0).astype(f32))


def _route(logits, n_tok, chunks):
    g_logits = logits[:, :N_GROUPS]
    g_prob = jax.nn.softmax(g_logits, axis=-1)
    g_idx = jnp.argmax(g_logits, axis=-1).astype(I32)
    g_gate = jnp.max(g_prob, axis=-1)
    e_logits = logits[:, N_GROUPS:N_GROUPS + EXPERTS_PER_GROUP]
    for g in range(1, N_GROUPS):
        lo = N_GROUPS + g * EXPERTS_PER_GROUP
        e_logits = jnp.where(g_idx[:, None] == g, logits[:, lo:lo + EXPERTS_PER_GROUP], e_logits)
    e_prob = jax.nn.softmax(e_logits, axis=-1)
    i1 = jnp.argmax(e_prob, axis=-1).astype(I32)
    p1 = jnp.max(e_prob, axis=-1)
    rest = jnp.where(jnp.arange(EXPERTS_PER_GROUP, dtype=I32)[None, :] == i1[:, None], -1.0, e_prob)
    i2 = jnp.argmax(rest, axis=-1).astype(I32)
    p2 = jnp.max(rest, axis=-1)
    top_p = jnp.stack([p1, p2], axis=-1)
    top_p = top_p / jnp.sum(top_p, axis=-1, keepdims=True)
    weights = g_gate[:, None] * top_p
    expert = g_idx[:, None] * EXPERTS_PER_GROUP + jnp.stack([i1, i2], axis=-1)

    n_slot = n_tok * TOP_K
    e_flat = expert.reshape(n_slot).astype(I32)
    order = jnp.argsort(e_flat).astype(I32)
    experts = jnp.arange(N_EXPERTS, dtype=I32)
    counts = jnp.sum(e_flat[:, None] == experts[None, :], axis=0, dtype=I32)
    padded = ((counts + MOE_BLOCK - 1) // MOE_BLOCK) * MOE_BLOCK
    p_ends = jnp.cumsum(padded)
    p_starts = p_ends - padded
    buf_len = n_slot + N_EXPERTS * MOE_BLOCK

    def pick(one_hot, table):
        return jnp.sum(jnp.where(one_hot, table[None, :], 0), axis=1, dtype=I32)

    pos = jnp.arange(buf_len, dtype=I32)[:, None]
    ended = p_ends[None, :] <= pos
    inside = jnp.logical_and(p_starts[None, :] <= pos, pos < p_ends[None, :])
    idx = pos[:, 0] - jnp.max(jnp.where(ended, p_ends[None, :], 0), axis=1)
    sorted_at = jnp.clip(pick(ended, counts) + idx, 0, n_slot - 1)
    slot_at = jnp.where(idx < pick(inside, counts), order[sorted_at], 0)
    tok_at, k_at = slot_at // TOP_K, slot_at % TOP_K
    src_slab = tok_at * chunks
    dst_slab = (k_at * n_tok + tok_at) * chunks

    n_items = N_EXPERTS + n_slot // ITEM_ROWS
    per_expert = (padded + ITEM_ROWS - 1) // ITEM_ROWS
    item_ends = jnp.cumsum(per_expert)
    total = item_ends[-1]
    ids = jnp.arange(n_items, dtype=I32)
    live = ids < total
    item = jnp.minimum(ids, jnp.maximum(total - 1, 0))[:, None]
    owner = jnp.logical_and((item_ends - per_expert)[None, :] <= item, item < item_ends[None, :])
    e_of = pick(owner, experts)
    chunk = item[:, 0] - pick(owner, item_ends - per_expert)
    item_row0 = jnp.where(live, pick(owner, p_starts) + chunk * ITEM_ROWS, 0).astype(I32)
    item_blocks = jnp.where(live, jnp.clip(pick(owner, padded) // MOE_BLOCK - chunk * ITEM_BLOCKS, 0, ITEM_BLOCKS), 0)
    item_real = jnp.where(live, jnp.clip(pick(owner, counts) - chunk * ITEM_ROWS, 0, ITEM_ROWS), 0)
    return weights, e_of, item_row0, item_blocks.astype(I32), item_real.astype(I32), src_slab, dst_slab


def kernel(x, w_in, q_norm_g, k_norm_g, w_branch_a, w_branch_b, w_out, ln1_g, ln1_b, w_group, b_group, w_router,
           b_router, w_gate, w_up, w_down, ln2_g, ln2_b):
    batch, seq, d_model = x.shape
    depth = w_in.shape[0]
    n_tok = batch * seq
    dn_alpha = (2 * depth) ** 0.25
    tabs = _rope_tables(seq)
    h = x.reshape(n_tok, d_model)
    for layer in range(depth):
        xb = h.astype(BF16)
        groups = [_proj_a(xb, w_in[layer], tabs, g, batch, seq) for g in range(N_DIL)]
        slots_b = _proj_b(xb, w_in[layer], tabs, q_norm_g[layer][None, :], k_norm_g[layer][None, :], seq)
        gates = _proj_gate(xb, w_in[layer])
        o_a = _attn_a(groups, batch, seq)
        score_bound = (jnp.max(jnp.abs(q_norm_g[layer])) * jnp.max(jnp.abs(k_norm_g[layer]))
                       * (HEAD_DIM ** 0.5 * LOG2E * BF16_SLACK))
        o_b = lax.cond(score_bound <= SCORE_BOUND,
                       lambda s: _attn_b(s, batch, seq, True), lambda s: _attn_b(s, batch, seq, False), slots_b)
        merged = _branch_mix(o_a, o_b, gates, w_branch_a[layer].astype(BF16), w_branch_b[layer].astype(BF16))
        w_r = jnp.concatenate([w_group[layer], w_router[layer]], axis=1)
        w_r = jnp.pad(w_r, ((0, 0), (0, LANES - w_r.shape[1])))
        r_hi = w_r.astype(BF16)
        r_lo = (w_r - r_hi.astype(F32)).astype(BF16)
        r_b = jnp.pad(jnp.concatenate([b_group[layer], b_router[layer]]), (0, LANES - N_GROUPS - N_EXPERTS))[None, :]
        h1_slabs, logits = _out_proj(merged, w_out[layer].astype(BF16), h, ln1_g[layer][None, :],
                                         ln1_b[layer][None, :], jnp.concatenate([r_hi, r_lo], axis=1), r_b, dn_alpha)
        weights, item_expert, item_row0, item_blocks, item_real, src_slab, dst_slab = _route(
            logits, n_tok, d_model // LANES)
        y_slabs = _moe(h1_slabs, w_gate[layer], w_up[layer], w_down[layer], item_expert, item_row0, item_blocks,
                       item_real, src_slab, dst_slab)
        h = _combine(y_slabs, weights, h1_slabs, ln2_g[layer][None, :], ln2_b[layer][None, :], dn_alpha)
    return h.reshape(batch, seq, d_model)
```

```python
import functools
import math

import jax
import jax.numpy as jnp
import numpy as np
from jax import lax
from jax.experimental import pallas as pl
from jax.experimental.pallas import tpu as pltpu

F32 = jnp.float32
BF16 = jnp.bfloat16
I32 = jnp.int32

HEAD_DIM = 128
ROPE_THETA = 10000.0
GRID_W = 64
DIL_PATTERNS = ((128, 1), (512, 4), (2048, 16))
N_DIL = len(DIL_PATTERNS)
A_HEADS = 8
BAND_HALF = 64
B_Q_HEADS = 16
B_KV_HEADS = 4
B_GROUP = B_Q_HEADS // B_KV_HEADS
N_GROUPS = 4
EXPERTS_PER_GROUP = 8
N_EXPERTS = N_GROUPS * EXPERTS_PER_GROUP
TOP_K = 2
MOE_BLOCK = 128
LN_EPS = 1e-5
RMS_EPS = 1e-6
NEG_BIG = -1e30
LOG2E = math.log2(math.e)

LANES = 128
COL_TILE = 8 * HEAD_DIM
ROW_TILE = 1024
SUB_ROWS = 256
ITEM_ROWS = 768
ITEM_BLOCKS = ITEM_ROWS // MOE_BLOCK
FF_TILE = 512
DMA_UNROLL = 16
VMEM_LIMIT = 56 * 1024 * 1024


def _params(sem, vmem=VMEM_LIMIT):
    return pltpu.CompilerParams(dimension_semantics=sem, vmem_limit_bytes=vmem)


_T_QA, _T_KA, _T_VA, _T_QB, _T_KVB, _T_GATE = 0, 3, 6, 9, 11, 12
_TAB_1D, _TAB_AXIAL = 0, 1
_RQ, _RK = 0, 2


def _cast_weights_once(w_ref, wbf_ref):
    @pl.when(pl.program_id(1) == 0)
    def _cast():
        wbf_ref[...] = w_ref[...].astype(BF16)


def _for_sub_tiles(x_ref, wbf_ref, emit):
    for rt in range(x_ref.shape[0] // SUB_ROWS):
        rows = slice(rt * SUB_ROWS, (rt + 1) * SUB_ROWS)
        emit(rt, rows, jnp.dot(x_ref[rows, :], wbf_ref[...], preferred_element_type=F32))


def _head(acc, h):
    return acc[:, h * HEAD_DIM:(h + 1) * HEAD_DIM]


def _rope(a, tab_ref, t, rows):
    return a * tab_ref[t, rows, :] + pltpu.roll(a, HEAD_DIM // 2, 1) * tab_ref[t + 1, rows, :]


def _rms(a, g_ref):
    ms = jnp.mean(a * a, axis=-1, keepdims=True)
    return a * lax.rsqrt(ms + RMS_EPS) * g_ref[...]


def _proj_a_kernel(x_ref, w_ref, tab_ref, out_ref, wbf_ref, *, dil):
    j = pl.program_id(0)
    _cast_weights_once(w_ref, wbf_ref)
    per_res = SUB_ROWS // dil

    def emit_with(fn):
        def emit(rt, rows, acc):
            if dil == 1:
                for h in range(A_HEADS):
                    out_ref[h, 0, rows, :] = fn(_head(acc, h), rows).astype(BF16)
                return
            for h in range(A_HEADS):
                by_res = fn(_head(acc, h), rows).reshape(per_res, dil, HEAD_DIM).swapaxes(0, 1)
                for r in range(dil):
                    out_ref[h, r, rt * per_res:(rt + 1) * per_res, :] = by_res[r].astype(BF16)
        _for_sub_tiles(x_ref, wbf_ref, emit)

    @pl.when(j == 0)
    def _q():
        emit_with(lambda a, rows: _rope(a, tab_ref, _RQ, rows))

    @pl.when(j == 1)
    def _k():
        emit_with(lambda a, rows: _rope(a, tab_ref, _RK, rows))

    @pl.when(j == 2)
    def _v():
        emit_with(lambda a, rows: a)


def _proj_b_kernel(x_ref, w_ref, tab_ref, gq_ref, gk_ref, out_ref, wbf_ref, *, q_tiles):
    j = pl.program_id(0)
    _cast_weights_once(w_ref, wbf_ref)
    heads = COL_TILE // HEAD_DIM

    @pl.when(j < q_tiles)
    def _qb():
        def emit(rt, rows, acc):
            for h in range(heads):
                out_ref[h, rows, :] = _rope(_rms(_head(acc, h), gq_ref), tab_ref, _RQ, rows).astype(BF16)
        _for_sub_tiles(x_ref, wbf_ref, emit)

    @pl.when(j == q_tiles)
    def _kvb():
        def emit(rt, rows, acc):
            for h in range(heads):
                if h < B_KV_HEADS:
                    out_ref[h, rows, :] = _rope(_rms(_head(acc, h), gk_ref), tab_ref, _RK, rows).astype(BF16)
                else:
                    out_ref[h, rows, :] = _head(acc, h).astype(BF16)
        _for_sub_tiles(x_ref, wbf_ref, emit)


def _proj_gate_kernel(x_ref, w_ref, out_ref, wbf_ref):
    _cast_weights_once(w_ref, wbf_ref)

    def emit(rt, rows, acc):
        out_ref[rows, :] = (1.0 / (1.0 + jnp.exp(-acc))).astype(BF16)
    _for_sub_tiles(x_ref, wbf_ref, emit)


def _proj_specs(d_model, seq, first_tile, tile_step, table_half):
    seq_blocks = seq // ROW_TILE
    return [
        pl.BlockSpec((ROW_TILE, d_model), lambda j, i: (i, 0)),
        pl.BlockSpec((d_model, COL_TILE), lambda j, i: (0, first_tile + tile_step * j)),
        pl.BlockSpec((4, ROW_TILE, HEAD_DIM), lambda j, i: (table_half, i % seq_blocks, 0)),
    ]


def _proj_a(xb, w_in, tabs, group, batch, seq):
    n_tok, d_model = xb.shape
    dil = DIL_PATTERNS[group][1]
    sub_len = seq // dil
    seq_blocks = seq // ROW_TILE
    return pl.pallas_call(
        functools.partial(_proj_a_kernel, dil=dil),
        out_shape=jax.ShapeDtypeStruct((3 * A_HEADS, batch, dil, sub_len, HEAD_DIM), BF16),
        grid=(3, n_tok // ROW_TILE),
        in_specs=_proj_specs(d_model, seq, group, N_DIL, _TAB_1D),
        out_specs=pl.BlockSpec((A_HEADS, None, dil, ROW_TILE // dil, HEAD_DIM),
                               lambda j, i: (j, i // seq_blocks, 0, i % seq_blocks, 0)),
        scratch_shapes=[pltpu.VMEM((d_model, COL_TILE), BF16)],
        compiler_params=_params(("arbitrary", "arbitrary")),
    )(xb, w_in, tabs)


def _proj_b(xb, w_in, tabs, gq, gk, seq):
    n_tok, d_model = xb.shape
    q_tiles = B_Q_HEADS * HEAD_DIM // COL_TILE
    heads = COL_TILE // HEAD_DIM
    fix = lambda j, i: (0, 0)
    return pl.pallas_call(
        functools.partial(_proj_b_kernel, q_tiles=q_tiles),
        out_shape=jax.ShapeDtypeStruct(((q_tiles + 1) * heads, n_tok, HEAD_DIM), BF16),
        grid=(q_tiles + 1, n_tok // ROW_TILE),
        in_specs=_proj_specs(d_model, seq, _T_QB, 1, _TAB_AXIAL) + [pl.BlockSpec((1, HEAD_DIM), fix), pl.BlockSpec((1, HEAD_DIM), fix)],
        out_specs=pl.BlockSpec((heads, ROW_TILE, HEAD_DIM), lambda j, i: (j, i, 0)),
        scratch_shapes=[pltpu.VMEM((d_model, COL_TILE), BF16)],
        compiler_params=_params(("arbitrary", "arbitrary")),
    )(xb, w_in, tabs, gq, gk)


def _proj_gate(xb, w_in):
    n_tok, d_model = xb.shape
    n_ct = w_in.shape[1] // COL_TILE - _T_GATE
    return pl.pallas_call(
        _proj_gate_kernel,
        out_shape=jax.ShapeDtypeStruct((n_tok, n_ct * COL_TILE), BF16),
        grid=(n_ct, n_tok // ROW_TILE),
        in_specs=[pl.BlockSpec((ROW_TILE, d_model), lambda j, i: (i, 0)),
                  pl.BlockSpec((d_model, COL_TILE), lambda j, i: (0, _T_GATE + j))],
        out_specs=pl.BlockSpec((ROW_TILE, COL_TILE), lambda j, i: (i, j)),
        scratch_shapes=[pltpu.VMEM((d_model, COL_TILE), BF16)],
        compiler_params=_params(("arbitrary", "arbitrary")),
    )(xb, w_in)


_SUB = 128
_WIN = _SUB + 2 * BAND_HALF
_MERGE_ROWS = 256
_A_UNROLL = 32


def _attn_a_group(g, dil, seq, q_ref, k_ref, v_ref, og, lg, bias):
    sub_len = seq // dil
    per_seq = sub_len // _SUB
    shift = per_seq.bit_length() - 1
    ones = jnp.ones((_WIN, HEAD_DIM), BF16)

    def body(i, carry):
        r = lax.shift_right_logical(i, shift)
        p0 = pl.multiple_of((i & (per_seq - 1)) * _SUB, _SUB)
        start = pl.multiple_of(jnp.clip(p0 - BAND_HALF, 0, sub_len - _WIN), BAND_HALF)
        q = q_ref[r, pl.ds(p0, _SUB), :]
        k = k_ref[r, pl.ds(start, _WIN), :]
        v = v_ref[r, pl.ds(start, _WIN), :]
        s = lax.dot_general(q, k, (((1,), (1,)), ((), ())), preferred_element_type=F32)
        s = s + bias[lax.shift_right_logical(p0 - start, BAND_HALF.bit_length() - 1)]
        m = jnp.max(s, axis=1, keepdims=True)
        p = jnp.exp2(s - m).astype(BF16)
        both = jnp.dot(p, jnp.concatenate([v, ones], axis=1), preferred_element_type=F32)
        l = both[:, HEAD_DIM:]
        o = both[:, :HEAD_DIM] / l
        lse = m + jnp.log2(l)
        if dil == 1:
            rows = pl.ds(p0, _SUB)
        else:
            rows = pl.ds(p0 * dil + r, _SUB, stride=dil)
        og[g, rows, :] = o
        lg[g, rows, :] = lse
        return carry

    lax.fori_loop(0, seq // _SUB, body, 0, unroll=_A_UNROLL)


def _attn_a_kernel(*refs, seq):
    qkv, o_ref = refs[:3 * N_DIL], refs[3 * N_DIL]
    og, lg, bias = refs[3 * N_DIL + 1:]
    row = lax.broadcasted_iota(I32, (_SUB, _WIN), 0)
    col = lax.broadcasted_iota(I32, (_SUB, _WIN), 1)
    for case in range(3):
        bias[case] = jnp.where(jnp.abs(col - row - case * BAND_HALF) <= BAND_HALF, 0.0, NEG_BIG)
    for g, (_, dil) in enumerate(DIL_PATTERNS):
        _attn_a_group(g, dil, seq, qkv[3 * g], qkv[3 * g + 1], qkv[3 * g + 2], og, lg, bias)

    def merge(c, carry):
        rows = pl.ds(pl.multiple_of(c * _MERGE_ROWS, _MERGE_ROWS), _MERGE_ROWS)
        ls = [lg[g, rows, :] for g in range(N_DIL)]
        mx = functools.reduce(jnp.maximum, ls)
        ws = [jnp.exp2(l - mx) for l in ls]
        num = functools.reduce(lambda a, b: a + b, [w * og[g, rows, :] for g, w in enumerate(ws)])
        o_ref[rows, :] = (num / functools.reduce(lambda a, b: a + b, ws)).astype(o_ref.dtype)
        return carry

    lax.fori_loop(0, seq // _MERGE_ROWS, merge, 0)


def _attn_a(groups, batch, seq):
    in_specs, operands = [], []
    for g, (_, dil) in enumerate(DIL_PATTERNS):
        for kind in range(3):
            first = kind * A_HEADS
            in_specs.append(pl.BlockSpec((None, None, dil, seq // dil, HEAD_DIM),
                                         lambda b, h, first=first: (first + h, b, 0, 0, 0)))
            operands.append(groups[g])
    return pl.pallas_call(
        functools.partial(_attn_a_kernel, seq=seq),
        out_shape=jax.ShapeDtypeStruct((batch * seq, A_HEADS * HEAD_DIM), BF16),
        grid=(batch, A_HEADS),
        in_specs=in_specs,
        out_specs=pl.BlockSpec((seq, HEAD_DIM), lambda b, h: (b, h)),
        scratch_shapes=[pltpu.VMEM((N_DIL, seq, HEAD_DIM), F32), pltpu.VMEM((N_DIL, seq, HEAD_DIM), F32),
                        pltpu.VMEM((3, _SUB, _WIN), F32)],
        compiler_params=_params(("arbitrary", "arbitrary")),
    )(*operands)


_BQ = 1024
_BK = 512
SCORE_BOUND = 64.0
BF16_SLACK = 1.02


def _attn_b_kernel(q_ref, k_ref, v_ref, o_ref, *, seq):
    rows = B_GROUP * _BQ
    q = q_ref[...].reshape(rows, HEAD_DIM)

    def body(c, carry):
        m, l, acc = carry
        c0 = pl.multiple_of(c * _BK, _BK)
        k = k_ref[pl.ds(c0, _BK), :]
        v = v_ref[pl.ds(c0, _BK), :]
        s = lax.dot_general(q, k, (((1,), (1,)), ((), ())), preferred_element_type=F32)
        m_new = jnp.maximum(m, jnp.max(s, axis=1, keepdims=True))
        alpha = jnp.exp2(m - m_new)
        p = jnp.exp2(s - m_new)
        l = alpha * l + jnp.sum(p, axis=1, keepdims=True)
        acc = alpha * acc + jnp.dot(p.astype(BF16), v, preferred_element_type=F32)
        return m_new, l, acc

    init = (jnp.full((rows, 1), NEG_BIG, F32), jnp.zeros((rows, 1), F32), jnp.zeros((rows, HEAD_DIM), F32))
    _, l, acc = lax.fori_loop(0, seq // _BK, body, init)
    o = acc / l
    for g in range(B_GROUP):
        o_ref[:, g * HEAD_DIM:(g + 1) * HEAD_DIM] = o[g * _BQ:(g + 1) * _BQ].astype(o_ref.dtype)


def _attn_b_bounded_kernel(q_ref, k_ref, v_ref, o_ref, *, seq):
    rows = B_GROUP * _BQ
    q = q_ref[...].reshape(rows, HEAD_DIM)
    ones = jnp.ones((_BK, HEAD_DIM), BF16)
    acc = jnp.zeros((rows, 2 * HEAD_DIM), F32)
    for c in range(seq // _BK):
        k = k_ref[c * _BK:(c + 1) * _BK, :]
        v = v_ref[c * _BK:(c + 1) * _BK, :]
        s = lax.dot_general(q, k, (((1,), (1,)), ((), ())), preferred_element_type=F32)
        p = jnp.exp2(s).astype(BF16)
        acc = acc + jnp.dot(p, jnp.concatenate([v, ones], axis=1), preferred_element_type=F32)
    o = acc[:, :HEAD_DIM] / acc[:, HEAD_DIM:HEAD_DIM + 1]
    for g in range(B_GROUP):
        o_ref[:, g * HEAD_DIM:(g + 1) * HEAD_DIM] = o[g * _BQ:(g + 1) * _BQ].astype(o_ref.dtype)


def _attn_b(slots, batch, seq, bounded):
    n_tok = batch * seq
    k0 = B_Q_HEADS
    v0 = k0 + B_KV_HEADS
    qblocks = seq // _BQ
    return pl.pallas_call(
        functools.partial(_attn_b_bounded_kernel if bounded else _attn_b_kernel, seq=seq),
        out_shape=jax.ShapeDtypeStruct((n_tok, B_Q_HEADS * HEAD_DIM), BF16),
        grid=(batch, B_KV_HEADS, qblocks),
        in_specs=[
            pl.BlockSpec((B_GROUP, _BQ, HEAD_DIM), lambda b, kv, qi: (kv, b * qblocks + qi, 0)),
            pl.BlockSpec((None, seq, HEAD_DIM), lambda b, kv, qi: (k0 + kv, b, 0)),
            pl.BlockSpec((None, seq, HEAD_DIM), lambda b, kv, qi: (v0 + kv, b, 0)),
        ],
        out_specs=pl.BlockSpec((_BQ, B_GROUP * HEAD_DIM), lambda b, kv, qi: (b * qblocks + qi, kv)),
        compiler_params=_params(("arbitrary", "arbitrary", "arbitrary")),
    )(slots, slots, slots)


_MIX_ROWS = 512


def _branch_kernel(oa_ref, ob_ref, ga_ref, gb_ref, wa_ref, wb_ref, out_ref):
    for rt in range(out_ref.shape[0] // SUB_ROWS):
        rows = slice(rt * SUB_ROWS, (rt + 1) * SUB_ROWS)
        y_a = jnp.dot(oa_ref[rows, :], wa_ref[...], preferred_element_type=F32)
        y_b = jnp.dot(ob_ref[rows, :], wb_ref[...], preferred_element_type=F32)
        out_ref[rows, :] = (ga_ref[rows, :].astype(F32) * y_a + gb_ref[rows, :].astype(F32) * y_b).astype(BF16)


def _branch_mix(o_a, o_b, gates, wa, wb):
    n_tok = o_b.shape[0]
    d_model = wa.shape[1]
    tm = _MIX_ROWS
    row = lambda i: (i, 0)
    return pl.pallas_call(
        _branch_kernel,
        out_shape=jax.ShapeDtypeStruct((n_tok, d_model), BF16),
        grid=(n_tok // tm,),
        in_specs=[pl.BlockSpec((tm, o_a.shape[1]), row),
                  pl.BlockSpec((tm, o_b.shape[1]), row),
                  pl.BlockSpec((tm, d_model), lambda i: (i, 0)),
                  pl.BlockSpec((tm, d_model), lambda i: (i, 1)),
                  pl.BlockSpec(wa.shape, lambda i: (0, 0)),
                  pl.BlockSpec(wb.shape, lambda i: (0, 0))],
        out_specs=pl.BlockSpec((tm, d_model), row),
        compiler_params=_params(("arbitrary",)),
    )(o_a, o_b, gates, gates, wa, wb)


def _layer_norm(z, g, b):
    mu = jnp.mean(z, axis=-1, keepdims=True)
    zc = z - mu
    var = jnp.mean(zc * zc, axis=-1, keepdims=True)
    return zc * lax.rsqrt(var + LN_EPS) * g + b


def _to_slabs(slab_ref, value, row0):
    rows, d = value.shape
    chunks = d // LANES
    for c in range(chunks):
        slab_ref[pl.ds(row0 * chunks + c, rows, stride=chunks), :] = value[:, c * LANES:(c + 1) * LANES]


def _from_slabs(slab_ref, rows, chunks, c):
    return slab_ref[pl.ds(c, rows, stride=chunks), :]


def _out_proj_kernel(m_ref, w_ref, x_ref, g_ref, b_ref, rcat_ref, rb_ref, hs_ref, lg_ref, *, alpha):
    for rt in range(m_ref.shape[0] // SUB_ROWS):
        rows = slice(rt * SUB_ROWS, (rt + 1) * SUB_ROWS)
        mix = jnp.dot(m_ref[rows, :], w_ref[...], preferred_element_type=F32)
        h = _layer_norm(alpha * x_ref[rows, :] + mix, g_ref[...], b_ref[...])
        _to_slabs(hs_ref, h, rt * SUB_ROWS)
        hi = h.astype(BF16)
        lo = (h - hi.astype(F32)).astype(BF16)
        both = jnp.dot(hi, rcat_ref[...], preferred_element_type=F32)
        lg = both[:, :LANES] + both[:, LANES:] + jnp.dot(lo, rcat_ref[:, :LANES], preferred_element_type=F32)
        lg_ref[rows, :] = lg + rb_ref[...]


def _out_proj(merged, w_out, x2, g, b, r_cat, r_b, alpha):
    n_tok, d_model = x2.shape
    tm = _MIX_ROWS
    chunks = d_model // LANES
    row = lambda i: (i, 0)
    fix = lambda i: (0, 0)
    return pl.pallas_call(
        functools.partial(_out_proj_kernel, alpha=alpha),
        out_shape=(jax.ShapeDtypeStruct((n_tok * chunks, LANES), F32),
                   jax.ShapeDtypeStruct((n_tok, LANES), F32)),
        grid=(n_tok // tm,),
        in_specs=[pl.BlockSpec((tm, d_model), row),
                  pl.BlockSpec(w_out.shape, fix),
                  pl.BlockSpec((tm, d_model), row),
                  pl.BlockSpec((1, d_model), fix),
                  pl.BlockSpec((1, d_model), fix),
                  pl.BlockSpec(r_cat.shape, fix),
                  pl.BlockSpec((1, LANES), fix)],
        out_specs=(pl.BlockSpec((tm * chunks, LANES), row), pl.BlockSpec((tm, LANES), row)),
        compiler_params=_params(("arbitrary",)),
    )(merged, w_out, x2, g, b, r_cat, r_b)


_BIG_CHUNK = 512


def _moe_kernel(item_expert, item_row0, item_blocks, item_real, src_slab, dst_slab,
                h_hbm, wg_ref, wu_ref, wd_ref, y_hbm,
                stage, xbf, acc, obuf, gsem, ssem):
    it = pl.program_id(0)
    f = pl.program_id(1)
    n_items = pl.num_programs(0)
    n_ff = pl.num_programs(1)
    n_blocks = item_blocks[it]
    d_model = xbf.shape[1]
    chunks = d_model // LANES

    def slab(ref, first):
        return ref.at[pl.ds(pl.multiple_of(first, chunks), chunks)]

    def gather_copy(base, r):
        return pltpu.make_async_copy(slab(h_hbm, src_slab[base + r]), slab(stage, r * chunks), gsem.at[0])

    def scatter_copy(base, r):
        return pltpu.make_async_copy(slab(obuf, r * chunks), slab(y_hbm, dst_slab[base + r]), ssem.at[0])

    def for_rows(n_rows, fn):
        def group(t, c):
            for u in range(DMA_UNROLL):
                fn(t * DMA_UNROLL + u)
            return c
        n_groups = n_rows // DMA_UNROLL
        lax.fori_loop(0, n_groups, group, 0)

        def single(r, c):
            fn(r)
            return c
        lax.fori_loop(n_groups * DMA_UNROLL, n_rows, single, 0)

    def gather(item, op):
        base = item_row0[item]
        for_rows(item_blocks[item] * MOE_BLOCK, lambda r: op(gather_copy(base, r)))

    def scatter(item, op):
        base = item_row0[item]
        for_rows(item_real[item], lambda r: op(scatter_copy(base, r)))

    start = lambda cp: cp.start()
    wait = lambda cp: cp.wait()

    @pl.when((it == 0) & (f == 0))
    def _first_gather():
        gather(it, start)

    @pl.when(f == 0)
    def _gathered():
        gather(it, wait)

        def cast_rows(b, carry):
            r0 = pl.multiple_of(b * MOE_BLOCK, MOE_BLOCK)
            for c in range(chunks):
                piece = stage[pl.ds(r0 * chunks + c, MOE_BLOCK, stride=chunks), :]
                xbf[pl.ds(r0, MOE_BLOCK), c * LANES:(c + 1) * LANES] = piece.astype(BF16)
            return carry
        lax.fori_loop(0, n_blocks, cast_rows, 0)

        @pl.when(it + 1 < n_items)
        def _prefetch_next():
            gather(it + 1, start)

    def chunk(r0, rows, mode):
        xb = xbf[pl.ds(r0, rows), :]
        a = jnp.dot(xb, wg_ref[...].astype(BF16), preferred_element_type=F32)
        u = jnp.dot(xb, wu_ref[...].astype(BF16), preferred_element_type=F32)
        hid = (a / (1.0 + jnp.exp(-a)) * u).astype(BF16)
        y = jnp.dot(hid, wd_ref[...].astype(BF16), preferred_element_type=F32)
        if mode == "first":
            acc[pl.ds(r0, rows), :] = y
        elif mode == "middle":
            acc[pl.ds(r0, rows), :] += y
        else:
            if mode == "last":
                y = acc[pl.ds(r0, rows), :] + y
            for c in range(chunks):
                obuf[pl.ds(r0 * chunks + c, rows, stride=chunks), :] = y[:, c * LANES:(c + 1) * LANES]

    def run_item(mode):
        per_big = _BIG_CHUNK // MOE_BLOCK

        def big_chunk(t, c):
            chunk(pl.multiple_of(t * _BIG_CHUNK, _BIG_CHUNK), _BIG_CHUNK, mode)
            return c
        n_big = n_blocks // per_big
        lax.fori_loop(0, n_big, big_chunk, 0)

        def small_chunk(b, c):
            chunk(pl.multiple_of(b * MOE_BLOCK, MOE_BLOCK), MOE_BLOCK, mode)
            return c
        lax.fori_loop(n_big * per_big, n_blocks, small_chunk, 0)

    last = n_ff - 1

    @pl.when((n_blocks > 0) & (f == last))
    def _final():
        @pl.when(it > 0)
        def _drain_prev():
            scatter(it - 1, wait)
        run_item("only" if n_ff == 1 else "last")
        scatter(it, start)

    if n_ff > 1:
        @pl.when((n_blocks > 0) & (f == 0))
        def _first():
            run_item("first")

    if n_ff > 2:
        @pl.when((n_blocks > 0) & (f > 0) & (f < last))
        def _middle():
            run_item("middle")

    @pl.when((f == last) & (it == n_items - 1))
    def _drain_last():
        scatter(jnp.where(n_blocks > 0, it, _last_live(item_blocks, n_items)), wait)


def _last_live(item_blocks, n_items):
    def body(i, best):
        return jnp.where(item_blocks[i] > 0, i, best)
    return lax.fori_loop(0, n_items, body, 0)


def _moe(h_slabs, w_gate, w_up, w_down, item_expert, item_row0, item_blocks, item_real, src_slab, dst_slab):
    d_model = w_gate.shape[1]
    chunks = d_model // LANES
    n_tok = h_slabs.shape[0] // chunks
    ff = w_gate.shape[2]
    n_ff = ff // FF_TILE
    n_items = item_expert.shape[0]

    def ff_idx(it, f, blocks):
        return jnp.where(blocks[it] > 0, f, n_ff - 1)

    grid_spec = pltpu.PrefetchScalarGridSpec(
        num_scalar_prefetch=6,
        grid=(n_items, n_ff),
        in_specs=[
            pl.BlockSpec(memory_space=pl.ANY),
            pl.BlockSpec((None, d_model, FF_TILE), lambda it, f, ie, ir, ib, nr, st, dr: (ie[it], 0, ff_idx(it, f, ib))),
            pl.BlockSpec((None, d_model, FF_TILE), lambda it, f, ie, ir, ib, nr, st, dr: (ie[it], 0, ff_idx(it, f, ib))),
            pl.BlockSpec((None, FF_TILE, d_model), lambda it, f, ie, ir, ib, nr, st, dr: (ie[it], ff_idx(it, f, ib), 0)),
        ],
        out_specs=pl.BlockSpec(memory_space=pl.ANY),
        scratch_shapes=[
            pltpu.VMEM((ITEM_ROWS * chunks, LANES), F32),
            pltpu.VMEM((ITEM_ROWS, d_model), BF16),
            pltpu.VMEM((ITEM_ROWS, d_model), F32),
            pltpu.VMEM((ITEM_ROWS * chunks, LANES), F32),
            pltpu.SemaphoreType.DMA((1,)),
            pltpu.SemaphoreType.DMA((1,)),
        ],
    )
    return pl.pallas_call(
        _moe_kernel,
        out_shape=jax.ShapeDtypeStruct((n_tok * TOP_K * chunks, LANES), F32),
        grid_spec=grid_spec,
        compiler_params=_params(("arbitrary", "arbitrary")),
    )(item_expert, item_row0, item_blocks, item_real, src_slab, dst_slab, h_slabs, w_gate, w_up, w_down)


_COMBINE_ROWS = 512


def _combine_kernel(y0_ref, y1_ref, wt_ref, hs_ref, g_ref, b_ref, out_ref, *, alpha):
    rows, d_model = out_ref.shape
    chunks = d_model // LANES
    wt = wt_ref[...]
    w0, w1 = wt[:, 0:1], wt[:, 1:2]
    for c in range(chunks):
        cols = slice(c * LANES, (c + 1) * LANES)
        ffn = w0 * _from_slabs(y0_ref, rows, chunks, c) + w1 * _from_slabs(y1_ref, rows, chunks, c)
        out_ref[:, cols] = alpha * _from_slabs(hs_ref, rows, chunks, c) + ffn
    out_ref[...] = _layer_norm(out_ref[...], g_ref[...], b_ref[...])


def _combine(y_slabs, weights, h_slabs, g, b, alpha):
    d_model = g.shape[1]
    n_tok = h_slabs.shape[0] * LANES // d_model
    tm = _COMBINE_ROWS
    chunks = d_model // LANES
    second = n_tok // tm
    row = lambda i: (i, 0)
    fix = lambda i: (0, 0)
    return pl.pallas_call(
        functools.partial(_combine_kernel, alpha=alpha),
        out_shape=jax.ShapeDtypeStruct((n_tok, d_model), F32),
        grid=(n_tok // tm,),
        in_specs=[pl.BlockSpec((tm * chunks, LANES), row),
                  pl.BlockSpec((tm * chunks, LANES), lambda i: (second + i, 0)),
                  pl.BlockSpec((tm, TOP_K), row),
                  pl.BlockSpec((tm * chunks, LANES), row),
                  pl.BlockSpec((1, d_model), fix),
                  pl.BlockSpec((1, d_model), fix)],
        out_specs=pl.BlockSpec((tm, d_model), row),
        compiler_params=_params(("arbitrary",)),
    )(y_slabs, y_slabs, weights, h_slabs, g, b)


def _rope_tables(seq):
    f32 = np.float32
    half = HEAD_DIM // 2
    inv1 = np.power(f32(ROPE_THETA), -(np.arange(half, dtype=f32) / f32(half)))
    ang1 = np.arange(seq, dtype=f32)[:, None] * inv1[None, :]
    t = np.arange(seq)
    n_axis = HEAD_DIM // 4
    inv2 = np.power(f32(ROPE_THETA), -(np.arange(n_axis, dtype=f32) / f32(n_axis)))
    ang2 = np.concatenate([(t // GRID_W).astype(f32)[:, None] * inv2[None, :],
                           (t % GRID_W).astype(f32)[:, None] * inv2[None, :]], axis=-1)
    q_scale = f32(HEAD_DIM ** -0.5 * LOG2E)
    out = []
    for ang in (ang1, ang2):
        cos = np.concatenate([np.cos(ang), np.cos(ang)], axis=-1)
        sin = np.concatenate([-np.sin(ang), np.sin(ang)], axis=-1)
        out += [cos * q_scale, sin * q_scale, cos, sin]
    return jnp.asarray(np.stack(out, axis=0).astype(f32))


def _route(logits, n_tok, chunks):
    g_logits = logits[:, :N_GROUPS]
    g_prob = jax.nn.softmax(g_logits, axis=-1)
    g_idx = jnp.argmax(g_logits, axis=-1).astype(I32)
    g_gate = jnp.max(g_prob, axis=-1)
    e_logits = logits[:, N_GROUPS:N_GROUPS + EXPERTS_PER_GROUP]
    for g in range(1, N_GROUPS):
        lo = N_GROUPS + g * EXPERTS_PER_GROUP
        e_logits = jnp.where(g_idx[:, None] == g, logits[:, lo:lo + EXPERTS_PER_GROUP], e_logits)
    e_prob = jax.nn.softmax(e_logits, axis=-1)
    i1 = jnp.argmax(e_prob, axis=-1).astype(I32)
    p1 = jnp.max(e_prob, axis=-1)
    rest = jnp.where(jnp.arange(EXPERTS_PER_GROUP, dtype=I32)[None, :] == i1[:, None], -1.0, e_prob)
    i2 = jnp.argmax(rest, axis=-1).astype(I32)
    p2 = jnp.max(rest, axis=-1)
    top_p = jnp.stack([p1, p2], axis=-1)
    top_p = top_p / jnp.sum(top_p, axis=-1, keepdims=True)
    weights = g_gate[:, None] * top_p
    expert = g_idx[:, None] * EXPERTS_PER_GROUP + jnp.stack([i1, i2], axis=-1)

    n_slot = n_tok * TOP_K
    e_flat = expert.reshape(n_slot).astype(I32)
    order = jnp.argsort(e_flat).astype(I32)
    experts = jnp.arange(N_EXPERTS, dtype=I32)
    counts = jnp.sum(e_flat[:, None] == experts[None, :], axis=0, dtype=I32)
    padded = ((counts + MOE_BLOCK - 1) // MOE_BLOCK) * MOE_BLOCK
    p_ends = jnp.cumsum(padded)
    p_starts = p_ends - padded
    buf_len = n_slot + N_EXPERTS * MOE_BLOCK

    def pick(one_hot, table):
        return jnp.sum(jnp.where(one_hot, table[None, :], 0), axis=1, dtype=I32)

    pos = jnp.arange(buf_len, dtype=I32)[:, None]
    ended = p_ends[None, :] <= pos
    inside = jnp.logical_and(p_starts[None, :] <= pos, pos < p_ends[None, :])
    idx = pos[:, 0] - jnp.max(jnp.where(ended, p_ends[None, :], 0), axis=1)
    sorted_at = jnp.clip(pick(ended, counts) + idx, 0, n_slot - 1)
    slot_at = jnp.where(idx < pick(inside, counts), order[sorted_at], 0)
    tok_at, k_at = slot_at // TOP_K, slot_at % TOP_K
    src_slab = tok_at * chunks
    dst_slab = (k_at * n_tok + tok_at) * chunks

    n_items = N_EXPERTS + n_slot // ITEM_ROWS
    per_expert = (padded + ITEM_ROWS - 1) // ITEM_ROWS
    item_ends = jnp.cumsum(per_expert)
    total = item_ends[-1]
    ids = jnp.arange(n_items, dtype=I32)
    live = ids < total
    item = jnp.minimum(ids, jnp.maximum(total - 1, 0))[:, None]
    owner = jnp.logical_and((item_ends - per_expert)[None, :] <= item, item < item_ends[None, :])
    e_of = pick(owner, experts)
    chunk = item[:, 0] - pick(owner, item_ends - per_expert)
    item_row0 = jnp.where(live, pick(owner, p_starts) + chunk * ITEM_ROWS, 0).astype(I32)
    item_blocks = jnp.where(live, jnp.clip(pick(owner, padded) // MOE_BLOCK - chunk * ITEM_BLOCKS, 0, ITEM_BLOCKS), 0)
    item_real = jnp.where(live, jnp.clip(pick(owner, counts) - chunk * ITEM_ROWS, 0, ITEM_ROWS), 0)
    return weights, e_of, item_row0, item_blocks.astype(I32), item_real.astype(I32), src_slab, dst_slab


def kernel(x, w_in, q_norm_g, k_norm_g, w_branch_a, w_branch_b, w_out, ln1_g, ln1_b, w_group, b_group, w_router,
           b_router, w_gate, w_up, w_down, ln2_g, ln2_b):
    batch, seq, d_model = x.shape
    depth = w_in.shape[0]
    n_tok = batch * seq
    dn_alpha = (2 * depth) ** 0.25
    tabs = _rope_tables(seq)
    h = x.reshape(n_tok, d_model)
    for layer in range(depth):
        xb = h.astype(BF16)
        groups = [_proj_a(xb, w_in[layer], tabs, g, batch, seq) for g in range(N_DIL)]
        slots_b = _proj_b(xb, w_in[layer], tabs, q_norm_g[layer][None, :], k_norm_g[layer][None, :], seq)
        gates = _proj_gate(xb, w_in[layer])
        o_a = _attn_a(groups, batch, seq)
        score_bound = (jnp.max(jnp.abs(q_norm_g[layer])) * jnp.max(jnp.abs(k_norm_g[layer]))
                       * (HEAD_DIM ** 0.5 * LOG2E * BF16_SLACK))
        o_b = lax.cond(score_bound <= SCORE_BOUND,
                       lambda s: _attn_b(s, batch, seq, True), lambda s: _attn_b(s, batch, seq, False), slots_b)
        merged = _branch_mix(o_a, o_b, gates, w_branch_a[layer].astype(BF16), w_branch_b[layer].astype(BF16))
        w_r = jnp.concatenate([w_group[layer], w_router[layer]], axis=1)
        w_r = jnp.pad(w_r, ((0, 0), (0, LANES - w_r.shape[1])))
        r_hi = w_r.astype(BF16)
        r_lo = (w_r - r_hi.astype(F32)).astype(BF16)
        r_b = jnp.pad(jnp.concatenate([b_group[layer], b_router[layer]]), (0, LANES - N_GROUPS - N_EXPERTS))[None, :]
        h1_slabs, logits = _out_proj(merged, w_out[layer].astype(BF16), h, ln1_g[layer][None, :],
                                         ln1_b[layer][None, :], jnp.concatenate([r_hi, r_lo], axis=1), r_b, dn_alpha)
        weights, item_expert, item_row0, item_blocks, item_real, src_slab, dst_slab = _route(
            logits, n_tok, d_model // LANES)
        y_slabs = _moe(h1_slabs, w_gate[layer], w_up[layer], w_down[layer], item_expert, item_row0, item_blocks,
                       item_real, src_slab, dst_slab)
        h = _combine(y_slabs, weights, h1_slabs, ln2_g[layer][None, :], ln2_b[layer][None, :], dn_alpha)
    return h.reshape(batch, seq, d_model)
```

```python
import functools
import math

import jax
import jax.numpy as jnp
import numpy as np
from jax import lax
from jax.experimental import pallas as pl
from jax.experimental.pallas import tpu as pltpu

F32 = jnp.float32
BF16 = jnp.bfloat16
I32 = jnp.int32

HEAD_DIM = 128
ROPE_THETA = 10000.0
GRID_W = 64
DIL_PATTERNS = ((128, 1), (512, 4), (2048, 16))
N_DIL = len(DIL_PATTERNS)
A_HEADS = 8
BAND_HALF = 64
B_Q_HEADS = 16
B_KV_HEADS = 4
B_GROUP = B_Q_HEADS // B_KV_HEADS
N_GROUPS = 4
EXPERTS_PER_GROUP = 8
N_EXPERTS = N_GROUPS * EXPERTS_PER_GROUP
TOP_K = 2
MOE_BLOCK = 128
LN_EPS = 1e-5
RMS_EPS = 1e-6
NEG_BIG = -1e30
LOG2E = math.log2(math.e)

LANES = 128
COL_TILE = 8 * HEAD_DIM
ROW_TILE = 1024
SUB_ROWS = 256
ITEM_ROWS = 768
ITEM_BLOCKS = ITEM_ROWS // MOE_BLOCK
FF_TILE = 512
DMA_UNROLL = 16
VMEM_LIMIT = 56 * 1024 * 1024


def _params(sem, vmem=VMEM_LIMIT):
    return pltpu.CompilerParams(dimension_semantics=sem, vmem_limit_bytes=vmem)


_T_QA, _T_KA, _T_VA, _T_QB, _T_KVB, _T_GATE = 0, 3, 6, 9, 11, 12
_TAB_1D, _TAB_AXIAL = 0, 1
_RQ, _RK = 0, 2


def _cast_weights_once(w_ref, wbf_ref):
    @pl.when(pl.program_id(1) == 0)
    def _cast():
        wbf_ref[...] = w_ref[...].astype(BF16)


def _for_sub_tiles(x_ref, wbf_ref, emit):
    for rt in range(x_ref.shape[0] // SUB_ROWS):
        rows = slice(rt * SUB_ROWS, (rt + 1) * SUB_ROWS)
        emit(rt, rows, jnp.dot(x_ref[rows, :], wbf_ref[...], preferred_element_type=F32))


def _head(acc, h):
    return acc[:, h * HEAD_DIM:(h + 1) * HEAD_DIM]


def _rope(a, tab_ref, t, rows):
    return a * tab_ref[t, rows, :] + pltpu.roll(a, HEAD_DIM // 2, 1) * tab_ref[t + 1, rows, :]


def _rms(a, g_ref):
    ms = jnp.mean(a * a, axis=-1, keepdims=True)
    return a * lax.rsqrt(ms + RMS_EPS) * g_ref[...]


def _proj_a_kernel(x_ref, w_ref, tab_ref, out_ref, wbf_ref, *, dil):
    j = pl.program_id(0)
    _cast_weights_once(w_ref, wbf_ref)
    per_res = SUB_ROWS // dil

    def emit_with(fn):
        def emit(rt, rows, acc):
            if dil == 1:
                for h in range(A_HEADS):
                    out_ref[h, 0, rows, :] = fn(_head(acc, h), rows).astype(BF16)
                return
            for h in range(A_HEADS):
                by_res = fn(_head(acc, h), rows).reshape(per_res, dil, HEAD_DIM).swapaxes(0, 1)
                for r in range(dil):
                    out_ref[h, r, rt * per_res:(rt + 1) * per_res, :] = by_res[r].astype(BF16)
        _for_sub_tiles(x_ref, wbf_ref, emit)

    @pl.when(j == 0)
    def _q():
        emit_with(lambda a, rows: _rope(a, tab_ref, _RQ, rows))

    @pl.when(j == 1)
    def _k():
        emit_with(lambda a, rows: _rope(a, tab_ref, _RK, rows))

    @pl.when(j == 2)
    def _v():
        emit_with(lambda a, rows: a)


def _proj_b_kernel(x_ref, w_ref, tab_ref, gq_ref, gk_ref, out_ref, wbf_ref, *, q_tiles):
    j = pl.program_id(0)
    _cast_weights_once(w_ref, wbf_ref)
    heads = COL_TILE // HEAD_DIM

    @pl.when(j < q_tiles)
    def _qb():
        def emit(rt, rows, acc):
            for h in range(heads):
                out_ref[h, rows, :] = _rope(_rms(_head(acc, h), gq_ref), tab_ref, _RQ, rows).astype(BF16)
        _for_sub_tiles(x_ref, wbf_ref, emit)

    @pl.when(j == q_tiles)
    def _kvb():
        def emit(rt, rows, acc):
            for h in range(heads):
                if h < B_KV_HEADS:
                    out_ref[h, rows, :] = _rope(_rms(_head(acc, h), gk_ref), tab_ref, _RK, rows).astype(BF16)
                else:
                    out_ref[h, rows, :] = _head(acc, h).astype(BF16)
        _for_sub_tiles(x_ref, wbf_ref, emit)


def _proj_gate_kernel(x_ref, w_ref, out_ref, wbf_ref):
    _cast_weights_once(w_ref, wbf_ref)

    def emit(rt, rows, acc):
        out_ref[rows, :] = (1.0 / (1.0 + jnp.exp(-acc))).astype(BF16)
    _for_sub_tiles(x_ref, wbf_ref, emit)


def _proj_specs(d_model, seq, first_tile, tile_step, table_half):
    seq_blocks = seq // ROW_TILE
    return [
        pl.BlockSpec((ROW_TILE, d_model), lambda j, i: (i, 0)),
        pl.BlockSpec((d_model, COL_TILE), lambda j, i: (0, first_tile + tile_step * j)),
        pl.BlockSpec((4, ROW_TILE, HEAD_DIM), lambda j, i: (table_half, i % seq_blocks, 0)),
    ]


def _proj_a(xb, w_in, tabs, group, batch, seq):
    n_tok, d_model = xb.shape
    dil = DIL_PATTERNS[group][1]
    sub_len = seq // dil
    seq_blocks = seq // ROW_TILE
    return pl.pallas_call(
        functools.partial(_proj_a_kernel, dil=dil),
        out_shape=jax.ShapeDtypeStruct((3 * A_HEADS, batch, dil, sub_len, HEAD_DIM), BF16),
        grid=(3, n_tok // ROW_TILE),
        in_specs=_proj_specs(d_model, seq, group, N_DIL, _TAB_1D),
        out_specs=pl.BlockSpec((A_HEADS, None, dil, ROW_TILE // dil, HEAD_DIM),
                               lambda j, i: (j, i // seq_blocks, 0, i % seq_blocks, 0)),
        scratch_shapes=[pltpu.VMEM((d_model, COL_TILE), BF16)],
        compiler_params=_params(("arbitrary", "arbitrary")),
    )(xb, w_in, tabs)


def _proj_b(xb, w_in, tabs, gq, gk, seq):
    n_tok, d_model = xb.shape
    q_tiles = B_Q_HEADS * HEAD_DIM // COL_TILE
    heads = COL_TILE // HEAD_DIM
    fix = lambda j, i: (0, 0)
    return pl.pallas_call(
        functools.partial(_proj_b_kernel, q_tiles=q_tiles),
        out_shape=jax.ShapeDtypeStruct(((q_tiles + 1) * heads, n_tok, HEAD_DIM), BF16),
        grid=(q_tiles + 1, n_tok // ROW_TILE),
        in_specs=_proj_specs(d_model, seq, _T_QB, 1, _TAB_AXIAL) + [pl.BlockSpec((1, HEAD_DIM), fix), pl.BlockSpec((1, HEAD_DIM), fix)],
        out_specs=pl.BlockSpec((heads, ROW_TILE, HEAD_DIM), lambda j, i: (j, i, 0)),
        scratch_shapes=[pltpu.VMEM((d_model, COL_TILE), BF16)],
        compiler_params=_params(("arbitrary", "arbitrary")),
    )(xb, w_in, tabs, gq, gk)


def _proj_gate(xb, w_in):
    n_tok, d_model = xb.shape
    n_ct = w_in.shape[1] // COL_TILE - _T_GATE
    return pl.pallas_call(
        _proj_gate_kernel,
        out_shape=jax.ShapeDtypeStruct((n_tok, n_ct * COL_TILE), BF16),
        grid=(n_ct, n_tok // ROW_TILE),
        in_specs=[pl.BlockSpec((ROW_TILE, d_model), lambda j, i: (i, 0)),
                  pl.BlockSpec((d_model, COL_TILE), lambda j, i: (0, _T_GATE + j))],
        out_specs=pl.BlockSpec((ROW_TILE, COL_TILE), lambda j, i: (i, j)),
        scratch_shapes=[pltpu.VMEM((d_model, COL_TILE), BF16)],
        compiler_params=_params(("arbitrary", "arbitrary")),
    )(xb, w_in)


_SUB = 128
_WIN = _SUB + 2 * BAND_HALF
_MERGE_ROWS = 256
_A_UNROLL = 32


def _attn_a_group(g, dil, seq, q_ref, k_ref, v_ref, og, lg, bias):
    sub_len = seq // dil
    per_seq = sub_len // _SUB
    shift = per_seq.bit_length() - 1
    ones = jnp.ones((_WIN, HEAD_DIM), BF16)

    def body(i, carry):
        r = lax.shift_right_logical(i, shift)
        p0 = pl.multiple_of((i & (per_seq - 1)) * _SUB, _SUB)
        start = pl.multiple_of(jnp.clip(p0 - BAND_HALF, 0, sub_len - _WIN), BAND_HALF)
        q = q_ref[r, pl.ds(p0, _SUB), :]
        k = k_ref[r, pl.ds(start, _WIN), :]
        v = v_ref[r, pl.ds(start, _WIN), :]
        s = lax.dot_general(q, k, (((1,), (1,)), ((), ())), preferred_element_type=F32)
        s = s + bias[lax.shift_right_logical(p0 - start, BAND_HALF.bit_length() - 1)]
        m = jnp.max(s, axis=1, keepdims=True)
        p = jnp.exp2(s - m).astype(BF16)
        both = jnp.dot(p, jnp.concatenate([v, ones], axis=1), preferred_element_type=F32)
        l = both[:, HEAD_DIM:]
        o = both[:, :HEAD_DIM] / l
        lse = m + jnp.log2(l)
        if dil == 1:
            rows = pl.ds(p0, _SUB)
        else:
            rows = pl.ds(p0 * dil + r, _SUB, stride=dil)
        og[g, rows, :] = o
        lg[g, rows, :] = lse
        return carry

    lax.fori_loop(0, seq // _SUB, body, 0, unroll=_A_UNROLL)


def _attn_a_kernel(*refs, seq):
    qkv, o_ref = refs[:3 * N_DIL], refs[3 * N_DIL]
    og, lg, bias = refs[3 * N_DIL + 1:]
    row = lax.broadcasted_iota(I32, (_SUB, _WIN), 0)
    col = lax.broadcasted_iota(I32, (_SUB, _WIN), 1)
    for case in range(3):
        bias[case] = jnp.where(jnp.abs(col - row - case * BAND_HALF) <= BAND_HALF, 0.0, NEG_BIG)
    for g, (_, dil) in enumerate(DIL_PATTERNS):
        _attn_a_group(g, dil, seq, qkv[3 * g], qkv[3 * g + 1], qkv[3 * g + 2], og, lg, bias)

    def merge(c, carry):
        rows = pl.ds(pl.multiple_of(c * _MERGE_ROWS, _MERGE_ROWS), _MERGE_ROWS)
        ls = [lg[g, rows, :] for g in range(N_DIL)]
        mx = functools.reduce(jnp.maximum, ls)
        ws = [jnp.exp2(l - mx) for l in ls]
        num = functools.reduce(lambda a, b: a + b, [w * og[g, rows, :] for g, w in enumerate(ws)])
        o_ref[rows, :] = (num / functools.reduce(lambda a, b: a + b, ws)).astype(o_ref.dtype)
        return carry

    lax.fori_loop(0, seq // _MERGE_ROWS, merge, 0)


def _attn_a(groups, batch, seq):
    in_specs, operands = [], []
    for g, (_, dil) in enumerate(DIL_PATTERNS):
        for kind in range(3):
            first = kind * A_HEADS
            in_specs.append(pl.BlockSpec((None, None, dil, seq // dil, HEAD_DIM),
                                         lambda b, h, first=first: (first + h, b, 0, 0, 0)))
            operands.append(groups[g])
    return pl.pallas_call(
        functools.partial(_attn_a_kernel, seq=seq),
        out_shape=jax.ShapeDtypeStruct((batch * seq, A_HEADS * HEAD_DIM), BF16),
        grid=(batch, A_HEADS),
        in_specs=in_specs,
        out_specs=pl.BlockSpec((seq, HEAD_DIM), lambda b, h: (b, h)),
        scratch_shapes=[pltpu.VMEM((N_DIL, seq, HEAD_DIM), F32), pltpu.VMEM((N_DIL, seq, HEAD_DIM), F32),
                        pltpu.VMEM((3, _SUB, _WIN), F32)],
        compiler_params=_params(("arbitrary", "arbitrary")),
    )(*operands)


_BQ = 1024
_BK = 512
SCORE_BOUND = 64.0
BF16_SLACK = 1.02


def _attn_b_kernel(q_ref, k_ref, v_ref, o_ref, *, seq):
    rows = B_GROUP * _BQ
    q = q_ref[...].reshape(rows, HEAD_DIM)

    def body(c, carry):
        m, l, acc = carry
        c0 = pl.multiple_of(c * _BK, _BK)
        k = k_ref[pl.ds(c0, _BK), :]
        v = v_ref[pl.ds(c0, _BK), :]
        s = lax.dot_general(q, k, (((1,), (1,)), ((), ())), preferred_element_type=F32)
        m_new = jnp.maximum(m, jnp.max(s, axis=1, keepdims=True))
        alpha = jnp.exp2(m - m_new)
        p = jnp.exp2(s - m_new)
        l = alpha * l + jnp.sum(p, axis=1, keepdims=True)
        acc = alpha * acc + jnp.dot(p.astype(BF16), v, preferred_element_type=F32)
        return m_new, l, acc

    init = (jnp.full((rows, 1), NEG_BIG, F32), jnp.zeros((rows, 1), F32), jnp.zeros((rows, HEAD_DIM), F32))
    _, l, acc = lax.fori_loop(0, seq // _BK, body, init)
    o = acc / l
    for g in range(B_GROUP):
        o_ref[:, g * HEAD_DIM:(g + 1) * HEAD_DIM] = o[g * _BQ:(g + 1) * _BQ].astype(o_ref.dtype)


def _attn_b_bounded_kernel(q_ref, k_ref, v_ref, o_ref, *, seq):
    rows = B_GROUP * _BQ
    q = q_ref[...].reshape(rows, HEAD_DIM)
    ones = jnp.ones((_BK, HEAD_DIM), BF16)
    acc = jnp.zeros((rows, 2 * HEAD_DIM), F32)
    for c in range(seq // _BK):
        k = k_ref[c * _BK:(c + 1) * _BK, :]
        v = v_ref[c * _BK:(c + 1) * _BK, :]
        s = lax.dot_general(q, k, (((1,), (1,)), ((), ())), preferred_element_type=F32)
        p = jnp.exp2(s).astype(BF16)
        acc = acc + jnp.dot(p, jnp.concatenate([v, ones], axis=1), preferred_element_type=F32)
    o = acc[:, :HEAD_DIM] / acc[:, HEAD_DIM:HEAD_DIM + 1]
    for g in range(B_GROUP):
        o_ref[:, g * HEAD_DIM:(g + 1) * HEAD_DIM] = o[g * _BQ:(g + 1) * _BQ].astype(o_ref.dtype)


def _attn_b(slots, batch, seq, bounded):
    n_tok = batch * seq
    k0 = B_Q_HEADS
    v0 = k0 + B_KV_HEADS
    qblocks = seq // _BQ
    return pl.pallas_call(
        functools.partial(_attn_b_bounded_kernel if bounded else _attn_b_kernel, seq=seq),
        out_shape=jax.ShapeDtypeStruct((n_tok, B_Q_HEADS * HEAD_DIM), BF16),
        grid=(batch, B_KV_HEADS, qblocks),
        in_specs=[
            pl.BlockSpec((B_GROUP, _BQ, HEAD_DIM), lambda b, kv, qi: (kv, b * qblocks + qi, 0)),
            pl.BlockSpec((None, seq, HEAD_DIM), lambda b, kv, qi: (k0 + kv, b, 0)),
            pl.BlockSpec((None, seq, HEAD_DIM), lambda b, kv, qi: (v0 + kv, b, 0)),
        ],
        out_specs=pl.BlockSpec((_BQ, B_GROUP * HEAD_DIM), lambda b, kv, qi: (b * qblocks + qi, kv)),
        compiler_params=_params(("arbitrary", "arbitrary", "arbitrary")),
    )(slots, slots, slots)


_MIX_ROWS = 512


def _branch_kernel(oa_ref, ob_ref, ga_ref, gb_ref, wa_ref, wb_ref, out_ref):
    for rt in range(out_ref.shape[0] // SUB_ROWS):
        rows = slice(rt * SUB_ROWS, (rt + 1) * SUB_ROWS)
        y_a = jnp.dot(oa_ref[rows, :], wa_ref[...], preferred_element_type=F32)
        y_b = jnp.dot(ob_ref[rows, :], wb_ref[...], preferred_element_type=F32)
        out_ref[rows, :] = (ga_ref[rows, :].astype(F32) * y_a + gb_ref[rows, :].astype(F32) * y_b).astype(BF16)


def _branch_mix(o_a, o_b, gates, wa, wb):
    n_tok = o_b.shape[0]
    d_model = wa.shape[1]
    tm = _MIX_ROWS
    row = lambda i: (i, 0)
    return pl.pallas_call(
        _branch_kernel,
        out_shape=jax.ShapeDtypeStruct((n_tok, d_model), BF16),
        grid=(n_tok // tm,),
        in_specs=[pl.BlockSpec((tm, o_a.shape[1]), row),
                  pl.BlockSpec((tm, o_b.shape[1]), row),
                  pl.BlockSpec((tm, d_model), lambda i: (i, 0)),
                  pl.BlockSpec((tm, d_model), lambda i: (i, 1)),
                  pl.BlockSpec(wa.shape, lambda i: (0, 0)),
                  pl.BlockSpec(wb.shape, lambda i: (0, 0))],
        out_specs=pl.BlockSpec((tm, d_model), row),
        compiler_params=_params(("arbitrary",)),
    )(o_a, o_b, gates, gates, wa, wb)


def _layer_norm(z, g, b):
    mu = jnp.mean(z, axis=-1, keepdims=True)
    zc = z - mu
    var = jnp.mean(zc * zc, axis=-1, keepdims=True)
    return zc * lax.rsqrt(var + LN_EPS) * g + b


def _to_slabs(slab_ref, value, row0):
    rows, d = value.shape
    chunks = d // LANES
    for c in range(chunks):
        slab_ref[pl.ds(row0 * chunks + c, rows, stride=chunks), :] = value[:, c * LANES:(c + 1) * LANES]


def _from_slabs(slab_ref, rows, chunks, c):
    return slab_ref[pl.ds(c, rows, stride=chunks), :]


def _out_proj_kernel(m_ref, w_ref, x_ref, g_ref, b_ref, rcat_ref, rb_ref, hs_ref, lg_ref, *, alpha):
    for rt in range(m_ref.shape[0] // SUB_ROWS):
        rows = slice(rt * SUB_ROWS, (rt + 1) * SUB_ROWS)
        mix = jnp.dot(m_ref[rows, :], w_ref[...], preferred_element_type=F32)
        h = _layer_norm(alpha * x_ref[rows, :] + mix, g_ref[...], b_ref[...])
        _to_slabs(hs_ref, h, rt * SUB_ROWS)
        hi = h.astype(BF16)
        lo = (h - hi.astype(F32)).astype(BF16)
        both = jnp.dot(hi, rcat_ref[...], preferred_element_type=F32)
        lg = both[:, :LANES] + both[:, LANES:] + jnp.dot(lo, rcat_ref[:, :LANES], preferred_element_type=F32)
        lg_ref[rows, :] = lg + rb_ref[...]


def _out_proj(merged, w_out, x2, g, b, r_cat, r_b, alpha):
    n_tok, d_model = x2.shape
    tm = _MIX_ROWS
    chunks = d_model // LANES
    row = lambda i: (i, 0)
    fix = lambda i: (0, 0)
    return pl.pallas_call(
        functools.partial(_out_proj_kernel, alpha=alpha),
        out_shape=(jax.ShapeDtypeStruct((n_tok * chunks, LANES), F32),
                   jax.ShapeDtypeStruct((n_tok, LANES), F32)),
        grid=(n_tok // tm,),
        in_specs=[pl.BlockSpec((tm, d_model), row),
                  pl.BlockSpec(w_out.shape, fix),
                  pl.BlockSpec((tm, d_model), row),
                  pl.BlockSpec((1, d_model), fix),
                  pl.BlockSpec((1, d_model), fix),
                  pl.BlockSpec(r_cat.shape, fix),
                  pl.BlockSpec((1, LANES), fix)],
        out_specs=(pl.BlockSpec((tm * chunks, LANES), row), pl.BlockSpec((tm, LANES), row)),
        compiler_params=_params(("arbitrary",)),
    )(merged, w_out, x2, g, b, r_cat, r_b)


_BIG_CHUNK = 512


def _moe_kernel(item_expert, item_row0, item_blocks, item_real, src_slab, dst_slab,
                h_hbm, wg_ref, wu_ref, wd_ref, y_hbm,
                stage, xbf, acc, obuf, gsem, ssem):
    it = pl.program_id(0)
    f = pl.program_id(1)
    n_items = pl.num_programs(0)
    n_ff = pl.num_programs(1)
    n_blocks = item_blocks[it]
    d_model = xbf.shape[1]
    chunks = d_model // LANES

    def slab(ref, first):
        return ref.at[pl.ds(pl.multiple_of(first, chunks), chunks)]

    def gather_copy(base, r):
        return pltpu.make_async_copy(slab(h_hbm, src_slab[base + r]), slab(stage, r * chunks), gsem.at[0])

    def scatter_copy(base, r):
        return pltpu.make_async_copy(slab(obuf, r * chunks), slab(y_hbm, dst_slab[base + r]), ssem.at[0])

    def for_rows(n_rows, fn):
        def group(t, c):
            for u in range(DMA_UNROLL):
                fn(t * DMA_UNROLL + u, u)
            return c
        n_groups = n_rows // DMA_UNROLL
        lax.fori_loop(0, n_groups, group, 0)

        def single(r, c):
            fn(r, 0)
            return c
        lax.fori_loop(n_groups * DMA_UNROLL, n_rows, single, 0)

    def gather(item, op):
        base = item_row0[item]
        for_rows(item_blocks[item] * MOE_BLOCK, lambda r, lane: op(gather_copy(base, r), 0))

    def scatter(item, op):
        base = item_row0[item]
        for_rows(item_real[item], lambda r, lane: op(scatter_copy(base, r), lane % 2))

    start = lambda cp, priority: cp.start(priority=priority)
    wait = lambda cp, priority: cp.wait()

    @pl.when((it == 0) & (f == 0))
    def _first_gather():
        gather(it, start)

    @pl.when(f == 0)
    def _gathered():
        gather(it, wait)

        def cast_rows(b, carry):
            r0 = pl.multiple_of(b * MOE_BLOCK, MOE_BLOCK)
            for c in range(chunks):
                piece = stage[pl.ds(r0 * chunks + c, MOE_BLOCK, stride=chunks), :]
                xbf[pl.ds(r0, MOE_BLOCK), c * LANES:(c + 1) * LANES] = piece.astype(BF16)
            return carry
        lax.fori_loop(0, n_blocks, cast_rows, 0)

        @pl.when(it + 1 < n_items)
        def _prefetch_next():
            gather(it + 1, start)

    def chunk(r0, rows, mode):
        xb = xbf[pl.ds(r0, rows), :]
        a = jnp.dot(xb, wg_ref[...].astype(BF16), preferred_element_type=F32)
        u = jnp.dot(xb, wu_ref[...].astype(BF16), preferred_element_type=F32)
        hid = (a / (1.0 + jnp.exp(-a)) * u).astype(BF16)
        y = jnp.dot(hid, wd_ref[...].astype(BF16), preferred_element_type=F32)
        if mode == "first":
            acc[pl.ds(r0, rows), :] = y
        elif mode == "middle":
            acc[pl.ds(r0, rows), :] += y
        else:
            if mode == "last":
                y = acc[pl.ds(r0, rows), :] + y
            for c in range(chunks):
                obuf[pl.ds(r0 * chunks + c, rows, stride=chunks), :] = y[:, c * LANES:(c + 1) * LANES]

    def run_item(mode):
        per_big = _BIG_CHUNK // MOE_BLOCK

        def big_chunk(t, c):
            chunk(pl.multiple_of(t * _BIG_CHUNK, _BIG_CHUNK), _BIG_CHUNK, mode)
            return c
        n_big = n_blocks // per_big
        lax.fori_loop(0, n_big, big_chunk, 0)

        def small_chunk(b, c):
            chunk(pl.multiple_of(b * MOE_BLOCK, MOE_BLOCK), MOE_BLOCK, mode)
            return c
        lax.fori_loop(n_big * per_big, n_blocks, small_chunk, 0)

    last = n_ff - 1

    @pl.when((n_blocks > 0) & (f == last))
    def _final():
        @pl.when(it > 0)
        def _drain_prev():
            scatter(it - 1, wait)
        run_item("only" if n_ff == 1 else "last")
        scatter(it, start)

    if n_ff > 1:
        @pl.when((n_blocks > 0) & (f == 0))
        def _first():
            run_item("first")

    if n_ff > 2:
        @pl.when((n_blocks > 0) & (f > 0) & (f < last))
        def _middle():
            run_item("middle")

    @pl.when((f == last) & (it == n_items - 1))
    def _drain_last():
        scatter(jnp.where(n_blocks > 0, it, _last_live(item_blocks, n_items)), wait)


def _last_live(item_blocks, n_items):
    def body(i, best):
        return jnp.where(item_blocks[i] > 0, i, best)
    return lax.fori_loop(0, n_items, body, 0)


def _moe(h_slabs, w_gate, w_up, w_down, item_expert, item_row0, item_blocks, item_real, src_slab, dst_slab):
    d_model = w_gate.shape[1]
    chunks = d_model // LANES
    n_tok = h_slabs.shape[0] // chunks
    ff = w_gate.shape[2]
    n_ff = ff // FF_TILE
    n_items = item_expert.shape[0]

    def ff_idx(it, f, blocks):
        return jnp.where(blocks[it] > 0, f, n_ff - 1)

    grid_spec = pltpu.PrefetchScalarGridSpec(
        num_scalar_prefetch=6,
        grid=(n_items, n_ff),
        in_specs=[
            pl.BlockSpec(memory_space=pl.ANY),
            pl.BlockSpec((None, d_model, FF_TILE), lambda it, f, ie, ir, ib, nr, st, dr: (ie[it], 0, ff_idx(it, f, ib))),
            pl.BlockSpec((None, d_model, FF_TILE), lambda it, f, ie, ir, ib, nr, st, dr: (ie[it], 0, ff_idx(it, f, ib))),
            pl.BlockSpec((None, FF_TILE, d_model), lambda it, f, ie, ir, ib, nr, st, dr: (ie[it], ff_idx(it, f, ib), 0)),
        ],
        out_specs=pl.BlockSpec(memory_space=pl.ANY),
        scratch_shapes=[
            pltpu.VMEM((ITEM_ROWS * chunks, LANES), F32),
            pltpu.VMEM((ITEM_ROWS, d_model), BF16),
            pltpu.VMEM((ITEM_ROWS, d_model), F32),
            pltpu.VMEM((ITEM_ROWS * chunks, LANES), F32),
            pltpu.SemaphoreType.DMA((1,)),
            pltpu.SemaphoreType.DMA((1,)),
        ],
    )
    return pl.pallas_call(
        _moe_kernel,
        out_shape=jax.ShapeDtypeStruct((n_tok * TOP_K * chunks, LANES), F32),
        grid_spec=grid_spec,
        compiler_params=_params(("arbitrary", "arbitrary")),
    )(item_expert, item_row0, item_blocks, item_real, src_slab, dst_slab, h_slabs, w_gate, w_up, w_down)


_COMBINE_ROWS = 512


def _combine_kernel(y0_ref, y1_ref, wt_ref, hs_ref, g_ref, b_ref, out_ref, *, alpha):
    rows, d_model = out_ref.shape
    chunks = d_model // LANES
    wt = wt_ref[...]
    w0, w1 = wt[:, 0:1], wt[:, 1:2]
    for c in range(chunks):
        cols = slice(c * LANES, (c + 1) * LANES)
        ffn = w0 * _from_slabs(y0_ref, rows, chunks, c) + w1 * _from_slabs(y1_ref, rows, chunks, c)
        out_ref[:, cols] = alpha * _from_slabs(hs_ref, rows, chunks, c) + ffn
    out_ref[...] = _layer_norm(out_ref[...], g_ref[...], b_ref[...])


def _combine(y_slabs, weights, h_slabs, g, b, alpha):
    d_model = g.shape[1]
    n_tok = h_slabs.shape[0] * LANES // d_model
    tm = _COMBINE_ROWS
    chunks = d_model // LANES
    second = n_tok // tm
    row = lambda i: (i, 0)
    fix = lambda i: (0, 0)
    return pl.pallas_call(
        functools.partial(_combine_kernel, alpha=alpha),
        out_shape=jax.ShapeDtypeStruct((n_tok, d_model), F32),
        grid=(n_tok // tm,),
        in_specs=[pl.BlockSpec((tm * chunks, LANES), row),
                  pl.BlockSpec((tm * chunks, LANES), lambda i: (second + i, 0)),
                  pl.BlockSpec((tm, TOP_K), row),
                  pl.BlockSpec((tm * chunks, LANES), row),
                  pl.BlockSpec((1, d_model), fix),
                  pl.BlockSpec((1, d_model), fix)],
        out_specs=pl.BlockSpec((tm, d_model), row),
        compiler_params=_params(("arbitrary",)),
    )(y_slabs, y_slabs, weights, h_slabs, g, b)


def _rope_tables(seq):
    f32 = np.float32
    half = HEAD_DIM // 2
    inv1 = np.power(f32(ROPE_THETA), -(np.arange(half, dtype=f32) / f32(half)))
    ang1 = np.arange(seq, dtype=f32)[:, None] * inv1[None, :]
    t = np.arange(seq)
    n_axis = HEAD_DIM // 4
    inv2 = np.power(f32(ROPE_THETA), -(np.arange(n_axis, dtype=f32) / f32(n_axis)))
    ang2 = np.concatenate([(t // GRID_W).astype(f32)[:, None] * inv2[None, :],
                           (t % GRID_W).astype(f32)[:, None] * inv2[None, :]], axis=-1)
    q_scale = f32(HEAD_DIM ** -0.5 * LOG2E)
    out = []
    for ang in (ang1, ang2):
        cos = np.concatenate([np.cos(ang), np.cos(ang)], axis=-1)
        sin = np.concatenate([-np.sin(ang), np.sin(ang)], axis=-1)
        out += [cos * q_scale, sin * q_scale, cos, sin]
    return jnp.asarray(np.stack(out, axis=0).astype(f32))


def _route(logits, n_tok, chunks):
    g_logits = logits[:, :N_GROUPS]
    g_prob = jax.nn.softmax(g_logits, axis=-1)
    g_idx = jnp.argmax(g_logits, axis=-1).astype(I32)
    g_gate = jnp.max(g_prob, axis=-1)
    e_logits = logits[:, N_GROUPS:N_GROUPS + EXPERTS_PER_GROUP]
    for g in range(1, N_GROUPS):
        lo = N_GROUPS + g * EXPERTS_PER_GROUP
        e_logits = jnp.where(g_idx[:, None] == g, logits[:, lo:lo + EXPERTS_PER_GROUP], e_logits)
    e_prob = jax.nn.softmax(e_logits, axis=-1)
    i1 = jnp.argmax(e_prob, axis=-1).astype(I32)
    p1 = jnp.max(e_prob, axis=-1)
    rest = jnp.where(jnp.arange(EXPERTS_PER_GROUP, dtype=I32)[None, :] == i1[:, None], -1.0, e_prob)
    i2 = jnp.argmax(rest, axis=-1).astype(I32)
    p2 = jnp.max(rest, axis=-1)
    top_p = jnp.stack([p1, p2], axis=-1)
    top_p = top_p / jnp.sum(top_p, axis=-1, keepdims=True)
    weights = g_gate[:, None] * top_p
    expert = g_idx[:, None] * EXPERTS_PER_GROUP + jnp.stack([i1, i2], axis=-1)

    n_slot = n_tok * TOP_K
    e_flat = expert.reshape(n_slot).astype(I32)
    order = jnp.argsort(e_flat).astype(I32)
    experts = jnp.arange(N_EXPERTS, dtype=I32)
    counts = jnp.sum(e_flat[:, None] == experts[None, :], axis=0, dtype=I32)
    padded = ((counts + MOE_BLOCK - 1) // MOE_BLOCK) * MOE_BLOCK
    p_ends = jnp.cumsum(padded)
    p_starts = p_ends - padded
    buf_len = n_slot + N_EXPERTS * MOE_BLOCK

    def pick(one_hot, table):
        return jnp.sum(jnp.where(one_hot, table[None, :], 0), axis=1, dtype=I32)

    pos = jnp.arange(buf_len, dtype=I32)[:, None]
    ended = p_ends[None, :] <= pos
    inside = jnp.logical_and(p_starts[None, :] <= pos, pos < p_ends[None, :])
    idx = pos[:, 0] - jnp.max(jnp.where(ended, p_ends[None, :], 0), axis=1)
    sorted_at = jnp.clip(pick(ended, counts) + idx, 0, n_slot - 1)
    slot_at = jnp.where(idx < pick(inside, counts), order[sorted_at], 0)
    tok_at, k_at = slot_at // TOP_K, slot_at % TOP_K
    src_slab = tok_at * chunks
    dst_slab = (k_at * n_tok + tok_at) * chunks

    n_items = N_EXPERTS + n_slot // ITEM_ROWS
    per_expert = (padded + ITEM_ROWS - 1) // ITEM_ROWS
    item_ends = jnp.cumsum(per_expert)
    total = item_ends[-1]
    ids = jnp.arange(n_items, dtype=I32)
    live = ids < total
    item = jnp.minimum(ids, jnp.maximum(total - 1, 0))[:, None]
    owner = jnp.logical_and((item_ends - per_expert)[None, :] <= item, item < item_ends[None, :])
    e_of = pick(owner, experts)
    chunk = item[:, 0] - pick(owner, item_ends - per_expert)
    item_row0 = jnp.where(live, pick(owner, p_starts) + chunk * ITEM_ROWS, 0).astype(I32)
    item_blocks = jnp.where(live, jnp.clip(pick(owner, padded) // MOE_BLOCK - chunk * ITEM_BLOCKS, 0, ITEM_BLOCKS), 0)
    item_real = jnp.where(live, jnp.clip(pick(owner, counts) - chunk * ITEM_ROWS, 0, ITEM_ROWS), 0)
    return weights, e_of, item_row0, item_blocks.astype(I32), item_real.astype(I32), src_slab, dst_slab


def kernel(x, w_in, q_norm_g, k_norm_g, w_branch_a, w_branch_b, w_out, ln1_g, ln1_b, w_group, b_group, w_router,
           b_router, w_gate, w_up, w_down, ln2_g, ln2_b):
    batch, seq, d_model = x.shape
    depth = w_in.shape[0]
    n_tok = batch * seq
    dn_alpha = (2 * depth) ** 0.25
    tabs = _rope_tables(seq)
    h = x.reshape(n_tok, d_model)
    for layer in range(depth):
        xb = h.astype(BF16)
        groups = [_proj_a(xb, w_in[layer], tabs, g, batch, seq) for g in range(N_DIL)]
        slots_b = _proj_b(xb, w_in[layer], tabs, q_norm_g[layer][None, :], k_norm_g[layer][None, :], seq)
        gates = _proj_gate(xb, w_in[layer])
        o_a = _attn_a(groups, batch, seq)
        score_bound = (jnp.max(jnp.abs(q_norm_g[layer])) * jnp.max(jnp.abs(k_norm_g[layer]))
                       * (HEAD_DIM ** 0.5 * LOG2E * BF16_SLACK))
        o_b = lax.cond(score_bound <= SCORE_BOUND,
                       lambda s: _attn_b(s, batch, seq, True), lambda s: _attn_b(s, batch, seq, False), slots_b)
        merged = _branch_mix(o_a, o_b, gates, w_branch_a[layer].astype(BF16), w_branch_b[layer].astype(BF16))
        w_r = jnp.concatenate([w_group[layer], w_router[layer]], axis=1)
        w_r = jnp.pad(w_r, ((0, 0), (0, LANES - w_r.shape[1])))
        r_hi = w_r.astype(BF16)
        r_lo = (w_r - r_hi.astype(F32)).astype(BF16)
        r_b = jnp.pad(jnp.concatenate([b_group[layer], b_router[layer]]), (0, LANES - N_GROUPS - N_EXPERTS))[None, :]
        h1_slabs, logits = _out_proj(merged, w_out[layer].astype(BF16), h, ln1_g[layer][None, :],
                                         ln1_b[layer][None, :], jnp.concatenate([r_hi, r_lo], axis=1), r_b, dn_alpha)
        weights, item_expert, item_row0, item_blocks, item_real, src_slab, dst_slab = _route(
            logits, n_tok, d_model // LANES)
        y_slabs = _moe(h1_slabs, w_gate[layer], w_up[layer], w_down[layer], item_expert, item_row0, item_blocks,
                       item_real, src_slab, dst_slab)
        h = _combine(y_slabs, weights, h1_slabs, ln2_g[layer][None, :], ln2_b[layer][None, :], dn_alpha)
    return h.reshape(batch, seq, d_model)
```
